```python
import math
import numpy as np
import jax
import jax.numpy as jnp
from jax import lax

D_MODEL = 1024
BATCH = 16
SEQ = 2048
DEPTH = 4

GROUP_W = D_MODEL // 4

GLA_H = 4
GLA_DV = GROUP_W // GLA_H
GLA_DK = GLA_DV // 2
GLA_RANK = 16
GLA_TAU = 16.0
GLA_CHUNK = 16

SGU_G = 4
SGU_CG = GROUP_W // SGU_G
SGU_CHUNK = 128

SSM_H = 4
SSM_P = GROUP_W // SSM_H
SSM_G = 2
SSM_N = 64
SSM_CONV = 4
SSM_CHUNK = 128
SSM_CONV_CH = GROUP_W + 2 * SSM_G * SSM_N

DIL_H = 4
DIL_DH = GROUP_W // DIL_H
ROT_DIM = DIL_DH // 4
ROPE_THETA = 500000.0
DIL_BRANCHES = ((128, 1), (512, 4), (2048, 16))

N_EXPERTS = 128
TOP_K = 8
N_EXPERT_GROUPS = 8
TOPK_GROUPS = 4
D_EXPERT = 256
ROUTED_SCALE = 1.0
MOE_BLOCK = 256

ALPHA = (2 * DEPTH) ** 0.25
BETA = (8 * DEPTH) ** -0.25
LN_EPS = 1e-5
RMS_EPS = 1e-6

IN_SPLITS = (GLA_H * GLA_DK, GLA_H * GLA_DK, GROUP_W, GROUP_W, GLA_RANK,
             GROUP_W, GROUP_W,
             GROUP_W, SSM_CONV_CH, SSM_H,
             GROUP_W, GROUP_W, GROUP_W)
N_IN = sum(IN_SPLITS)
IN_OFFSETS = [int(o) for o in np.cumsum(IN_SPLITS)[:-1]]

kernel_name = 'hybrid_parallel_heads_moe_deepnorm'


def _layernorm(x, g, b):
    xf = x.astype(jnp.float32)
    mu = jnp.mean(xf, axis=-1, keepdims=True)
    var = jnp.mean(jnp.square(xf - mu), axis=-1, keepdims=True)
    return ((xf - mu) * lax.rsqrt(var + LN_EPS) * g + b).astype(x.dtype)


def _rmsnorm(x, w):
    xf = x.astype(jnp.float32)
    return (xf * lax.rsqrt(jnp.mean(xf * xf, axis=-1, keepdims=True) + RMS_EPS) * w).astype(x.dtype)


def _gla_mixer(q, k, v, r, lr, w_gate, b_gate, norm_w):
    dt_in = q.dtype
    Bn, S, _ = q.shape
    f32 = jnp.float32
    C = GLA_CHUNK
    nc = S // C
    gk = jax.nn.log_sigmoid((lr @ w_gate + b_gate).astype(f32)) / GLA_TAU

    def chunk(t, d):
        return t.reshape(Bn, nc, C, GLA_H, d).transpose(0, 3, 1, 2, 4).astype(f32)

    qc = chunk(q, GLA_DK) * GLA_DK ** -0.5
    kc = chunk(k, GLA_DK)
    vc = chunk(v, GLA_DV)
    bc = jnp.cumsum(chunk(gk, GLA_DK), axis=3)
    causal = jnp.tril(jnp.ones((C, C), dtype=bool))
    diff = bc[..., :, None, :] - bc[..., None, :, :]
    decay = jnp.exp(jnp.where(causal[:, :, None], diff, -jnp.inf))
    attn = jnp.einsum('bhnid,bhnjd,bhnijd->bhnij', qc, kc, decay)
    o_intra = jnp.einsum('bhnij,bhnjv->bhniv', attn, vc)

    b_last = bc[..., -1:, :]
    q_in = qc * jnp.exp(bc)
    k_in = kc * jnp.exp(b_last - bc)
    d_state = jnp.einsum('bhnjd,bhnjv->nbhdv', k_in, vc)
    chunk_decay = jnp.exp(b_last[..., 0, :]).transpose(2, 0, 1, 3)

    def step(state, inp):
        dec, ds = inp
        return state * dec[..., None] + ds, state

    _, prev = lax.scan(step, jnp.zeros((Bn, GLA_H, GLA_DK, GLA_DV), f32), (chunk_decay, d_state))
    o_inter = jnp.einsum('bhnid,nbhdv->bhniv', q_in, prev)
    o = (o_intra + o_inter).transpose(0, 2, 3, 1, 4).reshape(Bn, S, GLA_H, GLA_DV)
    o = _rmsnorm(o, norm_w).reshape(Bn, S, GROUP_W) * jax.nn.silu(r.astype(f32))
    return o.astype(dt_in)


def _sgu_mixer(u, v, ln_g, ln_b, w_s, b_s):
    Bn, S, _ = u.shape
    nc = S // SGU_CHUNK
    u = jax.nn.gelu(u, approximate=False)
    v = _layernorm(jax.nn.gelu(v, approximate=False), ln_g, ln_b)
    vc = v.reshape(Bn, nc, SGU_CHUNK, SGU_G, SGU_CG)
    w = w_s * jnp.tril(jnp.ones((SGU_CHUNK, SGU_CHUNK), w_s.dtype))
    s = jnp.einsum('gij,bnjgc->bnigc', w, vc) + b_s.T[:, :, None]
    return u * s.reshape(Bn, S, GROUP_W)


def _ssd_chunked(X, dA, Bh, Ch):
    Bn, S, H, P = X.shape
    L = SSM_CHUNK
    nc = S // L
    X = X.reshape(Bn, nc, L, H, P)
    Bh = Bh.reshape(Bn, nc, L, H, SSM_N)
    Ch = Ch.reshape(Bn, nc, L, H, SSM_N)
    A = dA.reshape(Bn, nc, L, H).transpose(0, 3, 1, 2)
    Acs = jnp.cumsum(A, axis=-1)
    tril = jnp.tril(jnp.ones((L, L), dtype=bool))
    Lm = jnp.exp(jnp.where(tril, Acs[..., :, None] - Acs[..., None, :], -jnp.inf))
    y_diag = jnp.einsum('bclhn,bcshn,bhcls,bcshp->bclhp', Ch, Bh, Lm, X)
    decay_states = jnp.exp(Acs[..., -1:] - Acs)
    states = jnp.einsum('bclhn,bhcl,bclhp->bchpn', Bh, decay_states, X)
    cs = jnp.cumsum(jnp.pad(Acs[..., -1], ((0, 0), (0, 0), (1, 0))), axis=-1)
    trc = jnp.tril(jnp.ones((nc + 1, nc + 1), dtype=bool))
    decay_chunk = jnp.exp(jnp.where(trc, cs[..., :, None] - cs[..., None, :], -jnp.inf))
    states = jnp.concatenate([jnp.zeros_like(states[:, :1]), states], axis=1)
    new_states = jnp.einsum('bhzc,bchpn->bzhpn', decay_chunk, states)
    y_off = jnp.einsum('bclhn,bchpn,bhcl->bclhp', Ch, new_states[:, :-1], jnp.exp(Acs))
    return (y_diag + y_off).reshape(Bn, S, H, P)


def _ssd_mixer(z, xbc, dt, conv_w, conv_b, dt_bias, a_log, d_skip, norm_w):
    dt_in = z.dtype
    f32 = jnp.float32
    Bn, S, _ = z.shape
    xbc = lax.conv_general_dilated(xbc, conv_w[:, None, :], window_strides=(1,),
                                   padding=((SSM_CONV - 1, 0),),
                                   dimension_numbers=('NWC', 'WIO', 'NWC'),
                                   feature_group_count=SSM_CONV_CH)
    xbc = jax.nn.silu(xbc + conv_b)
    xs, Bm, Cm = jnp.split(xbc, [GROUP_W, GROUP_W + SSM_G * SSM_N], axis=-1)
    rep = SSM_H // SSM_G
    xs = xs.reshape(Bn, S, SSM_H, SSM_P).astype(f32)
    Bh = jnp.repeat(Bm.reshape(Bn, S, SSM_G, SSM_N), rep, axis=2).astype(f32)
    Ch = jnp.repeat(Cm.reshape(Bn, S, SSM_G, SSM_N), rep, axis=2).astype(f32)
    dt = jax.nn.softplus((dt + dt_bias).astype(f32))
    A = -jnp.exp(a_log.astype(f32))
    y = _ssd_chunked(xs * dt[..., None], dt * A, Bh, Ch)
    y = y + d_skip.astype(f32)[:, None] * xs
    y = y.reshape(Bn, S, GROUP_W) * jax.nn.silu(z.astype(f32))
    y = _rmsnorm(y.reshape(Bn, S, SSM_G, GROUP_W // SSM_G), norm_w.reshape(SSM_G, -1))
    return y.reshape(Bn, S, GROUP_W).astype(dt_in)


def _rope_partial(t, cos, sin):
    half = ROT_DIM // 2
    x1 = t[..., :half]
    x2 = t[..., half:ROT_DIM]
    c = cos[:, :, None, :]
    s = sin[:, :, None, :]
    return jnp.concatenate([x1 * c - x2 * s, x2 * c + x1 * s, t[..., ROT_DIM:]], axis=-1)


def _banded_attention(q, k, v, span):
    N, L, H, dh = q.shape
    nb = L // span
    qb = q.reshape(N, nb, span, H, dh)
    kb = k.reshape(N, nb, span, H, dh)
    vb = v.reshape(N, nb, span, H, dh)
    prev = lambda t: jnp.pad(t, ((0, 0), (1, 0), (0, 0), (0, 0), (0, 0)))[:, :-1]
    kk = jnp.concatenate([prev(kb), kb], axis=2)
    vv = jnp.concatenate([prev(vb), vb], axis=2)
    s = jnp.einsum('nbqhd,nbkhd->nbhqk', qb, kk).astype(jnp.float32) * dh ** -0.5
    i = jnp.arange(span)[:, None]
    j = jnp.arange(2 * span)[None, :]
    band = (j >= i) & (j <= i + span)
    has_prev = (jnp.arange(nb) > 0)[:, None, None] | (j >= span)[None]
    mask = band[None] & has_prev
    s = jnp.where(mask[None, :, None], s, -jnp.inf)
    m = jnp.max(s, axis=-1, keepdims=True)
    p = jnp.exp(s - m)
    den = jnp.sum(p, axis=-1, keepdims=True)
    o = jnp.einsum('nbhqk,nbkhd->nbqhd', p / den, vv.astype(jnp.float32))
    lse = (m + jnp.log(den))[..., 0]
    return o.reshape(N, L, H, dh), lse.transpose(0, 1, 3, 2).reshape(N, L, H)


def _dilated_branch(q, k, v, dil, span):
    Bn, S, H, dh = q.shape
    L = S // dil
    Lp = -(-L // span) * span

    def sub(t):
        t = t.reshape(Bn, L, dil, H, dh).transpose(0, 2, 1, 3, 4).reshape(Bn * dil, L, H, dh)
        return jnp.pad(t, ((0, 0), (0, Lp - L), (0, 0), (0, 0)))

    o, lse = _banded_attention(sub(q), sub(k), sub(v), span)
    o = o[:, :L].reshape(Bn, dil, L, H, dh).transpose(0, 2, 1, 3, 4).reshape(Bn, S, H, dh)
    lse = lse[:, :L].reshape(Bn, dil, L, H).transpose(0, 2, 1, 3).reshape(Bn, S, H)
    return o, lse


def _dilated_mixer(q, k, v, cos, sin):
    dt_in = q.dtype
    Bn, S, _ = q.shape
    q = _rope_partial(q.reshape(Bn, S, DIL_H, DIL_DH), cos, sin)
    k = _rope_partial(k.reshape(Bn, S, DIL_H, DIL_DH), cos, sin)
    v = v.reshape(Bn, S, DIL_H, DIL_DH)
    outs, lses = [], []
    for window, dil in DIL_BRANCHES:
        o, lse = _dilated_branch(q, k, v, dil, window // dil)
        outs.append(o)
        lses.append(lse)
    w = jax.nn.softmax(jnp.stack(lses, axis=0), axis=0)
    o = jnp.einsum('rbsh,rbshd->bshd', w, jnp.stack(outs, axis=0))
    return o.reshape(Bn, S, GROUP_W).astype(dt_in)


def _token_mixer(x, w_in, gla_w_gate, gla_b_gate, gla_norm_w, sgu_ln_g, sgu_ln_b, sgu_w, sgu_b,
                 ssm_conv_w, ssm_conv_b, ssm_dt_bias, ssm_a_log, ssm_d, ssm_norm_w, w_out, cos, sin):
    proj = x @ w_in
    (a_q, a_k, a_v, a_r, a_lr, b_u, b_v, c_z, c_xbc, c_dt, d_q, d_k, d_v) = jnp.split(proj, IN_OFFSETS, axis=-1)
    ya = _gla_mixer(a_q, a_k, a_v, a_r, a_lr, gla_w_gate, gla_b_gate, gla_norm_w)
    yb = _sgu_mixer(b_u, b_v, sgu_ln_g, sgu_ln_b, sgu_w, sgu_b).astype(x.dtype)
    yc = _ssd_mixer(c_z, c_xbc, c_dt, ssm_conv_w, ssm_conv_b, ssm_dt_bias, ssm_a_log, ssm_d, ssm_norm_w)
    yd = _dilated_mixer(d_q, d_k, d_v, cos, sin)
    y = jnp.concatenate([ya, yb, yc, yd], axis=-1).astype(x.dtype)
    return y @ w_out


def _swiglu(h, wg, wu, wd):
    return (jax.nn.silu(h @ wg) * (h @ wu)) @ wd


def _moe(h, router_w, router_bias, w_gate, w_up, w_down, sh_gate, sh_up, sh_down):
    Bn, S, Dm = h.shape
    T = Bn * S
    f32 = jnp.float32
    hf = h.reshape(T, Dm)
    scores = jax.nn.sigmoid((hf @ router_w).astype(f32))
    choice = scores + router_bias.astype(f32)
    per_group = N_EXPERTS // N_EXPERT_GROUPS
    grp_score = jnp.sum(lax.top_k(choice.reshape(T, N_EXPERT_GROUPS, per_group), 2)[0], axis=-1)
    _, gidx = lax.top_k(grp_score, TOPK_GROUPS)
    gmask = jnp.sum(jax.nn.one_hot(gidx, N_EXPERT_GROUPS, dtype=f32), axis=1) > 0
    choice = jnp.where(jnp.repeat(gmask, per_group, axis=1), choice, -jnp.inf)
    _, idx = lax.top_k(choice, TOP_K)
    gate = jnp.take_along_axis(scores, idx, axis=1)
    gate = gate / jnp.sum(gate, axis=-1, keepdims=True) * ROUTED_SCALE

    n_assign = T * TOP_K
    flat_e = idx.reshape(-1)
    order = jnp.argsort(flat_e)
    e_sorted = flat_e[order]
    tok_sorted = (order // TOP_K).astype(jnp.int32)
    g_sorted = gate.reshape(-1)[order]
    counts = jnp.bincount(flat_e, length=N_EXPERTS)
    padded = (counts + MOE_BLOCK - 1) // MOE_BLOCK * MOE_BLOCK
    pad_end = jnp.cumsum(padded)
    pad_start = pad_end - padded
    start = jnp.cumsum(counts) - counts
    dest = pad_start[e_sorted] + jnp.arange(n_assign) - start[e_sorted]
    n_blocks = -(-n_assign // MOE_BLOCK) + N_EXPERTS
    slot_tok = jnp.zeros((n_blocks * MOE_BLOCK,), jnp.int32).at[dest].set(tok_sorted)
    slot_gate = jnp.zeros((n_blocks * MOE_BLOCK,), f32).at[dest].set(g_sorted)
    blk_expert = jnp.minimum(jnp.searchsorted(pad_end, jnp.arange(n_blocks) * MOE_BLOCK, side='right'),
                             N_EXPERTS - 1)

    def body(acc, inp):
        tok, g, e = inp
        y = _swiglu(hf[tok], w_gate[e], w_up[e], w_down[e])
        return acc.at[tok].add((y * g[:, None]).astype(acc.dtype)), None

    routed, _ = lax.scan(body, jnp.zeros_like(hf),
                         (slot_tok.reshape(n_blocks, MOE_BLOCK), slot_gate.reshape(n_blocks, MOE_BLOCK), blk_expert))
    shared = _swiglu(hf, sh_gate, sh_up, sh_down)
    return (routed + shared).reshape(Bn, S, Dm)


def setup_inputs(seed: int = 0) -> dict:
    key = jax.random.key(seed)
    k = jax.random.split(key, 32)
    f32 = jnp.float32
    L = DEPTH

    def nrm(kk, shape, scale):
        return jax.random.normal(kk, shape, f32) * scale

    dt0 = jnp.exp(jax.random.uniform(k[10], (L, SSM_H), f32, math.log(1e-3), math.log(1e-1)))
    return {
        'x': nrm(k[0], (BATCH, SEQ, D_MODEL), 1.0),
        'positions': jnp.arange(SEQ, dtype=jnp.int32)[None, :]
                     + jax.random.randint(k[1], (BATCH, 1), 0, 4096, dtype=jnp.int32),
        'w_in': nrm(k[2], (L, D_MODEL, N_IN), D_MODEL ** -0.5),
        'gla_w_gate': nrm(k[3], (L, GLA_RANK, GLA_H * GLA_DK), GLA_RANK ** -0.5),
        'gla_b_gate': nrm(k[4], (L, GLA_H * GLA_DK), 0.1),
        'gla_norm_w': 1.0 + nrm(k[5], (L, GLA_DV), 0.02),
        'sgu_ln_g': 1.0 + nrm(k[6], (L, GROUP_W), 0.02),
        'sgu_ln_b': nrm(k[7], (L, GROUP_W), 0.02),
        'sgu_w': nrm(k[8], (L, SGU_G, SGU_CHUNK, SGU_CHUNK), SGU_CHUNK ** -0.5),
        'sgu_b': 1.0 + nrm(k[9], (L, SGU_G, SGU_CHUNK), 0.02),
        'ssm_conv_w': nrm(k[11], (L, SSM_CONV, SSM_CONV_CH), SSM_CONV ** -0.5),
        'ssm_conv_b': nrm(k[12], (L, SSM_CONV_CH), 0.02),
        'ssm_dt_bias': dt0 + jnp.log(-jnp.expm1(-dt0)),
        'ssm_a_log': jnp.log(jax.random.uniform(k[13], (L, SSM_H), f32, 1.0, 16.0)),
        'ssm_d': 1.0 + nrm(k[14], (L, SSM_H), 0.02),
        'ssm_norm_w': 1.0 + nrm(k[15], (L, GROUP_W), 0.02),
        'w_out': nrm(k[16], (L, D_MODEL, D_MODEL), D_MODEL ** -0.5 * BETA),
        'ln1_g': 1.0 + nrm(k[17], (L, D_MODEL), 0.02),
        'ln1_b': nrm(k[18], (L, D_MODEL), 0.02),
        'router_w': nrm(k[19], (L, D_MODEL, N_EXPERTS), D_MODEL ** -0.5),
        'router_bias': nrm(k[20], (L, N_EXPERTS), 0.01),
        'exp_w_gate': nrm(k[21], (L, N_EXPERTS, D_MODEL, D_EXPERT), D_MODEL ** -0.5),
        'exp_w_up': nrm(k[22], (L, N_EXPERTS, D_MODEL, D_EXPERT), D_MODEL ** -0.5),
        'exp_w_down': nrm(k[23], (L, N_EXPERTS, D_EXPERT, D_MODEL), D_EXPERT ** -0.5 * BETA),
        'sh_w_gate': nrm(k[24], (L, D_MODEL, D_EXPERT), D_MODEL ** -0.5),
        'sh_w_up': nrm(k[25], (L, D_MODEL, D_EXPERT), D_MODEL ** -0.5),
        'sh_w_down': nrm(k[26], (L, D_EXPERT, D_MODEL), D_EXPERT ** -0.5 * BETA),
        'ln2_g': 1.0 + nrm(k[27], (L, D_MODEL), 0.02),
        'ln2_b': nrm(k[28], (L, D_MODEL), 0.02),
    }


def reference(x, positions, w_in, gla_w_gate, gla_b_gate, gla_norm_w, sgu_ln_g, sgu_ln_b, sgu_w, sgu_b,
              ssm_conv_w, ssm_conv_b, ssm_dt_bias, ssm_a_log, ssm_d, ssm_norm_w, w_out, ln1_g, ln1_b,
              router_w, router_bias, exp_w_gate, exp_w_up, exp_w_down, sh_w_gate, sh_w_up, sh_w_down,
              ln2_g, ln2_b):
    inv_freq = ROPE_THETA ** (-jnp.arange(0, ROT_DIM, 2, dtype=jnp.float32) / ROT_DIM)
    ang = positions.astype(jnp.float32)[..., None] * inv_freq
    cos, sin = jnp.cos(ang), jnp.sin(ang)
    for l in range(DEPTH):
        y = _token_mixer(x, w_in[l], gla_w_gate[l], gla_b_gate[l], gla_norm_w[l], sgu_ln_g[l], sgu_ln_b[l],
                         sgu_w[l], sgu_b[l], ssm_conv_w[l], ssm_conv_b[l], ssm_dt_bias[l], ssm_a_log[l],
                         ssm_d[l], ssm_norm_w[l], w_out[l], cos, sin)
        x = _layernorm(ALPHA * x + y, ln1_g[l], ln1_b[l])
        y = _moe(x, router_w[l], router_bias[l], exp_w_gate[l], exp_w_up[l], exp_w_down[l],
                 sh_w_gate[l], sh_w_up[l], sh_w_down[l])
        x = _layernorm(ALPHA * x + y, ln2_g[l], ln2_b[l])
    return x
```

```python
import functools
import math

import numpy as np
import jax
import jax.numpy as jnp
from jax import lax
from jax.experimental import pallas as pl
from jax.experimental.pallas import tpu as pltpu

F32 = jnp.float32
BF16 = jnp.bfloat16
I32 = jnp.int32

D_MODEL = 1024
N_LAYERS = 4
GROUP_W = 256

GLA_H, GLA_DK, GLA_DV, GLA_RANK, GLA_TAU = 4, 32, 64, 16, 16.0
SGU_G, SGU_CHUNK = 4, 128
SSM_H, SSM_P, SSM_G, SSM_N, SSM_CONV, SSM_CHUNK = 4, 64, 2, 64, 4, 128
DIL_H, DIL_DH, ROT_DIM, ROPE_THETA = 4, 64, 16, 500000.0
DIL_BRANCHES = ((128, 1), (512, 4), (2048, 16))

N_EXPERTS, TOP_K, N_EXPERT_GROUPS, TOPK_GROUPS, D_EXPERT = 128, 8, 8, 4, 256
ROUTED_SCALE = 1.0

ALPHA = (2 * N_LAYERS) ** 0.25
LN_EPS = 1e-5
RMS_EPS = 1e-6

W_GLA = 896
W_SGU = 512
W_SSD = 1280
W_DIL = 768
W_PROJ = W_GLA + W_SGU + W_SSD + W_DIL

BLK = 128
QBLK = 256
MOE_BLK = 256
TOK_BLK = 256
CMB_BLK = 128
NEG = -1e30

VMEM_LIMIT = 56 * 1024 * 1024


def _cparams(sem):
    return pltpu.CompilerParams(dimension_semantics=sem, vmem_limit_bytes=VMEM_LIMIT)


def _dot(a, b):
    return jnp.dot(a.astype(BF16), b.astype(BF16), preferred_element_type=F32)


def _dot_nt(a, b):
    return lax.dot_general(a.astype(BF16), b.astype(BF16), (((1,), (1,)), ((), ())),
                           preferred_element_type=F32)


def _dot_tn(a, b):
    return lax.dot_general(a.astype(BF16), b.astype(BF16), (((0,), (0,)), ((), ())),
                           preferred_element_type=F32)


def _split(a, n):
    out = []
    r = a
    for _ in range(n):
        p = r.astype(BF16)
        out.append(p)
        r = r - p.astype(F32)
    return out


def _dot_hi(a, b):
    a1, a2 = _split(a, 2)
    b1, b2 = _split(b, 2)
    return _dot(a1, b1) + (_dot(a1, b2) + _dot(a2, b1))


def _dot_nt_hi(a, b):
    a1, a2 = _split(a, 2)
    b1, b2 = _split(b, 2)
    return _dot_nt(a1, b1) + (_dot_nt(a1, b2) + _dot_nt(a2, b1))


def _dot_lhs01(m, b):
    b1, b2, b3 = _split(b, 3)
    return _dot(m, b1) + (_dot(m, b2) + _dot(m, b3))


def _dot_rhs01(a, m):
    a1, a2, a3 = _split(a, 3)
    return _dot(a1, m) + (_dot(a2, m) + _dot(a3, m))


def _iota(shape, dim):
    return lax.broadcasted_iota(I32, shape, dim)


def _sigmoid(x):
    return 1.0 / (1.0 + jnp.exp(-x))


def _silu(x):
    return x * _sigmoid(x)


def _softplus(x):
    return jnp.maximum(x, 0.0) + jnp.log1p(jnp.exp(-jnp.abs(x)))


def _gelu(x):
    return 0.5 * x * (1.0 + lax.erf(x * (2.0 ** -0.5)))


def _inproj_kernel(x_ref, w_ref, o_gla, o_sgu, o_ssd, o_dil):
    xb = x_ref[...].astype(BF16)
    off = 0
    for o in (o_gla, o_sgu, o_ssd, o_dil):
        n = o.shape[1]
        o[...] = jnp.dot(xb, w_ref[:, off:off + n], preferred_element_type=F32)
        off += n


def _inproj(x2, w_p):
    T = x2.shape[0]
    tm = 512
    widths = (W_GLA, W_SGU, W_SSD, W_DIL)
    return pl.pallas_call(
        _inproj_kernel,
        grid=(T // tm,),
        in_specs=[pl.BlockSpec((tm, D_MODEL), lambda i: (i, 0)),
                  pl.BlockSpec((D_MODEL, W_PROJ), lambda i: (0, 0))],
        out_specs=[pl.BlockSpec((tm, w), lambda i: (i, 0)) for w in widths],
        out_shape=[jax.ShapeDtypeStruct((T, w), F32) for w in widths],
        compiler_params=_cparams(("arbitrary",)),
        name="inproj",
    )(x2, w_p)


def _gla_kernel(slab_ref, wg_ref, bg_ref, nw_ref, out_ref, st_ref):
    S = slab_ref.shape[1]
    st_ref[...] = jnp.zeros_like(st_ref)
    row = _iota((BLK, BLK), 0)
    col = _iota((BLK, BLK), 1)
    tril = col <= row
    tril_bf = jnp.where(tril, 1.0, 0.0).astype(BF16)
    qk_head = col >> 5
    v_head = _iota((BLK, GROUP_W), 1) >> 6
    st_diag = (_iota((GROUP_W, BLK), 0) >> 6) == (_iota((GROUP_W, BLK), 1) >> 5)
    seg = jnp.where((_iota((GROUP_W, GROUP_W), 0) >> 6) == (_iota((GROUP_W, GROUP_W), 1) >> 6),
                    1.0, 0.0).astype(BF16)
    wg = wg_ref[...]
    bg = bg_ref[...]
    nw = nw_ref[...]

    def body(n, carry):
        r0 = pl.multiple_of(n * BLK, BLK)
        blk = slab_ref[0, pl.ds(r0, BLK), :]
        q = blk[:, 0:128] * (GLA_DK ** -0.5)
        k = blk[:, 128:256]
        v = blk[:, 256:512]
        r = blk[:, 512:768]
        lr = blk[:, 768:896]
        z = _dot_hi(lr, wg) + bg
        gk = (jnp.minimum(z, 0.0) - jnp.log1p(jnp.exp(-jnp.abs(z)))) * (1.0 / GLA_TAU)
        g = _dot_lhs01(tril_bf, gk)
        g_last = g[BLK - 1:BLK, :]
        q_in = q * jnp.exp(g)
        k_out = k * jnp.exp(-g)
        k_in = k * jnp.exp(g_last - g)
        vb = v.astype(BF16)
        o = _dot_nt(q_in, st_ref[...])
        kob = k_out.astype(BF16)
        for h in range(GLA_H):
            qh = jnp.where(qk_head == h, q_in, 0.0)
            a = jnp.where(tril, _dot_nt(qh, kob), 0.0)
            o = o + jnp.where(v_head == h, _dot(a, vb), 0.0)
        st_ref[...] = st_ref[...] * jnp.exp(g_last) + jnp.where(st_diag, _dot_tn(vb, k_in), 0.0)
        ms = _dot_rhs01(o * o, seg) * (1.0 / GLA_DV)
        y = o * lax.rsqrt(ms + RMS_EPS) * nw * _silu(r)
        out_ref[0, pl.ds(r0, BLK), :] = y.astype(out_ref.dtype)
        return carry

    lax.fori_loop(0, S // BLK, body, 0)


def _gla(slab, wg_p, bg, nw_t):
    B, S, _ = slab.shape
    return pl.pallas_call(
        _gla_kernel,
        grid=(B,),
        in_specs=[pl.BlockSpec((1, S, W_GLA), lambda b: (b, 0, 0)),
                  pl.BlockSpec((128, 128), lambda b: (0, 0)),
                  pl.BlockSpec((1, 128), lambda b: (0, 0)),
                  pl.BlockSpec((1, GROUP_W), lambda b: (0, 0))],
        out_specs=pl.BlockSpec((1, S, GROUP_W), lambda b: (b, 0, 0)),
        out_shape=jax.ShapeDtypeStruct((B, S, GROUP_W), BF16),
        scratch_shapes=[pltpu.VMEM((GROUP_W, 128), F32)],
        compiler_params=_cparams(("arbitrary",)),
        name="gla",
    )(slab, wg_p, bg, nw_t)


def _sgu_kernel(slab_ref, w_ref, bias_ref, g_ref, b_ref, out_ref):
    S = slab_ref.shape[1]
    wmask = (_iota((BLK, SGU_G * BLK), 1) & (BLK - 1)) <= _iota((BLK, SGU_G * BLK), 0)
    w = jnp.where(wmask, w_ref[...], 0.0).astype(BF16)
    lane_grp = _iota((BLK, GROUP_W), 1) >> 6
    bias = bias_ref[...]
    ln_g = g_ref[...]
    ln_b = b_ref[...]

    def body(n, carry):
        r0 = pl.multiple_of(n * BLK, BLK)
        blk = slab_ref[0, pl.ds(r0, BLK), :]
        u = _gelu(blk[:, 0:256])
        v = _gelu(blk[:, 256:512])
        mu = jnp.mean(v, axis=-1, keepdims=True)
        var = jnp.mean(jnp.square(v - mu), axis=-1, keepdims=True)
        v = (v - mu) * lax.rsqrt(var + LN_EPS) * ln_g + ln_b
        vexp = jnp.concatenate([jnp.where(lane_grp == g, v, 0.0) for g in range(SGU_G)], axis=0)
        s = _dot(w, vexp) + bias
        out_ref[0, pl.ds(r0, BLK), :] = (u * s).astype(out_ref.dtype)
        return carry

    lax.fori_loop(0, S // BLK, body, 0)


def _sgu(slab, w_cat, bias_t, ln_g, ln_b):
    B, S, _ = slab.shape
    return pl.pallas_call(
        _sgu_kernel,
        grid=(B,),
        in_specs=[pl.BlockSpec((1, S, W_SGU), lambda b: (b, 0, 0)),
                  pl.BlockSpec((BLK, SGU_G * BLK), lambda b: (0, 0)),
                  pl.BlockSpec((BLK, GROUP_W), lambda b: (0, 0)),
                  pl.BlockSpec((1, GROUP_W), lambda b: (0, 0)),
                  pl.BlockSpec((1, GROUP_W), lambda b: (0, 0))],
        out_specs=pl.BlockSpec((1, S, GROUP_W), lambda b: (b, 0, 0)),
        out_shape=jax.ShapeDtypeStruct((B, S, GROUP_W), BF16),
        compiler_params=_cparams(("arbitrary",)),
        name="sgu",
    )(slab, w_cat, bias_t, ln_g, ln_b)


def _ssd_kernel(slab_ref, cw_ref, cb_ref, dtb_ref, a_ref, d_ref, nw_ref, out_ref, xpad_ref, st_ref):
    S = slab_ref.shape[1]
    CW = 3 * GROUP_W
    st_ref[...] = jnp.zeros_like(st_ref)
    xpad_ref[0:8, :] = jnp.zeros((8, CW), F32)
    xpad_ref[8:S + 8, :] = slab_ref[0, :, 256:256 + CW]
    row = _iota((BLK, BLK), 0)
    col = _iota((BLK, BLK), 1)
    tril = col <= row
    tril_bf = jnp.where(tril, 1.0, 0.0).astype(BF16)
    lane_head = _iota((BLK, GROUP_W), 1) >> 6
    st_diag = (_iota((GROUP_W, GROUP_W), 0) >> 6) == (_iota((GROUP_W, GROUP_W), 1) >> 6)
    cw = cw_ref[...]
    cb = cb_ref[...]
    dtb = dtb_ref[...]
    a_neg = -jnp.exp(a_ref[...])
    d_skip = d_ref[...]
    nw = nw_ref[...]

    def body(n, carry):
        r0 = pl.multiple_of(n * BLK, BLK)
        xw = xpad_ref[pl.ds(r0, BLK + 8), :]
        conv = cb + cw[0:1, :] * xw[5:5 + BLK, :]
        for w in range(1, SSM_CONV):
            conv = conv + cw[w:w + 1, :] * xw[5 + w:5 + w + BLK, :]
        xbc = _silu(conv)
        xs = xbc[:, 0:256]
        b_e = xbc[:, 256:512]
        c_e = xbc[:, 512:768]
        z = slab_ref[0, pl.ds(r0, BLK), 0:256]
        dt = _softplus(slab_ref[0, pl.ds(r0, BLK), 1024:1280] + dtb)
        acs = _dot_lhs01(tril_bf, dt * a_neg)
        acs_last = acs[BLK - 1:BLK, :]
        x_dt = (xs * dt).astype(BF16)
        acs_t = [jnp.transpose(acs[:, 0:128]), jnp.transpose(acs[:, 128:256])]
        y = _dot(c_e * jnp.exp(acs), st_ref[...])
        b_bf = b_e.astype(BF16)
        for h in range(SSM_H):
            a_col = jnp.broadcast_to(acs[:, 64 * h:64 * h + 1], (BLK, BLK))
            a_row = jnp.broadcast_to(acs_t[h // 2][64 * (h % 2):64 * (h % 2) + 1, :], (BLK, BLK))
            lm = jnp.exp(jnp.where(tril, a_col - a_row, -jnp.inf))
            ch = jnp.where(lane_head == h, c_e, 0.0)
            m = _dot_nt(ch, b_bf) * lm
            y = y + jnp.where(lane_head == h, _dot(m, x_dt), 0.0)
        new = _dot_tn(b_e * jnp.exp(acs_last - acs), x_dt)
        st_ref[...] = st_ref[...] * jnp.exp(acs_last) + jnp.where(st_diag, new, 0.0)
        y = (y + d_skip * xs) * _silu(z)
        parts = []
        for g in range(SSM_G):
            yg = y[:, 128 * g:128 * (g + 1)]
            ms = jnp.mean(yg * yg, axis=-1, keepdims=True)
            parts.append(yg * lax.rsqrt(ms + RMS_EPS))
        out = jnp.concatenate(parts, axis=1) * nw
        out_ref[0, pl.ds(r0, BLK), :] = out.astype(out_ref.dtype)
        return carry

    lax.fori_loop(0, S // BLK, body, 0)


def _ssd(slab, cw_e, cb_e, dtb_e, a_e, d_e, nw):
    B, S, _ = slab.shape
    CW = 3 * GROUP_W
    vec = lambda w: pl.BlockSpec((1, w), lambda b: (0, 0))
    return pl.pallas_call(
        _ssd_kernel,
        grid=(B,),
        in_specs=[pl.BlockSpec((1, S, W_SSD), lambda b: (b, 0, 0)),
                  pl.BlockSpec((SSM_CONV, CW), lambda b: (0, 0)),
                  vec(CW), vec(GROUP_W), vec(GROUP_W), vec(GROUP_W), vec(GROUP_W)],
        out_specs=pl.BlockSpec((1, S, GROUP_W), lambda b: (b, 0, 0)),
        out_shape=jax.ShapeDtypeStruct((B, S, GROUP_W), BF16),
        scratch_shapes=[pltpu.VMEM((S + 8, CW), F32), pltpu.VMEM((GROUP_W, GROUP_W), F32)],
        compiler_params=_cparams(("arbitrary",)),
        name="ssd",
    )(slab, cw_e, cb_e, dtb_e, a_e, d_e, nw)


def _branch_log_multiplicity(S):
    nq = S // QBLK
    r = np.arange(QBLK)[:, None]
    u = np.arange(S)[None, :]
    d = (nq - 1) * QBLK + r - u
    c = np.zeros(d.shape, np.float64)
    for window, dil in DIL_BRANCHES:
        c += (d >= 0) & (d % dil == 0) & (d <= window)
    with np.errstate(divide="ignore"):
        return np.where(c > 0, np.log(np.maximum(c, 1.0)), NEG).astype(np.float32)


def _dil_kernel(slab_ref, cos_ref, sin_ref, tb_ref, out_ref, q_ref, k_ref, v_ref):
    S = slab_ref.shape[1]
    nq = S // QBLK
    lane_d = _iota((1, 128), 1) & (DIL_DH - 1)

    def rope(t):
        fwd = pltpu.roll(t, 128 - ROT_DIM // 2, 1)
        bwd = pltpu.roll(t, ROT_DIM // 2, 1)
        return t * cos_ref[0] + jnp.where(lane_d < ROT_DIM // 2, fwd, bwd) * sin_ref[0]

    for c in range(0, GROUP_W, 128):
        q_ref[:, c:c + 128] = (rope(slab_ref[0, :, c:c + 128]) * (DIL_DH ** -0.5)).astype(BF16)
        k_ref[:, c:c + 128] = rope(slab_ref[0, :, 256 + c:384 + c]).astype(BF16)
    v_ref[...] = slab_ref[0, :, 512:768].astype(BF16)
    lane_head = _iota((QBLK, GROUP_W), 1) >> 6
    for i in range(nq):
        nk = (i + 1) * QBLK
        qi = q_ref[i * QBLK:(i + 1) * QBLK, :]
        acc = jnp.zeros((QBLK, GROUP_W), F32)
        for h in range(DIL_H):
            qh = jnp.where(lane_head == h, qi, jnp.zeros_like(qi))
            s = _dot_nt(qh, k_ref[0:nk, :]) + tb_ref[:, (nq - 1 - i) * QBLK:]
            m = jnp.max(s, axis=-1, keepdims=True)
            p = jnp.exp(s - m)
            den = jnp.sum(p, axis=-1, keepdims=True)
            o = _dot(p, v_ref[0:nk, :])
            acc = acc + jnp.where(lane_head == h, o / den, 0.0)
        out_ref[0, i * QBLK:(i + 1) * QBLK, :] = acc.astype(out_ref.dtype)


def _dil(slab, cos_t, sin_t, tb):
    B, S, _ = slab.shape
    return pl.pallas_call(
        _dil_kernel,
        grid=(B,),
        in_specs=[pl.BlockSpec((1, S, W_DIL), lambda b: (b, 0, 0)),
                  pl.BlockSpec((1, S, 128), lambda b: (b, 0, 0)),
                  pl.BlockSpec((1, S, 128), lambda b: (b, 0, 0)),
                  pl.BlockSpec((QBLK, S), lambda b: (0, 0), pipeline_mode=pl.Buffered(1))],
        out_specs=pl.BlockSpec((1, S, GROUP_W), lambda b: (b, 0, 0)),
        out_shape=jax.ShapeDtypeStruct((B, S, GROUP_W), BF16),
        scratch_shapes=[pltpu.VMEM((S, GROUP_W), BF16)] * 3,
        compiler_params=_cparams(("arbitrary",)),
        name="dil",
    )(slab, cos_t, sin_t, tb)


def _layernorm(x, g, b):
    mu = jnp.mean(x, axis=-1, keepdims=True)
    var = jnp.mean(jnp.square(x - mu), axis=-1, keepdims=True)
    return (x - mu) * lax.rsqrt(var + LN_EPS) * g + b


def _outproj_kernel(ya, yb, yc, yd, x_ref, w_ref, g_ref, b_ref, o_ref):
    y = jnp.concatenate([ya[...], yb[...], yc[...], yd[...]], axis=1)
    y = jnp.dot(y, w_ref[...], preferred_element_type=F32)
    o_ref[...] = _layernorm(ALPHA * x_ref[...] + y, g_ref[...], b_ref[...])


def _outproj(ya, yb, yc, yd, x2, w_out, g, b):
    T = x2.shape[0]
    tm = 512
    ysp = pl.BlockSpec((tm, GROUP_W), lambda i: (i, 0))
    vec = pl.BlockSpec((1, D_MODEL), lambda i: (0, 0))
    return pl.pallas_call(
        _outproj_kernel,
        grid=(T // tm,),
        in_specs=[ysp, ysp, ysp, ysp,
                  pl.BlockSpec((tm, D_MODEL), lambda i: (i, 0)),
                  pl.BlockSpec((D_MODEL, D_MODEL), lambda i: (0, 0)), vec, vec],
        out_specs=pl.BlockSpec((tm, D_MODEL), lambda i: (i, 0)),
        out_shape=jax.ShapeDtypeStruct((T, D_MODEL), F32),
        compiler_params=_cparams(("arbitrary",)),
        name="outproj_ln",
    )(ya, yb, yc, yd, x2, w_out, g, b)


def _router_kernel(x_ref, rw_ref, rb_ref, idx_ref, gate_ref, pos_ref, cnt_ref, carry_ref):
    TB = x_ref.shape[0]
    E = N_EXPERTS
    per = E // N_EXPERT_GROUPS

    @pl.when(pl.program_id(0) == 0)
    def _():
        carry_ref[...] = jnp.zeros_like(carry_ref)

    logits = _dot_nt_hi(rw_ref[...], x_ref[...])
    scores = _sigmoid(logits)
    choice = scores + rb_ref[...]
    e_iota = _iota((E, TB), 0)

    l_iota = _iota((per, TB), 0)
    grp = []
    for g in range(N_EXPERT_GROUPS):
        cg = choice[per * g:per * (g + 1), :]
        m1 = jnp.max(cg, axis=0, keepdims=True)
        i1 = jnp.min(jnp.where(cg == m1, l_iota, per), axis=0, keepdims=True)
        m2 = jnp.max(jnp.where(l_iota == i1, -jnp.inf, cg), axis=0, keepdims=True)
        grp.append(m1 + m2)
    gs = jnp.concatenate(grp, axis=0)
    g_iota = _iota((N_EXPERT_GROUPS, TB), 0)
    keep = jnp.zeros((N_EXPERT_GROUPS, TB), jnp.bool_)
    for _ in range(TOPK_GROUPS):
        m = jnp.max(gs, axis=0, keepdims=True)
        gi = jnp.min(jnp.where(gs == m, g_iota, N_EXPERT_GROUPS), axis=0, keepdims=True)
        hit = g_iota == gi
        keep = keep | hit
        gs = jnp.where(hit, -jnp.inf, gs)
    keep_f = jnp.where(keep, 1.0, 0.0)
    keep_e = jnp.concatenate([jnp.broadcast_to(keep_f[g:g + 1, :], (per, TB))
                              for g in range(N_EXPERT_GROUPS)], axis=0)
    cm = jnp.where(keep_e > 0.5, choice, -jnp.inf)

    hits, idxs, gates = [], [], []
    onehot = jnp.zeros((E, TB), F32)
    for _ in range(TOP_K):
        m = jnp.max(cm, axis=0, keepdims=True)
        ei = jnp.min(jnp.where(cm == m, e_iota, E), axis=0, keepdims=True)
        hit = e_iota == ei
        hits.append(hit)
        idxs.append(ei)
        gates.append(jnp.sum(jnp.where(hit, scores, 0.0), axis=0, keepdims=True))
        onehot = onehot + jnp.where(hit, 1.0, 0.0)
        cm = jnp.where(hit, -jnp.inf, cm)
    gate = jnp.concatenate(gates, axis=0)
    gate = gate / jnp.sum(gate, axis=0, keepdims=True) * ROUTED_SCALE

    before = jnp.where(_iota((TB, TB), 0) < _iota((TB, TB), 1), 1.0, 0.0).astype(BF16)
    cnt = carry_ref[...] + _dot(onehot, before)
    pos = jnp.concatenate([jnp.sum(jnp.where(h, cnt, 0.0), axis=0, keepdims=True) for h in hits], axis=0)
    carry_ref[...] = carry_ref[...] + jnp.sum(onehot, axis=1, keepdims=True)

    idx_ref[...] = jnp.concatenate(idxs, axis=0)
    gate_ref[...] = gate
    pos_ref[...] = pos.astype(I32)
    cnt_ref[...] = jnp.broadcast_to(carry_ref[...], cnt_ref.shape)


def _router(x1, rw_t, rb):
    T = x1.shape[0]
    TB = TOK_BLK
    kt = pl.BlockSpec((TOP_K, TB), lambda i: (0, i))
    return pl.pallas_call(
        _router_kernel,
        grid=(T // TB,),
        in_specs=[pl.BlockSpec((TB, D_MODEL), lambda i: (i, 0)),
                  pl.BlockSpec((N_EXPERTS, D_MODEL), lambda i: (0, 0)),
                  pl.BlockSpec((N_EXPERTS, 1), lambda i: (0, 0))],
        out_specs=[kt, kt, kt, pl.BlockSpec((N_EXPERTS, 128), lambda i: (0, 0))],
        out_shape=[jax.ShapeDtypeStruct((TOP_K, T), I32), jax.ShapeDtypeStruct((TOP_K, T), F32),
                   jax.ShapeDtypeStruct((TOP_K, T), I32), jax.ShapeDtypeStruct((N_EXPERTS, 128), F32)],
        scratch_shapes=[pltpu.VMEM((N_EXPERTS, 1), F32)],
        compiler_params=_cparams(("arbitrary",)),
        name="router",
    )(x1, rw_t, rb)


def _dispatch_kernel(dest_ref, x_ref, xs_in, xs_out, sem):
    del xs_in
    TB = x_ref.shape[0]

    def issue(t, c):
        for k in range(TOP_K):
            d = dest_ref[k, t]
            pltpu.make_async_copy(x_ref.at[pl.ds(t, 1), :], xs_out.at[pl.ds(d, 1), :], sem).start()
        return c

    lax.fori_loop(0, TB, issue, 0)

    def drain(t, c):
        for k in range(TOP_K):
            pltpu.make_async_copy(x_ref.at[pl.ds(0, 1), :], xs_out.at[pl.ds(0, 1), :], sem).wait()
        return c

    lax.fori_loop(0, TB, drain, 0)


def _dispatch(dest, x1, xs0):
    T = x1.shape[0]
    TB = TOK_BLK
    return pl.pallas_call(
        _dispatch_kernel,
        grid=(T // TB,),
        in_specs=[pl.BlockSpec((TOP_K, TB), lambda i: (0, i), memory_space=pltpu.SMEM),
                  pl.BlockSpec((TB, D_MODEL), lambda i: (i, 0)),
                  pl.BlockSpec(memory_space=pl.ANY)],
        out_specs=pl.BlockSpec(memory_space=pl.ANY),
        out_shape=jax.ShapeDtypeStruct(xs0.shape, xs0.dtype),
        scratch_shapes=[pltpu.SemaphoreType.DMA],
        input_output_aliases={2: 0},
        compiler_params=_cparams(("arbitrary",)),
        name="dispatch",
    )(dest, x1, xs0)


def _experts_kernel(be_ref, nu_ref, xs_ref, wg_ref, wu_ref, wd_ref, ys_ref, wgu_s, wd_s):
    i = pl.program_id(0)

    @pl.when(i < nu_ref[0])
    def _():
        prev = be_ref[jnp.maximum(i - 1, 0)]

        @pl.when((i == 0) | (be_ref[i] != prev))
        def _():
            wgu_s[:, 0:D_EXPERT] = wg_ref[0].astype(BF16)
            wgu_s[:, D_EXPERT:] = wu_ref[0].astype(BF16)
            wd_s[...] = wd_ref[0].astype(BF16)

        h = jnp.dot(xs_ref[...].astype(BF16), wgu_s[...], preferred_element_type=F32)
        a = _silu(h[:, 0:D_EXPERT]) * h[:, D_EXPERT:]
        ys_ref[...] = jnp.dot(a.astype(BF16), wd_s[...], preferred_element_type=F32)


def _experts(blk_expert, n_used, xs, wg, wu, wd):
    nblk = xs.shape[0] // MOE_BLK
    row = lambda i, be, nu: (jnp.minimum(i, nu[0] - 1), 0)
    wsel = lambda i, be, nu: (be[i], 0, 0)
    return pl.pallas_call(
        _experts_kernel,
        grid_spec=pltpu.PrefetchScalarGridSpec(
            num_scalar_prefetch=2,
            grid=(nblk,),
            in_specs=[pl.BlockSpec((MOE_BLK, D_MODEL), row),
                      pl.BlockSpec((1, D_MODEL, D_EXPERT), wsel),
                      pl.BlockSpec((1, D_MODEL, D_EXPERT), wsel),
                      pl.BlockSpec((1, D_EXPERT, D_MODEL), wsel)],
            out_specs=pl.BlockSpec((MOE_BLK, D_MODEL), row),
            scratch_shapes=[pltpu.VMEM((D_MODEL, 2 * D_EXPERT), BF16),
                            pltpu.VMEM((D_EXPERT, D_MODEL), BF16)]),
        out_shape=jax.ShapeDtypeStruct(xs.shape, F32),
        compiler_params=_cparams(("arbitrary",)),
        name="experts",
    )(blk_expert, n_used, xs, wg, wu, wd)


def _combine_kernel(dest_ref, gate_ref, x_ref, ys_ref, sgu_ref, sd_ref, g_ref, b_ref, o_ref, buf, sem):
    TB = x_ref.shape[0]

    def issue(t, c):
        for k in range(TOP_K):
            d = dest_ref[k, t]
            pltpu.make_async_copy(ys_ref.at[pl.ds(d, 1), :], buf.at[k, pl.ds(t, 1), :], sem).start()
        return c

    lax.fori_loop(0, TB, issue, 0)

    x = x_ref[...]
    h = jnp.dot(x.astype(BF16), sgu_ref[...], preferred_element_type=F32)
    a = _silu(h[:, 0:D_EXPERT]) * h[:, D_EXPERT:]
    y = ALPHA * x + jnp.dot(a.astype(BF16), sd_ref[...], preferred_element_type=F32)

    def drain(t, c):
        for k in range(TOP_K):
            pltpu.make_async_copy(ys_ref.at[pl.ds(0, 1), :], buf.at[0, pl.ds(0, 1), :], sem).wait()
        return c

    lax.fori_loop(0, TB, drain, 0)

    gate = gate_ref[...]
    for k in range(TOP_K):
        y = y + gate[:, k:k + 1] * buf[k]
    o_ref[...] = _layernorm(y, g_ref[...], b_ref[...])


def _combine(dest, gate_t, x1, ys, sh_gu, sh_d, g, b):
    T = x1.shape[0]
    TB = CMB_BLK
    vec = pl.BlockSpec((1, D_MODEL), lambda i: (0, 0))
    return pl.pallas_call(
        _combine_kernel,
        grid=(T // TB,),
        in_specs=[pl.BlockSpec((TOP_K, TB), lambda i: (0, i), memory_space=pltpu.SMEM),
                  pl.BlockSpec((TB, TOP_K), lambda i: (i, 0)),
                  pl.BlockSpec((TB, D_MODEL), lambda i: (i, 0)),
                  pl.BlockSpec(memory_space=pl.ANY),
                  pl.BlockSpec((D_MODEL, 2 * D_EXPERT), lambda i: (0, 0)),
                  pl.BlockSpec((D_EXPERT, D_MODEL), lambda i: (0, 0)), vec, vec],
        out_specs=pl.BlockSpec((TB, D_MODEL), lambda i: (i, 0)),
        out_shape=jax.ShapeDtypeStruct((T, D_MODEL), F32),
        scratch_shapes=[pltpu.VMEM((TOP_K, TB, D_MODEL), F32), pltpu.SemaphoreType.DMA],
        compiler_params=_cparams(("arbitrary",)),
        name="combine_ln",
    )(dest, gate_t, x1, ys, sh_gu, sh_d, g, b)


def _token_mixer(x2, B, S, w_p, gla_p, sgu_p, ssd_p, cos_t, sin_t, tb, w_out, ln_g, ln_b):
    s_gla, s_sgu, s_ssd, s_dil = _inproj(x2, w_p)
    ya = _gla(s_gla.reshape(B, S, W_GLA), *gla_p)
    yb = _sgu(s_sgu.reshape(B, S, W_SGU), *sgu_p)
    yc = _ssd(s_ssd.reshape(B, S, W_SSD), *ssd_p)
    yd = _dil(s_dil.reshape(B, S, W_DIL), cos_t, sin_t, tb)
    T = B * S
    flat = lambda y: y.reshape(T, GROUP_W)
    return _outproj(flat(ya), flat(yb), flat(yc), flat(yd), x2, w_out, ln_g, ln_b)


def _moe(x1, rw_t, rb, wg, wu, wd, sh_gu, sh_d, ln_g, ln_b):
    T = x1.shape[0]
    idx, gate, pos, cnt = _router(x1, rw_t, rb)
    counts = cnt[:, 0].astype(I32)
    padded = (counts + MOE_BLK - 1) // MOE_BLK * MOE_BLK
    pad_end = jnp.cumsum(padded)
    pad_start = pad_end - padded
    dest = pad_start[idx] + pos
    nblk = T * TOP_K // MOE_BLK + N_EXPERTS
    n_used = (pad_end[-1] // MOE_BLK).astype(I32)
    blk = jnp.minimum(jnp.arange(nblk, dtype=I32), n_used - 1) * MOE_BLK
    blk_expert = jnp.minimum(jnp.searchsorted(pad_end, blk, side="right"), N_EXPERTS - 1).astype(I32)
    xs = _dispatch(dest, x1, jnp.zeros((nblk * MOE_BLK, D_MODEL), F32))
    ys = _experts(blk_expert, n_used.reshape(1), xs, wg, wu, wd)
    return _combine(dest, gate.T, x1, ys, sh_gu, sh_d, ln_g, ln_b)


def _prep_w_in(w_in):
    o = np.cumsum((0, 128, 128, 256, 256, 16, 256, 256, 256, 512, 4, 256, 256, 256))
    c = lambda i: w_in[..., o[i]:o[i + 1]]
    xbc = c(8)
    x_cols = xbc[..., 0:256]
    per_head = lambda m: jnp.concatenate(
        [m[..., SSM_N * (h // (SSM_H // SSM_G)):SSM_N * (h // (SSM_H // SSM_G) + 1)] for h in range(SSM_H)], axis=-1)
    b_cols = per_head(xbc[..., 256:384])
    c_cols = per_head(xbc[..., 384:512])
    dt_cols = jnp.repeat(c(9), SSM_P, axis=-1)
    pad = jnp.zeros(w_in.shape[:-1] + (W_GLA - o[5],), w_in.dtype)
    cols = [c(0), c(1), c(2), c(3), c(4), pad, c(5), c(6), c(7), x_cols, b_cols, c_cols, dt_cols,
            c(10), c(11), c(12)]
    return jnp.concatenate(cols, axis=-1).astype(BF16)


def _per_head(v):
    rep = SSM_H // SSM_G
    xp = v[..., 0:256]
    pick = lambda m: jnp.concatenate([m[..., SSM_N * (h // rep):SSM_N * (h // rep + 1)] for h in range(SSM_H)], axis=-1)
    return jnp.concatenate([xp, pick(v[..., 256:384]), pick(v[..., 384:512])], axis=-1)


def kernel(x, positions, w_in, gla_w_gate, gla_b_gate, gla_norm_w, sgu_ln_g, sgu_ln_b, sgu_w, sgu_b, ssm_conv_w, ssm_conv_b, ssm_dt_bias, ssm_a_log, ssm_d, ssm_norm_w, w_out, ln1_g, ln1_b, router_w, router_bias, exp_w_gate, exp_w_up, exp_w_down, sh_w_gate, sh_w_up, sh_w_down, ln2_g, ln2_b):
    B, S, _ = x.shape
    L = w_in.shape[0]
    T = B * S

    inv_freq = ROPE_THETA ** (-jnp.arange(0, ROT_DIM, 2, dtype=F32) / ROT_DIM)
    ang = positions.astype(F32)[..., None] * inv_freq
    cos, sin = jnp.cos(ang), jnp.sin(ang)
    ones = jnp.ones((B, S, DIL_DH - ROT_DIM), F32)
    cos_t = jnp.tile(jnp.concatenate([cos, cos, ones], axis=-1), (1, 1, 2))
    sin_t = jnp.tile(jnp.concatenate([-sin, sin, 0.0 * ones], axis=-1), (1, 1, 2))
    tb = jnp.asarray(_branch_log_multiplicity(S))

    w_p = _prep_w_in(w_in)
    wg_p = jnp.pad(gla_w_gate, ((0, 0), (0, 128 - GLA_RANK), (0, 0)))
    row = lambda v: v.reshape(L, 1, -1)
    gla_nw = jnp.tile(gla_norm_w, (1, GLA_H))
    sgu_wc = jnp.transpose(sgu_w, (0, 2, 1, 3)).reshape(L, SGU_CHUNK, SGU_G * SGU_CHUNK)
    sgu_bt = jnp.repeat(jnp.transpose(sgu_b, (0, 2, 1)), GROUP_W // SGU_G, axis=-1)
    exp64 = lambda v: jnp.repeat(v, SSM_P, axis=-1)
    w_out_b = w_out.astype(BF16)
    rw_t = jnp.transpose(router_w, (0, 2, 1))
    sh_gu = jnp.concatenate([sh_w_gate, sh_w_up], axis=-1).astype(BF16)
    sh_d = sh_w_down.astype(BF16)

    x2 = x.reshape(T, D_MODEL)
    for l in range(L):
        gla_p = (wg_p[l], row(gla_b_gate)[l], row(gla_nw)[l])
        sgu_p = (sgu_wc[l], sgu_bt[l], row(sgu_ln_g)[l], row(sgu_ln_b)[l])
        ssd_p = (_per_head(ssm_conv_w[l]), row(_per_head(ssm_conv_b))[l], row(exp64(ssm_dt_bias))[l],
                 row(exp64(ssm_a_log))[l], row(exp64(ssm_d))[l], row(ssm_norm_w)[l])
        x1 = _token_mixer(x2, B, S, w_p[l], gla_p, sgu_p, ssd_p, cos_t, sin_t, tb,
                          w_out_b[l], row(ln1_g)[l], row(ln1_b)[l])
        x2 = _moe(x1, rw_t[l], router_bias[l].reshape(N_EXPERTS, 1), exp_w_gate[l], exp_w_up[l], exp_w_down[l],
                  sh_gu[l], sh_d[l], row(ln2_g)[l], row(ln2_b)[l])
    return x2.reshape(B, S, D_MODEL)
```

```python
import functools
import math

import numpy as np
import jax
import jax.numpy as jnp
from jax import lax
from jax.experimental import pallas as pl
from jax.experimental.pallas import tpu as pltpu

F32 = jnp.float32
BF16 = jnp.bfloat16
I32 = jnp.int32

D_MODEL = 1024
N_LAYERS = 4
GROUP_W = 256

GLA_H, GLA_DK, GLA_DV, GLA_RANK, GLA_TAU = 4, 32, 64, 16, 16.0
SGU_G, SGU_CHUNK = 4, 128
SSM_H, SSM_P, SSM_G, SSM_N, SSM_CONV, SSM_CHUNK = 4, 64, 2, 64, 4, 128
DIL_H, DIL_DH, ROT_DIM, ROPE_THETA = 4, 64, 16, 500000.0
DIL_BRANCHES = ((128, 1), (512, 4), (2048, 16))

N_EXPERTS, TOP_K, N_EXPERT_GROUPS, TOPK_GROUPS, D_EXPERT = 128, 8, 8, 4, 256
ROUTED_SCALE = 1.0

ALPHA = (2 * N_LAYERS) ** 0.25
LN_EPS = 1e-5
RMS_EPS = 1e-6

W_GLA = 896
W_SGU = 512
W_SSD = 1280
W_DIL = 768
W_PROJ = W_GLA + W_SGU + W_SSD + W_DIL

BLK = 128
QBLK = 256
MOE_BLK = 256
TOK_BLK = 256
CMB_BLK = 128
NEG = -1e30

VMEM_LIMIT = 56 * 1024 * 1024


def _cparams(sem):
    return pltpu.CompilerParams(dimension_semantics=sem, vmem_limit_bytes=VMEM_LIMIT)


def _dot(a, b):
    return jnp.dot(a.astype(BF16), b.astype(BF16), preferred_element_type=F32)


def _dot_nt(a, b):
    return lax.dot_general(a.astype(BF16), b.astype(BF16), (((1,), (1,)), ((), ())),
                           preferred_element_type=F32)


def _dot_tn(a, b):
    return lax.dot_general(a.astype(BF16), b.astype(BF16), (((0,), (0,)), ((), ())),
                           preferred_element_type=F32)


def _split(a, n):
    out = []
    r = a
    for _ in range(n):
        p = r.astype(BF16)
        out.append(p)
        r = r - p.astype(F32)
    return out


def _dot_hi(a, b):
    a1, a2 = _split(a, 2)
    b1, b2 = _split(b, 2)
    return _dot(a1, b1) + (_dot(a1, b2) + _dot(a2, b1))


def _dot_nt_hi(a, b):
    a1, a2 = _split(a, 2)
    b1, b2 = _split(b, 2)
    return _dot_nt(a1, b1) + (_dot_nt(a1, b2) + _dot_nt(a2, b1))


def _dot_lhs01(m, b):
    b1, b2, b3 = _split(b, 3)
    return _dot(m, b1) + (_dot(m, b2) + _dot(m, b3))


def _dot_rhs01(a, m):
    a1, a2, a3 = _split(a, 3)
    return _dot(a1, m) + (_dot(a2, m) + _dot(a3, m))


def _iota(shape, dim):
    return lax.broadcasted_iota(I32, shape, dim)


def _sigmoid(x):
    return 1.0 / (1.0 + jnp.exp(-x))


def _silu(x):
    return x * _sigmoid(x)


def _softplus(x):
    return jnp.maximum(x, 0.0) + jnp.log1p(jnp.exp(-jnp.abs(x)))


def _gelu(x):
    return 0.5 * x * (1.0 + lax.erf(x * (2.0 ** -0.5)))


def _inproj_kernel(x_ref, w_ref, o_gla, o_sgu, o_ssd, o_dil):
    xb = x_ref[...].astype(BF16)
    off = 0
    for o in (o_gla, o_sgu, o_ssd, o_dil):
        n = o.shape[1]
        o[...] = jnp.dot(xb, w_ref[:, off:off + n], preferred_element_type=F32)
        off += n


def _inproj(x2, w_p):
    T = x2.shape[0]
    tm = 512
    widths = (W_GLA, W_SGU, W_SSD, W_DIL)
    return pl.pallas_call(
        _inproj_kernel,
        grid=(T // tm,),
        in_specs=[pl.BlockSpec((tm, D_MODEL), lambda i: (i, 0)),
                  pl.BlockSpec((D_MODEL, W_PROJ), lambda i: (0, 0))],
        out_specs=[pl.BlockSpec((tm, w), lambda i: (i, 0)) for w in widths],
        out_shape=[jax.ShapeDtypeStruct((T, w), F32) for w in widths],
        compiler_params=_cparams(("arbitrary",)),
        name="inproj",
    )(x2, w_p)


def _gla_kernel(slab_ref, wg_ref, bg_ref, nw_ref, out_ref, st_ref):
    S = slab_ref.shape[1]
    st_ref[...] = jnp.zeros_like(st_ref)
    row = _iota((BLK, BLK), 0)
    col = _iota((BLK, BLK), 1)
    tril = col <= row
    tril_bf = jnp.where(tril, 1.0, 0.0).astype(BF16)
    qk_head = col >> 5
    v_head = _iota((BLK, GROUP_W), 1) >> 6
    st_diag = (_iota((GROUP_W, BLK), 0) >> 6) == (_iota((GROUP_W, BLK), 1) >> 5)
    seg = jnp.where((_iota((GROUP_W, GROUP_W), 0) >> 6) == (_iota((GROUP_W, GROUP_W), 1) >> 6),
                    1.0, 0.0).astype(BF16)
    wg = wg_ref[...]
    bg = bg_ref[...]
    nw = nw_ref[...]

    def body(n, carry):
        r0 = pl.multiple_of(n * BLK, BLK)
        blk = slab_ref[0, pl.ds(r0, BLK), :]
        q = blk[:, 0:128] * (GLA_DK ** -0.5)
        k = blk[:, 128:256]
        v = blk[:, 256:512]
        r = blk[:, 512:768]
        lr = blk[:, 768:896]
        z = _dot_hi(lr, wg) + bg
        gk = (jnp.minimum(z, 0.0) - jnp.log1p(jnp.exp(-jnp.abs(z)))) * (1.0 / GLA_TAU)
        g = _dot_lhs01(tril_bf, gk)
        g_last = g[BLK - 1:BLK, :]
        q_in = q * jnp.exp(g)
        k_out = k * jnp.exp(-g)
        k_in = k * jnp.exp(g_last - g)
        vb = v.astype(BF16)
        o = _dot_nt(q_in, st_ref[...])
        kob = k_out.astype(BF16)
        for h in range(GLA_H):
            qh = jnp.where(qk_head == h, q_in, 0.0)
            a = jnp.where(tril, _dot_nt(qh, kob), 0.0)
            o = o + jnp.where(v_head == h, _dot(a, vb), 0.0)
        st_ref[...] = st_ref[...] * jnp.exp(g_last) + jnp.where(st_diag, _dot_tn(vb, k_in), 0.0)
        ms = _dot_rhs01(o * o, seg) * (1.0 / GLA_DV)
        y = o * lax.rsqrt(ms + RMS_EPS) * nw * _silu(r)
        out_ref[0, pl.ds(r0, BLK), :] = y.astype(out_ref.dtype)
        return carry

    lax.fori_loop(0, S // BLK, body, 0)


def _gla(slab, wg_p, bg, nw_t):
    B, S, _ = slab.shape
    return pl.pallas_call(
        _gla_kernel,
        grid=(B,),
        in_specs=[pl.BlockSpec((1, S, W_GLA), lambda b: (b, 0, 0)),
                  pl.BlockSpec((128, 128), lambda b: (0, 0)),
                  pl.BlockSpec((1, 128), lambda b: (0, 0)),
                  pl.BlockSpec((1, GROUP_W), lambda b: (0, 0))],
        out_specs=pl.BlockSpec((1, S, GROUP_W), lambda b: (b, 0, 0)),
        out_shape=jax.ShapeDtypeStruct((B, S, GROUP_W), BF16),
        scratch_shapes=[pltpu.VMEM((GROUP_W, 128), F32)],
        compiler_params=_cparams(("arbitrary",)),
        name="gla",
    )(slab, wg_p, bg, nw_t)


def _sgu_kernel(slab_ref, w_ref, bias_ref, g_ref, b_ref, out_ref):
    S = slab_ref.shape[1]
    wmask = (_iota((BLK, SGU_G * BLK), 1) & (BLK - 1)) <= _iota((BLK, SGU_G * BLK), 0)
    w = jnp.where(wmask, w_ref[...], 0.0).astype(BF16)
    lane_grp = _iota((BLK, GROUP_W), 1) >> 6
    bias = bias_ref[...]
    ln_g = g_ref[...]
    ln_b = b_ref[...]

    def body(n, carry):
        r0 = pl.multiple_of(n * BLK, BLK)
        blk = slab_ref[0, pl.ds(r0, BLK), :]
        u = _gelu(blk[:, 0:256])
        v = _gelu(blk[:, 256:512])
        mu = jnp.mean(v, axis=-1, keepdims=True)
        var = jnp.mean(jnp.square(v - mu), axis=-1, keepdims=True)
        v = (v - mu) * lax.rsqrt(var + LN_EPS) * ln_g + ln_b
        vexp = jnp.concatenate([jnp.where(lane_grp == g, v, 0.0) for g in range(SGU_G)], axis=0)
        s = _dot(w, vexp) + bias
        out_ref[0, pl.ds(r0, BLK), :] = (u * s).astype(out_ref.dtype)
        return carry

    lax.fori_loop(0, S // BLK, body, 0)


def _sgu(slab, w_cat, bias_t, ln_g, ln_b):
    B, S, _ = slab.shape
    return pl.pallas_call(
        _sgu_kernel,
        grid=(B,),
        in_specs=[pl.BlockSpec((1, S, W_SGU), lambda b: (b, 0, 0)),
                  pl.BlockSpec((BLK, SGU_G * BLK), lambda b: (0, 0)),
                  pl.BlockSpec((BLK, GROUP_W), lambda b: (0, 0)),
                  pl.BlockSpec((1, GROUP_W), lambda b: (0, 0)),
                  pl.BlockSpec((1, GROUP_W), lambda b: (0, 0))],
        out_specs=pl.BlockSpec((1, S, GROUP_W), lambda b: (b, 0, 0)),
        out_shape=jax.ShapeDtypeStruct((B, S, GROUP_W), BF16),
        compiler_params=_cparams(("arbitrary",)),
        name="sgu",
    )(slab, w_cat, bias_t, ln_g, ln_b)


def _ssd_kernel(slab_ref, cw_ref, cb_ref, dtb_ref, a_ref, d_ref, nw_ref, out_ref, xpad_ref, st_ref):
    S = slab_ref.shape[1]
    CW = 3 * GROUP_W
    st_ref[...] = jnp.zeros_like(st_ref)
    xpad_ref[0:8, :] = jnp.zeros((8, CW), F32)
    xpad_ref[8:S + 8, :] = slab_ref[0, :, 256:256 + CW]
    row = _iota((BLK, BLK), 0)
    col = _iota((BLK, BLK), 1)
    tril = col <= row
    tril_bf = jnp.where(tril, 1.0, 0.0).astype(BF16)
    lane_head = _iota((BLK, GROUP_W), 1) >> 6
    st_diag = (_iota((GROUP_W, GROUP_W), 0) >> 6) == (_iota((GROUP_W, GROUP_W), 1) >> 6)
    cw = cw_ref[...]
    cb = cb_ref[...]
    dtb = dtb_ref[...]
    a_neg = -jnp.exp(a_ref[...])
    d_skip = d_ref[...]
    nw = nw_ref[...]

    def body(n, carry):
        r0 = pl.multiple_of(n * BLK, BLK)
        xw = xpad_ref[pl.ds(r0, BLK + 8), :]
        conv = cb + cw[0:1, :] * xw[5:5 + BLK, :]
        for w in range(1, SSM_CONV):
            conv = conv + cw[w:w + 1, :] * xw[5 + w:5 + w + BLK, :]
        xbc = _silu(conv)
        xs = xbc[:, 0:256]
        b_e = xbc[:, 256:512]
        c_e = xbc[:, 512:768]
        z = slab_ref[0, pl.ds(r0, BLK), 0:256]
        dt = _softplus(slab_ref[0, pl.ds(r0, BLK), 1024:1280] + dtb)
        acs = _dot_lhs01(tril_bf, dt * a_neg)
        acs_last = acs[BLK - 1:BLK, :]
        x_dt = (xs * dt).astype(BF16)
        acs_t = [jnp.transpose(acs[:, 0:128]), jnp.transpose(acs[:, 128:256])]
        y = _dot(c_e * jnp.exp(acs), st_ref[...])
        b_bf = b_e.astype(BF16)
        for h in range(SSM_H):
            a_col = jnp.broadcast_to(acs[:, 64 * h:64 * h + 1], (BLK, BLK))
            a_row = jnp.broadcast_to(acs_t[h // 2][64 * (h % 2):64 * (h % 2) + 1, :], (BLK, BLK))
            lm = jnp.exp(jnp.where(tril, a_col - a_row, -jnp.inf))
            ch = jnp.where(lane_head == h, c_e, 0.0)
            m = _dot_nt(ch, b_bf) * lm
            y = y + jnp.where(lane_head == h, _dot(m, x_dt), 0.0)
        new = _dot_tn(b_e * jnp.exp(acs_last - acs), x_dt)
        st_ref[...] = st_ref[...] * jnp.exp(acs_last) + jnp.where(st_diag, new, 0.0)
        y = (y + d_skip * xs) * _silu(z)
        parts = []
        for g in range(SSM_G):
            yg = y[:, 128 * g:128 * (g + 1)]
            ms = jnp.mean(yg * yg, axis=-1, keepdims=True)
            parts.append(yg * lax.rsqrt(ms + RMS_EPS))
        out = jnp.concatenate(parts, axis=1) * nw
        out_ref[0, pl.ds(r0, BLK), :] = out.astype(out_ref.dtype)
        return carry

    lax.fori_loop(0, S // BLK, body, 0)


def _ssd(slab, cw_e, cb_e, dtb_e, a_e, d_e, nw):
    B, S, _ = slab.shape
    CW = 3 * GROUP_W
    vec = lambda w: pl.BlockSpec((1, w), lambda b: (0, 0))
    return pl.pallas_call(
        _ssd_kernel,
        grid=(B,),
        in_specs=[pl.BlockSpec((1, S, W_SSD), lambda b: (b, 0, 0)),
                  pl.BlockSpec((SSM_CONV, CW), lambda b: (0, 0)),
                  vec(CW), vec(GROUP_W), vec(GROUP_W), vec(GROUP_W), vec(GROUP_W)],
        out_specs=pl.BlockSpec((1, S, GROUP_W), lambda b: (b, 0, 0)),
        out_shape=jax.ShapeDtypeStruct((B, S, GROUP_W), BF16),
        scratch_shapes=[pltpu.VMEM((S + 8, CW), F32), pltpu.VMEM((GROUP_W, GROUP_W), F32)],
        compiler_params=_cparams(("arbitrary",)),
        name="ssd",
    )(slab, cw_e, cb_e, dtb_e, a_e, d_e, nw)


def _branch_log_multiplicity(S):
    nq = S // QBLK
    r = np.arange(QBLK)[:, None]
    u = np.arange(S)[None, :]
    d = (nq - 1) * QBLK + r - u
    c = np.zeros(d.shape, np.float64)
    for window, dil in DIL_BRANCHES:
        c += (d >= 0) & (d % dil == 0) & (d <= window)
    with np.errstate(divide="ignore"):
        return np.where(c > 0, np.log(np.maximum(c, 1.0)), NEG).astype(np.float32)


def _dil_kernel(slab_ref, cos_ref, sin_ref, tb_ref, out_ref, q_ref, k_ref, v_ref):
    S = slab_ref.shape[1]
    nq = S // QBLK
    lane_d = _iota((1, 128), 1) & (DIL_DH - 1)

    def rope(t):
        fwd = pltpu.roll(t, 128 - ROT_DIM // 2, 1)
        bwd = pltpu.roll(t, ROT_DIM // 2, 1)
        return t * cos_ref[0] + jnp.where(lane_d < ROT_DIM // 2, fwd, bwd) * sin_ref[0]

    for c in range(0, GROUP_W, 128):
        q_ref[:, c:c + 128] = (rope(slab_ref[0, :, c:c + 128]) * (DIL_DH ** -0.5)).astype(BF16)
        k_ref[:, c:c + 128] = rope(slab_ref[0, :, 256 + c:384 + c]).astype(BF16)
    v_ref[...] = slab_ref[0, :, 512:768].astype(BF16)
    lane_head = _iota((QBLK, GROUP_W), 1) >> 6
    for i in range(nq):
        nk = (i + 1) * QBLK
        qi = q_ref[i * QBLK:(i + 1) * QBLK, :]
        acc = jnp.zeros((QBLK, GROUP_W), F32)
        for h in range(DIL_H):
            qh = jnp.where(lane_head == h, qi, jnp.zeros_like(qi))
            s = _dot_nt(qh, k_ref[0:nk, :]) + tb_ref[:, (nq - 1 - i) * QBLK:]
            m = jnp.max(s, axis=-1, keepdims=True)
            p = jnp.exp(s - m)
            den = jnp.sum(p, axis=-1, keepdims=True)
            o = _dot(p, v_ref[0:nk, :])
            acc = acc + jnp.where(lane_head == h, o / den, 0.0)
        out_ref[0, i * QBLK:(i + 1) * QBLK, :] = acc.astype(out_ref.dtype)


def _dil(slab, cos_t, sin_t, tb):
    B, S, _ = slab.shape
    return pl.pallas_call(
        _dil_kernel,
        grid=(B,),
        in_specs=[pl.BlockSpec((1, S, W_DIL), lambda b: (b, 0, 0)),
                  pl.BlockSpec((1, S, 128), lambda b: (b, 0, 0)),
                  pl.BlockSpec((1, S, 128), lambda b: (b, 0, 0)),
                  pl.BlockSpec((QBLK, S), lambda b: (0, 0), pipeline_mode=pl.Buffered(1))],
        out_specs=pl.BlockSpec((1, S, GROUP_W), lambda b: (b, 0, 0)),
        out_shape=jax.ShapeDtypeStruct((B, S, GROUP_W), BF16),
        scratch_shapes=[pltpu.VMEM((S, GROUP_W), BF16)] * 3,
        compiler_params=_cparams(("arbitrary",)),
        name="dil",
    )(slab, cos_t, sin_t, tb)


def _layernorm(x, g, b):
    mu = jnp.mean(x, axis=-1, keepdims=True)
    var = jnp.mean(jnp.square(x - mu), axis=-1, keepdims=True)
    return (x - mu) * lax.rsqrt(var + LN_EPS) * g + b


def _outproj_kernel(ya, yb, yc, yd, x_ref, w_ref, g_ref, b_ref, o_ref):
    y = jnp.concatenate([ya[...], yb[...], yc[...], yd[...]], axis=1)
    y = jnp.dot(y, w_ref[...], preferred_element_type=F32)
    o_ref[...] = _layernorm(ALPHA * x_ref[...] + y, g_ref[...], b_ref[...])


def _outproj(ya, yb, yc, yd, x2, w_out, g, b):
    T = x2.shape[0]
    tm = 512
    ysp = pl.BlockSpec((tm, GROUP_W), lambda i: (i, 0))
    vec = pl.BlockSpec((1, D_MODEL), lambda i: (0, 0))
    return pl.pallas_call(
        _outproj_kernel,
        grid=(T // tm,),
        in_specs=[ysp, ysp, ysp, ysp,
                  pl.BlockSpec((tm, D_MODEL), lambda i: (i, 0)),
                  pl.BlockSpec((D_MODEL, D_MODEL), lambda i: (0, 0)), vec, vec],
        out_specs=pl.BlockSpec((tm, D_MODEL), lambda i: (i, 0)),
        out_shape=jax.ShapeDtypeStruct((T, D_MODEL), F32),
        compiler_params=_cparams(("arbitrary",)),
        name="outproj_ln",
    )(ya, yb, yc, yd, x2, w_out, g, b)


def _router_kernel(x_ref, rw_ref, rb_ref, idx_ref, gate_ref, pos_ref, cnt_ref, carry_ref):
    TB = x_ref.shape[0]
    E = N_EXPERTS
    per = E // N_EXPERT_GROUPS

    @pl.when(pl.program_id(0) == 0)
    def _():
        carry_ref[...] = jnp.zeros_like(carry_ref)

    logits = _dot_nt_hi(rw_ref[...], x_ref[...])
    scores = _sigmoid(logits)
    choice = scores + rb_ref[...]
    e_iota = _iota((E, TB), 0)

    l_iota = _iota((per, TB), 0)
    grp = []
    for g in range(N_EXPERT_GROUPS):
        cg = choice[per * g:per * (g + 1), :]
        m1 = jnp.max(cg, axis=0, keepdims=True)
        i1 = jnp.min(jnp.where(cg == m1, l_iota, per), axis=0, keepdims=True)
        m2 = jnp.max(jnp.where(l_iota == i1, -jnp.inf, cg), axis=0, keepdims=True)
        grp.append(m1 + m2)
    gs = jnp.concatenate(grp, axis=0)
    g_iota = _iota((N_EXPERT_GROUPS, TB), 0)
    keep = jnp.zeros((N_EXPERT_GROUPS, TB), jnp.bool_)
    for _ in range(TOPK_GROUPS):
        m = jnp.max(gs, axis=0, keepdims=True)
        gi = jnp.min(jnp.where(gs == m, g_iota, N_EXPERT_GROUPS), axis=0, keepdims=True)
        hit = g_iota == gi
        keep = keep | hit
        gs = jnp.where(hit, -jnp.inf, gs)
    keep_f = jnp.where(keep, 1.0, 0.0)
    keep_e = jnp.concatenate([jnp.broadcast_to(keep_f[g:g + 1, :], (per, TB))
                              for g in range(N_EXPERT_GROUPS)], axis=0)
    cm = jnp.where(keep_e > 0.5, choice, -jnp.inf)

    hits, idxs, gates = [], [], []
    onehot = jnp.zeros((E, TB), F32)
    for _ in range(TOP_K):
        m = jnp.max(cm, axis=0, keepdims=True)
        ei = jnp.min(jnp.where(cm == m, e_iota, E), axis=0, keepdims=True)
        hit = e_iota == ei
        hits.append(hit)
        idxs.append(ei)
        gates.append(jnp.sum(jnp.where(hit, scores, 0.0), axis=0, keepdims=True))
        onehot = onehot + jnp.where(hit, 1.0, 0.0)
        cm = jnp.where(hit, -jnp.inf, cm)
    gate = jnp.concatenate(gates, axis=0)
    gate = gate / jnp.sum(gate, axis=0, keepdims=True) * ROUTED_SCALE

    before = jnp.where(_iota((TB, TB), 0) < _iota((TB, TB), 1), 1.0, 0.0).astype(BF16)
    cnt = carry_ref[...] + _dot(onehot, before)
    pos = jnp.concatenate([jnp.sum(jnp.where(h, cnt, 0.0), axis=0, keepdims=True) for h in hits], axis=0)
    carry_ref[...] = carry_ref[...] + jnp.sum(onehot, axis=1, keepdims=True)

    idx_ref[...] = jnp.concatenate(idxs, axis=0)
    gate_ref[...] = gate
    pos_ref[...] = pos.astype(I32)
    cnt_ref[...] = jnp.broadcast_to(carry_ref[...], cnt_ref.shape)


def _router(x1, rw_t, rb):
    T = x1.shape[0]
    TB = TOK_BLK
    kt = pl.BlockSpec((TOP_K, TB), lambda i: (0, i))
    return pl.pallas_call(
        _router_kernel,
        grid=(T // TB,),
        in_specs=[pl.BlockSpec((TB, D_MODEL), lambda i: (i, 0)),
                  pl.BlockSpec((N_EXPERTS, D_MODEL), lambda i: (0, 0)),
                  pl.BlockSpec((N_EXPERTS, 1), lambda i: (0, 0))],
        out_specs=[kt, kt, kt, pl.BlockSpec((N_EXPERTS, 128), lambda i: (0, 0))],
        out_shape=[jax.ShapeDtypeStruct((TOP_K, T), I32), jax.ShapeDtypeStruct((TOP_K, T), F32),
                   jax.ShapeDtypeStruct((TOP_K, T), I32), jax.ShapeDtypeStruct((N_EXPERTS, 128), F32)],
        scratch_shapes=[pltpu.VMEM((N_EXPERTS, 1), F32)],
        compiler_params=_cparams(("arbitrary",)),
        name="router",
    )(x1, rw_t, rb)


def _slots_kernel(ps_ref, idx_ref, pos_ref, dest_ref):
    idx = idx_ref[...]

    def body(e, acc):
        return acc + jnp.where(idx == e, ps_ref[e], 0)

    dest_ref[...] = lax.fori_loop(0, N_EXPERTS, body, pos_ref[...], unroll=8)


def _slots(pad_start, idx, pos):
    T = idx.shape[1]
    tb = min(T, 2048)
    kt = pl.BlockSpec((TOP_K, tb), lambda i, ps: (0, i))
    return pl.pallas_call(
        _slots_kernel,
        grid_spec=pltpu.PrefetchScalarGridSpec(
            num_scalar_prefetch=1, grid=(T // tb,), in_specs=[kt, kt], out_specs=kt),
        out_shape=jax.ShapeDtypeStruct((TOP_K, T), I32),
        compiler_params=_cparams(("arbitrary",)),
        name="slots",
    )(pad_start, idx, pos)


def _dispatch_kernel(dest_ref, x_ref, xs_in, xs_out, sem):
    del xs_in
    TB = x_ref.shape[0]

    def issue(t, c):
        for k in range(TOP_K):
            d = dest_ref[k, t]
            pltpu.make_async_copy(x_ref.at[pl.ds(t, 1), :], xs_out.at[pl.ds(d, 1), :], sem).start(priority=k % 2)
        return c

    lax.fori_loop(0, TB, issue, 0)

    def drain(t, c):
        for k in range(TOP_K):
            pltpu.make_async_copy(x_ref.at[pl.ds(0, 1), :], xs_out.at[pl.ds(0, 1), :], sem).wait()
        return c

    lax.fori_loop(0, TB, drain, 0)


def _dispatch(dest, x1, xs0):
    T = x1.shape[0]
    TB = TOK_BLK
    return pl.pallas_call(
        _dispatch_kernel,
        grid=(T // TB,),
        in_specs=[pl.BlockSpec((TOP_K, TB), lambda i: (0, i), memory_space=pltpu.SMEM),
                  pl.BlockSpec((TB, D_MODEL), lambda i: (i, 0)),
                  pl.BlockSpec(memory_space=pl.ANY)],
        out_specs=pl.BlockSpec(memory_space=pl.ANY),
        out_shape=jax.ShapeDtypeStruct(xs0.shape, xs0.dtype),
        scratch_shapes=[pltpu.SemaphoreType.DMA],
        input_output_aliases={2: 0},
        compiler_params=_cparams(("arbitrary",)),
        name="dispatch",
    )(dest, x1, xs0)


def _experts_kernel(be_ref, nu_ref, xs_ref, wg_ref, wu_ref, wd_ref, ys_ref, wgu_s, wd_s):
    i = pl.program_id(0)

    @pl.when(i < nu_ref[0])
    def _():
        prev = be_ref[jnp.maximum(i - 1, 0)]

        @pl.when((i == 0) | (be_ref[i] != prev))
        def _():
            wgu_s[:, 0:D_EXPERT] = wg_ref[0].astype(BF16)
            wgu_s[:, D_EXPERT:] = wu_ref[0].astype(BF16)
            wd_s[...] = wd_ref[0].astype(BF16)

        h = jnp.dot(xs_ref[...].astype(BF16), wgu_s[...], preferred_element_type=F32)
        a = _silu(h[:, 0:D_EXPERT]) * h[:, D_EXPERT:]
        ys_ref[...] = jnp.dot(a.astype(BF16), wd_s[...], preferred_element_type=F32)


def _experts(blk_expert, n_used, xs, wg, wu, wd):
    nblk = xs.shape[0] // MOE_BLK
    row = lambda i, be, nu: (jnp.minimum(i, nu[0] - 1), 0)
    wsel = lambda i, be, nu: (be[i], 0, 0)
    return pl.pallas_call(
        _experts_kernel,
        grid_spec=pltpu.PrefetchScalarGridSpec(
            num_scalar_prefetch=2,
            grid=(nblk,),
            in_specs=[pl.BlockSpec((MOE_BLK, D_MODEL), row),
                      pl.BlockSpec((1, D_MODEL, D_EXPERT), wsel),
                      pl.BlockSpec((1, D_MODEL, D_EXPERT), wsel),
                      pl.BlockSpec((1, D_EXPERT, D_MODEL), wsel)],
            out_specs=pl.BlockSpec((MOE_BLK, D_MODEL), row),
            scratch_shapes=[pltpu.VMEM((D_MODEL, 2 * D_EXPERT), BF16),
                            pltpu.VMEM((D_EXPERT, D_MODEL), BF16)]),
        out_shape=jax.ShapeDtypeStruct(xs.shape, F32),
        compiler_params=_cparams(("arbitrary",)),
        name="experts",
    )(blk_expert, n_used, xs, wg, wu, wd)


def _combine_kernel(dest_ref, gate_ref, x_ref, ys_ref, sgu_ref, sd_ref, g_ref, b_ref, o_ref, buf, sem):
    TB = x_ref.shape[0]

    def issue(t, c):
        for k in range(TOP_K):
            d = dest_ref[k, t]
            pltpu.make_async_copy(ys_ref.at[pl.ds(d, 1), :], buf.at[k, pl.ds(t, 1), :], sem).start(priority=k % 2)
        return c

    lax.fori_loop(0, TB, issue, 0)

    x = x_ref[...]
    h = jnp.dot(x.astype(BF16), sgu_ref[...], preferred_element_type=F32)
    a = _silu(h[:, 0:D_EXPERT]) * h[:, D_EXPERT:]
    y = ALPHA * x + jnp.dot(a.astype(BF16), sd_ref[...], preferred_element_type=F32)

    def drain(t, c):
        for k in range(TOP_K):
            pltpu.make_async_copy(ys_ref.at[pl.ds(0, 1), :], buf.at[0, pl.ds(0, 1), :], sem).wait()
        return c

    lax.fori_loop(0, TB, drain, 0)

    gate = gate_ref[...]
    for k in range(TOP_K):
        y = y + gate[:, k:k + 1] * buf[k]
    o_ref[...] = _layernorm(y, g_ref[...], b_ref[...])


def _combine(dest, gate_t, x1, ys, sh_gu, sh_d, g, b):
    T = x1.shape[0]
    TB = CMB_BLK
    vec = pl.BlockSpec((1, D_MODEL), lambda i: (0, 0))
    return pl.pallas_call(
        _combine_kernel,
        grid=(T // TB,),
        in_specs=[pl.BlockSpec((TOP_K, TB), lambda i: (0, i), memory_space=pltpu.SMEM),
                  pl.BlockSpec((TB, TOP_K), lambda i: (i, 0)),
                  pl.BlockSpec((TB, D_MODEL), lambda i: (i, 0)),
                  pl.BlockSpec(memory_space=pl.ANY),
                  pl.BlockSpec((D_MODEL, 2 * D_EXPERT), lambda i: (0, 0)),
                  pl.BlockSpec((D_EXPERT, D_MODEL), lambda i: (0, 0)), vec, vec],
        out_specs=pl.BlockSpec((TB, D_MODEL), lambda i: (i, 0)),
        out_shape=jax.ShapeDtypeStruct((T, D_MODEL), F32),
        scratch_shapes=[pltpu.VMEM((TOP_K, TB, D_MODEL), F32), pltpu.SemaphoreType.DMA],
        compiler_params=_cparams(("arbitrary",)),
        name="combine_ln",
    )(dest, gate_t, x1, ys, sh_gu, sh_d, g, b)


def _token_mixer(x2, B, S, w_p, gla_p, sgu_p, ssd_p, cos_t, sin_t, tb, w_out, ln_g, ln_b):
    s_gla, s_sgu, s_ssd, s_dil = _inproj(x2, w_p)
    ya = _gla(s_gla.reshape(B, S, W_GLA), *gla_p)
    yb = _sgu(s_sgu.reshape(B, S, W_SGU), *sgu_p)
    yc = _ssd(s_ssd.reshape(B, S, W_SSD), *ssd_p)
    yd = _dil(s_dil.reshape(B, S, W_DIL), cos_t, sin_t, tb)
    T = B * S
    flat = lambda y: y.reshape(T, GROUP_W)
    return _outproj(flat(ya), flat(yb), flat(yc), flat(yd), x2, w_out, ln_g, ln_b)


def _moe(x1, rw_t, rb, wg, wu, wd, sh_gu, sh_d, ln_g, ln_b):
    T = x1.shape[0]
    idx, gate, pos, cnt = _router(x1, rw_t, rb)
    counts = cnt[:, 0].astype(I32)
    padded = (counts + MOE_BLK - 1) // MOE_BLK * MOE_BLK
    pad_end = jnp.cumsum(padded)
    pad_start = pad_end - padded
    dest = _slots(pad_start.astype(I32), idx, pos)
    nblk = T * TOP_K // MOE_BLK + N_EXPERTS
    n_used = (pad_end[-1] // MOE_BLK).astype(I32)
    blk = jnp.minimum(jnp.arange(nblk, dtype=I32), n_used - 1) * MOE_BLK
    blk_expert = jnp.sum((pad_end[None, :] <= blk[:, None]).astype(I32), axis=1)
    blk_expert = jnp.minimum(blk_expert, N_EXPERTS - 1)
    xs = _dispatch(dest, x1, jnp.zeros((nblk * MOE_BLK, D_MODEL), F32))
    ys = _experts(blk_expert, n_used.reshape(1), xs, wg, wu, wd)
    return _combine(dest, gate.T, x1, ys, sh_gu, sh_d, ln_g, ln_b)


def _prep_w_in(w_in):
    o = np.cumsum((0, 128, 128, 256, 256, 16, 256, 256, 256, 512, 4, 256, 256, 256))
    c = lambda i: w_in[..., o[i]:o[i + 1]]
    xbc = c(8)
    x_cols = xbc[..., 0:256]
    per_head = lambda m: jnp.concatenate(
        [m[..., SSM_N * (h // (SSM_H // SSM_G)):SSM_N * (h // (SSM_H // SSM_G) + 1)] for h in range(SSM_H)], axis=-1)
    b_cols = per_head(xbc[..., 256:384])
    c_cols = per_head(xbc[..., 384:512])
    dt_cols = jnp.repeat(c(9), SSM_P, axis=-1)
    pad = jnp.zeros(w_in.shape[:-1] + (W_GLA - o[5],), w_in.dtype)
    cols = [c(0), c(1), c(2), c(3), c(4), pad, c(5), c(6), c(7), x_cols, b_cols, c_cols, dt_cols,
            c(10), c(11), c(12)]
    return jnp.concatenate(cols, axis=-1).astype(BF16)


def _per_head(v):
    rep = SSM_H // SSM_G
    xp = v[..., 0:256]
    pick = lambda m: jnp.concatenate([m[..., SSM_N * (h // rep):SSM_N * (h // rep + 1)] for h in range(SSM_H)], axis=-1)
    return jnp.concatenate([xp, pick(v[..., 256:384]), pick(v[..., 384:512])], axis=-1)


def kernel(x, positions, w_in, gla_w_gate, gla_b_gate, gla_norm_w, sgu_ln_g, sgu_ln_b, sgu_w, sgu_b, ssm_conv_w, ssm_conv_b, ssm_dt_bias, ssm_a_log, ssm_d, ssm_norm_w, w_out, ln1_g, ln1_b, router_w, router_bias, exp_w_gate, exp_w_up, exp_w_down, sh_w_gate, sh_w_up, sh_w_down, ln2_g, ln2_b):
    B, S, _ = x.shape
    L = w_in.shape[0]
    T = B * S

    inv_freq = ROPE_THETA ** (-jnp.arange(0, ROT_DIM, 2, dtype=F32) / ROT_DIM)
    ang = positions.astype(F32)[..., None] * inv_freq
    cos, sin = jnp.cos(ang), jnp.sin(ang)
    ones = jnp.ones((B, S, DIL_DH - ROT_DIM), F32)
    cos_t = jnp.tile(jnp.concatenate([cos, cos, ones], axis=-1), (1, 1, 2))
    sin_t = jnp.tile(jnp.concatenate([-sin, sin, 0.0 * ones], axis=-1), (1, 1, 2))
    tb = jnp.asarray(_branch_log_multiplicity(S))

    w_p = _prep_w_in(w_in)
    wg_p = jnp.pad(gla_w_gate, ((0, 0), (0, 128 - GLA_RANK), (0, 0)))
    row = lambda v: v.reshape(L, 1, -1)
    gla_nw = jnp.tile(gla_norm_w, (1, GLA_H))
    sgu_wc = jnp.transpose(sgu_w, (0, 2, 1, 3)).reshape(L, SGU_CHUNK, SGU_G * SGU_CHUNK)
    sgu_bt = jnp.repeat(jnp.transpose(sgu_b, (0, 2, 1)), GROUP_W // SGU_G, axis=-1)
    exp64 = lambda v: jnp.repeat(v, SSM_P, axis=-1)
    w_out_b = w_out.astype(BF16)
    rw_t = jnp.transpose(router_w, (0, 2, 1))
    sh_gu = jnp.concatenate([sh_w_gate, sh_w_up], axis=-1).astype(BF16)
    sh_d = sh_w_down.astype(BF16)

    x2 = x.reshape(T, D_MODEL)
    for l in range(L):
        gla_p = (wg_p[l], row(gla_b_gate)[l], row(gla_nw)[l])
        sgu_p = (sgu_wc[l], sgu_bt[l], row(sgu_ln_g)[l], row(sgu_ln_b)[l])
        ssd_p = (_per_head(ssm_conv_w[l]), row(_per_head(ssm_conv_b))[l], row(exp64(ssm_dt_bias))[l],
                 row(exp64(ssm_a_log))[l], row(exp64(ssm_d))[l], row(ssm_norm_w)[l])
        x1 = _token_mixer(x2, B, S, w_p[l], gla_p, sgu_p, ssd_p, cos_t, sin_t, tb,
                          w_out_b[l], row(ln1_g)[l], row(ln1_b)[l])
        x2 = _moe(x1, rw_t[l], router_bias[l].reshape(N_EXPERTS, 1), exp_w_gate[l], exp_w_up[l], exp_w_down[l],
                  sh_gu[l], sh_d[l], row(ln2_g)[l], row(ln2_b)[l])
    return x2.reshape(B, S, D_MODEL)
```

```python
import functools
import math

import numpy as np
import jax
import jax.numpy as jnp
from jax import lax
from jax.experimental import pallas as pl
from jax.experimental.pallas import tpu as pltpu

F32 = jnp.float32
BF16 = jnp.bfloat16
I32 = jnp.int32

D_MODEL = 1024
N_LAYERS = 4
GROUP_W = 256

GLA_H, GLA_DK, GLA_DV, GLA_RANK, GLA_TAU = 4, 32, 64, 16, 16.0
SGU_G, SGU_CHUNK = 4, 128
SSM_H, SSM_P, SSM_G, SSM_N, SSM_CONV, SSM_CHUNK = 4, 64, 2, 64, 4, 128
DIL_H, DIL_DH, ROT_DIM, ROPE_THETA = 4, 64, 16, 500000.0
DIL_BRANCHES = ((128, 1), (512, 4), (2048, 16))

N_EXPERTS, TOP_K, N_EXPERT_GROUPS, TOPK_GROUPS, D_EXPERT = 128, 8, 8, 4, 256
ROUTED_SCALE = 1.0

ALPHA = (2 * N_LAYERS) ** 0.25
LN_EPS = 1e-5
RMS_EPS = 1e-6

W_GLA = 896
W_SGU = 512
W_SSD = 1280
W_DIL = 768
W_PROJ = W_GLA + W_SGU + W_SSD + W_DIL

BLK = 128
QBLK = 256
MOE_BLK = 256
TOK_BLK = 256
CMB_BLK = 128
NEG = -1e30

VMEM_LIMIT = 56 * 1024 * 1024


def _cparams(sem):
    return pltpu.CompilerParams(dimension_semantics=sem, vmem_limit_bytes=VMEM_LIMIT)


def _dot(a, b):
    return jnp.dot(a.astype(BF16), b.astype(BF16), preferred_element_type=F32)


def _dot_nt(a, b):
    return lax.dot_general(a.astype(BF16), b.astype(BF16), (((1,), (1,)), ((), ())),
                           preferred_element_type=F32)


def _dot_tn(a, b):
    return lax.dot_general(a.astype(BF16), b.astype(BF16), (((0,), (0,)), ((), ())),
                           preferred_element_type=F32)


def _split(a, n):
    out = []
    r = a
    for _ in range(n):
        p = r.astype(BF16)
        out.append(p)
        r = r - p.astype(F32)
    return out


def _dot_hi(a, b):
    a1, a2 = _split(a, 2)
    b1, b2 = _split(b, 2)
    return _dot(a1, b1) + (_dot(a1, b2) + _dot(a2, b1))


def _dot_nt_hi(a, b):
    a1, a2 = _split(a, 2)
    b1, b2 = _split(b, 2)
    return _dot_nt(a1, b1) + (_dot_nt(a1, b2) + _dot_nt(a2, b1))


def _dot_lhs01(m, b):
    b1, b2, b3 = _split(b, 3)
    return _dot(m, b1) + (_dot(m, b2) + _dot(m, b3))


def _dot_rhs01(a, m):
    a1, a2, a3 = _split(a, 3)
    return _dot(a1, m) + (_dot(a2, m) + _dot(a3, m))


def _iota(shape, dim):
    return lax.broadcasted_iota(I32, shape, dim)


def _sigmoid(x):
    return 1.0 / (1.0 + jnp.exp(-x))


def _silu(x):
    return x * _sigmoid(x)


def _softplus(x):
    return jnp.maximum(x, 0.0) + jnp.log1p(jnp.exp(-jnp.abs(x)))


def _gelu(x):
    return 0.5 * x * (1.0 + lax.erf(x * (2.0 ** -0.5)))


def _inproj_kernel(x_ref, w_ref, o_gla, o_sgu, o_ssd, o_dil):
    xb = x_ref[...].astype(BF16)
    off = 0
    for o in (o_gla, o_sgu, o_ssd, o_dil):
        n = o.shape[1]
        o[...] = jnp.dot(xb, w_ref[:, off:off + n], preferred_element_type=F32)
        off += n


def _inproj(x2, w_p):
    T = x2.shape[0]
    tm = 512
    widths = (W_GLA, W_SGU, W_SSD, W_DIL)
    return pl.pallas_call(
        _inproj_kernel,
        grid=(T // tm,),
        in_specs=[pl.BlockSpec((tm, D_MODEL), lambda i: (i, 0)),
                  pl.BlockSpec((D_MODEL, W_PROJ), lambda i: (0, 0))],
        out_specs=[pl.BlockSpec((tm, w), lambda i: (i, 0)) for w in widths],
        out_shape=[jax.ShapeDtypeStruct((T, w), F32) for w in widths],
        compiler_params=_cparams(("arbitrary",)),
        name="inproj",
    )(x2, w_p)


def _gla_kernel(slab_ref, wg_ref, bg_ref, nw_ref, out_ref, st_ref):
    S = slab_ref.shape[1]
    st_ref[...] = jnp.zeros_like(st_ref)
    row = _iota((BLK, BLK), 0)
    col = _iota((BLK, BLK), 1)
    tril = col <= row
    tril_bf = jnp.where(tril, 1.0, 0.0).astype(BF16)
    qk_head = col >> 5
    v_head = _iota((BLK, GROUP_W), 1) >> 6
    st_diag = (_iota((GROUP_W, BLK), 0) >> 6) == (_iota((GROUP_W, BLK), 1) >> 5)
    seg = jnp.where((_iota((GROUP_W, GROUP_W), 0) >> 6) == (_iota((GROUP_W, GROUP_W), 1) >> 6),
                    1.0, 0.0).astype(BF16)
    wg = wg_ref[...]
    bg = bg_ref[...]
    nw = nw_ref[...]

    def body(n, carry):
        r0 = pl.multiple_of(n * BLK, BLK)
        blk = slab_ref[0, pl.ds(r0, BLK), :]
        q = blk[:, 0:128] * (GLA_DK ** -0.5)
        k = blk[:, 128:256]
        v = blk[:, 256:512]
        r = blk[:, 512:768]
        lr = blk[:, 768:896]
        z = _dot_hi(lr, wg) + bg
        gk = (jnp.minimum(z, 0.0) - jnp.log1p(jnp.exp(-jnp.abs(z)))) * (1.0 / GLA_TAU)
        g = _dot_lhs01(tril_bf, gk)
        g_last = g[BLK - 1:BLK, :]
        q_in = q * jnp.exp(g)
        k_out = k * jnp.exp(-g)
        k_in = k * jnp.exp(g_last - g)
        vb = v.astype(BF16)
        o = _dot_nt(q_in, st_ref[...])
        kob = k_out.astype(BF16)
        for h in range(GLA_H):
            qh = jnp.where(qk_head == h, q_in, 0.0)
            a = jnp.where(tril, _dot_nt(qh, kob), 0.0)
            o = o + jnp.where(v_head == h, _dot(a, vb), 0.0)
        st_ref[...] = st_ref[...] * jnp.exp(g_last) + jnp.where(st_diag, _dot_tn(vb, k_in), 0.0)
        ms = _dot_rhs01(o * o, seg) * (1.0 / GLA_DV)
        y = o * lax.rsqrt(ms + RMS_EPS) * nw * _silu(r)
        out_ref[0, pl.ds(r0, BLK), :] = y.astype(out_ref.dtype)
        return carry

    lax.fori_loop(0, S // BLK, body, 0)


def _gla(slab, wg_p, bg, nw_t):
    B, S, _ = slab.shape
    return pl.pallas_call(
        _gla_kernel,
        grid=(B,),
        in_specs=[pl.BlockSpec((1, S, W_GLA), lambda b: (b, 0, 0)),
                  pl.BlockSpec((128, 128), lambda b: (0, 0)),
                  pl.BlockSpec((1, 128), lambda b: (0, 0)),
                  pl.BlockSpec((1, GROUP_W), lambda b: (0, 0))],
        out_specs=pl.BlockSpec((1, S, GROUP_W), lambda b: (b, 0, 0)),
        out_shape=jax.ShapeDtypeStruct((B, S, GROUP_W), BF16),
        scratch_shapes=[pltpu.VMEM((GROUP_W, 128), F32)],
        compiler_params=_cparams(("arbitrary",)),
        name="gla",
    )(slab, wg_p, bg, nw_t)


def _sgu_kernel(slab_ref, w_ref, bias_ref, g_ref, b_ref, out_ref):
    S = slab_ref.shape[1]
    wmask = (_iota((BLK, SGU_G * BLK), 1) & (BLK - 1)) <= _iota((BLK, SGU_G * BLK), 0)
    w = jnp.where(wmask, w_ref[...], 0.0).astype(BF16)
    lane_grp = _iota((BLK, GROUP_W), 1) >> 6
    bias = bias_ref[...]
    ln_g = g_ref[...]
    ln_b = b_ref[...]

    def body(n, carry):
        r0 = pl.multiple_of(n * BLK, BLK)
        blk = slab_ref[0, pl.ds(r0, BLK), :]
        u = _gelu(blk[:, 0:256])
        v = _gelu(blk[:, 256:512])
        mu = jnp.mean(v, axis=-1, keepdims=True)
        var = jnp.mean(jnp.square(v - mu), axis=-1, keepdims=True)
        v = (v - mu) * lax.rsqrt(var + LN_EPS) * ln_g + ln_b
        vexp = jnp.concatenate([jnp.where(lane_grp == g, v, 0.0) for g in range(SGU_G)], axis=0)
        s = _dot(w, vexp) + bias
        out_ref[0, pl.ds(r0, BLK), :] = (u * s).astype(out_ref.dtype)
        return carry

    lax.fori_loop(0, S // BLK, body, 0)


def _sgu(slab, w_cat, bias_t, ln_g, ln_b):
    B, S, _ = slab.shape
    return pl.pallas_call(
        _sgu_kernel,
        grid=(B,),
        in_specs=[pl.BlockSpec((1, S, W_SGU), lambda b: (b, 0, 0)),
                  pl.BlockSpec((BLK, SGU_G * BLK), lambda b: (0, 0)),
                  pl.BlockSpec((BLK, GROUP_W), lambda b: (0, 0)),
                  pl.BlockSpec((1, GROUP_W), lambda b: (0, 0)),
                  pl.BlockSpec((1, GROUP_W), lambda b: (0, 0))],
        out_specs=pl.BlockSpec((1, S, GROUP_W), lambda b: (b, 0, 0)),
        out_shape=jax.ShapeDtypeStruct((B, S, GROUP_W), BF16),
        compiler_params=_cparams(("arbitrary",)),
        name="sgu",
    )(slab, w_cat, bias_t, ln_g, ln_b)


def _ssd_kernel(slab_ref, cw_ref, cb_ref, dtb_ref, a_ref, d_ref, nw_ref, out_ref, xpad_ref, st_ref):
    S = slab_ref.shape[1]
    CW = 3 * GROUP_W
    st_ref[...] = jnp.zeros_like(st_ref)
    xpad_ref[0:8, :] = jnp.zeros((8, CW), F32)
    xpad_ref[8:S + 8, :] = slab_ref[0, :, 256:256 + CW]
    row = _iota((BLK, BLK), 0)
    col = _iota((BLK, BLK), 1)
    tril = col <= row
    tril_bf = jnp.where(tril, 1.0, 0.0).astype(BF16)
    lane_head = _iota((BLK, GROUP_W), 1) >> 6
    st_diag = (_iota((GROUP_W, GROUP_W), 0) >> 6) == (_iota((GROUP_W, GROUP_W), 1) >> 6)
    cw = cw_ref[...]
    cb = cb_ref[...]
    dtb = dtb_ref[...]
    a_neg = -jnp.exp(a_ref[...])
    d_skip = d_ref[...]
    nw = nw_ref[...]

    def body(n, carry):
        r0 = pl.multiple_of(n * BLK, BLK)
        xw = xpad_ref[pl.ds(r0, BLK + 8), :]
        conv = cb + cw[0:1, :] * xw[5:5 + BLK, :]
        for w in range(1, SSM_CONV):
            conv = conv + cw[w:w + 1, :] * xw[5 + w:5 + w + BLK, :]
        xbc = _silu(conv)
        xs = xbc[:, 0:256]
        b_e = xbc[:, 256:512]
        c_e = xbc[:, 512:768]
        z = slab_ref[0, pl.ds(r0, BLK), 0:256]
        dt = _softplus(slab_ref[0, pl.ds(r0, BLK), 1024:1280] + dtb)
        acs = _dot_lhs01(tril_bf, dt * a_neg)
        acs_last = acs[BLK - 1:BLK, :]
        x_dt = (xs * dt).astype(BF16)
        acs_t = [jnp.transpose(acs[:, 0:128]), jnp.transpose(acs[:, 128:256])]
        y = _dot(c_e * jnp.exp(acs), st_ref[...])
        b_bf = b_e.astype(BF16)
        for h in range(SSM_H):
            a_col = jnp.broadcast_to(acs[:, 64 * h:64 * h + 1], (BLK, BLK))
            a_row = jnp.broadcast_to(acs_t[h // 2][64 * (h % 2):64 * (h % 2) + 1, :], (BLK, BLK))
            lm = jnp.exp(jnp.where(tril, a_col - a_row, -jnp.inf))
            ch = jnp.where(lane_head == h, c_e, 0.0)
            m = _dot_nt(ch, b_bf) * lm
            y = y + jnp.where(lane_head == h, _dot(m, x_dt), 0.0)
        new = _dot_tn(b_e * jnp.exp(acs_last - acs), x_dt)
        st_ref[...] = st_ref[...] * jnp.exp(acs_last) + jnp.where(st_diag, new, 0.0)
        y = (y + d_skip * xs) * _silu(z)
        parts = []
        for g in range(SSM_G):
            yg = y[:, 128 * g:128 * (g + 1)]
            ms = jnp.mean(yg * yg, axis=-1, keepdims=True)
            parts.append(yg * lax.rsqrt(ms + RMS_EPS))
        out = jnp.concatenate(parts, axis=1) * nw
        out_ref[0, pl.ds(r0, BLK), :] = out.astype(out_ref.dtype)
        return carry

    lax.fori_loop(0, S // BLK, body, 0)


def _ssd(slab, cw_e, cb_e, dtb_e, a_e, d_e, nw):
    B, S, _ = slab.shape
    CW = 3 * GROUP_W
    vec = lambda w: pl.BlockSpec((1, w), lambda b: (0, 0))
    return pl.pallas_call(
        _ssd_kernel,
        grid=(B,),
        in_specs=[pl.BlockSpec((1, S, W_SSD), lambda b: (b, 0, 0)),
                  pl.BlockSpec((SSM_CONV, CW), lambda b: (0, 0)),
                  vec(CW), vec(GROUP_W), vec(GROUP_W), vec(GROUP_W), vec(GROUP_W)],
        out_specs=pl.BlockSpec((1, S, GROUP_W), lambda b: (b, 0, 0)),
        out_shape=jax.ShapeDtypeStruct((B, S, GROUP_W), BF16),
        scratch_shapes=[pltpu.VMEM((S + 8, CW), F32), pltpu.VMEM((GROUP_W, GROUP_W), F32)],
        compiler_params=_cparams(("arbitrary",)),
        name="ssd",
    )(slab, cw_e, cb_e, dtb_e, a_e, d_e, nw)


def _branch_log_multiplicity(S):
    nq = S // QBLK
    r = np.arange(QBLK)[:, None]
    u = np.arange(S)[None, :]
    d = (nq - 1) * QBLK + r - u
    c = np.zeros(d.shape, np.float64)
    for window, dil in DIL_BRANCHES:
        c += (d >= 0) & (d % dil == 0) & (d <= window)
    with np.errstate(divide="ignore"):
        return np.where(c > 0, np.log(np.maximum(c, 1.0)), NEG).astype(np.float32)


def _dil_kernel(slab_ref, cos_ref, sin_ref, tb_ref, out_ref, q_ref, k_ref, v_ref):
    S = slab_ref.shape[1]
    nq = S // QBLK
    lane_d = _iota((1, 128), 1) & (DIL_DH - 1)

    def rope(t):
        fwd = pltpu.roll(t, 128 - ROT_DIM // 2, 1)
        bwd = pltpu.roll(t, ROT_DIM // 2, 1)
        return t * cos_ref[0] + jnp.where(lane_d < ROT_DIM // 2, fwd, bwd) * sin_ref[0]

    for c in range(0, GROUP_W, 128):
        q_ref[:, c:c + 128] = (rope(slab_ref[0, :, c:c + 128]) * (DIL_DH ** -0.5)).astype(BF16)
        k_ref[:, c:c + 128] = rope(slab_ref[0, :, 256 + c:384 + c]).astype(BF16)
    v_ref[...] = slab_ref[0, :, 512:768].astype(BF16)
    lane_head = _iota((QBLK, GROUP_W), 1) >> 6
    for i in range(nq):
        nk = (i + 1) * QBLK
        qi = q_ref[i * QBLK:(i + 1) * QBLK, :]
        acc = jnp.zeros((QBLK, GROUP_W), F32)
        for h in range(DIL_H):
            qh = jnp.where(lane_head == h, qi, jnp.zeros_like(qi))
            s = _dot_nt(qh, k_ref[0:nk, :]) + tb_ref[:, (nq - 1 - i) * QBLK:]
            m = jnp.max(s, axis=-1, keepdims=True)
            p = jnp.exp(s - m)
            den = jnp.sum(p, axis=-1, keepdims=True)
            o = _dot(p, v_ref[0:nk, :])
            acc = acc + jnp.where(lane_head == h, o / den, 0.0)
        out_ref[0, i * QBLK:(i + 1) * QBLK, :] = acc.astype(out_ref.dtype)


def _dil(slab, cos_t, sin_t, tb):
    B, S, _ = slab.shape
    return pl.pallas_call(
        _dil_kernel,
        grid=(B,),
        in_specs=[pl.BlockSpec((1, S, W_DIL), lambda b: (b, 0, 0)),
                  pl.BlockSpec((1, S, 128), lambda b: (b, 0, 0)),
                  pl.BlockSpec((1, S, 128), lambda b: (b, 0, 0)),
                  pl.BlockSpec((QBLK, S), lambda b: (0, 0), pipeline_mode=pl.Buffered(1))],
        out_specs=pl.BlockSpec((1, S, GROUP_W), lambda b: (b, 0, 0)),
        out_shape=jax.ShapeDtypeStruct((B, S, GROUP_W), BF16),
        scratch_shapes=[pltpu.VMEM((S, GROUP_W), BF16)] * 3,
        compiler_params=_cparams(("arbitrary",)),
        name="dil",
    )(slab, cos_t, sin_t, tb)


def _layernorm(x, g, b):
    mu = jnp.mean(x, axis=-1, keepdims=True)
    var = jnp.mean(jnp.square(x - mu), axis=-1, keepdims=True)
    return (x - mu) * lax.rsqrt(var + LN_EPS) * g + b


def _outproj_kernel(ya, yb, yc, yd, x_ref, w_ref, g_ref, b_ref, o_ref):
    y = jnp.concatenate([ya[...], yb[...], yc[...], yd[...]], axis=1)
    y = jnp.dot(y, w_ref[...], preferred_element_type=F32)
    o_ref[...] = _layernorm(ALPHA * x_ref[...] + y, g_ref[...], b_ref[...])


def _outproj(ya, yb, yc, yd, x2, w_out, g, b):
    T = x2.shape[0]
    tm = 512
    ysp = pl.BlockSpec((tm, GROUP_W), lambda i: (i, 0))
    vec = pl.BlockSpec((1, D_MODEL), lambda i: (0, 0))
    return pl.pallas_call(
        _outproj_kernel,
        grid=(T // tm,),
        in_specs=[ysp, ysp, ysp, ysp,
                  pl.BlockSpec((tm, D_MODEL), lambda i: (i, 0)),
                  pl.BlockSpec((D_MODEL, D_MODEL), lambda i: (0, 0)), vec, vec],
        out_specs=pl.BlockSpec((tm, D_MODEL), lambda i: (i, 0)),
        out_shape=jax.ShapeDtypeStruct((T, D_MODEL), F32),
        compiler_params=_cparams(("arbitrary",)),
        name="outproj_ln",
    )(ya, yb, yc, yd, x2, w_out, g, b)


def _router_kernel(x_ref, rw_ref, rb_ref, idx_ref, gate_ref, pos_ref, cnt_ref, carry_ref):
    TB = x_ref.shape[0]
    E = N_EXPERTS
    per = E // N_EXPERT_GROUPS

    @pl.when(pl.program_id(0) == 0)
    def _():
        carry_ref[...] = jnp.zeros_like(carry_ref)

    logits = _dot_nt_hi(rw_ref[...], x_ref[...])
    scores = _sigmoid(logits)
    choice = scores + rb_ref[...]
    e_iota = _iota((E, TB), 0)

    l_iota = _iota((per, TB), 0)
    grp = []
    for g in range(N_EXPERT_GROUPS):
        cg = choice[per * g:per * (g + 1), :]
        m1 = jnp.max(cg, axis=0, keepdims=True)
        i1 = jnp.min(jnp.where(cg == m1, l_iota, per), axis=0, keepdims=True)
        m2 = jnp.max(jnp.where(l_iota == i1, -jnp.inf, cg), axis=0, keepdims=True)
        grp.append(m1 + m2)
    gs = jnp.concatenate(grp, axis=0)
    g_iota = _iota((N_EXPERT_GROUPS, TB), 0)
    keep = jnp.zeros((N_EXPERT_GROUPS, TB), jnp.bool_)
    for _ in range(TOPK_GROUPS):
        m = jnp.max(gs, axis=0, keepdims=True)
        gi = jnp.min(jnp.where(gs == m, g_iota, N_EXPERT_GROUPS), axis=0, keepdims=True)
        hit = g_iota == gi
        keep = keep | hit
        gs = jnp.where(hit, -jnp.inf, gs)
    keep_f = jnp.where(keep, 1.0, 0.0)
    keep_e = jnp.concatenate([jnp.broadcast_to(keep_f[g:g + 1, :], (per, TB))
                              for g in range(N_EXPERT_GROUPS)], axis=0)
    cm = jnp.where(keep_e > 0.5, choice, -jnp.inf)

    hits, idxs, gates = [], [], []
    onehot = jnp.zeros((E, TB), F32)
    for _ in range(TOP_K):
        m = jnp.max(cm, axis=0, keepdims=True)
        ei = jnp.min(jnp.where(cm == m, e_iota, E), axis=0, keepdims=True)
        hit = e_iota == ei
        hits.append(hit)
        idxs.append(ei)
        gates.append(jnp.sum(jnp.where(hit, scores, 0.0), axis=0, keepdims=True))
        onehot = onehot + jnp.where(hit, 1.0, 0.0)
        cm = jnp.where(hit, -jnp.inf, cm)
    gate = jnp.concatenate(gates, axis=0)
    gate = gate / jnp.sum(gate, axis=0, keepdims=True) * ROUTED_SCALE

    before = jnp.where(_iota((TB, TB), 0) < _iota((TB, TB), 1), 1.0, 0.0).astype(BF16)
    cnt = carry_ref[...] + _dot(onehot, before)
    pos = jnp.concatenate([jnp.sum(jnp.where(h, cnt, 0.0), axis=0, keepdims=True) for h in hits], axis=0)
    carry_ref[...] = carry_ref[...] + jnp.sum(onehot, axis=1, keepdims=True)

    idx_ref[...] = jnp.concatenate(idxs, axis=0)
    gate_ref[...] = gate
    pos_ref[...] = pos.astype(I32)
    cnt_ref[...] = jnp.broadcast_to(carry_ref[...], cnt_ref.shape)


def _router(x1, rw_t, rb):
    T = x1.shape[0]
    TB = TOK_BLK
    kt = pl.BlockSpec((TOP_K, TB), lambda i: (0, i))
    return pl.pallas_call(
        _router_kernel,
        grid=(T // TB,),
        in_specs=[pl.BlockSpec((TB, D_MODEL), lambda i: (i, 0)),
                  pl.BlockSpec((N_EXPERTS, D_MODEL), lambda i: (0, 0)),
                  pl.BlockSpec((N_EXPERTS, 1), lambda i: (0, 0))],
        out_specs=[kt, kt, kt, pl.BlockSpec((N_EXPERTS, 128), lambda i: (0, 0))],
        out_shape=[jax.ShapeDtypeStruct((TOP_K, T), I32), jax.ShapeDtypeStruct((TOP_K, T), F32),
                   jax.ShapeDtypeStruct((TOP_K, T), I32), jax.ShapeDtypeStruct((N_EXPERTS, 128), F32)],
        scratch_shapes=[pltpu.VMEM((N_EXPERTS, 1), F32)],
        compiler_params=_cparams(("arbitrary",)),
        name="router",
    )(x1, rw_t, rb)


def _slots_kernel(ps_ref, idx_ref, pos_ref, dest_ref):
    idx = idx_ref[...]

    def body(e, acc):
        return acc + jnp.where(idx == e, ps_ref[e], 0)

    dest_ref[...] = lax.fori_loop(0, N_EXPERTS, body, pos_ref[...], unroll=8)


def _slots(pad_start, idx, pos):
    T = idx.shape[1]
    tb = min(T, 2048)
    kt = pl.BlockSpec((TOP_K, tb), lambda i, ps: (0, i))
    return pl.pallas_call(
        _slots_kernel,
        grid_spec=pltpu.PrefetchScalarGridSpec(
            num_scalar_prefetch=1, grid=(T // tb,), in_specs=[kt, kt], out_specs=kt),
        out_shape=jax.ShapeDtypeStruct((TOP_K, T), I32),
        compiler_params=_cparams(("arbitrary",)),
        name="slots",
    )(pad_start, idx, pos)


ROW_T = D_MODEL // 128


def _to_tiles(dst_ref, base, x):
    n = x.shape[0]
    for s in range(ROW_T):
        dst_ref[pl.ds(base + s, n, stride=ROW_T), :] = x[:, 128 * s:128 * (s + 1)]


def _from_tiles(src_ref, base, n):
    return jnp.concatenate([src_ref[pl.ds(base + s, n, stride=ROW_T), :] for s in range(ROW_T)], axis=1)


def _tailzero_kernel(tb_ref, xs_ref):
    del tb_ref
    xs_ref[...] = jnp.zeros_like(xs_ref)


def _tailzero(tail_blk, nblk):
    rows = MOE_BLK * ROW_T
    return pl.pallas_call(
        _tailzero_kernel,
        grid_spec=pltpu.PrefetchScalarGridSpec(
            num_scalar_prefetch=1, grid=(N_EXPERTS,), in_specs=[],
            out_specs=pl.BlockSpec((rows, 128), lambda e, tb: (tb[e], 0))),
        out_shape=jax.ShapeDtypeStruct((nblk * rows, 128), F32),
        compiler_params=_cparams(("arbitrary",)),
        name="tailzero",
    )(tail_blk)


def _dispatch_kernel(dest_ref, x_ref, xs_in, xs_out, xt, sem):
    del xs_in
    TB = x_ref.shape[0]
    _to_tiles(xt, 0, x_ref[...])

    def issue(t, c):
        src = xt.at[pl.ds(pl.multiple_of(t * ROW_T, ROW_T), ROW_T), :]
        for k in range(TOP_K):
            d = dest_ref[t * TOP_K + k]
            pltpu.make_async_copy(src, xs_out.at[pl.ds(pl.multiple_of(d * ROW_T, ROW_T), ROW_T), :], sem).start()
        return c

    lax.fori_loop(0, TB, issue, 0, unroll=2)

    for k in range(TOP_K):
        pltpu.make_async_copy(xt, xs_out.at[pl.ds(0, TB * ROW_T), :], sem).wait()


def _dispatch(dest_flat, x1, xs0):
    T = x1.shape[0]
    TB = TOK_BLK
    return pl.pallas_call(
        _dispatch_kernel,
        grid=(T // TB,),
        in_specs=[pl.BlockSpec((TB * TOP_K,), lambda i: (i,), memory_space=pltpu.SMEM),
                  pl.BlockSpec((TB, D_MODEL), lambda i: (i, 0)),
                  pl.BlockSpec(memory_space=pl.ANY)],
        out_specs=pl.BlockSpec(memory_space=pl.ANY),
        out_shape=jax.ShapeDtypeStruct(xs0.shape, xs0.dtype),
        scratch_shapes=[pltpu.VMEM((TB * ROW_T, 128), F32), pltpu.SemaphoreType.DMA],
        input_output_aliases={2: 0},
        compiler_params=_cparams(("arbitrary",)),
        name="dispatch",
    )(dest_flat, x1, xs0)


def _experts_kernel(be_ref, nu_ref, xs_ref, wg_ref, wu_ref, wd_ref, ys_ref, wgu_s, wd_s):
    i = pl.program_id(0)

    @pl.when(i < nu_ref[0])
    def _():
        prev = be_ref[jnp.maximum(i - 1, 0)]

        @pl.when((i == 0) | (be_ref[i] != prev))
        def _():
            wgu_s[:, 0:D_EXPERT] = wg_ref[0, 0].astype(BF16)
            wgu_s[:, D_EXPERT:] = wu_ref[0, 0].astype(BF16)
            wd_s[...] = wd_ref[0, 0].astype(BF16)

        x = _from_tiles(xs_ref, 0, MOE_BLK).astype(BF16)
        h = jnp.dot(x, wgu_s[...], preferred_element_type=F32)
        a = _silu(h[:, 0:D_EXPERT]) * h[:, D_EXPERT:]
        _to_tiles(ys_ref, 0, jnp.dot(a.astype(BF16), wd_s[...], preferred_element_type=F32))


def _experts(layer, blk_expert, n_used, xs, wg, wu, wd):
    rows = MOE_BLK * ROW_T
    nblk = xs.shape[0] // rows
    row = lambda i, be, nu: (jnp.minimum(i, nu[0] - 1), 0)
    wsel = lambda i, be, nu: (layer, be[i], 0, 0)
    return pl.pallas_call(
        _experts_kernel,
        grid_spec=pltpu.PrefetchScalarGridSpec(
            num_scalar_prefetch=2,
            grid=(nblk,),
            in_specs=[pl.BlockSpec((rows, 128), row),
                      pl.BlockSpec((1, 1, D_MODEL, D_EXPERT), wsel),
                      pl.BlockSpec((1, 1, D_MODEL, D_EXPERT), wsel),
                      pl.BlockSpec((1, 1, D_EXPERT, D_MODEL), wsel)],
            out_specs=pl.BlockSpec((rows, 128), row),
            scratch_shapes=[pltpu.VMEM((D_MODEL, 2 * D_EXPERT), BF16),
                            pltpu.VMEM((D_EXPERT, D_MODEL), BF16)]),
        out_shape=jax.ShapeDtypeStruct(xs.shape, F32),
        compiler_params=_cparams(("arbitrary",)),
        name="experts",
    )(blk_expert, n_used, xs, wg, wu, wd)


def _combine_kernel(dest_ref, gate_ref, x_ref, ys_ref, sgu_ref, sd_ref, g_ref, b_ref, o_ref, buf, sem):
    TB = x_ref.shape[0]

    def issue(t, c):
        for k in range(TOP_K):
            d = dest_ref[t * TOP_K + k]
            pltpu.make_async_copy(ys_ref.at[pl.ds(pl.multiple_of(d * ROW_T, ROW_T), ROW_T), :],
                                  buf.at[pl.ds(pl.multiple_of((k * TB + t) * ROW_T, ROW_T), ROW_T), :], sem).start()
        return c

    lax.fori_loop(0, TB, issue, 0, unroll=2)

    x = x_ref[...]
    h = jnp.dot(x.astype(BF16), sgu_ref[...], preferred_element_type=F32)
    a = _silu(h[:, 0:D_EXPERT]) * h[:, D_EXPERT:]
    y = ALPHA * x + jnp.dot(a.astype(BF16), sd_ref[...], preferred_element_type=F32)

    pltpu.make_async_copy(ys_ref.at[pl.ds(0, TOP_K * TB * ROW_T), :], buf, sem).wait()

    gate = gate_ref[...]
    for k in range(TOP_K):
        y = y + gate[:, k:k + 1] * _from_tiles(buf, k * TB * ROW_T, TB)
    o_ref[...] = _layernorm(y, g_ref[...], b_ref[...])


def _combine(dest_flat, gate_t, x1, ys, sh_gu, sh_d, g, b):
    T = x1.shape[0]
    TB = CMB_BLK
    vec = pl.BlockSpec((1, D_MODEL), lambda i: (0, 0))
    return pl.pallas_call(
        _combine_kernel,
        grid=(T // TB,),
        in_specs=[pl.BlockSpec((TB * TOP_K,), lambda i: (i,), memory_space=pltpu.SMEM),
                  pl.BlockSpec((TB, TOP_K), lambda i: (i, 0)),
                  pl.BlockSpec((TB, D_MODEL), lambda i: (i, 0)),
                  pl.BlockSpec(memory_space=pl.ANY),
                  pl.BlockSpec((D_MODEL, 2 * D_EXPERT), lambda i: (0, 0)),
                  pl.BlockSpec((D_EXPERT, D_MODEL), lambda i: (0, 0)), vec, vec],
        out_specs=pl.BlockSpec((TB, D_MODEL), lambda i: (i, 0)),
        out_shape=jax.ShapeDtypeStruct((T, D_MODEL), F32),
        scratch_shapes=[pltpu.VMEM((TOP_K * TB * ROW_T, 128), F32), pltpu.SemaphoreType.DMA],
        compiler_params=_cparams(("arbitrary",)),
        name="combine_ln",
    )(dest_flat, gate_t, x1, ys, sh_gu, sh_d, g, b)


def _token_mixer(x2, B, S, w_p, gla_p, sgu_p, ssd_p, cos_t, sin_t, tb, w_out, ln_g, ln_b):
    s_gla, s_sgu, s_ssd, s_dil = _inproj(x2, w_p)
    ya = _gla(s_gla.reshape(B, S, W_GLA), *gla_p)
    yb = _sgu(s_sgu.reshape(B, S, W_SGU), *sgu_p)
    yc = _ssd(s_ssd.reshape(B, S, W_SSD), *ssd_p)
    yd = _dil(s_dil.reshape(B, S, W_DIL), cos_t, sin_t, tb)
    T = B * S
    flat = lambda y: y.reshape(T, GROUP_W)
    return _outproj(flat(ya), flat(yb), flat(yc), flat(yd), x2, w_out, ln_g, ln_b)


def _moe(layer, x1, rw_t, rb, wg, wu, wd, sh_gu, sh_d, ln_g, ln_b):
    T = x1.shape[0]
    idx, gate, pos, cnt = _router(x1, rw_t, rb)
    counts = cnt[:, 0].astype(I32)
    padded = (counts + MOE_BLK - 1) // MOE_BLK * MOE_BLK
    pad_end = jnp.cumsum(padded)
    pad_start = pad_end - padded
    dest = _slots(pad_start.astype(I32), idx, pos)
    dest_flat = dest.T.reshape(T * TOP_K)
    nblk = T * TOP_K // MOE_BLK + N_EXPERTS
    n_used = (pad_end[-1] // MOE_BLK).astype(I32)
    blk = jnp.minimum(jnp.arange(nblk, dtype=I32), n_used - 1) * MOE_BLK
    blk_expert = jnp.sum((pad_end[None, :] <= blk[:, None]).astype(I32), axis=1)
    blk_expert = jnp.minimum(blk_expert, N_EXPERTS - 1)
    tail_blk = jnp.maximum(pad_end // MOE_BLK - 1, 0).astype(I32)
    xs = _dispatch(dest_flat, x1, _tailzero(tail_blk, nblk))
    ys = _experts(layer, blk_expert, n_used.reshape(1), xs, wg, wu, wd)
    return _combine(dest_flat, gate.T, x1, ys, sh_gu, sh_d, ln_g, ln_b)


def _prep_w_in(w_in):
    o = np.cumsum((0, 128, 128, 256, 256, 16, 256, 256, 256, 512, 4, 256, 256, 256))
    c = lambda i: w_in[..., o[i]:o[i + 1]]
    xbc = c(8)
    x_cols = xbc[..., 0:256]
    per_head = lambda m: jnp.concatenate(
        [m[..., SSM_N * (h // (SSM_H // SSM_G)):SSM_N * (h // (SSM_H // SSM_G) + 1)] for h in range(SSM_H)], axis=-1)
    b_cols = per_head(xbc[..., 256:384])
    c_cols = per_head(xbc[..., 384:512])
    dt_cols = jnp.repeat(c(9), SSM_P, axis=-1)
    pad = jnp.zeros(w_in.shape[:-1] + (W_GLA - o[5],), w_in.dtype)
    cols = [c(0), c(1), c(2), c(3), c(4), pad, c(5), c(6), c(7), x_cols, b_cols, c_cols, dt_cols,
            c(10), c(11), c(12)]
    return jnp.concatenate(cols, axis=-1).astype(BF16)


def _per_head(v):
    rep = SSM_H // SSM_G
    xp = v[..., 0:256]
    pick = lambda m: jnp.concatenate([m[..., SSM_N * (h // rep):SSM_N * (h // rep + 1)] for h in range(SSM_H)], axis=-1)
    return jnp.concatenate([xp, pick(v[..., 256:384]), pick(v[..., 384:512])], axis=-1)


def kernel(x, positions, w_in, gla_w_gate, gla_b_gate, gla_norm_w, sgu_ln_g, sgu_ln_b, sgu_w, sgu_b, ssm_conv_w, ssm_conv_b, ssm_dt_bias, ssm_a_log, ssm_d, ssm_norm_w, w_out, ln1_g, ln1_b, router_w, router_bias, exp_w_gate, exp_w_up, exp_w_down, sh_w_gate, sh_w_up, sh_w_down, ln2_g, ln2_b):
    B, S, _ = x.shape
    L = w_in.shape[0]
    T = B * S

    inv_freq = ROPE_THETA ** (-jnp.arange(0, ROT_DIM, 2, dtype=F32) / ROT_DIM)
    ang = positions.astype(F32)[..., None] * inv_freq
    cos, sin = jnp.cos(ang), jnp.sin(ang)
    ones = jnp.ones((B, S, DIL_DH - ROT_DIM), F32)
    cos_t = jnp.tile(jnp.concatenate([cos, cos, ones], axis=-1), (1, 1, 2))
    sin_t = jnp.tile(jnp.concatenate([-sin, sin, 0.0 * ones], axis=-1), (1, 1, 2))
    tb = jnp.asarray(_branch_log_multiplicity(S))

    w_p = _prep_w_in(w_in)
    wg_p = jnp.pad(gla_w_gate, ((0, 0), (0, 128 - GLA_RANK), (0, 0)))
    row = lambda v: v.reshape(L, 1, -1)
    gla_nw = jnp.tile(gla_norm_w, (1, GLA_H))
    sgu_wc = jnp.transpose(sgu_w, (0, 2, 1, 3)).reshape(L, SGU_CHUNK, SGU_G * SGU_CHUNK)
    sgu_bt = jnp.repeat(jnp.transpose(sgu_b, (0, 2, 1)), GROUP_W // SGU_G, axis=-1)
    exp64 = lambda v: jnp.repeat(v, SSM_P, axis=-1)
    w_out_b = w_out.astype(BF16)
    rw_t = jnp.transpose(router_w, (0, 2, 1))
    sh_gu = jnp.concatenate([sh_w_gate, sh_w_up], axis=-1).astype(BF16)
    sh_d = sh_w_down.astype(BF16)

    x2 = x.reshape(T, D_MODEL)
    for l in range(L):
        gla_p = (wg_p[l], row(gla_b_gate)[l], row(gla_nw)[l])
        sgu_p = (sgu_wc[l], sgu_bt[l], row(sgu_ln_g)[l], row(sgu_ln_b)[l])
        ssd_p = (_per_head(ssm_conv_w[l]), row(_per_head(ssm_conv_b))[l], row(exp64(ssm_dt_bias))[l],
                 row(exp64(ssm_a_log))[l], row(exp64(ssm_d))[l], row(ssm_norm_w)[l])
        x1 = _token_mixer(x2, B, S, w_p[l], gla_p, sgu_p, ssd_p, cos_t, sin_t, tb,
                          w_out_b[l], row(ln1_g)[l], row(ln1_b)[l])
        x2 = _moe(l, x1, rw_t[l], router_bias[l].reshape(N_EXPERTS, 1), exp_w_gate, exp_w_up, exp_w_down,
                  sh_gu[l], sh_d[l], row(ln2_g)[l], row(ln2_b)[l])
    return x2.reshape(B, S, D_MODEL)
```

```python
import functools
import math

import numpy as np
import jax
import jax.numpy as jnp
from jax import lax
from jax.experimental import pallas as pl
from jax.experimental.pallas import tpu as pltpu

F32 = jnp.float32
BF16 = jnp.bfloat16
I32 = jnp.int32

D_MODEL = 1024
N_LAYERS = 4
GROUP_W = 256

GLA_H, GLA_DK, GLA_DV, GLA_RANK, GLA_TAU = 4, 32, 64, 16, 16.0
SGU_G, SGU_CHUNK = 4, 128
SSM_H, SSM_P, SSM_G, SSM_N, SSM_CONV, SSM_CHUNK = 4, 64, 2, 64, 4, 128
DIL_H, DIL_DH, ROT_DIM, ROPE_THETA = 4, 64, 16, 500000.0
DIL_BRANCHES = ((128, 1), (512, 4), (2048, 16))

N_EXPERTS, TOP_K, N_EXPERT_GROUPS, TOPK_GROUPS, D_EXPERT = 128, 8, 8, 4, 256
ROUTED_SCALE = 1.0

ALPHA = (2 * N_LAYERS) ** 0.25
LN_EPS = 1e-5
RMS_EPS = 1e-6

W_GLA = 896
W_SGU = 512
W_SSD = 1280
W_DIL = 768
W_PROJ = W_GLA + W_SGU + W_SSD + W_DIL

BLK = 128
QBLK = 256
MOE_BLK = 256
TOK_BLK = 256
CMB_BLK = 128
NEG = -1e30

VMEM_LIMIT = 56 * 1024 * 1024


def _cparams(sem):
    return pltpu.CompilerParams(dimension_semantics=sem, vmem_limit_bytes=VMEM_LIMIT)


def _dot(a, b):
    return jnp.dot(a.astype(BF16), b.astype(BF16), preferred_element_type=F32)


def _dot_nt(a, b):
    return lax.dot_general(a.astype(BF16), b.astype(BF16), (((1,), (1,)), ((), ())),
                           preferred_element_type=F32)


def _dot_tn(a, b):
    return lax.dot_general(a.astype(BF16), b.astype(BF16), (((0,), (0,)), ((), ())),
                           preferred_element_type=F32)


def _split(a, n):
    out = []
    r = a
    for _ in range(n):
        p = r.astype(BF16)
        out.append(p)
        r = r - p.astype(F32)
    return out


def _dot_hi(a, b):
    a1, a2 = _split(a, 2)
    b1, b2 = _split(b, 2)
    return _dot(a1, b1) + (_dot(a1, b2) + _dot(a2, b1))


def _dot_nt_hi(a, b):
    a1, a2 = _split(a, 2)
    b1, b2 = _split(b, 2)
    return _dot_nt(a1, b1) + (_dot_nt(a1, b2) + _dot_nt(a2, b1))


def _dot_lhs01(m, b):
    b1, b2, b3 = _split(b, 3)
    return _dot(m, b1) + (_dot(m, b2) + _dot(m, b3))


def _dot_rhs01(a, m):
    a1, a2, a3 = _split(a, 3)
    return _dot(a1, m) + (_dot(a2, m) + _dot(a3, m))


def _iota(shape, dim):
    return lax.broadcasted_iota(I32, shape, dim)


def _sigmoid(x):
    return 1.0 / (1.0 + jnp.exp(-x))


def _silu(x):
    return x * _sigmoid(x)


def _softplus(x):
    return jnp.maximum(x, 0.0) + jnp.log1p(jnp.exp(-jnp.abs(x)))


def _gelu(x):
    return 0.5 * x * (1.0 + lax.erf(x * (2.0 ** -0.5)))


ROW_T = D_MODEL // 128


def _to_tiles(dst_ref, base, x, first=0):
    n = x.shape[0]
    for s in range(x.shape[1] // 128):
        dst_ref[pl.ds(base + first + s, n, stride=ROW_T), :] = x[:, 128 * s:128 * (s + 1)]


def _from_tiles(src_ref, base, n, pitch=ROW_T):
    return jnp.concatenate([src_ref[pl.ds(base + s, n, stride=pitch), :] for s in range(ROW_T)], axis=1)


def _inproj_kernel(x_ref, w_ref, o_gla, o_sgu, o_ssd, o_dil):
    xb = x_ref[...].astype(BF16)
    off = 0
    for o in (o_gla, o_sgu, o_ssd, o_dil):
        n = o.shape[1]
        o[...] = jnp.dot(xb, w_ref[:, off:off + n], preferred_element_type=F32)
        off += n


def _inproj(x2, w_p):
    T = x2.shape[0]
    tm = 512
    widths = (W_GLA, W_SGU, W_SSD, W_DIL)
    return pl.pallas_call(
        _inproj_kernel,
        grid=(T // tm,),
        in_specs=[pl.BlockSpec((tm, D_MODEL), lambda i: (i, 0)),
                  pl.BlockSpec((D_MODEL, W_PROJ), lambda i: (0, 0))],
        out_specs=[pl.BlockSpec((tm, w), lambda i: (i, 0)) for w in widths],
        out_shape=[jax.ShapeDtypeStruct((T, w), F32) for w in widths],
        compiler_params=_cparams(("arbitrary",)),
        name="inproj",
    )(x2, w_p)


def _gla_kernel(slab_ref, wg_ref, bg_ref, nw_ref, out_ref, st_ref):
    S = slab_ref.shape[1]
    st_ref[...] = jnp.zeros_like(st_ref)
    row = _iota((BLK, BLK), 0)
    col = _iota((BLK, BLK), 1)
    tril = col <= row
    tril_bf = jnp.where(tril, 1.0, 0.0).astype(BF16)
    qk_head = col >> 5
    v_head = _iota((BLK, GROUP_W), 1) >> 6
    st_diag = (_iota((GROUP_W, BLK), 0) >> 6) == (_iota((GROUP_W, BLK), 1) >> 5)
    seg = jnp.where((_iota((GROUP_W, GROUP_W), 0) >> 6) == (_iota((GROUP_W, GROUP_W), 1) >> 6),
                    1.0, 0.0).astype(BF16)
    wg = wg_ref[...]
    bg = bg_ref[...]
    nw = nw_ref[...]

    def body(n, carry):
        r0 = pl.multiple_of(n * BLK, BLK)
        blk = slab_ref[0, pl.ds(r0, BLK), :]
        q = blk[:, 0:128] * (GLA_DK ** -0.5)
        k = blk[:, 128:256]
        v = blk[:, 256:512]
        r = blk[:, 512:768]
        lr = blk[:, 768:896]
        z = _dot_hi(lr, wg) + bg
        gk = (jnp.minimum(z, 0.0) - jnp.log1p(jnp.exp(-jnp.abs(z)))) * (1.0 / GLA_TAU)
        g = _dot_lhs01(tril_bf, gk)
        g_last = g[BLK - 1:BLK, :]
        q_in = q * jnp.exp(g)
        k_out = k * jnp.exp(-g)
        k_in = k * jnp.exp(g_last - g)
        vb = v.astype(BF16)
        o = _dot_nt(q_in, st_ref[...])
        kob = k_out.astype(BF16)
        for h in range(GLA_H):
            qh = jnp.where(qk_head == h, q_in, 0.0)
            a = jnp.where(tril, _dot_nt(qh, kob), 0.0)
            o = o + jnp.where(v_head == h, _dot(a, vb), 0.0)
        st_ref[...] = st_ref[...] * jnp.exp(g_last) + jnp.where(st_diag, _dot_tn(vb, k_in), 0.0)
        ms = _dot_rhs01(o * o, seg) * (1.0 / GLA_DV)
        y = o * lax.rsqrt(ms + RMS_EPS) * nw * _silu(r)
        out_ref[0, pl.ds(r0, BLK), :] = y.astype(out_ref.dtype)
        return carry

    lax.fori_loop(0, S // BLK, body, 0)


def _gla(slab, wg_p, bg, nw_t):
    B, S, _ = slab.shape
    return pl.pallas_call(
        _gla_kernel,
        grid=(B,),
        in_specs=[pl.BlockSpec((1, S, W_GLA), lambda b: (b, 0, 0)),
                  pl.BlockSpec((128, 128), lambda b: (0, 0)),
                  pl.BlockSpec((1, 128), lambda b: (0, 0)),
                  pl.BlockSpec((1, GROUP_W), lambda b: (0, 0))],
        out_specs=pl.BlockSpec((1, S, GROUP_W), lambda b: (b, 0, 0)),
        out_shape=jax.ShapeDtypeStruct((B, S, GROUP_W), BF16),
        scratch_shapes=[pltpu.VMEM((GROUP_W, 128), F32)],
        compiler_params=_cparams(("arbitrary",)),
        name="gla",
    )(slab, wg_p, bg, nw_t)


def _sgu_kernel(slab_ref, w_ref, bias_ref, g_ref, b_ref, out_ref):
    S = slab_ref.shape[1]
    wmask = (_iota((BLK, SGU_G * BLK), 1) & (BLK - 1)) <= _iota((BLK, SGU_G * BLK), 0)
    w = jnp.where(wmask, w_ref[...], 0.0).astype(BF16)
    lane_grp = _iota((BLK, GROUP_W), 1) >> 6
    bias = bias_ref[...]
    ln_g = g_ref[...]
    ln_b = b_ref[...]

    def body(n, carry):
        r0 = pl.multiple_of(n * BLK, BLK)
        blk = slab_ref[0, pl.ds(r0, BLK), :]
        u = _gelu(blk[:, 0:256])
        v = _gelu(blk[:, 256:512])
        mu = jnp.mean(v, axis=-1, keepdims=True)
        var = jnp.mean(jnp.square(v - mu), axis=-1, keepdims=True)
        v = (v - mu) * lax.rsqrt(var + LN_EPS) * ln_g + ln_b
        vexp = jnp.concatenate([jnp.where(lane_grp == g, v, 0.0) for g in range(SGU_G)], axis=0)
        s = _dot(w, vexp) + bias
        out_ref[0, pl.ds(r0, BLK), :] = (u * s).astype(out_ref.dtype)
        return carry

    lax.fori_loop(0, S // BLK, body, 0)


def _sgu(slab, w_cat, bias_t, ln_g, ln_b):
    B, S, _ = slab.shape
    return pl.pallas_call(
        _sgu_kernel,
        grid=(B,),
        in_specs=[pl.BlockSpec((1, S, W_SGU), lambda b: (b, 0, 0)),
                  pl.BlockSpec((BLK, SGU_G * BLK), lambda b: (0, 0)),
                  pl.BlockSpec((BLK, GROUP_W), lambda b: (0, 0)),
                  pl.BlockSpec((1, GROUP_W), lambda b: (0, 0)),
                  pl.BlockSpec((1, GROUP_W), lambda b: (0, 0))],
        out_specs=pl.BlockSpec((1, S, GROUP_W), lambda b: (b, 0, 0)),
        out_shape=jax.ShapeDtypeStruct((B, S, GROUP_W), BF16),
        compiler_params=_cparams(("arbitrary",)),
        name="sgu",
    )(slab, w_cat, bias_t, ln_g, ln_b)


def _ssd_kernel(slab_ref, cw_ref, cb_ref, dtb_ref, a_ref, d_ref, nw_ref, out_ref, xpad_ref, st_ref):
    S = slab_ref.shape[1]
    CW = 3 * GROUP_W
    st_ref[...] = jnp.zeros_like(st_ref)
    xpad_ref[0:8, :] = jnp.zeros((8, CW), F32)
    xpad_ref[8:S + 8, :] = slab_ref[0, :, 256:256 + CW]
    row = _iota((BLK, BLK), 0)
    col = _iota((BLK, BLK), 1)
    tril = col <= row
    tril_bf = jnp.where(tril, 1.0, 0.0).astype(BF16)
    lane_head = _iota((BLK, GROUP_W), 1) >> 6
    st_diag = (_iota((GROUP_W, GROUP_W), 0) >> 6) == (_iota((GROUP_W, GROUP_W), 1) >> 6)
    cw = cw_ref[...]
    cb = cb_ref[...]
    dtb = dtb_ref[...]
    a_neg = -jnp.exp(a_ref[...])
    d_skip = d_ref[...]
    nw = nw_ref[...]

    def body(n, carry):
        r0 = pl.multiple_of(n * BLK, BLK)
        xw = xpad_ref[pl.ds(r0, BLK + 8), :]
        conv = cb + cw[0:1, :] * xw[5:5 + BLK, :]
        for w in range(1, SSM_CONV):
            conv = conv + cw[w:w + 1, :] * xw[5 + w:5 + w + BLK, :]
        xbc = _silu(conv)
        xs = xbc[:, 0:256]
        b_e = xbc[:, 256:512]
        c_e = xbc[:, 512:768]
        z = slab_ref[0, pl.ds(r0, BLK), 0:256]
        dt = _softplus(slab_ref[0, pl.ds(r0, BLK), 1024:1280] + dtb)
        acs = _dot_lhs01(tril_bf, dt * a_neg)
        acs_last = acs[BLK - 1:BLK, :]
        x_dt = (xs * dt).astype(BF16)
        acs_t = [jnp.transpose(acs[:, 0:128]), jnp.transpose(acs[:, 128:256])]
        y = _dot(c_e * jnp.exp(acs), st_ref[...])
        b_bf = b_e.astype(BF16)
        for h in range(SSM_H):
            a_col = jnp.broadcast_to(acs[:, 64 * h:64 * h + 1], (BLK, BLK))
            a_row = jnp.broadcast_to(acs_t[h // 2][64 * (h % 2):64 * (h % 2) + 1, :], (BLK, BLK))
            lm = jnp.exp(jnp.where(tril, a_col - a_row, -jnp.inf))
            ch = jnp.where(lane_head == h, c_e, 0.0)
            m = _dot_nt(ch, b_bf) * lm
            y = y + jnp.where(lane_head == h, _dot(m, x_dt), 0.0)
        new = _dot_tn(b_e * jnp.exp(acs_last - acs), x_dt)
        st_ref[...] = st_ref[...] * jnp.exp(acs_last) + jnp.where(st_diag, new, 0.0)
        y = (y + d_skip * xs) * _silu(z)
        parts = []
        for g in range(SSM_G):
            yg = y[:, 128 * g:128 * (g + 1)]
            ms = jnp.mean(yg * yg, axis=-1, keepdims=True)
            parts.append(yg * lax.rsqrt(ms + RMS_EPS))
        out = jnp.concatenate(parts, axis=1) * nw
        out_ref[0, pl.ds(r0, BLK), :] = out.astype(out_ref.dtype)
        return carry

    lax.fori_loop(0, S // BLK, body, 0)


def _ssd(slab, cw_e, cb_e, dtb_e, a_e, d_e, nw):
    B, S, _ = slab.shape
    CW = 3 * GROUP_W
    vec = lambda w: pl.BlockSpec((1, w), lambda b: (0, 0))
    return pl.pallas_call(
        _ssd_kernel,
        grid=(B,),
        in_specs=[pl.BlockSpec((1, S, W_SSD), lambda b: (b, 0, 0)),
                  pl.BlockSpec((SSM_CONV, CW), lambda b: (0, 0)),
                  vec(CW), vec(GROUP_W), vec(GROUP_W), vec(GROUP_W), vec(GROUP_W)],
        out_specs=pl.BlockSpec((1, S, GROUP_W), lambda b: (b, 0, 0)),
        out_shape=jax.ShapeDtypeStruct((B, S, GROUP_W), BF16),
        scratch_shapes=[pltpu.VMEM((S + 8, CW), F32), pltpu.VMEM((GROUP_W, GROUP_W), F32)],
        compiler_params=_cparams(("arbitrary",)),
        name="ssd",
    )(slab, cw_e, cb_e, dtb_e, a_e, d_e, nw)


def _branch_log_multiplicity(S):
    nq = S // QBLK
    r = np.arange(QBLK)[:, None]
    u = np.arange(S)[None, :]
    d = (nq - 1) * QBLK + r - u
    c = np.zeros(d.shape, np.float64)
    for window, dil in DIL_BRANCHES:
        c += (d >= 0) & (d % dil == 0) & (d <= window)
    with np.errstate(divide="ignore"):
        return np.where(c > 0, np.log(np.maximum(c, 1.0)), NEG).astype(np.float32)


def _dil_kernel(slab_ref, cos_ref, sin_ref, tb_ref, out_ref, q_ref, k_ref, v_ref):
    S = slab_ref.shape[1]
    nq = S // QBLK
    lane_d = _iota((1, 128), 1) & (DIL_DH - 1)

    def rope(t):
        fwd = pltpu.roll(t, 128 - ROT_DIM // 2, 1)
        bwd = pltpu.roll(t, ROT_DIM // 2, 1)
        return t * cos_ref[0] + jnp.where(lane_d < ROT_DIM // 2, fwd, bwd) * sin_ref[0]

    for c in range(0, GROUP_W, 128):
        q_ref[:, c:c + 128] = (rope(slab_ref[0, :, c:c + 128]) * (DIL_DH ** -0.5)).astype(BF16)
        k_ref[:, c:c + 128] = rope(slab_ref[0, :, 256 + c:384 + c]).astype(BF16)
    v_ref[...] = slab_ref[0, :, 512:768].astype(BF16)
    lane_head = _iota((QBLK, GROUP_W), 1) >> 6
    for i in range(nq):
        nk = (i + 1) * QBLK
        qi = q_ref[i * QBLK:(i + 1) * QBLK, :]
        acc = jnp.zeros((QBLK, GROUP_W), F32)
        for h in range(DIL_H):
            qh = jnp.where(lane_head == h, qi, jnp.zeros_like(qi))
            s = _dot_nt(qh, k_ref[0:nk, :]) + tb_ref[:, (nq - 1 - i) * QBLK:]
            m = jnp.max(s, axis=-1, keepdims=True)
            p = jnp.exp(s - m)
            den = jnp.sum(p, axis=-1, keepdims=True)
            o = _dot(p, v_ref[0:nk, :])
            acc = acc + jnp.where(lane_head == h, o / den, 0.0)
        out_ref[0, i * QBLK:(i + 1) * QBLK, :] = acc.astype(out_ref.dtype)


def _dil(slab, cos_t, sin_t, tb):
    B, S, _ = slab.shape
    return pl.pallas_call(
        _dil_kernel,
        grid=(B,),
        in_specs=[pl.BlockSpec((1, S, W_DIL), lambda b: (b, 0, 0)),
                  pl.BlockSpec((1, S, 128), lambda b: (b, 0, 0)),
                  pl.BlockSpec((1, S, 128), lambda b: (b, 0, 0)),
                  pl.BlockSpec((QBLK, S), lambda b: (0, 0), pipeline_mode=pl.Buffered(1))],
        out_specs=pl.BlockSpec((1, S, GROUP_W), lambda b: (b, 0, 0)),
        out_shape=jax.ShapeDtypeStruct((B, S, GROUP_W), BF16),
        scratch_shapes=[pltpu.VMEM((S, GROUP_W), BF16)] * 3,
        compiler_params=_cparams(("arbitrary",)),
        name="dil",
    )(slab, cos_t, sin_t, tb)


def _layernorm(x, g, b):
    mu = jnp.mean(x, axis=-1, keepdims=True)
    var = jnp.mean(jnp.square(x - mu), axis=-1, keepdims=True)
    return (x - mu) * lax.rsqrt(var + LN_EPS) * g + b


def _outproj_kernel(ya, yb, yc, yd, x_ref, w_ref, g_ref, b_ref, o_ref, ot_ref):
    y = jnp.concatenate([ya[...], yb[...], yc[...], yd[...]], axis=1)
    y = jnp.dot(y, w_ref[...], preferred_element_type=F32)
    x1 = _layernorm(ALPHA * x_ref[...] + y, g_ref[...], b_ref[...])
    o_ref[...] = x1
    _to_tiles(ot_ref, 0, x1)


def _outproj(ya, yb, yc, yd, x2, w_out, g, b):
    T = x2.shape[0]
    tm = 512
    ysp = pl.BlockSpec((tm, GROUP_W), lambda i: (i, 0))
    vec = pl.BlockSpec((1, D_MODEL), lambda i: (0, 0))
    return pl.pallas_call(
        _outproj_kernel,
        grid=(T // tm,),
        in_specs=[ysp, ysp, ysp, ysp,
                  pl.BlockSpec((tm, D_MODEL), lambda i: (i, 0)),
                  pl.BlockSpec((D_MODEL, D_MODEL), lambda i: (0, 0)), vec, vec],
        out_specs=[pl.BlockSpec((tm, D_MODEL), lambda i: (i, 0)),
                   pl.BlockSpec((tm * ROW_T, 128), lambda i: (i, 0))],
        out_shape=[jax.ShapeDtypeStruct((T, D_MODEL), F32),
                   jax.ShapeDtypeStruct((T * ROW_T, 128), F32)],
        compiler_params=_cparams(("arbitrary",)),
        name="outproj_ln",
    )(ya, yb, yc, yd, x2, w_out, g, b)


def _router_kernel(x_ref, rw_ref, rb_ref, idx_ref, gate_ref, pos_ref, cnt_ref, carry_ref):
    TB = x_ref.shape[0]
    E = N_EXPERTS
    per = E // N_EXPERT_GROUPS

    @pl.when(pl.program_id(0) == 0)
    def _():
        carry_ref[...] = jnp.zeros_like(carry_ref)

    logits = _dot_nt_hi(rw_ref[...], x_ref[...])
    scores = _sigmoid(logits)
    choice = scores + rb_ref[...]
    e_iota = _iota((E, TB), 0)

    l_iota = _iota((per, TB), 0)
    grp = []
    for g in range(N_EXPERT_GROUPS):
        cg = choice[per * g:per * (g + 1), :]
        m1 = jnp.max(cg, axis=0, keepdims=True)
        i1 = jnp.min(jnp.where(cg == m1, l_iota, per), axis=0, keepdims=True)
        m2 = jnp.max(jnp.where(l_iota == i1, -jnp.inf, cg), axis=0, keepdims=True)
        grp.append(m1 + m2)
    gs = jnp.concatenate(grp, axis=0)
    g_iota = _iota((N_EXPERT_GROUPS, TB), 0)
    keep = jnp.zeros((N_EXPERT_GROUPS, TB), jnp.bool_)
    for _ in range(TOPK_GROUPS):
        m = jnp.max(gs, axis=0, keepdims=True)
        gi = jnp.min(jnp.where(gs == m, g_iota, N_EXPERT_GROUPS), axis=0, keepdims=True)
        hit = g_iota == gi
        keep = keep | hit
        gs = jnp.where(hit, -jnp.inf, gs)
    keep_f = jnp.where(keep, 1.0, 0.0)
    keep_e = jnp.concatenate([jnp.broadcast_to(keep_f[g:g + 1, :], (per, TB))
                              for g in range(N_EXPERT_GROUPS)], axis=0)
    cm = jnp.where(keep_e > 0.5, choice, -jnp.inf)

    hits, idxs, gates = [], [], []
    onehot = jnp.zeros((E, TB), F32)
    for _ in range(TOP_K):
        m = jnp.max(cm, axis=0, keepdims=True)
        ei = jnp.min(jnp.where(cm == m, e_iota, E), axis=0, keepdims=True)
        hit = e_iota == ei
        hits.append(hit)
        idxs.append(ei)
        gates.append(jnp.sum(jnp.where(hit, scores, 0.0), axis=0, keepdims=True))
        onehot = onehot + jnp.where(hit, 1.0, 0.0)
        cm = jnp.where(hit, -jnp.inf, cm)
    gate = jnp.concatenate(gates, axis=0)
    gate = gate / jnp.sum(gate, axis=0, keepdims=True) * ROUTED_SCALE

    before = jnp.where(_iota((TB, TB), 0) < _iota((TB, TB), 1), 1.0, 0.0).astype(BF16)
    cnt = carry_ref[...] + _dot(onehot, before)
    pos = jnp.concatenate([jnp.sum(jnp.where(h, cnt, 0.0), axis=0, keepdims=True) for h in hits], axis=0)
    carry_ref[...] = carry_ref[...] + jnp.sum(onehot, axis=1, keepdims=True)

    idx_ref[...] = jnp.concatenate(idxs, axis=0)
    gate_ref[...] = gate
    pos_ref[...] = pos.astype(I32)
    cnt_ref[...] = jnp.broadcast_to(carry_ref[...], cnt_ref.shape)


def _router(x1, rw_t, rb):
    T = x1.shape[0]
    TB = TOK_BLK
    kt = pl.BlockSpec((TOP_K, TB), lambda i: (0, i))
    return pl.pallas_call(
        _router_kernel,
        grid=(T // TB,),
        in_specs=[pl.BlockSpec((TB, D_MODEL), lambda i: (i, 0)),
                  pl.BlockSpec((N_EXPERTS, D_MODEL), lambda i: (0, 0)),
                  pl.BlockSpec((N_EXPERTS, 1), lambda i: (0, 0))],
        out_specs=[kt, kt, kt, pl.BlockSpec((N_EXPERTS, 128), lambda i: (0, 0))],
        out_shape=[jax.ShapeDtypeStruct((TOP_K, T), I32), jax.ShapeDtypeStruct((TOP_K, T), F32),
                   jax.ShapeDtypeStruct((TOP_K, T), I32), jax.ShapeDtypeStruct((N_EXPERTS, 128), F32)],
        scratch_shapes=[pltpu.VMEM((N_EXPERTS, 1), F32)],
        compiler_params=_cparams(("arbitrary",)),
        name="router",
    )(x1, rw_t, rb)


def _slots_kernel(ps_ref, idx_ref, pos_ref, dest_ref):
    idx = idx_ref[...]

    def body(e, acc):
        return acc + jnp.where(idx == e, ps_ref[e], 0)

    dest_ref[...] = lax.fori_loop(0, N_EXPERTS, body, pos_ref[...], unroll=8)


def _slots(pad_start, idx, pos):
    T = idx.shape[1]
    tb = min(T, 2048)
    kt = pl.BlockSpec((TOP_K, tb), lambda i, ps: (0, i))
    return pl.pallas_call(
        _slots_kernel,
        grid_spec=pltpu.PrefetchScalarGridSpec(
            num_scalar_prefetch=1, grid=(T // tb,), in_specs=[kt, kt], out_specs=kt),
        out_shape=jax.ShapeDtypeStruct((TOP_K, T), I32),
        compiler_params=_cparams(("arbitrary",)),
        name="slots",
    )(pad_start, idx, pos)


ORD_CHUNK = 2048
ISSUE_PHASES = 4


def _experts_kernel(be_ref, r0_ref, nv_ref, ord_hbm, x_hbm, wg_ref, wu_ref, wd_ref, yk_hbm,
                    ord_s, xbuf, ybuf, xb_s, h_s, a_s, wgu_s, wd_s, fsem, gsem, ssem):
    i = pl.program_id(0)
    nb = pl.num_programs(0)
    per_phase = MOE_BLK // ISSUE_PHASES

    def fetch(j):
        c0 = pl.multiple_of((r0_ref[j] >> 10) << 10, 1024)
        return pltpu.make_async_copy(ord_hbm.at[pl.ds(c0, ORD_CHUNK)], ord_s.at[j & 3], fsem.at[j & 3])

    def tile(ref, r):
        return ref.at[pl.ds(pl.multiple_of(r * ROW_T, ROW_T), ROW_T), :]

    def gather_rows(j, lo, hi):
        base = r0_ref[j] & 1023

        def body(r, c):
            a = ord_s[j & 3, base + r]
            pltpu.make_async_copy(tile(x_hbm, a >> 3), tile(xbuf.at[j & 1], r), gsem.at[j & 1]).start()
            return c

        lax.fori_loop(lo, hi, body, 0)

    def scatter_rows(j, lo, hi):
        base = r0_ref[j] & 1023

        def body(r, c):
            a = ord_s[j & 3, base + r]
            pltpu.make_async_copy(tile(ybuf.at[j & 1], r), tile(yk_hbm, a), ssem.at[j & 1]).start(priority=1)
            return c

        lax.fori_loop(lo, hi, body, 0)

    def wait_rows(sem, n):
        def body(r, c):
            pltpu.make_async_copy(tile(x_hbm, 0), tile(xbuf.at[0], 0), sem).wait()
            return c

        lax.fori_loop(0, n, body, 0)

    @pl.when(i == 0)
    def _():
        xbuf[...] = jnp.zeros_like(xbuf)
        fetch(0).start()
        fetch(0).wait()

        @pl.when(nb > 1)
        def _():
            fetch(1).start()

        gather_rows(0, 0, nv_ref[0])

    @pl.when(i + 1 < nb)
    def _():
        fetch(i + 1).wait()

    @pl.when(i + 2 < nb)
    def _():
        fetch(i + 2).start()

    wait_rows(gsem.at[i & 1], nv_ref[i])

    @pl.when(i >= 2)
    def _():
        wait_rows(ssem.at[i & 1], nv_ref[jnp.maximum(i - 2, 0)])

    @pl.when((i == 0) | (be_ref[i] != be_ref[jnp.maximum(i - 1, 0)]))
    def _():
        wgu_s[:, 0:D_EXPERT] = wg_ref[0, 0].astype(BF16)
        wgu_s[:, D_EXPERT:] = wu_ref[0, 0].astype(BF16)
        wd_s[...] = wd_ref[0, 0].astype(BF16)

    jn = jnp.minimum(i + 1, nb - 1)
    jp = jnp.maximum(i - 1, 0)
    nvn = jnp.where(i + 1 < nb, nv_ref[jn], 0)
    nvp = jnp.where(i >= 1, nv_ref[jp], 0)
    active = nv_ref[i] > 0
    xslot = xbuf.at[i & 1]
    yslot = ybuf.at[i & 1]
    half = D_MODEL // 2

    def phase(p):
        if p == 0:
            xb_s[...] = _from_tiles(xslot, 0, MOE_BLK).astype(BF16)
            h_s[:, 0:D_EXPERT] = jnp.dot(xb_s[...], wgu_s[:, 0:D_EXPERT], preferred_element_type=F32)
        elif p == 1:
            h_s[:, D_EXPERT:] = jnp.dot(xb_s[...], wgu_s[:, D_EXPERT:], preferred_element_type=F32)
        elif p == 2:
            a_s[...] = (_silu(h_s[:, 0:D_EXPERT]) * h_s[:, D_EXPERT:]).astype(BF16)
            _to_tiles(yslot, 0, jnp.dot(a_s[...], wd_s[:, 0:half], preferred_element_type=F32))
        else:
            _to_tiles(yslot, 0, jnp.dot(a_s[...], wd_s[:, half:], preferred_element_type=F32), first=ROW_T // 2)

    for p in range(ISSUE_PHASES):
        lo, hi = p * per_phase, (p + 1) * per_phase
        gather_rows(jn, jnp.minimum(lo, nvn), jnp.minimum(hi, nvn))
        scatter_rows(jp, jnp.minimum(lo, nvp), jnp.minimum(hi, nvp))
        pl.when(active)(functools.partial(phase, p))


def _experts(layer, blk_expert, rank0, nvalid, order, x1t, wg, wu, wd):
    nb = blk_expert.shape[0]
    rows = MOE_BLK * ROW_T
    wsel = lambda i, be, r0, nv: (layer, be[i], 0, 0)
    any_spec = pl.BlockSpec(memory_space=pl.ANY)
    return pl.pallas_call(
        _experts_kernel,
        grid_spec=pltpu.PrefetchScalarGridSpec(
            num_scalar_prefetch=3,
            grid=(nb,),
            in_specs=[any_spec, any_spec,
                      pl.BlockSpec((1, 1, D_MODEL, D_EXPERT), wsel),
                      pl.BlockSpec((1, 1, D_MODEL, D_EXPERT), wsel),
                      pl.BlockSpec((1, 1, D_EXPERT, D_MODEL), wsel)],
            out_specs=any_spec,
            scratch_shapes=[pltpu.SMEM((4, ORD_CHUNK), I32),
                            pltpu.VMEM((2, rows, 128), F32),
                            pltpu.VMEM((2, rows, 128), F32),
                            pltpu.VMEM((MOE_BLK, D_MODEL), BF16),
                            pltpu.VMEM((MOE_BLK, 2 * D_EXPERT), F32),
                            pltpu.VMEM((MOE_BLK, D_EXPERT), BF16),
                            pltpu.VMEM((D_MODEL, 2 * D_EXPERT), BF16),
                            pltpu.VMEM((D_EXPERT, D_MODEL), BF16),
                            pltpu.SemaphoreType.DMA((4,)),
                            pltpu.SemaphoreType.DMA((2,)),
                            pltpu.SemaphoreType.DMA((2,))]),
        out_shape=jax.ShapeDtypeStruct((x1t.shape[0] * TOP_K, 128), F32),
        compiler_params=_cparams(("arbitrary",)),
        name="experts",
    )(blk_expert, rank0, nvalid, order, x1t, wg, wu, wd)


def _combine_kernel(gate_ref, x_ref, yk_ref, sgu_ref, sd_ref, g_ref, b_ref, o_ref):
    TB = x_ref.shape[0]
    x = x_ref[...]
    h = jnp.dot(x.astype(BF16), sgu_ref[...], preferred_element_type=F32)
    a = _silu(h[:, 0:D_EXPERT]) * h[:, D_EXPERT:]
    y = ALPHA * x + jnp.dot(a.astype(BF16), sd_ref[...], preferred_element_type=F32)
    gate = gate_ref[...]
    for k in range(TOP_K):
        y = y + gate[:, k:k + 1] * _from_tiles(yk_ref, k * ROW_T, TB, pitch=TOP_K * ROW_T)
    o_ref[...] = _layernorm(y, g_ref[...], b_ref[...])


def _combine(gate_t, x1, yk, sh_gu, sh_d, g, b):
    T = x1.shape[0]
    TB = CMB_BLK
    vec = pl.BlockSpec((1, D_MODEL), lambda i: (0, 0))
    return pl.pallas_call(
        _combine_kernel,
        grid=(T // TB,),
        in_specs=[pl.BlockSpec((TB, TOP_K), lambda i: (i, 0)),
                  pl.BlockSpec((TB, D_MODEL), lambda i: (i, 0)),
                  pl.BlockSpec((TB * TOP_K * ROW_T, 128), lambda i: (i, 0)),
                  pl.BlockSpec((D_MODEL, 2 * D_EXPERT), lambda i: (0, 0)),
                  pl.BlockSpec((D_EXPERT, D_MODEL), lambda i: (0, 0)), vec, vec],
        out_specs=pl.BlockSpec((TB, D_MODEL), lambda i: (i, 0)),
        out_shape=jax.ShapeDtypeStruct((T, D_MODEL), F32),
        compiler_params=_cparams(("arbitrary",)),
        name="combine_ln",
    )(gate_t, x1, yk, sh_gu, sh_d, g, b)


def _token_mixer(x2, B, S, w_p, gla_p, sgu_p, ssd_p, cos_t, sin_t, tb, w_out, ln_g, ln_b):
    s_gla, s_sgu, s_ssd, s_dil = _inproj(x2, w_p)
    ya = _gla(s_gla.reshape(B, S, W_GLA), *gla_p)
    yb = _sgu(s_sgu.reshape(B, S, W_SGU), *sgu_p)
    yc = _ssd(s_ssd.reshape(B, S, W_SSD), *ssd_p)
    yd = _dil(s_dil.reshape(B, S, W_DIL), cos_t, sin_t, tb)
    T = B * S
    flat = lambda y: y.reshape(T, GROUP_W)
    return _outproj(flat(ya), flat(yb), flat(yc), flat(yd), x2, w_out, ln_g, ln_b)


def _moe(layer, x1, x1t, rw_t, rb, wg, wu, wd, sh_gu, sh_d, ln_g, ln_b):
    T = x1.shape[0]
    idx, gate, pos, cnt = _router(x1, rw_t, rb)
    counts = cnt[:, 0].astype(I32)
    start = jnp.cumsum(counts) - counts
    rank = _slots(start.astype(I32), idx, pos)
    order = jnp.argsort(rank.T.reshape(T * TOP_K)).astype(I32)
    order = jnp.concatenate([order, jnp.zeros((ORD_CHUNK,), I32)])
    e_blocks = (counts + MOE_BLK - 1) // MOE_BLK
    blk_end = jnp.cumsum(e_blocks)
    n_used = blk_end[-1]
    nb = T * TOP_K // MOE_BLK + N_EXPERTS + 2
    i = jnp.arange(nb, dtype=I32)
    used = i < n_used
    ic = jnp.minimum(i, n_used - 1)
    blk_expert = jnp.minimum(jnp.sum((blk_end[None, :] <= ic[:, None]).astype(I32), axis=1), N_EXPERTS - 1)
    off = (ic - (blk_end - e_blocks)[blk_expert]) * MOE_BLK
    nvalid = jnp.where(used, jnp.clip(counts[blk_expert] - off, 0, MOE_BLK), 0).astype(I32)
    rank0 = jnp.where(used, start[blk_expert] + off, 0).astype(I32)
    yk = _experts(layer, blk_expert.astype(I32), rank0, nvalid, order, x1t, wg, wu, wd)
    return _combine(gate.T, x1, yk, sh_gu, sh_d, ln_g, ln_b)


def _prep_w_in(w_in):
    o = np.cumsum((0, 128, 128, 256, 256, 16, 256, 256, 256, 512, 4, 256, 256, 256))
    c = lambda i: w_in[..., o[i]:o[i + 1]]
    xbc = c(8)
    x_cols = xbc[..., 0:256]
    per_head = lambda m: jnp.concatenate(
        [m[..., SSM_N * (h // (SSM_H // SSM_G)):SSM_N * (h // (SSM_H // SSM_G) + 1)] for h in range(SSM_H)], axis=-1)
    b_cols = per_head(xbc[..., 256:384])
    c_cols = per_head(xbc[..., 384:512])
    dt_cols = jnp.repeat(c(9), SSM_P, axis=-1)
    pad = jnp.zeros(w_in.shape[:-1] + (W_GLA - o[5],), w_in.dtype)
    cols = [c(0), c(1), c(2), c(3), c(4), pad, c(5), c(6), c(7), x_cols, b_cols, c_cols, dt_cols,
            c(10), c(11), c(12)]
    return jnp.concatenate(cols, axis=-1).astype(BF16)


def _per_head(v):
    rep = SSM_H // SSM_G
    xp = v[..., 0:256]
    pick = lambda m: jnp.concatenate([m[..., SSM_N * (h // rep):SSM_N * (h // rep + 1)] for h in range(SSM_H)], axis=-1)
    return jnp.concatenate([xp, pick(v[..., 256:384]), pick(v[..., 384:512])], axis=-1)


def kernel(x, positions, w_in, gla_w_gate, gla_b_gate, gla_norm_w, sgu_ln_g, sgu_ln_b, sgu_w, sgu_b, ssm_conv_w, ssm_conv_b, ssm_dt_bias, ssm_a_log, ssm_d, ssm_norm_w, w_out, ln1_g, ln1_b, router_w, router_bias, exp_w_gate, exp_w_up, exp_w_down, sh_w_gate, sh_w_up, sh_w_down, ln2_g, ln2_b):
    B, S, _ = x.shape
    L = w_in.shape[0]
    T = B * S

    inv_freq = ROPE_THETA ** (-jnp.arange(0, ROT_DIM, 2, dtype=F32) / ROT_DIM)
    ang = positions.astype(F32)[..., None] * inv_freq
    cos, sin = jnp.cos(ang), jnp.sin(ang)
    ones = jnp.ones((B, S, DIL_DH - ROT_DIM), F32)
    cos_t = jnp.tile(jnp.concatenate([cos, cos, ones], axis=-1), (1, 1, 2))
    sin_t = jnp.tile(jnp.concatenate([-sin, sin, 0.0 * ones], axis=-1), (1, 1, 2))
    tb = jnp.asarray(_branch_log_multiplicity(S))

    w_p = _prep_w_in(w_in)
    wg_p = jnp.pad(gla_w_gate, ((0, 0), (0, 128 - GLA_RANK), (0, 0)))
    row = lambda v: v.reshape(L, 1, -1)
    gla_nw = jnp.tile(gla_norm_w, (1, GLA_H))
    sgu_wc = jnp.transpose(sgu_w, (0, 2, 1, 3)).reshape(L, SGU_CHUNK, SGU_G * SGU_CHUNK)
    sgu_bt = jnp.repeat(jnp.transpose(sgu_b, (0, 2, 1)), GROUP_W // SGU_G, axis=-1)
    exp64 = lambda v: jnp.repeat(v, SSM_P, axis=-1)
    w_out_b = w_out.astype(BF16)
    rw_t = jnp.transpose(router_w, (0, 2, 1))
    sh_gu = jnp.concatenate([sh_w_gate, sh_w_up], axis=-1).astype(BF16)
    sh_d = sh_w_down.astype(BF16)

    x2 = x.reshape(T, D_MODEL)
    for l in range(L):
        gla_p = (wg_p[l], row(gla_b_gate)[l], row(gla_nw)[l])
        sgu_p = (sgu_wc[l], sgu_bt[l], row(sgu_ln_g)[l], row(sgu_ln_b)[l])
        ssd_p = (_per_head(ssm_conv_w[l]), row(_per_head(ssm_conv_b))[l], row(exp64(ssm_dt_bias))[l],
                 row(exp64(ssm_a_log))[l], row(exp64(ssm_d))[l], row(ssm_norm_w)[l])
        x1, x1t = _token_mixer(x2, B, S, w_p[l], gla_p, sgu_p, ssd_p, cos_t, sin_t, tb,
                               w_out_b[l], row(ln1_g)[l], row(ln1_b)[l])
        x2 = _moe(l, x1, x1t, rw_t[l], router_bias[l].reshape(N_EXPERTS, 1), exp_w_gate, exp_w_up, exp_w_down,
                  sh_gu[l], sh_d[l], row(ln2_g)[l], row(ln2_b)[l])
    return x2.reshape(B, S, D_MODEL)
```

```python
import functools
import math

import numpy as np
import jax
import jax.numpy as jnp
from jax import lax
from jax.experimental import pallas as pl
from jax.experimental.pallas import tpu as pltpu

F32 = jnp.float32
BF16 = jnp.bfloat16
I32 = jnp.int32

D_MODEL = 1024
N_LAYERS = 4
GROUP_W = 256

GLA_H, GLA_DK, GLA_DV, GLA_RANK, GLA_TAU = 4, 32, 64, 16, 16.0
SGU_G, SGU_CHUNK = 4, 128
SSM_H, SSM_P, SSM_G, SSM_N, SSM_CONV, SSM_CHUNK = 4, 64, 2, 64, 4, 128
DIL_H, DIL_DH, ROT_DIM, ROPE_THETA = 4, 64, 16, 500000.0
DIL_BRANCHES = ((128, 1), (512, 4), (2048, 16))

N_EXPERTS, TOP_K, N_EXPERT_GROUPS, TOPK_GROUPS, D_EXPERT = 128, 8, 8, 4, 256
ROUTED_SCALE = 1.0

ALPHA = (2 * N_LAYERS) ** 0.25
LN_EPS = 1e-5
RMS_EPS = 1e-6

W_GLA = 896
W_SGU = 512
W_SSD = 1280
W_DIL = 768
W_PROJ = W_GLA + W_SGU + W_SSD + W_DIL

BLK = 128
QBLK = 256
MOE_BLK = 256
TOK_BLK = 256
CMB_BLK = 128
NEG = -1e30

VMEM_LIMIT = 56 * 1024 * 1024


def _cparams(sem):
    return pltpu.CompilerParams(dimension_semantics=sem, vmem_limit_bytes=VMEM_LIMIT)


def _dot(a, b):
    return jnp.dot(a.astype(BF16), b.astype(BF16), preferred_element_type=F32)


def _dot_nt(a, b):
    return lax.dot_general(a.astype(BF16), b.astype(BF16), (((1,), (1,)), ((), ())),
                           preferred_element_type=F32)


def _dot_tn(a, b):
    return lax.dot_general(a.astype(BF16), b.astype(BF16), (((0,), (0,)), ((), ())),
                           preferred_element_type=F32)


def _split(a, n):
    out = []
    r = a
    for _ in range(n):
        p = r.astype(BF16)
        out.append(p)
        r = r - p.astype(F32)
    return out


def _dot_hi(a, b):
    a1, a2 = _split(a, 2)
    b1, b2 = _split(b, 2)
    return _dot(a1, b1) + (_dot(a1, b2) + _dot(a2, b1))


def _dot_nt_hi(a, b):
    a1, a2 = _split(a, 2)
    b1, b2 = _split(b, 2)
    return _dot_nt(a1, b1) + (_dot_nt(a1, b2) + _dot_nt(a2, b1))


def _dot_lhs01(m, b):
    b1, b2, b3 = _split(b, 3)
    return _dot(m, b1) + (_dot(m, b2) + _dot(m, b3))


def _dot_rhs01(a, m):
    a1, a2, a3 = _split(a, 3)
    return _dot(a1, m) + (_dot(a2, m) + _dot(a3, m))


def _iota(shape, dim):
    return lax.broadcasted_iota(I32, shape, dim)


def _sigmoid(x):
    return 1.0 / (1.0 + jnp.exp(-x))


def _silu(x):
    return x * _sigmoid(x)


def _softplus(x):
    return jnp.maximum(x, 0.0) + jnp.log1p(jnp.exp(-jnp.abs(x)))


def _gelu(x):
    return 0.5 * x * (1.0 + lax.erf(x * (2.0 ** -0.5)))


ROW_T = D_MODEL // 128


def _to_tiles(dst_ref, base, x, first=0):
    n = x.shape[0]
    for s in range(x.shape[1] // 128):
        dst_ref[pl.ds(base + first + s, n, stride=ROW_T), :] = x[:, 128 * s:128 * (s + 1)]


def _from_tiles(src_ref, base, n, pitch=ROW_T):
    return jnp.concatenate([src_ref[pl.ds(base + s, n, stride=pitch), :] for s in range(ROW_T)], axis=1)


def _inproj_kernel(x_ref, w_ref, o_gla, o_sgu, o_ssd, o_dil):
    xb = x_ref[...].astype(BF16)
    off = 0
    for o in (o_gla, o_sgu, o_ssd, o_dil):
        n = o.shape[1]
        o[...] = jnp.dot(xb, w_ref[:, off:off + n], preferred_element_type=F32)
        off += n


def _inproj(x2, w_p):
    T = x2.shape[0]
    tm = 512
    widths = (W_GLA, W_SGU, W_SSD, W_DIL)
    return pl.pallas_call(
        _inproj_kernel,
        grid=(T // tm,),
        in_specs=[pl.BlockSpec((tm, D_MODEL), lambda i: (i, 0)),
                  pl.BlockSpec((D_MODEL, W_PROJ), lambda i: (0, 0))],
        out_specs=[pl.BlockSpec((tm, w), lambda i: (i, 0)) for w in widths],
        out_shape=[jax.ShapeDtypeStruct((T, w), F32) for w in widths],
        compiler_params=_cparams(("arbitrary",)),
        name="inproj",
    )(x2, w_p)


def _gla_kernel(slab_ref, wg_ref, bg_ref, nw_ref, out_ref, st_ref):
    S = slab_ref.shape[1]
    st_ref[...] = jnp.zeros_like(st_ref)
    row = _iota((BLK, BLK), 0)
    col = _iota((BLK, BLK), 1)
    tril = col <= row
    tril_bf = jnp.where(tril, 1.0, 0.0).astype(BF16)
    qk_head = col >> 5
    v_head = _iota((BLK, GROUP_W), 1) >> 6
    st_diag = (_iota((GROUP_W, BLK), 0) >> 6) == (_iota((GROUP_W, BLK), 1) >> 5)
    seg = jnp.where((_iota((GROUP_W, GROUP_W), 0) >> 6) == (_iota((GROUP_W, GROUP_W), 1) >> 6),
                    1.0, 0.0).astype(BF16)
    wg = wg_ref[...]
    bg = bg_ref[...]
    nw = nw_ref[...]

    def body(n, carry):
        r0 = pl.multiple_of(n * BLK, BLK)
        blk = slab_ref[0, pl.ds(r0, BLK), :]
        q = blk[:, 0:128] * (GLA_DK ** -0.5)
        k = blk[:, 128:256]
        v = blk[:, 256:512]
        r = blk[:, 512:768]
        lr = blk[:, 768:896]
        z = _dot_hi(lr, wg) + bg
        gk = (jnp.minimum(z, 0.0) - jnp.log1p(jnp.exp(-jnp.abs(z)))) * (1.0 / GLA_TAU)
        g = _dot_lhs01(tril_bf, gk)
        g_last = g[BLK - 1:BLK, :]
        q_in = q * jnp.exp(g)
        k_out = k * jnp.exp(-g)
        k_in = k * jnp.exp(g_last - g)
        vb = v.astype(BF16)
        o = _dot_nt(q_in, st_ref[...])
        kob = k_out.astype(BF16)
        for h in range(GLA_H):
            qh = jnp.where(qk_head == h, q_in, 0.0)
            a = jnp.where(tril, _dot_nt(qh, kob), 0.0)
            o = o + jnp.where(v_head == h, _dot(a, vb), 0.0)
        st_ref[...] = st_ref[...] * jnp.exp(g_last) + jnp.where(st_diag, _dot_tn(vb, k_in), 0.0)
        ms = _dot_rhs01(o * o, seg) * (1.0 / GLA_DV)
        y = o * lax.rsqrt(ms + RMS_EPS) * nw * _silu(r)
        out_ref[0, pl.ds(r0, BLK), :] = y.astype(out_ref.dtype)
        return carry

    lax.fori_loop(0, S // BLK, body, 0)


def _gla(slab, wg_p, bg, nw_t):
    B, S, _ = slab.shape
    return pl.pallas_call(
        _gla_kernel,
        grid=(B,),
        in_specs=[pl.BlockSpec((1, S, W_GLA), lambda b: (b, 0, 0)),
                  pl.BlockSpec((128, 128), lambda b: (0, 0)),
                  pl.BlockSpec((1, 128), lambda b: (0, 0)),
                  pl.BlockSpec((1, GROUP_W), lambda b: (0, 0))],
        out_specs=pl.BlockSpec((1, S, GROUP_W), lambda b: (b, 0, 0)),
        out_shape=jax.ShapeDtypeStruct((B, S, GROUP_W), BF16),
        scratch_shapes=[pltpu.VMEM((GROUP_W, 128), F32)],
        compiler_params=_cparams(("arbitrary",)),
        name="gla",
    )(slab, wg_p, bg, nw_t)


def _sgu_kernel(slab_ref, w_ref, bias_ref, g_ref, b_ref, out_ref):
    S = slab_ref.shape[1]
    wmask = (_iota((BLK, SGU_G * BLK), 1) & (BLK - 1)) <= _iota((BLK, SGU_G * BLK), 0)
    w = jnp.where(wmask, w_ref[...], 0.0).astype(BF16)
    lane_grp = _iota((BLK, GROUP_W), 1) >> 6
    bias = bias_ref[...]
    ln_g = g_ref[...]
    ln_b = b_ref[...]

    def body(n, carry):
        r0 = pl.multiple_of(n * BLK, BLK)
        blk = slab_ref[0, pl.ds(r0, BLK), :]
        u = _gelu(blk[:, 0:256])
        v = _gelu(blk[:, 256:512])
        mu = jnp.mean(v, axis=-1, keepdims=True)
        var = jnp.mean(jnp.square(v - mu), axis=-1, keepdims=True)
        v = (v - mu) * lax.rsqrt(var + LN_EPS) * ln_g + ln_b
        vexp = jnp.concatenate([jnp.where(lane_grp == g, v, 0.0) for g in range(SGU_G)], axis=0)
        s = _dot(w, vexp) + bias
        out_ref[0, pl.ds(r0, BLK), :] = (u * s).astype(out_ref.dtype)
        return carry

    lax.fori_loop(0, S // BLK, body, 0)


def _sgu(slab, w_cat, bias_t, ln_g, ln_b):
    B, S, _ = slab.shape
    return pl.pallas_call(
        _sgu_kernel,
        grid=(B,),
        in_specs=[pl.BlockSpec((1, S, W_SGU), lambda b: (b, 0, 0)),
                  pl.BlockSpec((BLK, SGU_G * BLK), lambda b: (0, 0)),
                  pl.BlockSpec((BLK, GROUP_W), lambda b: (0, 0)),
                  pl.BlockSpec((1, GROUP_W), lambda b: (0, 0)),
                  pl.BlockSpec((1, GROUP_W), lambda b: (0, 0))],
        out_specs=pl.BlockSpec((1, S, GROUP_W), lambda b: (b, 0, 0)),
        out_shape=jax.ShapeDtypeStruct((B, S, GROUP_W), BF16),
        compiler_params=_cparams(("arbitrary",)),
        name="sgu",
    )(slab, w_cat, bias_t, ln_g, ln_b)


def _ssd_kernel(slab_ref, cw_ref, cb_ref, dtb_ref, a_ref, d_ref, nw_ref, out_ref, xpad_ref, st_ref):
    S = slab_ref.shape[1]
    CW = 3 * GROUP_W
    st_ref[...] = jnp.zeros_like(st_ref)
    xpad_ref[0:8, :] = jnp.zeros((8, CW), F32)
    xpad_ref[8:S + 8, :] = slab_ref[0, :, 256:256 + CW]
    row = _iota((BLK, BLK), 0)
    col = _iota((BLK, BLK), 1)
    tril = col <= row
    tril_bf = jnp.where(tril, 1.0, 0.0).astype(BF16)
    lane_head = _iota((BLK, GROUP_W), 1) >> 6
    st_diag = (_iota((GROUP_W, GROUP_W), 0) >> 6) == (_iota((GROUP_W, GROUP_W), 1) >> 6)
    cw = cw_ref[...]
    cb = cb_ref[...]
    dtb = dtb_ref[...]
    a_neg = -jnp.exp(a_ref[...])
    d_skip = d_ref[...]
    nw = nw_ref[...]

    def body(n, carry):
        r0 = pl.multiple_of(n * BLK, BLK)
        xw = xpad_ref[pl.ds(r0, BLK + 8), :]
        conv = cb + cw[0:1, :] * xw[5:5 + BLK, :]
        for w in range(1, SSM_CONV):
            conv = conv + cw[w:w + 1, :] * xw[5 + w:5 + w + BLK, :]
        xbc = _silu(conv)
        xs = xbc[:, 0:256]
        b_e = xbc[:, 256:512]
        c_e = xbc[:, 512:768]
        z = slab_ref[0, pl.ds(r0, BLK), 0:256]
        dt = _softplus(slab_ref[0, pl.ds(r0, BLK), 1024:1280] + dtb)
        acs = _dot_lhs01(tril_bf, dt * a_neg)
        acs_last = acs[BLK - 1:BLK, :]
        x_dt = (xs * dt).astype(BF16)
        acs_t = [jnp.transpose(acs[:, 0:128]), jnp.transpose(acs[:, 128:256])]
        y = _dot(c_e * jnp.exp(acs), st_ref[...])
        b_bf = b_e.astype(BF16)
        for h in range(SSM_H):
            a_col = jnp.broadcast_to(acs[:, 64 * h:64 * h + 1], (BLK, BLK))
            a_row = jnp.broadcast_to(acs_t[h // 2][64 * (h % 2):64 * (h % 2) + 1, :], (BLK, BLK))
            lm = jnp.exp(jnp.where(tril, a_col - a_row, -jnp.inf))
            ch = jnp.where(lane_head == h, c_e, 0.0)
            m = _dot_nt(ch, b_bf) * lm
            y = y + jnp.where(lane_head == h, _dot(m, x_dt), 0.0)
        new = _dot_tn(b_e * jnp.exp(acs_last - acs), x_dt)
        st_ref[...] = st_ref[...] * jnp.exp(acs_last) + jnp.where(st_diag, new, 0.0)
        y = (y + d_skip * xs) * _silu(z)
        parts = []
        for g in range(SSM_G):
            yg = y[:, 128 * g:128 * (g + 1)]
            ms = jnp.mean(yg * yg, axis=-1, keepdims=True)
            parts.append(yg * lax.rsqrt(ms + RMS_EPS))
        out = jnp.concatenate(parts, axis=1) * nw
        out_ref[0, pl.ds(r0, BLK), :] = out.astype(out_ref.dtype)
        return carry

    lax.fori_loop(0, S // BLK, body, 0)


def _ssd(slab, cw_e, cb_e, dtb_e, a_e, d_e, nw):
    B, S, _ = slab.shape
    CW = 3 * GROUP_W
    vec = lambda w: pl.BlockSpec((1, w), lambda b: (0, 0))
    return pl.pallas_call(
        _ssd_kernel,
        grid=(B,),
        in_specs=[pl.BlockSpec((1, S, W_SSD), lambda b: (b, 0, 0)),
                  pl.BlockSpec((SSM_CONV, CW), lambda b: (0, 0)),
                  vec(CW), vec(GROUP_W), vec(GROUP_W), vec(GROUP_W), vec(GROUP_W)],
        out_specs=pl.BlockSpec((1, S, GROUP_W), lambda b: (b, 0, 0)),
        out_shape=jax.ShapeDtypeStruct((B, S, GROUP_W), BF16),
        scratch_shapes=[pltpu.VMEM((S + 8, CW), F32), pltpu.VMEM((GROUP_W, GROUP_W), F32)],
        compiler_params=_cparams(("arbitrary",)),
        name="ssd",
    )(slab, cw_e, cb_e, dtb_e, a_e, d_e, nw)


def _branch_log_multiplicity(S):
    nq = S // QBLK
    r = np.arange(QBLK)[:, None]
    u = np.arange(S)[None, :]
    d = (nq - 1) * QBLK + r - u
    c = np.zeros(d.shape, np.float64)
    for window, dil in DIL_BRANCHES:
        c += (d >= 0) & (d % dil == 0) & (d <= window)
    with np.errstate(divide="ignore"):
        return np.where(c > 0, np.log(np.maximum(c, 1.0)), NEG).astype(np.float32)


def _dil_kernel(slab_ref, cos_ref, sin_ref, tb_ref, out_ref, q_ref, k_ref, v_ref):
    S = slab_ref.shape[1]
    nq = S // QBLK
    lane_d = _iota((1, 128), 1) & (DIL_DH - 1)

    def rope(t):
        fwd = pltpu.roll(t, 128 - ROT_DIM // 2, 1)
        bwd = pltpu.roll(t, ROT_DIM // 2, 1)
        return t * cos_ref[0] + jnp.where(lane_d < ROT_DIM // 2, fwd, bwd) * sin_ref[0]

    for c in range(0, GROUP_W, 128):
        q_ref[:, c:c + 128] = (rope(slab_ref[0, :, c:c + 128]) * (DIL_DH ** -0.5)).astype(BF16)
        k_ref[:, c:c + 128] = rope(slab_ref[0, :, 256 + c:384 + c]).astype(BF16)
    v_ref[...] = slab_ref[0, :, 512:768].astype(BF16)
    lane_head = _iota((QBLK, GROUP_W), 1) >> 6
    for i in range(nq):
        nk = (i + 1) * QBLK
        qi = q_ref[i * QBLK:(i + 1) * QBLK, :]
        acc = jnp.zeros((QBLK, GROUP_W), F32)
        for h in range(DIL_H):
            qh = jnp.where(lane_head == h, qi, jnp.zeros_like(qi))
            s = _dot_nt(qh, k_ref[0:nk, :]) + tb_ref[:, (nq - 1 - i) * QBLK:]
            m = jnp.max(s, axis=-1, keepdims=True)
            p = jnp.exp(s - m)
            den = jnp.sum(p, axis=-1, keepdims=True)
            o = _dot(p, v_ref[0:nk, :])
            acc = acc + jnp.where(lane_head == h, o / den, 0.0)
        out_ref[0, i * QBLK:(i + 1) * QBLK, :] = acc.astype(out_ref.dtype)


def _dil(slab, cos_t, sin_t, tb):
    B, S, _ = slab.shape
    return pl.pallas_call(
        _dil_kernel,
        grid=(B,),
        in_specs=[pl.BlockSpec((1, S, W_DIL), lambda b: (b, 0, 0)),
                  pl.BlockSpec((1, S, 128), lambda b: (b, 0, 0)),
                  pl.BlockSpec((1, S, 128), lambda b: (b, 0, 0)),
                  pl.BlockSpec((QBLK, S), lambda b: (0, 0), pipeline_mode=pl.Buffered(1))],
        out_specs=pl.BlockSpec((1, S, GROUP_W), lambda b: (b, 0, 0)),
        out_shape=jax.ShapeDtypeStruct((B, S, GROUP_W), BF16),
        scratch_shapes=[pltpu.VMEM((S, GROUP_W), BF16)] * 3,
        compiler_params=_cparams(("arbitrary",)),
        name="dil",
    )(slab, cos_t, sin_t, tb)


def _layernorm(x, g, b):
    mu = jnp.mean(x, axis=-1, keepdims=True)
    var = jnp.mean(jnp.square(x - mu), axis=-1, keepdims=True)
    return (x - mu) * lax.rsqrt(var + LN_EPS) * g + b


def _outproj_kernel(ya, yb, yc, yd, x_ref, w_ref, g_ref, b_ref, o_ref, ot_ref):
    y = jnp.concatenate([ya[...], yb[...], yc[...], yd[...]], axis=1)
    y = jnp.dot(y, w_ref[...], preferred_element_type=F32)
    x1 = _layernorm(ALPHA * x_ref[...] + y, g_ref[...], b_ref[...])
    o_ref[...] = x1
    _to_tiles(ot_ref, 0, x1)


def _outproj(ya, yb, yc, yd, x2, w_out, g, b):
    T = x2.shape[0]
    tm = 512
    ysp = pl.BlockSpec((tm, GROUP_W), lambda i: (i, 0))
    vec = pl.BlockSpec((1, D_MODEL), lambda i: (0, 0))
    return pl.pallas_call(
        _outproj_kernel,
        grid=(T // tm,),
        in_specs=[ysp, ysp, ysp, ysp,
                  pl.BlockSpec((tm, D_MODEL), lambda i: (i, 0)),
                  pl.BlockSpec((D_MODEL, D_MODEL), lambda i: (0, 0)), vec, vec],
        out_specs=[pl.BlockSpec((tm, D_MODEL), lambda i: (i, 0)),
                   pl.BlockSpec((tm * ROW_T, 128), lambda i: (i, 0))],
        out_shape=[jax.ShapeDtypeStruct((T, D_MODEL), F32),
                   jax.ShapeDtypeStruct((T * ROW_T, 128), F32)],
        compiler_params=_cparams(("arbitrary",)),
        name="outproj_ln",
    )(ya, yb, yc, yd, x2, w_out, g, b)


def _router_kernel(x_ref, rw_ref, rb_ref, idx_ref, gate_ref, pos_ref, cnt_ref, carry_ref):
    TB = x_ref.shape[0]
    E = N_EXPERTS
    per = E // N_EXPERT_GROUPS

    @pl.when(pl.program_id(0) == 0)
    def _():
        carry_ref[...] = jnp.zeros_like(carry_ref)

    logits = _dot_nt_hi(rw_ref[...], x_ref[...])
    scores = _sigmoid(logits)
    choice = scores + rb_ref[...]
    e_iota = _iota((E, TB), 0)

    l_iota = _iota((per, TB), 0)
    grp = []
    for g in range(N_EXPERT_GROUPS):
        cg = choice[per * g:per * (g + 1), :]
        m1 = jnp.max(cg, axis=0, keepdims=True)
        i1 = jnp.min(jnp.where(cg == m1, l_iota, per), axis=0, keepdims=True)
        m2 = jnp.max(jnp.where(l_iota == i1, -jnp.inf, cg), axis=0, keepdims=True)
        grp.append(m1 + m2)
    gs = jnp.concatenate(grp, axis=0)
    g_iota = _iota((N_EXPERT_GROUPS, TB), 0)
    keep = jnp.zeros((N_EXPERT_GROUPS, TB), jnp.bool_)
    for _ in range(TOPK_GROUPS):
        m = jnp.max(gs, axis=0, keepdims=True)
        gi = jnp.min(jnp.where(gs == m, g_iota, N_EXPERT_GROUPS), axis=0, keepdims=True)
        hit = g_iota == gi
        keep = keep | hit
        gs = jnp.where(hit, -jnp.inf, gs)
    keep_f = jnp.where(keep, 1.0, 0.0)
    keep_e = jnp.concatenate([jnp.broadcast_to(keep_f[g:g + 1, :], (per, TB))
                              for g in range(N_EXPERT_GROUPS)], axis=0)
    cm = jnp.where(keep_e > 0.5, choice, -jnp.inf)

    hits, idxs, gates = [], [], []
    onehot = jnp.zeros((E, TB), F32)
    for _ in range(TOP_K):
        m = jnp.max(cm, axis=0, keepdims=True)
        ei = jnp.min(jnp.where(cm == m, e_iota, E), axis=0, keepdims=True)
        hit = e_iota == ei
        hits.append(hit)
        idxs.append(ei)
        gates.append(jnp.sum(jnp.where(hit, scores, 0.0), axis=0, keepdims=True))
        onehot = onehot + jnp.where(hit, 1.0, 0.0)
        cm = jnp.where(hit, -jnp.inf, cm)
    gate = jnp.concatenate(gates, axis=0)
    gate = gate / jnp.sum(gate, axis=0, keepdims=True) * ROUTED_SCALE

    before = jnp.where(_iota((TB, TB), 0) < _iota((TB, TB), 1), 1.0, 0.0).astype(BF16)
    cnt = carry_ref[...] + _dot(onehot, before)
    pos = jnp.concatenate([jnp.sum(jnp.where(h, cnt, 0.0), axis=0, keepdims=True) for h in hits], axis=0)
    carry_ref[...] = carry_ref[...] + jnp.sum(onehot, axis=1, keepdims=True)

    idx_ref[...] = jnp.concatenate(idxs, axis=0)
    gate_ref[...] = gate
    pos_ref[...] = pos.astype(I32)
    cnt_ref[...] = jnp.broadcast_to(carry_ref[...], cnt_ref.shape)


def _router(x1, rw_t, rb):
    T = x1.shape[0]
    TB = TOK_BLK
    kt = pl.BlockSpec((TOP_K, TB), lambda i: (0, i))
    return pl.pallas_call(
        _router_kernel,
        grid=(T // TB,),
        in_specs=[pl.BlockSpec((TB, D_MODEL), lambda i: (i, 0)),
                  pl.BlockSpec((N_EXPERTS, D_MODEL), lambda i: (0, 0)),
                  pl.BlockSpec((N_EXPERTS, 1), lambda i: (0, 0))],
        out_specs=[kt, kt, kt, pl.BlockSpec((N_EXPERTS, 128), lambda i: (0, 0))],
        out_shape=[jax.ShapeDtypeStruct((TOP_K, T), I32), jax.ShapeDtypeStruct((TOP_K, T), F32),
                   jax.ShapeDtypeStruct((TOP_K, T), I32), jax.ShapeDtypeStruct((N_EXPERTS, 128), F32)],
        scratch_shapes=[pltpu.VMEM((N_EXPERTS, 1), F32)],
        compiler_params=_cparams(("arbitrary",)),
        name="router",
    )(x1, rw_t, rb)


def _slots_kernel(ps_ref, idx_ref, pos_ref, dest_ref):
    idx = idx_ref[...]

    def body(e, acc):
        return acc + jnp.where(idx == e, ps_ref[e], 0)

    dest_ref[...] = lax.fori_loop(0, N_EXPERTS, body, pos_ref[...], unroll=8)


def _slots(pad_start, idx, pos):
    T = idx.shape[1]
    tb = min(T, 2048)
    kt = pl.BlockSpec((TOP_K, tb), lambda i, ps: (0, i))
    return pl.pallas_call(
        _slots_kernel,
        grid_spec=pltpu.PrefetchScalarGridSpec(
            num_scalar_prefetch=1, grid=(T // tb,), in_specs=[kt, kt], out_specs=kt),
        out_shape=jax.ShapeDtypeStruct((TOP_K, T), I32),
        compiler_params=_cparams(("arbitrary",)),
        name="slots",
    )(pad_start, idx, pos)


ORD_CHUNK = 2048
ISSUE_PHASES = 4
UNROLL = 8


def _experts_kernel(be_ref, r0_ref, nv_ref, ord_hbm, x_hbm, wg_ref, wu_ref, wd_ref, yk_hbm,
                    ord_s, xbuf, ybuf, xb_s, h_s, a_s, wgu_s, wd_s, fsem, gsem, ssem):
    i = pl.program_id(0)
    nb = pl.num_programs(0)
    per_phase = MOE_BLK // ISSUE_PHASES
    rows = MOE_BLK * ROW_T
    dummy = (yk_hbm.shape[0] - ROW_T) // ROW_T

    def fetch(j):
        c0 = pl.multiple_of((r0_ref[j] >> 10) << 10, 1024)
        dst = ord_s.at[pl.ds(pl.multiple_of((j & 3) * ORD_CHUNK, ORD_CHUNK), ORD_CHUNK)]
        return pltpu.make_async_copy(ord_hbm.at[pl.ds(c0, ORD_CHUNK)], dst, fsem.at[j & 3])

    def tile(ref, r):
        return ref.at[pl.ds(pl.multiple_of(r * ROW_T, ROW_T), ROW_T), :]

    def issue(j, lo, hi, scatter):
        obase = (j & 3) * ORD_CHUNK + (r0_ref[j] & 1023)
        nvj = nv_ref[j]
        slot = j & 1

        def body(it, c):
            for u in range(UNROLL):
                r = lo + it * UNROLL + u
                a = ord_s[obase + r]
                if scatter:
                    a = jnp.where(r < nvj, a, dummy)
                    pltpu.make_async_copy(tile(ybuf.at[slot], r), tile(yk_hbm, a), ssem.at[slot]).start(priority=1)
                else:
                    pltpu.make_async_copy(tile(x_hbm, a >> 3), tile(xbuf.at[slot], r), gsem.at[slot]).start()
            return c

        lax.fori_loop(0, (hi - lo) // UNROLL, body, 0)

    @pl.when(i == 0)
    def _():
        fetch(0).start()
        fetch(0).wait()

        @pl.when(nb > 1)
        def _():
            fetch(1).start()

        issue(0, 0, MOE_BLK, False)

    @pl.when(i + 1 < nb)
    def _():
        fetch(i + 1).wait()

    @pl.when(i + 2 < nb)
    def _():
        fetch(i + 2).start()

    jn = jnp.minimum(i + 1, nb - 1)
    jp = jnp.maximum(i - 1, 0)
    jpp = jnp.maximum(i - 2, 0)
    active = nv_ref[i] > 0
    next_active = (i + 1 < nb) & (nv_ref[jn] > 0)
    prev_active = (i >= 1) & (nv_ref[jp] > 0)

    @pl.when(active | (i == 0))
    def _():
        pltpu.make_async_copy(x_hbm.at[pl.ds(0, rows), :], xbuf.at[i & 1], gsem.at[i & 1]).wait()

    @pl.when((i >= 2) & (nv_ref[jpp] > 0))
    def _():
        pltpu.make_async_copy(ybuf.at[i & 1], yk_hbm.at[pl.ds(0, rows), :], ssem.at[i & 1]).wait()

    @pl.when((i == 0) | (be_ref[i] != be_ref[jp]))
    def _():
        wgu_s[:, 0:D_EXPERT] = wg_ref[0, 0].astype(BF16)
        wgu_s[:, D_EXPERT:] = wu_ref[0, 0].astype(BF16)
        wd_s[...] = wd_ref[0, 0].astype(BF16)
    xslot = xbuf.at[i & 1]
    yslot = ybuf.at[i & 1]
    half = D_MODEL // 2

    def phase(p):
        if p == 0:
            xb_s[...] = _from_tiles(xslot, 0, MOE_BLK).astype(BF16)
            h_s[:, 0:D_EXPERT] = jnp.dot(xb_s[...], wgu_s[:, 0:D_EXPERT], preferred_element_type=F32)
        elif p == 1:
            h_s[:, D_EXPERT:] = jnp.dot(xb_s[...], wgu_s[:, D_EXPERT:], preferred_element_type=F32)
        elif p == 2:
            a_s[...] = (_silu(h_s[:, 0:D_EXPERT]) * h_s[:, D_EXPERT:]).astype(BF16)
            _to_tiles(yslot, 0, jnp.dot(a_s[...], wd_s[:, 0:half], preferred_element_type=F32))
        else:
            _to_tiles(yslot, 0, jnp.dot(a_s[...], wd_s[:, half:], preferred_element_type=F32), first=ROW_T // 2)

    for p in range(ISSUE_PHASES):
        lo, hi = p * per_phase, (p + 1) * per_phase
        pl.when(next_active)(functools.partial(issue, jn, lo, hi, False))
        pl.when(prev_active)(functools.partial(issue, jp, lo, hi, True))
        pl.when(active)(functools.partial(phase, p))


def _experts(layer, blk_expert, rank0, nvalid, order, x1t, wg, wu, wd):
    nb = blk_expert.shape[0]
    rows = MOE_BLK * ROW_T
    wsel = lambda i, be, r0, nv: (layer, be[i], 0, 0)
    any_spec = pl.BlockSpec(memory_space=pl.ANY)
    return pl.pallas_call(
        _experts_kernel,
        grid_spec=pltpu.PrefetchScalarGridSpec(
            num_scalar_prefetch=3,
            grid=(nb,),
            in_specs=[any_spec, any_spec,
                      pl.BlockSpec((1, 1, D_MODEL, D_EXPERT), wsel),
                      pl.BlockSpec((1, 1, D_MODEL, D_EXPERT), wsel),
                      pl.BlockSpec((1, 1, D_EXPERT, D_MODEL), wsel)],
            out_specs=any_spec,
            scratch_shapes=[pltpu.SMEM((4 * ORD_CHUNK,), I32),
                            pltpu.VMEM((2, rows, 128), F32),
                            pltpu.VMEM((2, rows, 128), F32),
                            pltpu.VMEM((MOE_BLK, D_MODEL), BF16),
                            pltpu.VMEM((MOE_BLK, 2 * D_EXPERT), F32),
                            pltpu.VMEM((MOE_BLK, D_EXPERT), BF16),
                            pltpu.VMEM((D_MODEL, 2 * D_EXPERT), BF16),
                            pltpu.VMEM((D_EXPERT, D_MODEL), BF16),
                            pltpu.SemaphoreType.DMA((4,)),
                            pltpu.SemaphoreType.DMA((2,)),
                            pltpu.SemaphoreType.DMA((2,))]),
        out_shape=jax.ShapeDtypeStruct((x1t.shape[0] * TOP_K + ROW_T, 128), F32),
        compiler_params=_cparams(("arbitrary",)),
        name="experts",
    )(blk_expert, rank0, nvalid, order, x1t, wg, wu, wd)


def _combine_kernel(gate_ref, x_ref, yk_ref, sgu_ref, sd_ref, g_ref, b_ref, o_ref):
    TB = x_ref.shape[0]
    x = x_ref[...]
    h = jnp.dot(x.astype(BF16), sgu_ref[...], preferred_element_type=F32)
    a = _silu(h[:, 0:D_EXPERT]) * h[:, D_EXPERT:]
    y = ALPHA * x + jnp.dot(a.astype(BF16), sd_ref[...], preferred_element_type=F32)
    gate = gate_ref[...]
    for k in range(TOP_K):
        y = y + gate[:, k:k + 1] * _from_tiles(yk_ref, k * ROW_T, TB, pitch=TOP_K * ROW_T)
    o_ref[...] = _layernorm(y, g_ref[...], b_ref[...])


def _combine(gate_t, x1, yk, sh_gu, sh_d, g, b):
    T = x1.shape[0]
    TB = CMB_BLK
    vec = pl.BlockSpec((1, D_MODEL), lambda i: (0, 0))
    return pl.pallas_call(
        _combine_kernel,
        grid=(T // TB,),
        in_specs=[pl.BlockSpec((TB, TOP_K), lambda i: (i, 0)),
                  pl.BlockSpec((TB, D_MODEL), lambda i: (i, 0)),
                  pl.BlockSpec((TB * TOP_K * ROW_T, 128), lambda i: (i, 0)),
                  pl.BlockSpec((D_MODEL, 2 * D_EXPERT), lambda i: (0, 0)),
                  pl.BlockSpec((D_EXPERT, D_MODEL), lambda i: (0, 0)), vec, vec],
        out_specs=pl.BlockSpec((TB, D_MODEL), lambda i: (i, 0)),
        out_shape=jax.ShapeDtypeStruct((T, D_MODEL), F32),
        compiler_params=_cparams(("arbitrary",)),
        name="combine_ln",
    )(gate_t, x1, yk, sh_gu, sh_d, g, b)


def _token_mixer(x2, B, S, w_p, gla_p, sgu_p, ssd_p, cos_t, sin_t, tb, w_out, ln_g, ln_b):
    s_gla, s_sgu, s_ssd, s_dil = _inproj(x2, w_p)
    ya = _gla(s_gla.reshape(B, S, W_GLA), *gla_p)
    yb = _sgu(s_sgu.reshape(B, S, W_SGU), *sgu_p)
    yc = _ssd(s_ssd.reshape(B, S, W_SSD), *ssd_p)
    yd = _dil(s_dil.reshape(B, S, W_DIL), cos_t, sin_t, tb)
    T = B * S
    flat = lambda y: y.reshape(T, GROUP_W)
    return _outproj(flat(ya), flat(yb), flat(yc), flat(yd), x2, w_out, ln_g, ln_b)


def _moe(layer, x1, x1t, rw_t, rb, wg, wu, wd, sh_gu, sh_d, ln_g, ln_b):
    T = x1.shape[0]
    idx, gate, pos, cnt = _router(x1, rw_t, rb)
    counts = cnt[:, 0].astype(I32)
    start = jnp.cumsum(counts) - counts
    rank = _slots(start.astype(I32), idx, pos)
    order = jnp.argsort(rank.T.reshape(T * TOP_K)).astype(I32)
    order = jnp.concatenate([order, jnp.zeros((ORD_CHUNK,), I32)])
    e_blocks = (counts + MOE_BLK - 1) // MOE_BLK
    blk_end = jnp.cumsum(e_blocks)
    n_used = blk_end[-1]
    nb = T * TOP_K // MOE_BLK + N_EXPERTS + 2
    i = jnp.arange(nb, dtype=I32)
    used = i < n_used
    ic = jnp.minimum(i, n_used - 1)
    blk_expert = jnp.minimum(jnp.sum((blk_end[None, :] <= ic[:, None]).astype(I32), axis=1), N_EXPERTS - 1)
    off = (ic - (blk_end - e_blocks)[blk_expert]) * MOE_BLK
    nvalid = jnp.where(used, jnp.clip(counts[blk_expert] - off, 0, MOE_BLK), 0).astype(I32)
    rank0 = jnp.where(used, start[blk_expert] + off, 0).astype(I32)
    yk = _experts(layer, blk_expert.astype(I32), rank0, nvalid, order, x1t, wg, wu, wd)
    return _combine(gate.T, x1, yk, sh_gu, sh_d, ln_g, ln_b)


def _prep_w_in(w_in):
    o = np.cumsum((0, 128, 128, 256, 256, 16, 256, 256, 256, 512, 4, 256, 256, 256))
    c = lambda i: w_in[..., o[i]:o[i + 1]]
    xbc = c(8)
    x_cols = xbc[..., 0:256]
    per_head = lambda m: jnp.concatenate(
        [m[..., SSM_N * (h // (SSM_H // SSM_G)):SSM_N * (h // (SSM_H // SSM_G) + 1)] for h in range(SSM_H)], axis=-1)
    b_cols = per_head(xbc[..., 256:384])
    c_cols = per_head(xbc[..., 384:512])
    dt_cols = jnp.repeat(c(9), SSM_P, axis=-1)
    pad = jnp.zeros(w_in.shape[:-1] + (W_GLA - o[5],), w_in.dtype)
    cols = [c(0), c(1), c(2), c(3), c(4), pad, c(5), c(6), c(7), x_cols, b_cols, c_cols, dt_cols,
            c(10), c(11), c(12)]
    return jnp.concatenate(cols, axis=-1).astype(BF16)


def _per_head(v):
    rep = SSM_H // SSM_G
    xp = v[..., 0:256]
    pick = lambda m: jnp.concatenate([m[..., SSM_N * (h // rep):SSM_N * (h // rep + 1)] for h in range(SSM_H)], axis=-1)
    return jnp.concatenate([xp, pick(v[..., 256:384]), pick(v[..., 384:512])], axis=-1)


def kernel(x, positions, w_in, gla_w_gate, gla_b_gate, gla_norm_w, sgu_ln_g, sgu_ln_b, sgu_w, sgu_b, ssm_conv_w, ssm_conv_b, ssm_dt_bias, ssm_a_log, ssm_d, ssm_norm_w, w_out, ln1_g, ln1_b, router_w, router_bias, exp_w_gate, exp_w_up, exp_w_down, sh_w_gate, sh_w_up, sh_w_down, ln2_g, ln2_b):
    B, S, _ = x.shape
    L = w_in.shape[0]
    T = B * S

    inv_freq = ROPE_THETA ** (-jnp.arange(0, ROT_DIM, 2, dtype=F32) / ROT_DIM)
    ang = positions.astype(F32)[..., None] * inv_freq
    cos, sin = jnp.cos(ang), jnp.sin(ang)
    ones = jnp.ones((B, S, DIL_DH - ROT_DIM), F32)
    cos_t = jnp.tile(jnp.concatenate([cos, cos, ones], axis=-1), (1, 1, 2))
    sin_t = jnp.tile(jnp.concatenate([-sin, sin, 0.0 * ones], axis=-1), (1, 1, 2))
    tb = jnp.asarray(_branch_log_multiplicity(S))

    w_p = _prep_w_in(w_in)
    wg_p = jnp.pad(gla_w_gate, ((0, 0), (0, 128 - GLA_RANK), (0, 0)))
    row = lambda v: v.reshape(L, 1, -1)
    gla_nw = jnp.tile(gla_norm_w, (1, GLA_H))
    sgu_wc = jnp.transpose(sgu_w, (0, 2, 1, 3)).reshape(L, SGU_CHUNK, SGU_G * SGU_CHUNK)
    sgu_bt = jnp.repeat(jnp.transpose(sgu_b, (0, 2, 1)), GROUP_W // SGU_G, axis=-1)
    exp64 = lambda v: jnp.repeat(v, SSM_P, axis=-1)
    w_out_b = w_out.astype(BF16)
    rw_t = jnp.transpose(router_w, (0, 2, 1))
    sh_gu = jnp.concatenate([sh_w_gate, sh_w_up], axis=-1).astype(BF16)
    sh_d = sh_w_down.astype(BF16)

    x2 = x.reshape(T, D_MODEL)
    for l in range(L):
        gla_p = (wg_p[l], row(gla_b_gate)[l], row(gla_nw)[l])
        sgu_p = (sgu_wc[l], sgu_bt[l], row(sgu_ln_g)[l], row(sgu_ln_b)[l])
        ssd_p = (_per_head(ssm_conv_w[l]), row(_per_head(ssm_conv_b))[l], row(exp64(ssm_dt_bias))[l],
                 row(exp64(ssm_a_log))[l], row(exp64(ssm_d))[l], row(ssm_norm_w)[l])
        x1, x1t = _token_mixer(x2, B, S, w_p[l], gla_p, sgu_p, ssd_p, cos_t, sin_t, tb,
                               w_out_b[l], row(ln1_g)[l], row(ln1_b)[l])
        x2 = _moe(l, x1, x1t, rw_t[l], router_bias[l].reshape(N_EXPERTS, 1), exp_w_gate, exp_w_up, exp_w_down,
                  sh_gu[l], sh_d[l], row(ln2_g)[l], row(ln2_b)[l])
    return x2.reshape(B, S, D_MODEL)
```

```python
import functools
import math

import numpy as np
import jax
import jax.numpy as jnp
from jax import lax
from jax.experimental import pallas as pl
from jax.experimental.pallas import tpu as pltpu

F32 = jnp.float32
BF16 = jnp.bfloat16
I32 = jnp.int32

D_MODEL = 1024
N_LAYERS = 4
GROUP_W = 256

GLA_H, GLA_DK, GLA_DV, GLA_RANK, GLA_TAU = 4, 32, 64, 16, 16.0
SGU_G, SGU_CHUNK = 4, 128
SSM_H, SSM_P, SSM_G, SSM_N, SSM_CONV, SSM_CHUNK = 4, 64, 2, 64, 4, 128
DIL_H, DIL_DH, ROT_DIM, ROPE_THETA = 4, 64, 16, 500000.0
DIL_BRANCHES = ((128, 1), (512, 4), (2048, 16))

N_EXPERTS, TOP_K, N_EXPERT_GROUPS, TOPK_GROUPS, D_EXPERT = 128, 8, 8, 4, 256
ROUTED_SCALE = 1.0

ALPHA = (2 * N_LAYERS) ** 0.25
LN_EPS = 1e-5
RMS_EPS = 1e-6

W_GLA = 896
W_SGU = 512
W_SSD = 1280
W_DIL = 768
W_PROJ = W_GLA + W_SGU + W_SSD + W_DIL

BLK = 128
QBLK = 256
MOE_BLK = 256
TOK_BLK = 256
CMB_BLK = 128
NEG = -1e30

VMEM_LIMIT = 56 * 1024 * 1024


def _cparams(sem):
    return pltpu.CompilerParams(dimension_semantics=sem, vmem_limit_bytes=VMEM_LIMIT)


def _dot(a, b):
    return jnp.dot(a.astype(BF16), b.astype(BF16), preferred_element_type=F32)


def _dot_nt(a, b):
    return lax.dot_general(a.astype(BF16), b.astype(BF16), (((1,), (1,)), ((), ())),
                           preferred_element_type=F32)


def _dot_tn(a, b):
    return lax.dot_general(a.astype(BF16), b.astype(BF16), (((0,), (0,)), ((), ())),
                           preferred_element_type=F32)


def _split(a, n):
    out = []
    r = a
    for _ in range(n):
        p = r.astype(BF16)
        out.append(p)
        r = r - p.astype(F32)
    return out


def _dot_hi(a, b):
    a1, a2 = _split(a, 2)
    b1, b2 = _split(b, 2)
    return _dot(a1, b1) + (_dot(a1, b2) + _dot(a2, b1))


def _dot_nt_hi(a, b):
    a1, a2 = _split(a, 2)
    b1, b2 = _split(b, 2)
    return _dot_nt(a1, b1) + (_dot_nt(a1, b2) + _dot_nt(a2, b1))


def _dot_lhs01(m, b):
    b1, b2, b3 = _split(b, 3)
    return _dot(m, b1) + (_dot(m, b2) + _dot(m, b3))


def _dot_rhs01(a, m):
    a1, a2, a3 = _split(a, 3)
    return _dot(a1, m) + (_dot(a2, m) + _dot(a3, m))


def _iota(shape, dim):
    return lax.broadcasted_iota(I32, shape, dim)


def _sigmoid(x):
    return 1.0 / (1.0 + jnp.exp(-x))


def _silu(x):
    return x * _sigmoid(x)


def _softplus(x):
    return jnp.maximum(x, 0.0) + jnp.log1p(jnp.exp(-jnp.abs(x)))


def _gelu(x):
    return 0.5 * x * (1.0 + lax.erf(x * (2.0 ** -0.5)))


ROW_T = D_MODEL // 128


def _to_tiles(dst_ref, base, x, first=0):
    n = x.shape[0]
    for s in range(x.shape[1] // 128):
        dst_ref[pl.ds(base + first + s, n, stride=ROW_T), :] = x[:, 128 * s:128 * (s + 1)]


def _from_tiles(src_ref, base, n, pitch=ROW_T):
    return jnp.concatenate([src_ref[pl.ds(base + s, n, stride=pitch), :] for s in range(ROW_T)], axis=1)


def _inproj_kernel(x_ref, w_ref, o_gla, o_sgu, o_ssd, o_dil):
    xb = x_ref[...].astype(BF16)
    off = 0
    for o in (o_gla, o_sgu, o_ssd, o_dil):
        n = o.shape[1]
        o[...] = jnp.dot(xb, w_ref[:, off:off + n], preferred_element_type=F32)
        off += n


def _inproj(x2, w_p):
    T = x2.shape[0]
    tm = 512
    widths = (W_GLA, W_SGU, W_SSD, W_DIL)
    return pl.pallas_call(
        _inproj_kernel,
        grid=(T // tm,),
        in_specs=[pl.BlockSpec((tm, D_MODEL), lambda i: (i, 0)),
                  pl.BlockSpec((D_MODEL, W_PROJ), lambda i: (0, 0))],
        out_specs=[pl.BlockSpec((tm, w), lambda i: (i, 0)) for w in widths],
        out_shape=[jax.ShapeDtypeStruct((T, w), F32) for w in widths],
        compiler_params=_cparams(("arbitrary",)),
        name="inproj",
    )(x2, w_p)


def _gla_kernel(slab_ref, wg_ref, bg_ref, nw_ref, out_ref, st_ref):
    S = slab_ref.shape[1]
    st_ref[...] = jnp.zeros_like(st_ref)
    row = _iota((BLK, BLK), 0)
    col = _iota((BLK, BLK), 1)
    tril = col <= row
    tril_bf = jnp.where(tril, 1.0, 0.0).astype(BF16)
    qk_head = col >> 5
    v_head = _iota((BLK, GROUP_W), 1) >> 6
    st_diag = (_iota((GROUP_W, BLK), 0) >> 6) == (_iota((GROUP_W, BLK), 1) >> 5)
    seg = jnp.where((_iota((GROUP_W, GROUP_W), 0) >> 6) == (_iota((GROUP_W, GROUP_W), 1) >> 6),
                    1.0, 0.0).astype(BF16)
    wg = wg_ref[...]
    bg = bg_ref[...]
    nw = nw_ref[...]

    def body(n, carry):
        r0 = pl.multiple_of(n * BLK, BLK)
        blk = slab_ref[0, pl.ds(r0, BLK), :]
        q = blk[:, 0:128] * (GLA_DK ** -0.5)
        k = blk[:, 128:256]
        v = blk[:, 256:512]
        r = blk[:, 512:768]
        lr = blk[:, 768:896]
        z = _dot_hi(lr, wg) + bg
        gk = (jnp.minimum(z, 0.0) - jnp.log1p(jnp.exp(-jnp.abs(z)))) * (1.0 / GLA_TAU)
        g = _dot_lhs01(tril_bf, gk)
        g_last = g[BLK - 1:BLK, :]
        q_in = q * jnp.exp(g)
        k_out = k * jnp.exp(-g)
        k_in = k * jnp.exp(g_last - g)
        vb = v.astype(BF16)
        o = _dot_nt(q_in, st_ref[...])
        kob = k_out.astype(BF16)
        for h in range(GLA_H):
            qh = jnp.where(qk_head == h, q_in, 0.0)
            a = jnp.where(tril, _dot_nt(qh, kob), 0.0)
            o = o + jnp.where(v_head == h, _dot(a, vb), 0.0)
        st_ref[...] = st_ref[...] * jnp.exp(g_last) + jnp.where(st_diag, _dot_tn(vb, k_in), 0.0)
        ms = _dot_rhs01(o * o, seg) * (1.0 / GLA_DV)
        y = o * lax.rsqrt(ms + RMS_EPS) * nw * _silu(r)
        out_ref[0, pl.ds(r0, BLK), :] = y.astype(out_ref.dtype)
        return carry

    lax.fori_loop(0, S // BLK, body, 0)


def _gla(slab, wg_p, bg, nw_t):
    B, S, _ = slab.shape
    return pl.pallas_call(
        _gla_kernel,
        grid=(B,),
        in_specs=[pl.BlockSpec((1, S, W_GLA), lambda b: (b, 0, 0)),
                  pl.BlockSpec((128, 128), lambda b: (0, 0)),
                  pl.BlockSpec((1, 128), lambda b: (0, 0)),
                  pl.BlockSpec((1, GROUP_W), lambda b: (0, 0))],
        out_specs=pl.BlockSpec((1, S, GROUP_W), lambda b: (b, 0, 0)),
        out_shape=jax.ShapeDtypeStruct((B, S, GROUP_W), BF16),
        scratch_shapes=[pltpu.VMEM((GROUP_W, 128), F32)],
        compiler_params=_cparams(("arbitrary",)),
        name="gla",
    )(slab, wg_p, bg, nw_t)


def _sgu_kernel(slab_ref, w_ref, bias_ref, g_ref, b_ref, out_ref):
    S = slab_ref.shape[1]
    wmask = (_iota((BLK, SGU_G * BLK), 1) & (BLK - 1)) <= _iota((BLK, SGU_G * BLK), 0)
    w = jnp.where(wmask, w_ref[...], 0.0).astype(BF16)
    lane_grp = _iota((BLK, GROUP_W), 1) >> 6
    bias = bias_ref[...]
    ln_g = g_ref[...]
    ln_b = b_ref[...]

    def body(n, carry):
        r0 = pl.multiple_of(n * BLK, BLK)
        blk = slab_ref[0, pl.ds(r0, BLK), :]
        u = _gelu(blk[:, 0:256])
        v = _gelu(blk[:, 256:512])
        mu = jnp.mean(v, axis=-1, keepdims=True)
        var = jnp.mean(jnp.square(v - mu), axis=-1, keepdims=True)
        v = (v - mu) * lax.rsqrt(var + LN_EPS) * ln_g + ln_b
        vexp = jnp.concatenate([jnp.where(lane_grp == g, v, 0.0) for g in range(SGU_G)], axis=0)
        s = _dot(w, vexp) + bias
        out_ref[0, pl.ds(r0, BLK), :] = (u * s).astype(out_ref.dtype)
        return carry

    lax.fori_loop(0, S // BLK, body, 0)


def _sgu(slab, w_cat, bias_t, ln_g, ln_b):
    B, S, _ = slab.shape
    return pl.pallas_call(
        _sgu_kernel,
        grid=(B,),
        in_specs=[pl.BlockSpec((1, S, W_SGU), lambda b: (b, 0, 0)),
                  pl.BlockSpec((BLK, SGU_G * BLK), lambda b: (0, 0)),
                  pl.BlockSpec((BLK, GROUP_W), lambda b: (0, 0)),
                  pl.BlockSpec((1, GROUP_W), lambda b: (0, 0)),
                  pl.BlockSpec((1, GROUP_W), lambda b: (0, 0))],
        out_specs=pl.BlockSpec((1, S, GROUP_W), lambda b: (b, 0, 0)),
        out_shape=jax.ShapeDtypeStruct((B, S, GROUP_W), BF16),
        compiler_params=_cparams(("arbitrary",)),
        name="sgu",
    )(slab, w_cat, bias_t, ln_g, ln_b)


def _ssd_kernel(slab_ref, cw_ref, cb_ref, dtb_ref, a_ref, d_ref, nw_ref, out_ref, xpad_ref, st_ref):
    S = slab_ref.shape[1]
    CW = 3 * GROUP_W
    st_ref[...] = jnp.zeros_like(st_ref)
    xpad_ref[0:8, :] = jnp.zeros((8, CW), F32)
    xpad_ref[8:S + 8, :] = slab_ref[0, :, 256:256 + CW]
    row = _iota((BLK, BLK), 0)
    col = _iota((BLK, BLK), 1)
    tril = col <= row
    tril_bf = jnp.where(tril, 1.0, 0.0).astype(BF16)
    lane_head = _iota((BLK, GROUP_W), 1) >> 6
    st_diag = (_iota((GROUP_W, GROUP_W), 0) >> 6) == (_iota((GROUP_W, GROUP_W), 1) >> 6)
    cw = cw_ref[...]
    cb = cb_ref[...]
    dtb = dtb_ref[...]
    a_neg = -jnp.exp(a_ref[...])
    d_skip = d_ref[...]
    nw = nw_ref[...]

    def body(n, carry):
        r0 = pl.multiple_of(n * BLK, BLK)
        xw = xpad_ref[pl.ds(r0, BLK + 8), :]
        conv = cb + cw[0:1, :] * xw[5:5 + BLK, :]
        for w in range(1, SSM_CONV):
            conv = conv + cw[w:w + 1, :] * xw[5 + w:5 + w + BLK, :]
        xbc = _silu(conv)
        xs = xbc[:, 0:256]
        b_e = xbc[:, 256:512]
        c_e = xbc[:, 512:768]
        z = slab_ref[0, pl.ds(r0, BLK), 0:256]
        dt = _softplus(slab_ref[0, pl.ds(r0, BLK), 1024:1280] + dtb)
        acs = _dot_lhs01(tril_bf, dt * a_neg)
        acs_last = acs[BLK - 1:BLK, :]
        x_dt = (xs * dt).astype(BF16)
        acs_t = [jnp.transpose(acs[:, 0:128]), jnp.transpose(acs[:, 128:256])]
        y = _dot(c_e * jnp.exp(acs), st_ref[...])
        b_bf = b_e.astype(BF16)
        for h in range(SSM_H):
            a_col = jnp.broadcast_to(acs[:, 64 * h:64 * h + 1], (BLK, BLK))
            a_row = jnp.broadcast_to(acs_t[h // 2][64 * (h % 2):64 * (h % 2) + 1, :], (BLK, BLK))
            lm = jnp.exp(jnp.where(tril, a_col - a_row, -jnp.inf))
            ch = jnp.where(lane_head == h, c_e, 0.0)
            m = _dot_nt(ch, b_bf) * lm
            y = y + jnp.where(lane_head == h, _dot(m, x_dt), 0.0)
        new = _dot_tn(b_e * jnp.exp(acs_last - acs), x_dt)
        st_ref[...] = st_ref[...] * jnp.exp(acs_last) + jnp.where(st_diag, new, 0.0)
        y = (y + d_skip * xs) * _silu(z)
        parts = []
        for g in range(SSM_G):
            yg = y[:, 128 * g:128 * (g + 1)]
            ms = jnp.mean(yg * yg, axis=-1, keepdims=True)
            parts.append(yg * lax.rsqrt(ms + RMS_EPS))
        out = jnp.concatenate(parts, axis=1) * nw
        out_ref[0, pl.ds(r0, BLK), :] = out.astype(out_ref.dtype)
        return carry

    lax.fori_loop(0, S // BLK, body, 0)


def _ssd(slab, cw_e, cb_e, dtb_e, a_e, d_e, nw):
    B, S, _ = slab.shape
    CW = 3 * GROUP_W
    vec = lambda w: pl.BlockSpec((1, w), lambda b: (0, 0))
    return pl.pallas_call(
        _ssd_kernel,
        grid=(B,),
        in_specs=[pl.BlockSpec((1, S, W_SSD), lambda b: (b, 0, 0)),
                  pl.BlockSpec((SSM_CONV, CW), lambda b: (0, 0)),
                  vec(CW), vec(GROUP_W), vec(GROUP_W), vec(GROUP_W), vec(GROUP_W)],
        out_specs=pl.BlockSpec((1, S, GROUP_W), lambda b: (b, 0, 0)),
        out_shape=jax.ShapeDtypeStruct((B, S, GROUP_W), BF16),
        scratch_shapes=[pltpu.VMEM((S + 8, CW), F32), pltpu.VMEM((GROUP_W, GROUP_W), F32)],
        compiler_params=_cparams(("arbitrary",)),
        name="ssd",
    )(slab, cw_e, cb_e, dtb_e, a_e, d_e, nw)


def _branch_log_multiplicity(S):
    nq = S // QBLK
    r = np.arange(QBLK)[:, None]
    u = np.arange(S)[None, :]
    d = (nq - 1) * QBLK + r - u
    c = np.zeros(d.shape, np.float64)
    for window, dil in DIL_BRANCHES:
        c += (d >= 0) & (d % dil == 0) & (d <= window)
    with np.errstate(divide="ignore"):
        return np.where(c > 0, np.log(np.maximum(c, 1.0)), NEG).astype(np.float32)


def _dil_kernel(slab_ref, cos_ref, sin_ref, tb_ref, out_ref, q_ref, k_ref, v_ref):
    S = slab_ref.shape[1]
    nq = S // QBLK
    lane_d = _iota((1, 128), 1) & (DIL_DH - 1)

    def rope(t):
        fwd = pltpu.roll(t, 128 - ROT_DIM // 2, 1)
        bwd = pltpu.roll(t, ROT_DIM // 2, 1)
        return t * cos_ref[0] + jnp.where(lane_d < ROT_DIM // 2, fwd, bwd) * sin_ref[0]

    for c in range(0, GROUP_W, 128):
        q_ref[:, c:c + 128] = (rope(slab_ref[0, :, c:c + 128]) * (DIL_DH ** -0.5)).astype(BF16)
        k_ref[:, c:c + 128] = rope(slab_ref[0, :, 256 + c:384 + c]).astype(BF16)
    v_ref[...] = slab_ref[0, :, 512:768].astype(BF16)
    lane_head = _iota((QBLK, GROUP_W), 1) >> 6
    for i in range(nq):
        nk = (i + 1) * QBLK
        qi = q_ref[i * QBLK:(i + 1) * QBLK, :]
        acc = jnp.zeros((QBLK, GROUP_W), F32)
        for h in range(DIL_H):
            qh = jnp.where(lane_head == h, qi, jnp.zeros_like(qi))
            s = _dot_nt(qh, k_ref[0:nk, :]) + tb_ref[:, (nq - 1 - i) * QBLK:]
            m = jnp.max(s, axis=-1, keepdims=True)
            p = jnp.exp(s - m)
            den = jnp.sum(p, axis=-1, keepdims=True)
            o = _dot(p, v_ref[0:nk, :])
            acc = acc + jnp.where(lane_head == h, o / den, 0.0)
        out_ref[0, i * QBLK:(i + 1) * QBLK, :] = acc.astype(out_ref.dtype)


def _dil(slab, cos_t, sin_t, tb):
    B, S, _ = slab.shape
    return pl.pallas_call(
        _dil_kernel,
        grid=(B,),
        in_specs=[pl.BlockSpec((1, S, W_DIL), lambda b: (b, 0, 0)),
                  pl.BlockSpec((1, S, 128), lambda b: (b, 0, 0)),
                  pl.BlockSpec((1, S, 128), lambda b: (b, 0, 0)),
                  pl.BlockSpec((QBLK, S), lambda b: (0, 0), pipeline_mode=pl.Buffered(1))],
        out_specs=pl.BlockSpec((1, S, GROUP_W), lambda b: (b, 0, 0)),
        out_shape=jax.ShapeDtypeStruct((B, S, GROUP_W), BF16),
        scratch_shapes=[pltpu.VMEM((S, GROUP_W), BF16)] * 3,
        compiler_params=_cparams(("arbitrary",)),
        name="dil",
    )(slab, cos_t, sin_t, tb)


def _layernorm(x, g, b):
    mu = jnp.mean(x, axis=-1, keepdims=True)
    var = jnp.mean(jnp.square(x - mu), axis=-1, keepdims=True)
    return (x - mu) * lax.rsqrt(var + LN_EPS) * g + b


def _outproj_kernel(ya, yb, yc, yd, x_ref, w_ref, g_ref, b_ref, o_ref, ot_ref):
    y = jnp.concatenate([ya[...], yb[...], yc[...], yd[...]], axis=1)
    y = jnp.dot(y, w_ref[...], preferred_element_type=F32)
    x1 = _layernorm(ALPHA * x_ref[...] + y, g_ref[...], b_ref[...])
    o_ref[...] = x1
    _to_tiles(ot_ref, 0, x1)


def _outproj(ya, yb, yc, yd, x2, w_out, g, b):
    T = x2.shape[0]
    tm = 512
    ysp = pl.BlockSpec((tm, GROUP_W), lambda i: (i, 0))
    vec = pl.BlockSpec((1, D_MODEL), lambda i: (0, 0))
    return pl.pallas_call(
        _outproj_kernel,
        grid=(T // tm,),
        in_specs=[ysp, ysp, ysp, ysp,
                  pl.BlockSpec((tm, D_MODEL), lambda i: (i, 0)),
                  pl.BlockSpec((D_MODEL, D_MODEL), lambda i: (0, 0)), vec, vec],
        out_specs=[pl.BlockSpec((tm, D_MODEL), lambda i: (i, 0)),
                   pl.BlockSpec((tm * ROW_T, 128), lambda i: (i, 0))],
        out_shape=[jax.ShapeDtypeStruct((T, D_MODEL), F32),
                   jax.ShapeDtypeStruct((T * ROW_T, 128), F32)],
        compiler_params=_cparams(("arbitrary",)),
        name="outproj_ln",
    )(ya, yb, yc, yd, x2, w_out, g, b)


def _router_kernel(x_ref, rw_ref, rb_ref, idx_ref, gate_ref, pos_ref, cnt_ref, carry_ref):
    TB = x_ref.shape[0]
    E = N_EXPERTS
    per = E // N_EXPERT_GROUPS

    @pl.when(pl.program_id(0) == 0)
    def _():
        carry_ref[...] = jnp.zeros_like(carry_ref)

    logits = _dot_nt_hi(rw_ref[...], x_ref[...])
    scores = _sigmoid(logits)
    choice = scores + rb_ref[...]
    e_iota = _iota((E, TB), 0)

    l_iota = _iota((per, TB), 0)
    grp = []
    for g in range(N_EXPERT_GROUPS):
        cg = choice[per * g:per * (g + 1), :]
        m1 = jnp.max(cg, axis=0, keepdims=True)
        i1 = jnp.min(jnp.where(cg == m1, l_iota, per), axis=0, keepdims=True)
        m2 = jnp.max(jnp.where(l_iota == i1, -jnp.inf, cg), axis=0, keepdims=True)
        grp.append(m1 + m2)
    gs = jnp.concatenate(grp, axis=0)
    g_iota = _iota((N_EXPERT_GROUPS, TB), 0)
    keep = jnp.zeros((N_EXPERT_GROUPS, TB), jnp.bool_)
    for _ in range(TOPK_GROUPS):
        m = jnp.max(gs, axis=0, keepdims=True)
        gi = jnp.min(jnp.where(gs == m, g_iota, N_EXPERT_GROUPS), axis=0, keepdims=True)
        hit = g_iota == gi
        keep = keep | hit
        gs = jnp.where(hit, -jnp.inf, gs)
    keep_f = jnp.where(keep, 1.0, 0.0)
    keep_e = jnp.concatenate([jnp.broadcast_to(keep_f[g:g + 1, :], (per, TB))
                              for g in range(N_EXPERT_GROUPS)], axis=0)
    cm = jnp.where(keep_e > 0.5, choice, -jnp.inf)

    hits, idxs, gates = [], [], []
    onehot = jnp.zeros((E, TB), F32)
    for _ in range(TOP_K):
        m = jnp.max(cm, axis=0, keepdims=True)
        ei = jnp.min(jnp.where(cm == m, e_iota, E), axis=0, keepdims=True)
        hit = e_iota == ei
        hits.append(hit)
        idxs.append(ei)
        gates.append(jnp.sum(jnp.where(hit, scores, 0.0), axis=0, keepdims=True))
        onehot = onehot + jnp.where(hit, 1.0, 0.0)
        cm = jnp.where(hit, -jnp.inf, cm)
    gate = jnp.concatenate(gates, axis=0)
    gate = gate / jnp.sum(gate, axis=0, keepdims=True) * ROUTED_SCALE

    before = jnp.where(_iota((TB, TB), 0) < _iota((TB, TB), 1), 1.0, 0.0).astype(BF16)
    cnt = carry_ref[...] + _dot(onehot, before)
    pos = jnp.concatenate([jnp.sum(jnp.where(h, cnt, 0.0), axis=0, keepdims=True) for h in hits], axis=0)
    carry_ref[...] = carry_ref[...] + jnp.sum(onehot, axis=1, keepdims=True)

    idx_ref[...] = jnp.concatenate(idxs, axis=0)
    gate_ref[...] = gate
    pos_ref[...] = pos.astype(I32)
    cnt_ref[...] = jnp.broadcast_to(carry_ref[...], cnt_ref.shape)


def _router(x1, rw_t, rb):
    T = x1.shape[0]
    TB = TOK_BLK
    kt = pl.BlockSpec((TOP_K, TB), lambda i: (0, i))
    return pl.pallas_call(
        _router_kernel,
        grid=(T // TB,),
        in_specs=[pl.BlockSpec((TB, D_MODEL), lambda i: (i, 0)),
                  pl.BlockSpec((N_EXPERTS, D_MODEL), lambda i: (0, 0)),
                  pl.BlockSpec((N_EXPERTS, 1), lambda i: (0, 0))],
        out_specs=[kt, kt, kt, pl.BlockSpec((N_EXPERTS, 128), lambda i: (0, 0))],
        out_shape=[jax.ShapeDtypeStruct((TOP_K, T), I32), jax.ShapeDtypeStruct((TOP_K, T), F32),
                   jax.ShapeDtypeStruct((TOP_K, T), I32), jax.ShapeDtypeStruct((N_EXPERTS, 128), F32)],
        scratch_shapes=[pltpu.VMEM((N_EXPERTS, 1), F32)],
        compiler_params=_cparams(("arbitrary",)),
        name="router",
    )(x1, rw_t, rb)


def _slots_kernel(ps_ref, idx_ref, pos_ref, dest_ref):
    idx = idx_ref[...]

    def body(e, acc):
        return acc + jnp.where(idx == e, ps_ref[e], 0)

    dest_ref[...] = lax.fori_loop(0, N_EXPERTS, body, pos_ref[...], unroll=8)


def _slots(pad_start, idx, pos):
    T = idx.shape[1]
    tb = min(T, 2048)
    kt = pl.BlockSpec((TOP_K, tb), lambda i, ps: (0, i))
    return pl.pallas_call(
        _slots_kernel,
        grid_spec=pltpu.PrefetchScalarGridSpec(
            num_scalar_prefetch=1, grid=(T // tb,), in_specs=[kt, kt], out_specs=kt),
        out_shape=jax.ShapeDtypeStruct((TOP_K, T), I32),
        compiler_params=_cparams(("arbitrary",)),
        name="slots",
    )(pad_start, idx, pos)


ORD_CHUNK = 2048
ISSUE_PHASES = 4
UNROLL = 8


def _experts_kernel(be_ref, r0_ref, nv_ref, ord_hbm, x_hbm, wg_ref, wu_ref, wd_ref, yk_hbm,
                    ord_s, xbuf, ybuf, xb_s, h_s, a_s, wgu_s, wd_s, fsem, gsem, ssem):
    i = pl.program_id(0)
    nb = pl.num_programs(0)
    per_phase = MOE_BLK // ISSUE_PHASES
    rows = MOE_BLK * ROW_T
    dummy = (yk_hbm.shape[0] - ROW_T) // ROW_T

    def fetch(j):
        c0 = pl.multiple_of((r0_ref[j] >> 10) << 10, 1024)
        dst = ord_s.at[pl.ds(pl.multiple_of((j & 3) * ORD_CHUNK, ORD_CHUNK), ORD_CHUNK)]
        return pltpu.make_async_copy(ord_hbm.at[pl.ds(c0, ORD_CHUNK)], dst, fsem.at[j & 3])

    def tile(ref, r):
        return ref.at[pl.ds(pl.multiple_of(r * ROW_T, ROW_T), ROW_T), :]

    def issue(jg, js, lo, hi):
        if jg is not None:
            gbase = (jg & 3) * ORD_CHUNK + (r0_ref[jg] & 1023)
        if js is not None:
            sbase = (js & 3) * ORD_CHUNK + (r0_ref[js] & 1023)
            nvs = nv_ref[js]

        def body(it, c):
            for u in range(UNROLL):
                r = lo + it * UNROLL + u
                if jg is not None:
                    a = ord_s[gbase + r]
                    pltpu.make_async_copy(tile(x_hbm, a >> 3), tile(xbuf.at[jg & 1], r), gsem.at[jg & 1]).start()
                if js is not None:
                    a = jnp.where(r < nvs, ord_s[sbase + r], dummy)
                    pltpu.make_async_copy(tile(ybuf.at[js & 1], r), tile(yk_hbm, a),
                                          ssem.at[js & 1]).start(priority=1)
            return c

        lax.fori_loop(0, (hi - lo) // UNROLL, body, 0)

    @pl.when(i == 0)
    def _():
        fetch(0).start()
        fetch(0).wait()

        @pl.when(nb > 1)
        def _():
            fetch(1).start()

        issue(0, None, 0, MOE_BLK)

    @pl.when(i + 1 < nb)
    def _():
        fetch(i + 1).wait()

    @pl.when(i + 2 < nb)
    def _():
        fetch(i + 2).start()

    jn = jnp.minimum(i + 1, nb - 1)
    jp = jnp.maximum(i - 1, 0)
    jpp = jnp.maximum(i - 2, 0)
    active = nv_ref[i] > 0
    next_active = (i + 1 < nb) & (nv_ref[jn] > 0)
    prev_active = (i >= 1) & (nv_ref[jp] > 0)

    @pl.when(active | (i == 0))
    def _():
        pltpu.make_async_copy(x_hbm.at[pl.ds(0, rows), :], xbuf.at[i & 1], gsem.at[i & 1]).wait()

    @pl.when((i >= 2) & (nv_ref[jpp] > 0))
    def _():
        pltpu.make_async_copy(ybuf.at[i & 1], yk_hbm.at[pl.ds(0, rows), :], ssem.at[i & 1]).wait()

    @pl.when((i == 0) | (be_ref[i] != be_ref[jp]))
    def _():
        wgu_s[:, 0:D_EXPERT] = wg_ref[0, 0].astype(BF16)
        wgu_s[:, D_EXPERT:] = wu_ref[0, 0].astype(BF16)
        wd_s[...] = wd_ref[0, 0].astype(BF16)
    xslot = xbuf.at[i & 1]
    yslot = ybuf.at[i & 1]
    half = D_MODEL // 2

    def phase(p):
        if p == 0:
            xb_s[...] = _from_tiles(xslot, 0, MOE_BLK).astype(BF16)
            h_s[:, 0:D_EXPERT] = jnp.dot(xb_s[...], wgu_s[:, 0:D_EXPERT], preferred_element_type=F32)
        elif p == 1:
            h_s[:, D_EXPERT:] = jnp.dot(xb_s[...], wgu_s[:, D_EXPERT:], preferred_element_type=F32)
        elif p == 2:
            a_s[...] = (_silu(h_s[:, 0:D_EXPERT]) * h_s[:, D_EXPERT:]).astype(BF16)
            _to_tiles(yslot, 0, jnp.dot(a_s[...], wd_s[:, 0:half], preferred_element_type=F32))
        else:
            _to_tiles(yslot, 0, jnp.dot(a_s[...], wd_s[:, half:], preferred_element_type=F32), first=ROW_T // 2)

    for p in range(ISSUE_PHASES):
        lo, hi = p * per_phase, (p + 1) * per_phase
        pl.when(next_active & prev_active)(functools.partial(issue, jn, jp, lo, hi))
        pl.when(next_active & jnp.logical_not(prev_active))(functools.partial(issue, jn, None, lo, hi))
        pl.when(prev_active & jnp.logical_not(next_active))(functools.partial(issue, None, jp, lo, hi))
        pl.when(active)(functools.partial(phase, p))


def _experts(layer, blk_expert, rank0, nvalid, order, x1t, wg, wu, wd):
    nb = blk_expert.shape[0]
    rows = MOE_BLK * ROW_T
    wsel = lambda i, be, r0, nv: (layer, be[i], 0, 0)
    any_spec = pl.BlockSpec(memory_space=pl.ANY)
    return pl.pallas_call(
        _experts_kernel,
        grid_spec=pltpu.PrefetchScalarGridSpec(
            num_scalar_prefetch=3,
            grid=(nb,),
            in_specs=[any_spec, any_spec,
                      pl.BlockSpec((1, 1, D_MODEL, D_EXPERT), wsel),
                      pl.BlockSpec((1, 1, D_MODEL, D_EXPERT), wsel),
                      pl.BlockSpec((1, 1, D_EXPERT, D_MODEL), wsel)],
            out_specs=any_spec,
            scratch_shapes=[pltpu.SMEM((4 * ORD_CHUNK,), I32),
                            pltpu.VMEM((2, rows, 128), F32),
                            pltpu.VMEM((2, rows, 128), F32),
                            pltpu.VMEM((MOE_BLK, D_MODEL), BF16),
                            pltpu.VMEM((MOE_BLK, 2 * D_EXPERT), F32),
                            pltpu.VMEM((MOE_BLK, D_EXPERT), BF16),
                            pltpu.VMEM((D_MODEL, 2 * D_EXPERT), BF16),
                            pltpu.VMEM((D_EXPERT, D_MODEL), BF16),
                            pltpu.SemaphoreType.DMA((4,)),
                            pltpu.SemaphoreType.DMA((2,)),
                            pltpu.SemaphoreType.DMA((2,))]),
        out_shape=jax.ShapeDtypeStruct((x1t.shape[0] * TOP_K + ROW_T, 128), F32),
        compiler_params=_cparams(("arbitrary",)),
        name="experts",
    )(blk_expert, rank0, nvalid, order, x1t, wg, wu, wd)


def _combine_kernel(gate_ref, x_ref, yk_ref, sgu_ref, sd_ref, g_ref, b_ref, o_ref):
    TB = x_ref.shape[0]
    x = x_ref[...]
    h = jnp.dot(x.astype(BF16), sgu_ref[...], preferred_element_type=F32)
    a = _silu(h[:, 0:D_EXPERT]) * h[:, D_EXPERT:]
    y = ALPHA * x + jnp.dot(a.astype(BF16), sd_ref[...], preferred_element_type=F32)
    gate = gate_ref[...]
    for k in range(TOP_K):
        y = y + gate[:, k:k + 1] * _from_tiles(yk_ref, k * ROW_T, TB, pitch=TOP_K * ROW_T)
    o_ref[...] = _layernorm(y, g_ref[...], b_ref[...])


def _combine(gate_t, x1, yk, sh_gu, sh_d, g, b):
    T = x1.shape[0]
    TB = CMB_BLK
    vec = pl.BlockSpec((1, D_MODEL), lambda i: (0, 0))
    return pl.pallas_call(
        _combine_kernel,
        grid=(T // TB,),
        in_specs=[pl.BlockSpec((TB, TOP_K), lambda i: (i, 0)),
                  pl.BlockSpec((TB, D_MODEL), lambda i: (i, 0)),
                  pl.BlockSpec((TB * TOP_K * ROW_T, 128), lambda i: (i, 0)),
                  pl.BlockSpec((D_MODEL, 2 * D_EXPERT), lambda i: (0, 0)),
                  pl.BlockSpec((D_EXPERT, D_MODEL), lambda i: (0, 0)), vec, vec],
        out_specs=pl.BlockSpec((TB, D_MODEL), lambda i: (i, 0)),
        out_shape=jax.ShapeDtypeStruct((T, D_MODEL), F32),
        compiler_params=_cparams(("arbitrary",)),
        name="combine_ln",
    )(gate_t, x1, yk, sh_gu, sh_d, g, b)


def _token_mixer(x2, B, S, w_p, gla_p, sgu_p, ssd_p, cos_t, sin_t, tb, w_out, ln_g, ln_b):
    s_gla, s_sgu, s_ssd, s_dil = _inproj(x2, w_p)
    ya = _gla(s_gla.reshape(B, S, W_GLA), *gla_p)
    yb = _sgu(s_sgu.reshape(B, S, W_SGU), *sgu_p)
    yc = _ssd(s_ssd.reshape(B, S, W_SSD), *ssd_p)
    yd = _dil(s_dil.reshape(B, S, W_DIL), cos_t, sin_t, tb)
    T = B * S
    flat = lambda y: y.reshape(T, GROUP_W)
    return _outproj(flat(ya), flat(yb), flat(yc), flat(yd), x2, w_out, ln_g, ln_b)


def _moe(layer, x1, x1t, rw_t, rb, wg, wu, wd, sh_gu, sh_d, ln_g, ln_b):
    T = x1.shape[0]
    idx, gate, pos, cnt = _router(x1, rw_t, rb)
    counts = cnt[:, 0].astype(I32)
    start = jnp.cumsum(counts) - counts
    rank = _slots(start.astype(I32), idx, pos)
    order = jnp.argsort(rank.T.reshape(T * TOP_K)).astype(I32)
    order = jnp.concatenate([order, jnp.zeros((ORD_CHUNK,), I32)])
    e_blocks = (counts + MOE_BLK - 1) // MOE_BLK
    blk_end = jnp.cumsum(e_blocks)
    n_used = blk_end[-1]
    nb = T * TOP_K // MOE_BLK + N_EXPERTS + 2
    i = jnp.arange(nb, dtype=I32)
    used = i < n_used
    ic = jnp.minimum(i, n_used - 1)
    blk_expert = jnp.minimum(jnp.sum((blk_end[None, :] <= ic[:, None]).astype(I32), axis=1), N_EXPERTS - 1)
    off = (ic - (blk_end - e_blocks)[blk_expert]) * MOE_BLK
    nvalid = jnp.where(used, jnp.clip(counts[blk_expert] - off, 0, MOE_BLK), 0).astype(I32)
    rank0 = jnp.where(used, start[blk_expert] + off, 0).astype(I32)
    yk = _experts(layer, blk_expert.astype(I32), rank0, nvalid, order, x1t, wg, wu, wd)
    return _combine(gate.T, x1, yk, sh_gu, sh_d, ln_g, ln_b)


def _prep_w_in(w_in):
    o = np.cumsum((0, 128, 128, 256, 256, 16, 256, 256, 256, 512, 4, 256, 256, 256))
    c = lambda i: w_in[..., o[i]:o[i + 1]]
    xbc = c(8)
    x_cols = xbc[..., 0:256]
    per_head = lambda m: jnp.concatenate(
        [m[..., SSM_N * (h // (SSM_H // SSM_G)):SSM_N * (h // (SSM_H // SSM_G) + 1)] for h in range(SSM_H)], axis=-1)
    b_cols = per_head(xbc[..., 256:384])
    c_cols = per_head(xbc[..., 384:512])
    dt_cols = jnp.repeat(c(9), SSM_P, axis=-1)
    pad = jnp.zeros(w_in.shape[:-1] + (W_GLA - o[5],), w_in.dtype)
    cols = [c(0), c(1), c(2), c(3), c(4), pad, c(5), c(6), c(7), x_cols, b_cols, c_cols, dt_cols,
            c(10), c(11), c(12)]
    return jnp.concatenate(cols, axis=-1).astype(BF16)


def _per_head(v):
    rep = SSM_H // SSM_G
    xp = v[..., 0:256]
    pick = lambda m: jnp.concatenate([m[..., SSM_N * (h // rep):SSM_N * (h // rep + 1)] for h in range(SSM_H)], axis=-1)
    return jnp.concatenate([xp, pick(v[..., 256:384]), pick(v[..., 384:512])], axis=-1)


def kernel(x, positions, w_in, gla_w_gate, gla_b_gate, gla_norm_w, sgu_ln_g, sgu_ln_b, sgu_w, sgu_b, ssm_conv_w, ssm_conv_b, ssm_dt_bias, ssm_a_log, ssm_d, ssm_norm_w, w_out, ln1_g, ln1_b, router_w, router_bias, exp_w_gate, exp_w_up, exp_w_down, sh_w_gate, sh_w_up, sh_w_down, ln2_g, ln2_b):
    B, S, _ = x.shape
    L = w_in.shape[0]
    T = B * S

    inv_freq = ROPE_THETA ** (-jnp.arange(0, ROT_DIM, 2, dtype=F32) / ROT_DIM)
    ang = positions.astype(F32)[..., None] * inv_freq
    cos, sin = jnp.cos(ang), jnp.sin(ang)
    ones = jnp.ones((B, S, DIL_DH - ROT_DIM), F32)
    cos_t = jnp.tile(jnp.concatenate([cos, cos, ones], axis=-1), (1, 1, 2))
    sin_t = jnp.tile(jnp.concatenate([-sin, sin, 0.0 * ones], axis=-1), (1, 1, 2))
    tb = jnp.asarray(_branch_log_multiplicity(S))

    w_p = _prep_w_in(w_in)
    wg_p = jnp.pad(gla_w_gate, ((0, 0), (0, 128 - GLA_RANK), (0, 0)))
    row = lambda v: v.reshape(L, 1, -1)
    gla_nw = jnp.tile(gla_norm_w, (1, GLA_H))
    sgu_wc = jnp.transpose(sgu_w, (0, 2, 1, 3)).reshape(L, SGU_CHUNK, SGU_G * SGU_CHUNK)
    sgu_bt = jnp.repeat(jnp.transpose(sgu_b, (0, 2, 1)), GROUP_W // SGU_G, axis=-1)
    exp64 = lambda v: jnp.repeat(v, SSM_P, axis=-1)
    w_out_b = w_out.astype(BF16)
    rw_t = jnp.transpose(router_w, (0, 2, 1))
    sh_gu = jnp.concatenate([sh_w_gate, sh_w_up], axis=-1).astype(BF16)
    sh_d = sh_w_down.astype(BF16)

    x2 = x.reshape(T, D_MODEL)
    for l in range(L):
        gla_p = (wg_p[l], row(gla_b_gate)[l], row(gla_nw)[l])
        sgu_p = (sgu_wc[l], sgu_bt[l], row(sgu_ln_g)[l], row(sgu_ln_b)[l])
        ssd_p = (_per_head(ssm_conv_w[l]), row(_per_head(ssm_conv_b))[l], row(exp64(ssm_dt_bias))[l],
                 row(exp64(ssm_a_log))[l], row(exp64(ssm_d))[l], row(ssm_norm_w)[l])
        x1, x1t = _token_mixer(x2, B, S, w_p[l], gla_p, sgu_p, ssd_p, cos_t, sin_t, tb,
                               w_out_b[l], row(ln1_g)[l], row(ln1_b)[l])
        x2 = _moe(l, x1, x1t, rw_t[l], router_bias[l].reshape(N_EXPERTS, 1), exp_w_gate, exp_w_up, exp_w_down,
                  sh_gu[l], sh_d[l], row(ln2_g)[l], row(ln2_b)[l])
    return x2.reshape(B, S, D_MODEL)
```

```python
import functools
import math

import numpy as np
import jax
import jax.numpy as jnp
from jax import lax
from jax.experimental import pallas as pl
from jax.experimental.pallas import tpu as pltpu

F32 = jnp.float32
BF16 = jnp.bfloat16
I32 = jnp.int32

D_MODEL = 1024
N_LAYERS = 4
GROUP_W = 256

GLA_H, GLA_DK, GLA_DV, GLA_RANK, GLA_TAU = 4, 32, 64, 16, 16.0
SGU_G, SGU_CHUNK = 4, 128
SSM_H, SSM_P, SSM_G, SSM_N, SSM_CONV, SSM_CHUNK = 4, 64, 2, 64, 4, 128
DIL_H, DIL_DH, ROT_DIM, ROPE_THETA = 4, 64, 16, 500000.0
DIL_BRANCHES = ((128, 1), (512, 4), (2048, 16))

N_EXPERTS, TOP_K, N_EXPERT_GROUPS, TOPK_GROUPS, D_EXPERT = 128, 8, 8, 4, 256
ROUTED_SCALE = 1.0

ALPHA = (2 * N_LAYERS) ** 0.25
LN_EPS = 1e-5
RMS_EPS = 1e-6

W_GLA = 896
W_SGU = 512
W_SSD = 1280
W_DIL = 768
W_PROJ = W_GLA + W_SGU + W_SSD + W_DIL

BLK = 128
QBLK = 256
MOE_BLK = 256
TOK_BLK = 256
CMB_BLK = 128
NEG = -1e30

VMEM_LIMIT = 56 * 1024 * 1024


def _cparams(sem):
    return pltpu.CompilerParams(dimension_semantics=sem, vmem_limit_bytes=VMEM_LIMIT)


def _dot(a, b):
    return jnp.dot(a.astype(BF16), b.astype(BF16), preferred_element_type=F32)


def _dot_nt(a, b):
    return lax.dot_general(a.astype(BF16), b.astype(BF16), (((1,), (1,)), ((), ())),
                           preferred_element_type=F32)


def _dot_tn(a, b):
    return lax.dot_general(a.astype(BF16), b.astype(BF16), (((0,), (0,)), ((), ())),
                           preferred_element_type=F32)


def _split(a, n):
    out = []
    r = a
    for _ in range(n):
        p = r.astype(BF16)
        out.append(p)
        r = r - p.astype(F32)
    return out


def _dot_hi(a, b):
    a1, a2 = _split(a, 2)
    b1, b2 = _split(b, 2)
    return _dot(a1, b1) + (_dot(a1, b2) + _dot(a2, b1))


def _dot_nt_hi(a, b):
    a1, a2 = _split(a, 2)
    b1, b2 = _split(b, 2)
    return _dot_nt(a1, b1) + (_dot_nt(a1, b2) + _dot_nt(a2, b1))


def _dot_lhs01(m, b):
    b1, b2, b3 = _split(b, 3)
    return _dot(m, b1) + (_dot(m, b2) + _dot(m, b3))


def _dot_rhs01(a, m):
    a1, a2, a3 = _split(a, 3)
    return _dot(a1, m) + (_dot(a2, m) + _dot(a3, m))


def _iota(shape, dim):
    return lax.broadcasted_iota(I32, shape, dim)


def _sigmoid(x):
    return 1.0 / (1.0 + jnp.exp(-x))


def _silu(x):
    return x * _sigmoid(x)


def _softplus(x):
    return jnp.maximum(x, 0.0) + jnp.log1p(jnp.exp(-jnp.abs(x)))


def _gelu(x):
    return 0.5 * x * (1.0 + lax.erf(x * (2.0 ** -0.5)))


def _inproj_kernel(x_ref, w_ref, o_gla, o_sgu, o_ssd, o_dil):
    xb = x_ref[...].astype(BF16)
    off = 0
    for o in (o_gla, o_sgu, o_ssd, o_dil):
        n = o.shape[1]
        o[...] = jnp.dot(xb, w_ref[:, off:off + n], preferred_element_type=F32)
        off += n


def _inproj(x2, w_p):
    T = x2.shape[0]
    tm = 512
    widths = (W_GLA, W_SGU, W_SSD, W_DIL)
    return pl.pallas_call(
        _inproj_kernel,
        grid=(T // tm,),
        in_specs=[pl.BlockSpec((tm, D_MODEL), lambda i: (i, 0)),
                  pl.BlockSpec((D_MODEL, W_PROJ), lambda i: (0, 0))],
        out_specs=[pl.BlockSpec((tm, w), lambda i: (i, 0)) for w in widths],
        out_shape=[jax.ShapeDtypeStruct((T, w), F32) for w in widths],
        compiler_params=_cparams(("arbitrary",)),
        name="inproj",
    )(x2, w_p)


GLA_NB = 2


def _gla_kernel(slab_ref, wg_ref, bg_ref, nw_ref, out_ref, st_ref):
    S = slab_ref.shape[1]
    st_ref[...] = jnp.zeros_like(st_ref)
    row = _iota((BLK, BLK), 0)
    col = _iota((BLK, BLK), 1)
    tril = col <= row
    tril_bf = jnp.where(tril, 1.0, 0.0).astype(BF16)
    qk_head = col >> 5
    v_head = _iota((BLK, GROUP_W), 1) >> 6
    st_diag = (_iota((GROUP_W, BLK), 0) >> 6) == (_iota((GROUP_W, BLK), 1) >> 5)
    seg = jnp.where((_iota((GROUP_W, GROUP_W), 0) >> 6) == (_iota((GROUP_W, GROUP_W), 1) >> 6),
                    1.0, 0.0).astype(BF16)
    wg = wg_ref[...]
    bg = bg_ref[...]
    nw = nw_ref[...]

    def block(b, r0):
        blk = slab_ref[b, pl.ds(r0, BLK), :]
        q = blk[:, 0:128] * (GLA_DK ** -0.5)
        k = blk[:, 128:256]
        v = blk[:, 256:512]
        r = blk[:, 512:768]
        lr = blk[:, 768:896]
        z = _dot_hi(lr, wg) + bg
        gk = (jnp.minimum(z, 0.0) - jnp.log1p(jnp.exp(-jnp.abs(z)))) * (1.0 / GLA_TAU)
        g = _dot_lhs01(tril_bf, gk)
        g_last = g[BLK - 1:BLK, :]
        q_in = q * jnp.exp(g)
        k_out = k * jnp.exp(-g)
        k_in = k * jnp.exp(g_last - g)
        vb = v.astype(BF16)
        o = _dot_nt(q_in, st_ref[b])
        kob = k_out.astype(BF16)
        for h in range(GLA_H):
            qh = jnp.where(qk_head == h, q_in, 0.0)
            a = jnp.where(tril, _dot_nt(qh, kob), 0.0)
            o = o + jnp.where(v_head == h, _dot(a, vb), 0.0)
        st_ref[b] = st_ref[b] * jnp.exp(g_last) + jnp.where(st_diag, _dot_tn(vb, k_in), 0.0)
        ms = _dot_rhs01(o * o, seg) * (1.0 / GLA_DV)
        y = o * lax.rsqrt(ms + RMS_EPS) * nw * _silu(r)
        out_ref[b, pl.ds(r0, BLK), :] = y.astype(out_ref.dtype)

    def body(n, carry):
        r0 = pl.multiple_of(n * BLK, BLK)
        for b in range(slab_ref.shape[0]):
            block(b, r0)
        return carry

    lax.fori_loop(0, S // BLK, body, 0)


def _gla(slab, wg_p, bg, nw_t):
    B, S, _ = slab.shape
    nb = GLA_NB if B % GLA_NB == 0 else 1
    return pl.pallas_call(
        _gla_kernel,
        grid=(B // nb,),
        in_specs=[pl.BlockSpec((nb, S, W_GLA), lambda b: (b, 0, 0)),
                  pl.BlockSpec((128, 128), lambda b: (0, 0)),
                  pl.BlockSpec((1, 128), lambda b: (0, 0)),
                  pl.BlockSpec((1, GROUP_W), lambda b: (0, 0))],
        out_specs=pl.BlockSpec((nb, S, GROUP_W), lambda b: (b, 0, 0)),
        out_shape=jax.ShapeDtypeStruct((B, S, GROUP_W), BF16),
        scratch_shapes=[pltpu.VMEM((nb, GROUP_W, 128), F32)],
        compiler_params=_cparams(("arbitrary",)),
        name="gla",
    )(slab, wg_p, bg, nw_t)


def _sgu_kernel(slab_ref, w_ref, bias_ref, g_ref, b_ref, out_ref):
    S = slab_ref.shape[1]
    wmask = (_iota((BLK, SGU_G * BLK), 1) & (BLK - 1)) <= _iota((BLK, SGU_G * BLK), 0)
    w = jnp.where(wmask, w_ref[...], 0.0).astype(BF16)
    lane_grp = _iota((BLK, GROUP_W), 1) >> 6
    bias = bias_ref[...]
    ln_g = g_ref[...]
    ln_b = b_ref[...]

    def body(n, carry):
        r0 = pl.multiple_of(n * BLK, BLK)
        blk = slab_ref[0, pl.ds(r0, BLK), :]
        u = _gelu(blk[:, 0:256])
        v = _gelu(blk[:, 256:512])
        mu = jnp.mean(v, axis=-1, keepdims=True)
        var = jnp.mean(jnp.square(v - mu), axis=-1, keepdims=True)
        v = (v - mu) * lax.rsqrt(var + LN_EPS) * ln_g + ln_b
        vexp = jnp.concatenate([jnp.where(lane_grp == g, v, 0.0) for g in range(SGU_G)], axis=0)
        s = _dot(w, vexp) + bias
        out_ref[0, pl.ds(r0, BLK), :] = (u * s).astype(out_ref.dtype)
        return carry

    lax.fori_loop(0, S // BLK, body, 0)


def _sgu(slab, w_cat, bias_t, ln_g, ln_b):
    B, S, _ = slab.shape
    return pl.pallas_call(
        _sgu_kernel,
        grid=(B,),
        in_specs=[pl.BlockSpec((1, S, W_SGU), lambda b: (b, 0, 0)),
                  pl.BlockSpec((BLK, SGU_G * BLK), lambda b: (0, 0)),
                  pl.BlockSpec((BLK, GROUP_W), lambda b: (0, 0)),
                  pl.BlockSpec((1, GROUP_W), lambda b: (0, 0)),
                  pl.BlockSpec((1, GROUP_W), lambda b: (0, 0))],
        out_specs=pl.BlockSpec((1, S, GROUP_W), lambda b: (b, 0, 0)),
        out_shape=jax.ShapeDtypeStruct((B, S, GROUP_W), BF16),
        compiler_params=_cparams(("arbitrary",)),
        name="sgu",
    )(slab, w_cat, bias_t, ln_g, ln_b)


def _ssd_kernel(slab_ref, cw_ref, cb_ref, dtb_ref, a_ref, d_ref, nw_ref, out_ref, xpad_ref, st_ref):
    S = slab_ref.shape[1]
    CW = 3 * GROUP_W
    st_ref[...] = jnp.zeros_like(st_ref)
    xpad_ref[0:8, :] = jnp.zeros((8, CW), F32)
    xpad_ref[8:S + 8, :] = slab_ref[0, :, 256:256 + CW]
    row = _iota((BLK, BLK), 0)
    col = _iota((BLK, BLK), 1)
    tril = col <= row
    tril_bf = jnp.where(tril, 1.0, 0.0).astype(BF16)
    lane_head = _iota((BLK, GROUP_W), 1) >> 6
    st_diag = (_iota((GROUP_W, GROUP_W), 0) >> 6) == (_iota((GROUP_W, GROUP_W), 1) >> 6)
    cw = cw_ref[...]
    cb = cb_ref[...]
    dtb = dtb_ref[...]
    a_neg = -jnp.exp(a_ref[...])
    d_skip = d_ref[...]
    nw = nw_ref[...]

    def body(n, carry):
        r0 = pl.multiple_of(n * BLK, BLK)
        xw = xpad_ref[pl.ds(r0, BLK + 8), :]
        conv = cb + cw[0:1, :] * xw[5:5 + BLK, :]
        for w in range(1, SSM_CONV):
            conv = conv + cw[w:w + 1, :] * xw[5 + w:5 + w + BLK, :]
        xbc = _silu(conv)
        xs = xbc[:, 0:256]
        b_e = xbc[:, 256:512]
        c_e = xbc[:, 512:768]
        z = slab_ref[0, pl.ds(r0, BLK), 0:256]
        dt = _softplus(slab_ref[0, pl.ds(r0, BLK), 1024:1280] + dtb)
        acs = _dot_lhs01(tril_bf, dt * a_neg)
        acs_last = acs[BLK - 1:BLK, :]
        x_dt = (xs * dt).astype(BF16)
        acs_t = [jnp.transpose(acs[:, 0:128]), jnp.transpose(acs[:, 128:256])]
        y = _dot(c_e * jnp.exp(acs), st_ref[...])
        b_bf = b_e.astype(BF16)
        for h in range(SSM_H):
            a_col = jnp.broadcast_to(acs[:, 64 * h:64 * h + 1], (BLK, BLK))
            a_row = jnp.broadcast_to(acs_t[h // 2][64 * (h % 2):64 * (h % 2) + 1, :], (BLK, BLK))
            lm = jnp.exp(jnp.where(tril, a_col - a_row, -jnp.inf))
            ch = jnp.where(lane_head == h, c_e, 0.0)
            m = _dot_nt(ch, b_bf) * lm
            y = y + jnp.where(lane_head == h, _dot(m, x_dt), 0.0)
        new = _dot_tn(b_e * jnp.exp(acs_last - acs), x_dt)
        st_ref[...] = st_ref[...] * jnp.exp(acs_last) + jnp.where(st_diag, new, 0.0)
        y = (y + d_skip * xs) * _silu(z)
        parts = []
        for g in range(SSM_G):
            yg = y[:, 128 * g:128 * (g + 1)]
            ms = jnp.mean(yg * yg, axis=-1, keepdims=True)
            parts.append(yg * lax.rsqrt(ms + RMS_EPS))
        out = jnp.concatenate(parts, axis=1) * nw
        out_ref[0, pl.ds(r0, BLK), :] = out.astype(out_ref.dtype)
        return carry

    lax.fori_loop(0, S // BLK, body, 0)


def _ssd(slab, cw_e, cb_e, dtb_e, a_e, d_e, nw):
    B, S, _ = slab.shape
    CW = 3 * GROUP_W
    vec = lambda w: pl.BlockSpec((1, w), lambda b: (0, 0))
    return pl.pallas_call(
        _ssd_kernel,
        grid=(B,),
        in_specs=[pl.BlockSpec((1, S, W_SSD), lambda b: (b, 0, 0)),
                  pl.BlockSpec((SSM_CONV, CW), lambda b: (0, 0)),
                  vec(CW), vec(GROUP_W), vec(GROUP_W), vec(GROUP_W), vec(GROUP_W)],
        out_specs=pl.BlockSpec((1, S, GROUP_W), lambda b: (b, 0, 0)),
        out_shape=jax.ShapeDtypeStruct((B, S, GROUP_W), BF16),
        scratch_shapes=[pltpu.VMEM((S + 8, CW), F32), pltpu.VMEM((GROUP_W, GROUP_W), F32)],
        compiler_params=_cparams(("arbitrary",)),
        name="ssd",
    )(slab, cw_e, cb_e, dtb_e, a_e, d_e, nw)


def _branch_log_multiplicity(S):
    nq = S // QBLK
    r = np.arange(QBLK)[:, None]
    u = np.arange(S)[None, :]
    d = (nq - 1) * QBLK + r - u
    c = np.zeros(d.shape, np.float64)
    for window, dil in DIL_BRANCHES:
        c += (d >= 0) & (d % dil == 0) & (d <= window)
    with np.errstate(divide="ignore"):
        return np.where(c > 0, np.log(np.maximum(c, 1.0)), NEG).astype(np.float32)


def _dil_kernel(slab_ref, cos_ref, sin_ref, tb_ref, out_ref, q_ref, k_ref, v_ref):
    S = slab_ref.shape[1]
    nq = S // QBLK
    lane_d = _iota((1, 128), 1) & (DIL_DH - 1)

    def rope(t):
        fwd = pltpu.roll(t, 128 - ROT_DIM // 2, 1)
        bwd = pltpu.roll(t, ROT_DIM // 2, 1)
        return t * cos_ref[0] + jnp.where(lane_d < ROT_DIM // 2, fwd, bwd) * sin_ref[0]

    for c in range(0, GROUP_W, 128):
        q_ref[:, c:c + 128] = (rope(slab_ref[0, :, c:c + 128]) * (DIL_DH ** -0.5)).astype(BF16)
        k_ref[:, c:c + 128] = rope(slab_ref[0, :, 256 + c:384 + c]).astype(BF16)
    v_ref[...] = slab_ref[0, :, 512:768].astype(BF16)
    lane_head = _iota((QBLK, GROUP_W), 1) >> 6
    for i in range(nq):
        nk = (i + 1) * QBLK
        qi = q_ref[i * QBLK:(i + 1) * QBLK, :]
        acc = jnp.zeros((QBLK, GROUP_W), F32)
        for h in range(DIL_H):
            qh = jnp.where(lane_head == h, qi, jnp.zeros_like(qi))
            s = _dot_nt(qh, k_ref[0:nk, :]) + tb_ref[:, (nq - 1 - i) * QBLK:]
            m = jnp.max(s, axis=-1, keepdims=True)
            p = jnp.exp(s - m)
            den = jnp.sum(p, axis=-1, keepdims=True)
            o = _dot(p, v_ref[0:nk, :])
            acc = acc + jnp.where(lane_head == h, o / den, 0.0)
        out_ref[0, i * QBLK:(i + 1) * QBLK, :] = acc.astype(out_ref.dtype)


def _dil(slab, cos_t, sin_t, tb):
    B, S, _ = slab.shape
    return pl.pallas_call(
        _dil_kernel,
        grid=(B,),
        in_specs=[pl.BlockSpec((1, S, W_DIL), lambda b: (b, 0, 0)),
                  pl.BlockSpec((1, S, 128), lambda b: (b, 0, 0)),
                  pl.BlockSpec((1, S, 128), lambda b: (b, 0, 0)),
                  pl.BlockSpec((QBLK, S), lambda b: (0, 0), pipeline_mode=pl.Buffered(1))],
        out_specs=pl.BlockSpec((1, S, GROUP_W), lambda b: (b, 0, 0)),
        out_shape=jax.ShapeDtypeStruct((B, S, GROUP_W), BF16),
        scratch_shapes=[pltpu.VMEM((S, GROUP_W), BF16)] * 3,
        compiler_params=_cparams(("arbitrary",)),
        name="dil",
    )(slab, cos_t, sin_t, tb)


def _layernorm(x, g, b):
    mu = jnp.mean(x, axis=-1, keepdims=True)
    var = jnp.mean(jnp.square(x - mu), axis=-1, keepdims=True)
    return (x - mu) * lax.rsqrt(var + LN_EPS) * g + b


def _outproj_kernel(ya, yb, yc, yd, x_ref, w_ref, g_ref, b_ref, o_ref):
    y = jnp.concatenate([ya[...], yb[...], yc[...], yd[...]], axis=1)
    y = jnp.dot(y, w_ref[...], preferred_element_type=F32)
    o_ref[...] = _layernorm(ALPHA * x_ref[...] + y, g_ref[...], b_ref[...])


def _outproj(ya, yb, yc, yd, x2, w_out, g, b):
    T = x2.shape[0]
    tm = 512
    ysp = pl.BlockSpec((tm, GROUP_W), lambda i: (i, 0))
    vec = pl.BlockSpec((1, D_MODEL), lambda i: (0, 0))
    return pl.pallas_call(
        _outproj_kernel,
        grid=(T // tm,),
        in_specs=[ysp, ysp, ysp, ysp,
                  pl.BlockSpec((tm, D_MODEL), lambda i: (i, 0)),
                  pl.BlockSpec((D_MODEL, D_MODEL), lambda i: (0, 0)), vec, vec],
        out_specs=pl.BlockSpec((tm, D_MODEL), lambda i: (i, 0)),
        out_shape=jax.ShapeDtypeStruct((T, D_MODEL), F32),
        compiler_params=_cparams(("arbitrary",)),
        name="outproj_ln",
    )(ya, yb, yc, yd, x2, w_out, g, b)


def _router_kernel(x_ref, rw_ref, rb_ref, idx_ref, gate_ref, pos_ref, cnt_ref, carry_ref):
    TB = x_ref.shape[0]
    E = N_EXPERTS
    per = E // N_EXPERT_GROUPS

    @pl.when(pl.program_id(0) == 0)
    def _():
        carry_ref[...] = jnp.zeros_like(carry_ref)

    logits = _dot_nt_hi(rw_ref[...], x_ref[...])
    scores = _sigmoid(logits)
    choice = scores + rb_ref[...]
    e_iota = _iota((E, TB), 0)

    l_iota = _iota((per, TB), 0)
    grp = []
    for g in range(N_EXPERT_GROUPS):
        cg = choice[per * g:per * (g + 1), :]
        m1 = jnp.max(cg, axis=0, keepdims=True)
        i1 = jnp.min(jnp.where(cg == m1, l_iota, per), axis=0, keepdims=True)
        m2 = jnp.max(jnp.where(l_iota == i1, -jnp.inf, cg), axis=0, keepdims=True)
        grp.append(m1 + m2)
    gs = jnp.concatenate(grp, axis=0)
    g_iota = _iota((N_EXPERT_GROUPS, TB), 0)
    keep = jnp.zeros((N_EXPERT_GROUPS, TB), jnp.bool_)
    for _ in range(TOPK_GROUPS):
        m = jnp.max(gs, axis=0, keepdims=True)
        gi = jnp.min(jnp.where(gs == m, g_iota, N_EXPERT_GROUPS), axis=0, keepdims=True)
        hit = g_iota == gi
        keep = keep | hit
        gs = jnp.where(hit, -jnp.inf, gs)
    keep_f = jnp.where(keep, 1.0, 0.0)
    keep_e = jnp.concatenate([jnp.broadcast_to(keep_f[g:g + 1, :], (per, TB))
                              for g in range(N_EXPERT_GROUPS)], axis=0)
    cm = jnp.where(keep_e > 0.5, choice, -jnp.inf)

    hits, idxs, gates = [], [], []
    onehot = jnp.zeros((E, TB), F32)
    for _ in range(TOP_K):
        m = jnp.max(cm, axis=0, keepdims=True)
        ei = jnp.min(jnp.where(cm == m, e_iota, E), axis=0, keepdims=True)
        hit = e_iota == ei
        hits.append(hit)
        idxs.append(ei)
        gates.append(jnp.sum(jnp.where(hit, scores, 0.0), axis=0, keepdims=True))
        onehot = onehot + jnp.where(hit, 1.0, 0.0)
        cm = jnp.where(hit, -jnp.inf, cm)
    gate = jnp.concatenate(gates, axis=0)
    gate = gate / jnp.sum(gate, axis=0, keepdims=True) * ROUTED_SCALE

    before = jnp.where(_iota((TB, TB), 0) < _iota((TB, TB), 1), 1.0, 0.0).astype(BF16)
    cnt = carry_ref[...] + _dot(onehot, before)
    pos = jnp.concatenate([jnp.sum(jnp.where(h, cnt, 0.0), axis=0, keepdims=True) for h in hits], axis=0)
    carry_ref[...] = carry_ref[...] + jnp.sum(onehot, axis=1, keepdims=True)

    idx_ref[...] = jnp.concatenate(idxs, axis=0)
    gate_ref[...] = gate
    pos_ref[...] = pos.astype(I32)
    cnt_ref[...] = jnp.broadcast_to(carry_ref[...], cnt_ref.shape)


def _router(x1, rw_t, rb):
    T = x1.shape[0]
    TB = TOK_BLK
    kt = pl.BlockSpec((TOP_K, TB), lambda i: (0, i))
    return pl.pallas_call(
        _router_kernel,
        grid=(T // TB,),
        in_specs=[pl.BlockSpec((TB, D_MODEL), lambda i: (i, 0)),
                  pl.BlockSpec((N_EXPERTS, D_MODEL), lambda i: (0, 0)),
                  pl.BlockSpec((N_EXPERTS, 1), lambda i: (0, 0))],
        out_specs=[kt, kt, kt, pl.BlockSpec((N_EXPERTS, 128), lambda i: (0, 0))],
        out_shape=[jax.ShapeDtypeStruct((TOP_K, T), I32), jax.ShapeDtypeStruct((TOP_K, T), F32),
                   jax.ShapeDtypeStruct((TOP_K, T), I32), jax.ShapeDtypeStruct((N_EXPERTS, 128), F32)],
        scratch_shapes=[pltpu.VMEM((N_EXPERTS, 1), F32)],
        compiler_params=_cparams(("arbitrary",)),
        name="router",
    )(x1, rw_t, rb)


def _slots_kernel(ps_ref, idx_ref, pos_ref, dest_ref):
    idx = idx_ref[...]

    def body(e, acc):
        return acc + jnp.where(idx == e, ps_ref[e], 0)

    dest_ref[...] = lax.fori_loop(0, N_EXPERTS, body, pos_ref[...], unroll=8)


def _slots(pad_start, idx, pos):
    T = idx.shape[1]
    tb = min(T, 2048)
    kt = pl.BlockSpec((TOP_K, tb), lambda i, ps: (0, i))
    return pl.pallas_call(
        _slots_kernel,
        grid_spec=pltpu.PrefetchScalarGridSpec(
            num_scalar_prefetch=1, grid=(T // tb,), in_specs=[kt, kt], out_specs=kt),
        out_shape=jax.ShapeDtypeStruct((TOP_K, T), I32),
        compiler_params=_cparams(("arbitrary",)),
        name="slots",
    )(pad_start, idx, pos)


ROW_T = D_MODEL // 2 // 128
U32 = jnp.uint32


def _pack_rows(x):
    bits = lax.bitcast_convert_type(x.astype(BF16).astype(F32), U32)
    half = D_MODEL // 2
    return (bits[:, :half] >> 16) | (bits[:, half:] & U32(0xFFFF0000))


def _unpack_rows(w):
    lo = lax.bitcast_convert_type(w << 16, F32)
    hi = lax.bitcast_convert_type(w & U32(0xFFFF0000), F32)
    return jnp.concatenate([lo, hi], axis=1)


def _to_tiles(dst_ref, base, w):
    n = w.shape[0]
    for s in range(ROW_T):
        dst_ref[pl.ds(base + s, n, stride=ROW_T), :] = w[:, 128 * s:128 * (s + 1)]


def _from_tiles(src_ref, base, n):
    return jnp.concatenate([src_ref[pl.ds(base + s, n, stride=ROW_T), :] for s in range(ROW_T)], axis=1)


def _tailzero_kernel(tb_ref, xs_ref):
    del tb_ref
    xs_ref[...] = jnp.zeros_like(xs_ref)


def _tailzero(tail_blk, nblk):
    rows = MOE_BLK * ROW_T
    return pl.pallas_call(
        _tailzero_kernel,
        grid_spec=pltpu.PrefetchScalarGridSpec(
            num_scalar_prefetch=1, grid=(N_EXPERTS,), in_specs=[],
            out_specs=pl.BlockSpec((rows, 128), lambda e, tb: (tb[e], 0))),
        out_shape=jax.ShapeDtypeStruct((nblk * rows, 128), U32),
        compiler_params=_cparams(("arbitrary",)),
        name="tailzero",
    )(tail_blk)


def _dispatch_kernel(dest_ref, x_ref, xs_in, xs_out, xt, sem):
    del xs_in
    TB = x_ref.shape[0]
    _to_tiles(xt, 0, _pack_rows(x_ref[...]))

    def issue(t, c):
        src = xt.at[pl.ds(pl.multiple_of(t * ROW_T, ROW_T), ROW_T), :]
        for k in range(TOP_K):
            d = dest_ref[t * TOP_K + k]
            pltpu.make_async_copy(src, xs_out.at[pl.ds(pl.multiple_of(d * ROW_T, ROW_T), ROW_T), :], sem).start()
        return c

    lax.fori_loop(0, TB, issue, 0, unroll=2)

    for k in range(TOP_K):
        pltpu.make_async_copy(xt, xs_out.at[pl.ds(0, TB * ROW_T), :], sem).wait()


def _dispatch(dest_flat, x1, xs0):
    T = x1.shape[0]
    TB = TOK_BLK
    return pl.pallas_call(
        _dispatch_kernel,
        grid=(T // TB,),
        in_specs=[pl.BlockSpec((TB * TOP_K,), lambda i: (i,), memory_space=pltpu.SMEM),
                  pl.BlockSpec((TB, D_MODEL), lambda i: (i, 0)),
                  pl.BlockSpec(memory_space=pl.ANY)],
        out_specs=pl.BlockSpec(memory_space=pl.ANY),
        out_shape=jax.ShapeDtypeStruct(xs0.shape, xs0.dtype),
        scratch_shapes=[pltpu.VMEM((TB * ROW_T, 128), U32), pltpu.SemaphoreType.DMA],
        input_output_aliases={2: 0},
        compiler_params=_cparams(("arbitrary",)),
        name="dispatch",
    )(dest_flat, x1, xs0)


def _experts_kernel(be_ref, nu_ref, xs_ref, wg_ref, wu_ref, wd_ref, ys_ref, wgu_s, wd_s):
    i = pl.program_id(0)

    @pl.when(i < nu_ref[0])
    def _():
        prev = be_ref[jnp.maximum(i - 1, 0)]

        @pl.when((i == 0) | (be_ref[i] != prev))
        def _():
            wgu_s[:, 0:D_EXPERT] = wg_ref[0, 0].astype(BF16)
            wgu_s[:, D_EXPERT:] = wu_ref[0, 0].astype(BF16)
            wd_s[...] = wd_ref[0, 0].astype(BF16)

        x = _unpack_rows(_from_tiles(xs_ref, 0, MOE_BLK)).astype(BF16)
        h = jnp.dot(x, wgu_s[...], preferred_element_type=F32)
        a = _silu(h[:, 0:D_EXPERT]) * h[:, D_EXPERT:]
        _to_tiles(ys_ref, 0, _pack_rows(jnp.dot(a.astype(BF16), wd_s[...], preferred_element_type=F32)))


def _experts(layer, blk_expert, n_used, xs, wg, wu, wd):
    rows = MOE_BLK * ROW_T
    nblk = xs.shape[0] // rows
    row = lambda i, be, nu: (jnp.minimum(i, nu[0] - 1), 0)
    wsel = lambda i, be, nu: (layer, be[i], 0, 0)
    return pl.pallas_call(
        _experts_kernel,
        grid_spec=pltpu.PrefetchScalarGridSpec(
            num_scalar_prefetch=2,
            grid=(nblk,),
            in_specs=[pl.BlockSpec((rows, 128), row),
                      pl.BlockSpec((1, 1, D_MODEL, D_EXPERT), wsel),
                      pl.BlockSpec((1, 1, D_MODEL, D_EXPERT), wsel),
                      pl.BlockSpec((1, 1, D_EXPERT, D_MODEL), wsel)],
            out_specs=pl.BlockSpec((rows, 128), row),
            scratch_shapes=[pltpu.VMEM((D_MODEL, 2 * D_EXPERT), BF16),
                            pltpu.VMEM((D_EXPERT, D_MODEL), BF16)]),
        out_shape=jax.ShapeDtypeStruct(xs.shape, U32),
        compiler_params=_cparams(("arbitrary",)),
        name="experts",
    )(blk_expert, n_used, xs, wg, wu, wd)


def _combine_kernel(dest_ref, gate_ref, x_ref, ys_ref, sgu_ref, sd_ref, g_ref, b_ref, o_ref, buf, sem):
    TB = x_ref.shape[0]

    def issue(t, c):
        for k in range(TOP_K):
            d = dest_ref[t * TOP_K + k]
            pltpu.make_async_copy(ys_ref.at[pl.ds(pl.multiple_of(d * ROW_T, ROW_T), ROW_T), :],
                                  buf.at[pl.ds(pl.multiple_of((k * TB + t) * ROW_T, ROW_T), ROW_T), :], sem).start()
        return c

    lax.fori_loop(0, TB, issue, 0, unroll=2)

    x = x_ref[...]
    h = jnp.dot(x.astype(BF16), sgu_ref[...], preferred_element_type=F32)
    a = _silu(h[:, 0:D_EXPERT]) * h[:, D_EXPERT:]
    y = ALPHA * x + jnp.dot(a.astype(BF16), sd_ref[...], preferred_element_type=F32)

    pltpu.make_async_copy(ys_ref.at[pl.ds(0, TOP_K * TB * ROW_T), :], buf, sem).wait()

    gate = gate_ref[...]
    for k in range(TOP_K):
        y = y + gate[:, k:k + 1] * _unpack_rows(_from_tiles(buf, k * TB * ROW_T, TB))
    o_ref[...] = _layernorm(y, g_ref[...], b_ref[...])


def _combine(dest_flat, gate_t, x1, ys, sh_gu, sh_d, g, b):
    T = x1.shape[0]
    TB = CMB_BLK
    vec = pl.BlockSpec((1, D_MODEL), lambda i: (0, 0))
    return pl.pallas_call(
        _combine_kernel,
        grid=(T // TB,),
        in_specs=[pl.BlockSpec((TB * TOP_K,), lambda i: (i,), memory_space=pltpu.SMEM),
                  pl.BlockSpec((TB, TOP_K), lambda i: (i, 0)),
                  pl.BlockSpec((TB, D_MODEL), lambda i: (i, 0)),
                  pl.BlockSpec(memory_space=pl.ANY),
                  pl.BlockSpec((D_MODEL, 2 * D_EXPERT), lambda i: (0, 0)),
                  pl.BlockSpec((D_EXPERT, D_MODEL), lambda i: (0, 0)), vec, vec],
        out_specs=pl.BlockSpec((TB, D_MODEL), lambda i: (i, 0)),
        out_shape=jax.ShapeDtypeStruct((T, D_MODEL), F32),
        scratch_shapes=[pltpu.VMEM((TOP_K * TB * ROW_T, 128), U32), pltpu.SemaphoreType.DMA],
        compiler_params=_cparams(("arbitrary",)),
        name="combine_ln",
    )(dest_flat, gate_t, x1, ys, sh_gu, sh_d, g, b)


def _token_mixer(x2, B, S, w_p, gla_p, sgu_p, ssd_p, cos_t, sin_t, tb, w_out, ln_g, ln_b):
    s_gla, s_sgu, s_ssd, s_dil = _inproj(x2, w_p)
    ya = _gla(s_gla.reshape(B, S, W_GLA), *gla_p)
    yb = _sgu(s_sgu.reshape(B, S, W_SGU), *sgu_p)
    yc = _ssd(s_ssd.reshape(B, S, W_SSD), *ssd_p)
    yd = _dil(s_dil.reshape(B, S, W_DIL), cos_t, sin_t, tb)
    T = B * S
    flat = lambda y: y.reshape(T, GROUP_W)
    return _outproj(flat(ya), flat(yb), flat(yc), flat(yd), x2, w_out, ln_g, ln_b)


def _moe(layer, x1, rw_t, rb, wg, wu, wd, sh_gu, sh_d, ln_g, ln_b):
    T = x1.shape[0]
    idx, gate, pos, cnt = _router(x1, rw_t, rb)
    counts = cnt[:, 0].astype(I32)
    padded = (counts + MOE_BLK - 1) // MOE_BLK * MOE_BLK
    pad_end = jnp.cumsum(padded)
    pad_start = pad_end - padded
    dest = _slots(pad_start.astype(I32), idx, pos)
    dest_flat = dest.T.reshape(T * TOP_K)
    nblk = T * TOP_K // MOE_BLK + N_EXPERTS
    n_used = (pad_end[-1] // MOE_BLK).astype(I32)
    blk = jnp.minimum(jnp.arange(nblk, dtype=I32), n_used - 1) * MOE_BLK
    blk_expert = jnp.sum((pad_end[None, :] <= blk[:, None]).astype(I32), axis=1)
    blk_expert = jnp.minimum(blk_expert, N_EXPERTS - 1)
    tail_blk = jnp.maximum(pad_end // MOE_BLK - 1, 0).astype(I32)
    xs = _dispatch(dest_flat, x1, _tailzero(tail_blk, nblk))
    ys = _experts(layer, blk_expert, n_used.reshape(1), xs, wg, wu, wd)
    return _combine(dest_flat, gate.T, x1, ys, sh_gu, sh_d, ln_g, ln_b)


def _prep_w_in(w_in):
    o = np.cumsum((0, 128, 128, 256, 256, 16, 256, 256, 256, 512, 4, 256, 256, 256))
    c = lambda i: w_in[..., o[i]:o[i + 1]]
    xbc = c(8)
    x_cols = xbc[..., 0:256]
    per_head = lambda m: jnp.concatenate(
        [m[..., SSM_N * (h // (SSM_H // SSM_G)):SSM_N * (h // (SSM_H // SSM_G) + 1)] for h in range(SSM_H)], axis=-1)
    b_cols = per_head(xbc[..., 256:384])
    c_cols = per_head(xbc[..., 384:512])
    dt_cols = jnp.repeat(c(9), SSM_P, axis=-1)
    pad = jnp.zeros(w_in.shape[:-1] + (W_GLA - o[5],), w_in.dtype)
    cols = [c(0), c(1), c(2), c(3), c(4), pad, c(5), c(6), c(7), x_cols, b_cols, c_cols, dt_cols,
            c(10), c(11), c(12)]
    return jnp.concatenate(cols, axis=-1).astype(BF16)


def _per_head(v):
    rep = SSM_H // SSM_G
    xp = v[..., 0:256]
    pick = lambda m: jnp.concatenate([m[..., SSM_N * (h // rep):SSM_N * (h // rep + 1)] for h in range(SSM_H)], axis=-1)
    return jnp.concatenate([xp, pick(v[..., 256:384]), pick(v[..., 384:512])], axis=-1)


def kernel(x, positions, w_in, gla_w_gate, gla_b_gate, gla_norm_w, sgu_ln_g, sgu_ln_b, sgu_w, sgu_b, ssm_conv_w, ssm_conv_b, ssm_dt_bias, ssm_a_log, ssm_d, ssm_norm_w, w_out, ln1_g, ln1_b, router_w, router_bias, exp_w_gate, exp_w_up, exp_w_down, sh_w_gate, sh_w_up, sh_w_down, ln2_g, ln2_b):
    B, S, _ = x.shape
    L = w_in.shape[0]
    T = B * S

    inv_freq = ROPE_THETA ** (-jnp.arange(0, ROT_DIM, 2, dtype=F32) / ROT_DIM)
    ang = positions.astype(F32)[..., None] * inv_freq
    cos, sin = jnp.cos(ang), jnp.sin(ang)
    ones = jnp.ones((B, S, DIL_DH - ROT_DIM), F32)
    cos_t = jnp.tile(jnp.concatenate([cos, cos, ones], axis=-1), (1, 1, 2))
    sin_t = jnp.tile(jnp.concatenate([-sin, sin, 0.0 * ones], axis=-1), (1, 1, 2))
    tb = jnp.asarray(_branch_log_multiplicity(S))

    w_p = _prep_w_in(w_in)
    wg_p = jnp.pad(gla_w_gate, ((0, 0), (0, 128 - GLA_RANK), (0, 0)))
    row = lambda v: v.reshape(L, 1, -1)
    gla_nw = jnp.tile(gla_norm_w, (1, GLA_H))
    sgu_wc = jnp.transpose(sgu_w, (0, 2, 1, 3)).reshape(L, SGU_CHUNK, SGU_G * SGU_CHUNK)
    sgu_bt = jnp.repeat(jnp.transpose(sgu_b, (0, 2, 1)), GROUP_W // SGU_G, axis=-1)
    exp64 = lambda v: jnp.repeat(v, SSM_P, axis=-1)
    w_out_b = w_out.astype(BF16)
    rw_t = jnp.transpose(router_w, (0, 2, 1))
    sh_gu = jnp.concatenate([sh_w_gate, sh_w_up], axis=-1).astype(BF16)
    sh_d = sh_w_down.astype(BF16)

    x2 = x.reshape(T, D_MODEL)
    for l in range(L):
        gla_p = (wg_p[l], row(gla_b_gate)[l], row(gla_nw)[l])
        sgu_p = (sgu_wc[l], sgu_bt[l], row(sgu_ln_g)[l], row(sgu_ln_b)[l])
        ssd_p = (_per_head(ssm_conv_w[l]), row(_per_head(ssm_conv_b))[l], row(exp64(ssm_dt_bias))[l],
                 row(exp64(ssm_a_log))[l], row(exp64(ssm_d))[l], row(ssm_norm_w)[l])
        x1 = _token_mixer(x2, B, S, w_p[l], gla_p, sgu_p, ssd_p, cos_t, sin_t, tb,
                          w_out_b[l], row(ln1_g)[l], row(ln1_b)[l])
        x2 = _moe(l, x1, rw_t[l], router_bias[l].reshape(N_EXPERTS, 1), exp_w_gate, exp_w_up, exp_w_down,
                  sh_gu[l], sh_d[l], row(ln2_g)[l], row(ln2_b)[l])
    return x2.reshape(B, S, D_MODEL)
```

```python
import functools
import math

import numpy as np
import jax
import jax.numpy as jnp
from jax import lax
from jax.experimental import pallas as pl
from jax.experimental.pallas import tpu as pltpu

F32 = jnp.float32
BF16 = jnp.bfloat16
I32 = jnp.int32

D_MODEL = 1024
N_LAYERS = 4
GROUP_W = 256

GLA_H, GLA_DK, GLA_DV, GLA_RANK, GLA_TAU = 4, 32, 64, 16, 16.0
SGU_G, SGU_CHUNK = 4, 128
SSM_H, SSM_P, SSM_G, SSM_N, SSM_CONV, SSM_CHUNK = 4, 64, 2, 64, 4, 128
DIL_H, DIL_DH, ROT_DIM, ROPE_THETA = 4, 64, 16, 500000.0
DIL_BRANCHES = ((128, 1), (512, 4), (2048, 16))

N_EXPERTS, TOP_K, N_EXPERT_GROUPS, TOPK_GROUPS, D_EXPERT = 128, 8, 8, 4, 256
ROUTED_SCALE = 1.0

ALPHA = (2 * N_LAYERS) ** 0.25
LN_EPS = 1e-5
RMS_EPS = 1e-6

W_GLA = 896
W_SGU = 512
W_SSD = 1280
W_DIL = 768
W_PROJ = W_GLA + W_SGU + W_SSD + W_DIL

BLK = 128
QBLK = 256
MOE_BLK = 256
TOK_BLK = 256
CMB_BLK = 128
NEG = -1e30

VMEM_LIMIT = 56 * 1024 * 1024


def _cparams(sem):
    return pltpu.CompilerParams(dimension_semantics=sem, vmem_limit_bytes=VMEM_LIMIT)


def _dot(a, b):
    return jnp.dot(a.astype(BF16), b.astype(BF16), preferred_element_type=F32)


def _dot_nt(a, b):
    return lax.dot_general(a.astype(BF16), b.astype(BF16), (((1,), (1,)), ((), ())),
                           preferred_element_type=F32)


def _dot_tn(a, b):
    return lax.dot_general(a.astype(BF16), b.astype(BF16), (((0,), (0,)), ((), ())),
                           preferred_element_type=F32)


def _split(a, n):
    out = []
    r = a
    for _ in range(n):
        p = r.astype(BF16)
        out.append(p)
        r = r - p.astype(F32)
    return out


def _dot_hi(a, b):
    a1, a2 = _split(a, 2)
    b1, b2 = _split(b, 2)
    return _dot(a1, b1) + (_dot(a1, b2) + _dot(a2, b1))


def _dot_nt_hi(a, b):
    a1, a2 = _split(a, 2)
    b1, b2 = _split(b, 2)
    return _dot_nt(a1, b1) + (_dot_nt(a1, b2) + _dot_nt(a2, b1))


def _dot_lhs01(m, b):
    b1, b2, b3 = _split(b, 3)
    return _dot(m, b1) + (_dot(m, b2) + _dot(m, b3))


def _dot_rhs01(a, m):
    a1, a2, a3 = _split(a, 3)
    return _dot(a1, m) + (_dot(a2, m) + _dot(a3, m))


def _iota(shape, dim):
    return lax.broadcasted_iota(I32, shape, dim)


def _sigmoid(x):
    return 1.0 / (1.0 + jnp.exp(-x))


def _silu(x):
    return x * _sigmoid(x)


def _softplus(x):
    return jnp.maximum(x, 0.0) + jnp.log1p(jnp.exp(-jnp.abs(x)))


def _gelu(x):
    return 0.5 * x * (1.0 + lax.erf(x * (2.0 ** -0.5)))


def _inproj_kernel(x_ref, w_ref, o_gla, o_sgu, o_ssd, o_dil):
    xb = x_ref[...].astype(BF16)
    off = 0
    for o in (o_gla, o_sgu, o_ssd, o_dil):
        n = o.shape[1]
        o[...] = jnp.dot(xb, w_ref[:, off:off + n], preferred_element_type=F32)
        off += n


def _inproj(x2, w_p):
    T = x2.shape[0]
    tm = 512
    widths = (W_GLA, W_SGU, W_SSD, W_DIL)
    return pl.pallas_call(
        _inproj_kernel,
        grid=(T // tm,),
        in_specs=[pl.BlockSpec((tm, D_MODEL), lambda i: (i, 0)),
                  pl.BlockSpec((D_MODEL, W_PROJ), lambda i: (0, 0))],
        out_specs=[pl.BlockSpec((tm, w), lambda i: (i, 0)) for w in widths],
        out_shape=[jax.ShapeDtypeStruct((T, w), F32) for w in widths],
        compiler_params=_cparams(("arbitrary",)),
        name="inproj",
    )(x2, w_p)


GLA_NB = 2
GLA_SUB = 16


def _gla_kernel(slab_ref, wg_ref, bg_ref, nw_ref, out_ref, st_ref, a_ref):
    S = slab_ref.shape[1]
    st_ref[...] = jnp.zeros_like(st_ref)
    row = _iota((BLK, BLK), 0)
    col = _iota((BLK, BLK), 1)
    tril = col <= row
    tril_bf = jnp.where(tril, 1.0, 0.0).astype(BF16)
    sub_row = _iota((GLA_SUB, BLK), 0)
    sub_col = _iota((GLA_SUB, BLK), 1)
    sub_head = sub_col >> 5
    v_head = _iota((BLK, GROUP_W), 1) >> 6
    st_diag = (_iota((GROUP_W, BLK), 0) >> 6) == (_iota((GROUP_W, BLK), 1) >> 5)
    seg = jnp.where((_iota((GROUP_W, GROUP_W), 0) >> 6) == (_iota((GROUP_W, GROUP_W), 1) >> 6),
                    1.0, 0.0).astype(BF16)
    wg = wg_ref[...]
    bg = bg_ref[...]
    nw = nw_ref[...]

    def block(b, r0):
        blk = slab_ref[b, pl.ds(r0, BLK), :]
        q = blk[:, 0:128] * (GLA_DK ** -0.5)
        k = blk[:, 128:256]
        v = blk[:, 256:512]
        r = blk[:, 512:768]
        lr = blk[:, 768:896]
        z = _dot_hi(lr, wg) + bg
        gk = (jnp.minimum(z, 0.0) - jnp.log1p(jnp.exp(-jnp.abs(z)))) * (1.0 / GLA_TAU)
        g = _dot_lhs01(tril_bf, gk)
        g_last = g[BLK - 1:BLK, :]
        q_in = q * jnp.exp(g)
        k_in = k * jnp.exp(g_last - g)
        vb = v.astype(BF16)
        o = _dot_nt(q_in, st_ref[b])
        for c in range(BLK // GLA_SUB):
            lo = c * GLA_SUB
            n_c = g[lo - 1:lo, :] if c else jnp.zeros((1, BLK), F32)
            qc = q[lo:lo + GLA_SUB, :] * jnp.exp(g[lo:lo + GLA_SUB, :] - n_c)
            kc = k * jnp.exp(jnp.where(row < lo + GLA_SUB, n_c - g, -jnp.inf))
            qs = jnp.concatenate([jnp.where(sub_head == h, qc, 0.0) for h in range(GLA_H)], axis=0)
            sc = _dot_nt(qs, kc)
            causal = sub_col <= sub_row + lo
            for h in range(GLA_H):
                a_ref[b, h, lo:lo + GLA_SUB, :] = jnp.where(causal, sc[h * GLA_SUB:(h + 1) * GLA_SUB, :], 0.0)
        for h in range(GLA_H):
            o = o + jnp.where(v_head == h, _dot(a_ref[b, h], vb), 0.0)
        st_ref[b] = st_ref[b] * jnp.exp(g_last) + jnp.where(st_diag, _dot_tn(vb, k_in), 0.0)
        ms = _dot_rhs01(o * o, seg) * (1.0 / GLA_DV)
        y = o * lax.rsqrt(ms + RMS_EPS) * nw * _silu(r)
        out_ref[b, pl.ds(r0, BLK), :] = y.astype(out_ref.dtype)

    def body(n, carry):
        r0 = pl.multiple_of(n * BLK, BLK)
        for b in range(slab_ref.shape[0]):
            block(b, r0)
        return carry

    lax.fori_loop(0, S // BLK, body, 0)


def _gla(slab, wg_p, bg, nw_t):
    B, S, _ = slab.shape
    nb = GLA_NB if B % GLA_NB == 0 else 1
    return pl.pallas_call(
        _gla_kernel,
        grid=(B // nb,),
        in_specs=[pl.BlockSpec((nb, S, W_GLA), lambda b: (b, 0, 0)),
                  pl.BlockSpec((128, 128), lambda b: (0, 0)),
                  pl.BlockSpec((1, 128), lambda b: (0, 0)),
                  pl.BlockSpec((1, GROUP_W), lambda b: (0, 0))],
        out_specs=pl.BlockSpec((nb, S, GROUP_W), lambda b: (b, 0, 0)),
        out_shape=jax.ShapeDtypeStruct((B, S, GROUP_W), BF16),
        scratch_shapes=[pltpu.VMEM((nb, GROUP_W, 128), F32), pltpu.VMEM((nb, GLA_H, BLK, BLK), F32)],
        compiler_params=_cparams(("arbitrary",)),
        name="gla",
    )(slab, wg_p, bg, nw_t)


def _sgu_kernel(slab_ref, w_ref, bias_ref, g_ref, b_ref, out_ref):
    S = slab_ref.shape[1]
    wmask = (_iota((BLK, SGU_G * BLK), 1) & (BLK - 1)) <= _iota((BLK, SGU_G * BLK), 0)
    w = jnp.where(wmask, w_ref[...], 0.0).astype(BF16)
    lane_grp = _iota((BLK, GROUP_W), 1) >> 6
    bias = bias_ref[...]
    ln_g = g_ref[...]
    ln_b = b_ref[...]

    def body(n, carry):
        r0 = pl.multiple_of(n * BLK, BLK)
        blk = slab_ref[0, pl.ds(r0, BLK), :]
        u = _gelu(blk[:, 0:256])
        v = _gelu(blk[:, 256:512])
        mu = jnp.mean(v, axis=-1, keepdims=True)
        var = jnp.mean(jnp.square(v - mu), axis=-1, keepdims=True)
        v = (v - mu) * lax.rsqrt(var + LN_EPS) * ln_g + ln_b
        vexp = jnp.concatenate([jnp.where(lane_grp == g, v, 0.0) for g in range(SGU_G)], axis=0)
        s = _dot(w, vexp) + bias
        out_ref[0, pl.ds(r0, BLK), :] = (u * s).astype(out_ref.dtype)
        return carry

    lax.fori_loop(0, S // BLK, body, 0)


def _sgu(slab, w_cat, bias_t, ln_g, ln_b):
    B, S, _ = slab.shape
    return pl.pallas_call(
        _sgu_kernel,
        grid=(B,),
        in_specs=[pl.BlockSpec((1, S, W_SGU), lambda b: (b, 0, 0)),
                  pl.BlockSpec((BLK, SGU_G * BLK), lambda b: (0, 0)),
                  pl.BlockSpec((BLK, GROUP_W), lambda b: (0, 0)),
                  pl.BlockSpec((1, GROUP_W), lambda b: (0, 0)),
                  pl.BlockSpec((1, GROUP_W), lambda b: (0, 0))],
        out_specs=pl.BlockSpec((1, S, GROUP_W), lambda b: (b, 0, 0)),
        out_shape=jax.ShapeDtypeStruct((B, S, GROUP_W), BF16),
        compiler_params=_cparams(("arbitrary",)),
        name="sgu",
    )(slab, w_cat, bias_t, ln_g, ln_b)


def _ssd_kernel(slab_ref, cw_ref, cb_ref, dtb_ref, a_ref, d_ref, nw_ref, out_ref, xpad_ref, st_ref):
    S = slab_ref.shape[1]
    CW = 3 * GROUP_W
    st_ref[...] = jnp.zeros_like(st_ref)
    xpad_ref[0:8, :] = jnp.zeros((8, CW), F32)
    xpad_ref[8:S + 8, :] = slab_ref[0, :, 256:256 + CW]
    row = _iota((BLK, BLK), 0)
    col = _iota((BLK, BLK), 1)
    tril = col <= row
    tril_bf = jnp.where(tril, 1.0, 0.0).astype(BF16)
    lane_head = _iota((BLK, GROUP_W), 1) >> 6
    st_diag = (_iota((GROUP_W, GROUP_W), 0) >> 6) == (_iota((GROUP_W, GROUP_W), 1) >> 6)
    cw = cw_ref[...]
    cb = cb_ref[...]
    dtb = dtb_ref[...]
    a_neg = -jnp.exp(a_ref[...])
    d_skip = d_ref[...]
    nw = nw_ref[...]

    def body(n, carry):
        r0 = pl.multiple_of(n * BLK, BLK)
        xw = xpad_ref[pl.ds(r0, BLK + 8), :]
        conv = cb + cw[0:1, :] * xw[5:5 + BLK, :]
        for w in range(1, SSM_CONV):
            conv = conv + cw[w:w + 1, :] * xw[5 + w:5 + w + BLK, :]
        xbc = _silu(conv)
        xs = xbc[:, 0:256]
        b_e = xbc[:, 256:512]
        c_e = xbc[:, 512:768]
        z = slab_ref[0, pl.ds(r0, BLK), 0:256]
        dt = _softplus(slab_ref[0, pl.ds(r0, BLK), 1024:1280] + dtb)
        acs = _dot_lhs01(tril_bf, dt * a_neg)
        acs_last = acs[BLK - 1:BLK, :]
        x_dt = (xs * dt).astype(BF16)
        acs_t = [jnp.transpose(acs[:, 0:128]), jnp.transpose(acs[:, 128:256])]
        y = _dot(c_e * jnp.exp(acs), st_ref[...])
        b_bf = b_e.astype(BF16)
        for h in range(SSM_H):
            a_col = jnp.broadcast_to(acs[:, 64 * h:64 * h + 1], (BLK, BLK))
            a_row = jnp.broadcast_to(acs_t[h // 2][64 * (h % 2):64 * (h % 2) + 1, :], (BLK, BLK))
            lm = jnp.exp(jnp.where(tril, a_col - a_row, -jnp.inf))
            ch = jnp.where(lane_head == h, c_e, 0.0)
            m = _dot_nt(ch, b_bf) * lm
            y = y + jnp.where(lane_head == h, _dot(m, x_dt), 0.0)
        new = _dot_tn(b_e * jnp.exp(acs_last - acs), x_dt)
        st_ref[...] = st_ref[...] * jnp.exp(acs_last) + jnp.where(st_diag, new, 0.0)
        y = (y + d_skip * xs) * _silu(z)
        parts = []
        for g in range(SSM_G):
            yg = y[:, 128 * g:128 * (g + 1)]
            ms = jnp.mean(yg * yg, axis=-1, keepdims=True)
            parts.append(yg * lax.rsqrt(ms + RMS_EPS))
        out = jnp.concatenate(parts, axis=1) * nw
        out_ref[0, pl.ds(r0, BLK), :] = out.astype(out_ref.dtype)
        return carry

    lax.fori_loop(0, S // BLK, body, 0)


def _ssd(slab, cw_e, cb_e, dtb_e, a_e, d_e, nw):
    B, S, _ = slab.shape
    CW = 3 * GROUP_W
    vec = lambda w: pl.BlockSpec((1, w), lambda b: (0, 0))
    return pl.pallas_call(
        _ssd_kernel,
        grid=(B,),
        in_specs=[pl.BlockSpec((1, S, W_SSD), lambda b: (b, 0, 0)),
                  pl.BlockSpec((SSM_CONV, CW), lambda b: (0, 0)),
                  vec(CW), vec(GROUP_W), vec(GROUP_W), vec(GROUP_W), vec(GROUP_W)],
        out_specs=pl.BlockSpec((1, S, GROUP_W), lambda b: (b, 0, 0)),
        out_shape=jax.ShapeDtypeStruct((B, S, GROUP_W), BF16),
        scratch_shapes=[pltpu.VMEM((S + 8, CW), F32), pltpu.VMEM((GROUP_W, GROUP_W), F32)],
        compiler_params=_cparams(("arbitrary",)),
        name="ssd",
    )(slab, cw_e, cb_e, dtb_e, a_e, d_e, nw)


def _branch_log_multiplicity(S):
    nq = S // QBLK
    r = np.arange(QBLK)[:, None]
    u = np.arange(S)[None, :]
    d = (nq - 1) * QBLK + r - u
    c = np.zeros(d.shape, np.float64)
    for window, dil in DIL_BRANCHES:
        c += (d >= 0) & (d % dil == 0) & (d <= window)
    with np.errstate(divide="ignore"):
        return np.where(c > 0, np.log(np.maximum(c, 1.0)), NEG).astype(np.float32)


def _dil_kernel(slab_ref, cos_ref, sin_ref, tb_ref, out_ref, q_ref, k_ref, v_ref):
    S = slab_ref.shape[1]
    nq = S // QBLK
    lane_d = _iota((1, 128), 1) & (DIL_DH - 1)

    def rope(t):
        fwd = pltpu.roll(t, 128 - ROT_DIM // 2, 1)
        bwd = pltpu.roll(t, ROT_DIM // 2, 1)
        return t * cos_ref[0] + jnp.where(lane_d < ROT_DIM // 2, fwd, bwd) * sin_ref[0]

    for c in range(0, GROUP_W, 128):
        q_ref[:, c:c + 128] = (rope(slab_ref[0, :, c:c + 128]) * (DIL_DH ** -0.5)).astype(BF16)
        k_ref[:, c:c + 128] = rope(slab_ref[0, :, 256 + c:384 + c]).astype(BF16)
    v_ref[...] = slab_ref[0, :, 512:768].astype(BF16)
    lane_head = _iota((QBLK, GROUP_W), 1) >> 6
    for i in range(nq):
        nk = (i + 1) * QBLK
        qi = q_ref[i * QBLK:(i + 1) * QBLK, :]
        acc = jnp.zeros((QBLK, GROUP_W), F32)
        for h in range(DIL_H):
            qh = jnp.where(lane_head == h, qi, jnp.zeros_like(qi))
            s = _dot_nt(qh, k_ref[0:nk, :]) + tb_ref[:, (nq - 1 - i) * QBLK:]
            m = jnp.max(s, axis=-1, keepdims=True)
            p = jnp.exp(s - m)
            den = jnp.sum(p, axis=-1, keepdims=True)
            o = _dot(p, v_ref[0:nk, :])
            acc = acc + jnp.where(lane_head == h, o / den, 0.0)
        out_ref[0, i * QBLK:(i + 1) * QBLK, :] = acc.astype(out_ref.dtype)


def _dil(slab, cos_t, sin_t, tb):
    B, S, _ = slab.shape
    return pl.pallas_call(
        _dil_kernel,
        grid=(B,),
        in_specs=[pl.BlockSpec((1, S, W_DIL), lambda b: (b, 0, 0)),
                  pl.BlockSpec((1, S, 128), lambda b: (b, 0, 0)),
                  pl.BlockSpec((1, S, 128), lambda b: (b, 0, 0)),
                  pl.BlockSpec((QBLK, S), lambda b: (0, 0), pipeline_mode=pl.Buffered(1))],
        out_specs=pl.BlockSpec((1, S, GROUP_W), lambda b: (b, 0, 0)),
        out_shape=jax.ShapeDtypeStruct((B, S, GROUP_W), BF16),
        scratch_shapes=[pltpu.VMEM((S, GROUP_W), BF16)] * 3,
        compiler_params=_cparams(("arbitrary",)),
        name="dil",
    )(slab, cos_t, sin_t, tb)


def _layernorm(x, g, b):
    mu = jnp.mean(x, axis=-1, keepdims=True)
    var = jnp.mean(jnp.square(x - mu), axis=-1, keepdims=True)
    return (x - mu) * lax.rsqrt(var + LN_EPS) * g + b


def _outproj_kernel(ya, yb, yc, yd, x_ref, w_ref, g_ref, b_ref, o_ref):
    y = jnp.concatenate([ya[...], yb[...], yc[...], yd[...]], axis=1)
    y = jnp.dot(y, w_ref[...], preferred_element_type=F32)
    o_ref[...] = _layernorm(ALPHA * x_ref[...] + y, g_ref[...], b_ref[...])


def _outproj(ya, yb, yc, yd, x2, w_out, g, b):
    T = x2.shape[0]
    tm = 512
    ysp = pl.BlockSpec((tm, GROUP_W), lambda i: (i, 0))
    vec = pl.BlockSpec((1, D_MODEL), lambda i: (0, 0))
    return pl.pallas_call(
        _outproj_kernel,
        grid=(T // tm,),
        in_specs=[ysp, ysp, ysp, ysp,
                  pl.BlockSpec((tm, D_MODEL), lambda i: (i, 0)),
                  pl.BlockSpec((D_MODEL, D_MODEL), lambda i: (0, 0)), vec, vec],
        out_specs=pl.BlockSpec((tm, D_MODEL), lambda i: (i, 0)),
        out_shape=jax.ShapeDtypeStruct((T, D_MODEL), F32),
        compiler_params=_cparams(("arbitrary",)),
        name="outproj_ln",
    )(ya, yb, yc, yd, x2, w_out, g, b)


def _router_kernel(x_ref, rw_ref, rb_ref, idx_ref, gate_ref, pos_ref, cnt_ref, carry_ref):
    TB = x_ref.shape[0]
    E = N_EXPERTS
    per = E // N_EXPERT_GROUPS

    @pl.when(pl.program_id(0) == 0)
    def _():
        carry_ref[...] = jnp.zeros_like(carry_ref)

    logits = _dot_nt_hi(rw_ref[...], x_ref[...])
    scores = _sigmoid(logits)
    choice = scores + rb_ref[...]
    e_iota = _iota((E, TB), 0)

    l_iota = _iota((per, TB), 0)
    grp = []
    for g in range(N_EXPERT_GROUPS):
        cg = choice[per * g:per * (g + 1), :]
        m1 = jnp.max(cg, axis=0, keepdims=True)
        i1 = jnp.min(jnp.where(cg == m1, l_iota, per), axis=0, keepdims=True)
        m2 = jnp.max(jnp.where(l_iota == i1, -jnp.inf, cg), axis=0, keepdims=True)
        grp.append(m1 + m2)
    gs = jnp.concatenate(grp, axis=0)
    g_iota = _iota((N_EXPERT_GROUPS, TB), 0)
    keep = jnp.zeros((N_EXPERT_GROUPS, TB), jnp.bool_)
    for _ in range(TOPK_GROUPS):
        m = jnp.max(gs, axis=0, keepdims=True)
        gi = jnp.min(jnp.where(gs == m, g_iota, N_EXPERT_GROUPS), axis=0, keepdims=True)
        hit = g_iota == gi
        keep = keep | hit
        gs = jnp.where(hit, -jnp.inf, gs)
    keep_f = jnp.where(keep, 1.0, 0.0)
    keep_e = jnp.concatenate([jnp.broadcast_to(keep_f[g:g + 1, :], (per, TB))
                              for g in range(N_EXPERT_GROUPS)], axis=0)
    cm = jnp.where(keep_e > 0.5, choice, -jnp.inf)

    hits, idxs, gates = [], [], []
    onehot = jnp.zeros((E, TB), F32)
    for _ in range(TOP_K):
        m = jnp.max(cm, axis=0, keepdims=True)
        ei = jnp.min(jnp.where(cm == m, e_iota, E), axis=0, keepdims=True)
        hit = e_iota == ei
        hits.append(hit)
        idxs.append(ei)
        gates.append(jnp.sum(jnp.where(hit, scores, 0.0), axis=0, keepdims=True))
        onehot = onehot + jnp.where(hit, 1.0, 0.0)
        cm = jnp.where(hit, -jnp.inf, cm)
    gate = jnp.concatenate(gates, axis=0)
    gate = gate / jnp.sum(gate, axis=0, keepdims=True) * ROUTED_SCALE

    before = jnp.where(_iota((TB, TB), 0) < _iota((TB, TB), 1), 1.0, 0.0).astype(BF16)
    cnt = carry_ref[...] + _dot(onehot, before)
    pos = jnp.concatenate([jnp.sum(jnp.where(h, cnt, 0.0), axis=0, keepdims=True) for h in hits], axis=0)
    carry_ref[...] = carry_ref[...] + jnp.sum(onehot, axis=1, keepdims=True)

    idx_ref[...] = jnp.concatenate(idxs, axis=0)
    gate_ref[...] = gate
    pos_ref[...] = pos.astype(I32)
    cnt_ref[...] = jnp.broadcast_to(carry_ref[...], cnt_ref.shape)


def _router(x1, rw_t, rb):
    T = x1.shape[0]
    TB = TOK_BLK
    kt = pl.BlockSpec((TOP_K, TB), lambda i: (0, i))
    return pl.pallas_call(
        _router_kernel,
        grid=(T // TB,),
        in_specs=[pl.BlockSpec((TB, D_MODEL), lambda i: (i, 0)),
                  pl.BlockSpec((N_EXPERTS, D_MODEL), lambda i: (0, 0)),
                  pl.BlockSpec((N_EXPERTS, 1), lambda i: (0, 0))],
        out_specs=[kt, kt, kt, pl.BlockSpec((N_EXPERTS, 128), lambda i: (0, 0))],
        out_shape=[jax.ShapeDtypeStruct((TOP_K, T), I32), jax.ShapeDtypeStruct((TOP_K, T), F32),
                   jax.ShapeDtypeStruct((TOP_K, T), I32), jax.ShapeDtypeStruct((N_EXPERTS, 128), F32)],
        scratch_shapes=[pltpu.VMEM((N_EXPERTS, 1), F32)],
        compiler_params=_cparams(("arbitrary",)),
        name="router",
    )(x1, rw_t, rb)


def _slots_kernel(ps_ref, idx_ref, pos_ref, dest_ref):
    idx = idx_ref[...]

    def body(e, acc):
        return acc + jnp.where(idx == e, ps_ref[e], 0)

    dest_ref[...] = lax.fori_loop(0, N_EXPERTS, body, pos_ref[...], unroll=8)


def _slots(pad_start, idx, pos):
    T = idx.shape[1]
    tb = min(T, 2048)
    kt = pl.BlockSpec((TOP_K, tb), lambda i, ps: (0, i))
    return pl.pallas_call(
        _slots_kernel,
        grid_spec=pltpu.PrefetchScalarGridSpec(
            num_scalar_prefetch=1, grid=(T // tb,), in_specs=[kt, kt], out_specs=kt),
        out_shape=jax.ShapeDtypeStruct((TOP_K, T), I32),
        compiler_params=_cparams(("arbitrary",)),
        name="slots",
    )(pad_start, idx, pos)


ROW_T = D_MODEL // 2 // 128
U32 = jnp.uint32


def _pack_rows(x):
    bits = lax.bitcast_convert_type(x.astype(BF16).astype(F32), U32)
    half = D_MODEL // 2
    return (bits[:, :half] >> 16) | (bits[:, half:] & U32(0xFFFF0000))


def _unpack_rows(w):
    lo = lax.bitcast_convert_type(w << 16, F32)
    hi = lax.bitcast_convert_type(w & U32(0xFFFF0000), F32)
    return jnp.concatenate([lo, hi], axis=1)


def _to_tiles(dst_ref, base, w):
    n = w.shape[0]
    for s in range(ROW_T):
        dst_ref[pl.ds(base + s, n, stride=ROW_T), :] = w[:, 128 * s:128 * (s + 1)]


def _from_tiles(src_ref, base, n):
    return jnp.concatenate([src_ref[pl.ds(base + s, n, stride=ROW_T), :] for s in range(ROW_T)], axis=1)


def _tailzero_kernel(tb_ref, xs_ref):
    del tb_ref
    xs_ref[...] = jnp.zeros_like(xs_ref)


def _tailzero(tail_blk, nblk):
    rows = MOE_BLK * ROW_T
    return pl.pallas_call(
        _tailzero_kernel,
        grid_spec=pltpu.PrefetchScalarGridSpec(
            num_scalar_prefetch=1, grid=(N_EXPERTS,), in_specs=[],
            out_specs=pl.BlockSpec((rows, 128), lambda e, tb: (tb[e], 0))),
        out_shape=jax.ShapeDtypeStruct((nblk * rows, 128), U32),
        compiler_params=_cparams(("arbitrary",)),
        name="tailzero",
    )(tail_blk)


def _dispatch_kernel(dest_ref, x_ref, xs_in, xs_out, xt, sem):
    del xs_in
    TB = x_ref.shape[0]
    _to_tiles(xt, 0, _pack_rows(x_ref[...]))

    def issue(t, c):
        src = xt.at[pl.ds(pl.multiple_of(t * ROW_T, ROW_T), ROW_T), :]
        for k in range(TOP_K):
            d = dest_ref[t * TOP_K + k]
            pltpu.make_async_copy(src, xs_out.at[pl.ds(pl.multiple_of(d * ROW_T, ROW_T), ROW_T), :], sem).start()
        return c

    lax.fori_loop(0, TB, issue, 0, unroll=2)

    for k in range(TOP_K):
        pltpu.make_async_copy(xt, xs_out.at[pl.ds(0, TB * ROW_T), :], sem).wait()


def _dispatch(dest_flat, x1, xs0):
    T = x1.shape[0]
    TB = TOK_BLK
    return pl.pallas_call(
        _dispatch_kernel,
        grid=(T // TB,),
        in_specs=[pl.BlockSpec((TB * TOP_K,), lambda i: (i,), memory_space=pltpu.SMEM),
                  pl.BlockSpec((TB, D_MODEL), lambda i: (i, 0)),
                  pl.BlockSpec(memory_space=pl.ANY)],
        out_specs=pl.BlockSpec(memory_space=pl.ANY),
        out_shape=jax.ShapeDtypeStruct(xs0.shape, xs0.dtype),
        scratch_shapes=[pltpu.VMEM((TB * ROW_T, 128), U32), pltpu.SemaphoreType.DMA],
        input_output_aliases={2: 0},
        compiler_params=_cparams(("arbitrary",)),
        name="dispatch",
    )(dest_flat, x1, xs0)


def _experts_kernel(be_ref, nu_ref, xs_ref, wg_ref, wu_ref, wd_ref, ys_ref, wgu_s, wd_s):
    i = pl.program_id(0)

    @pl.when(i < nu_ref[0])
    def _():
        prev = be_ref[jnp.maximum(i - 1, 0)]

        @pl.when((i == 0) | (be_ref[i] != prev))
        def _():
            wgu_s[:, 0:D_EXPERT] = wg_ref[0, 0].astype(BF16)
            wgu_s[:, D_EXPERT:] = wu_ref[0, 0].astype(BF16)
            wd_s[...] = wd_ref[0, 0].astype(BF16)

        x = _unpack_rows(_from_tiles(xs_ref, 0, MOE_BLK)).astype(BF16)
        h = jnp.dot(x, wgu_s[...], preferred_element_type=F32)
        a = _silu(h[:, 0:D_EXPERT]) * h[:, D_EXPERT:]
        _to_tiles(ys_ref, 0, _pack_rows(jnp.dot(a.astype(BF16), wd_s[...], preferred_element_type=F32)))


def _experts(layer, blk_expert, n_used, xs, wg, wu, wd):
    rows = MOE_BLK * ROW_T
    nblk = xs.shape[0] // rows
    row = lambda i, be, nu: (jnp.minimum(i, nu[0] - 1), 0)
    wsel = lambda i, be, nu: (layer, be[i], 0, 0)
    return pl.pallas_call(
        _experts_kernel,
        grid_spec=pltpu.PrefetchScalarGridSpec(
            num_scalar_prefetch=2,
            grid=(nblk,),
            in_specs=[pl.BlockSpec((rows, 128), row),
                      pl.BlockSpec((1, 1, D_MODEL, D_EXPERT), wsel),
                      pl.BlockSpec((1, 1, D_MODEL, D_EXPERT), wsel),
                      pl.BlockSpec((1, 1, D_EXPERT, D_MODEL), wsel)],
            out_specs=pl.BlockSpec((rows, 128), row),
            scratch_shapes=[pltpu.VMEM((D_MODEL, 2 * D_EXPERT), BF16),
                            pltpu.VMEM((D_EXPERT, D_MODEL), BF16)]),
        out_shape=jax.ShapeDtypeStruct(xs.shape, U32),
        compiler_params=_cparams(("arbitrary",)),
        name="experts",
    )(blk_expert, n_used, xs, wg, wu, wd)


def _combine_kernel(dest_ref, gate_ref, x_ref, ys_ref, sgu_ref, sd_ref, g_ref, b_ref, o_ref, buf, sem):
    TB = x_ref.shape[0]

    def issue(t, c):
        for k in range(TOP_K):
            d = dest_ref[t * TOP_K + k]
            pltpu.make_async_copy(ys_ref.at[pl.ds(pl.multiple_of(d * ROW_T, ROW_T), ROW_T), :],
                                  buf.at[pl.ds(pl.multiple_of((k * TB + t) * ROW_T, ROW_T), ROW_T), :], sem).start()
        return c

    lax.fori_loop(0, TB, issue, 0, unroll=2)

    x = x_ref[...]
    h = jnp.dot(x.astype(BF16), sgu_ref[...], preferred_element_type=F32)
    a = _silu(h[:, 0:D_EXPERT]) * h[:, D_EXPERT:]
    y = ALPHA * x + jnp.dot(a.astype(BF16), sd_ref[...], preferred_element_type=F32)

    pltpu.make_async_copy(ys_ref.at[pl.ds(0, TOP_K * TB * ROW_T), :], buf, sem).wait()

    gate = gate_ref[...]
    for k in range(TOP_K):
        y = y + gate[:, k:k + 1] * _unpack_rows(_from_tiles(buf, k * TB * ROW_T, TB))
    o_ref[...] = _layernorm(y, g_ref[...], b_ref[...])


def _combine(dest_flat, gate_t, x1, ys, sh_gu, sh_d, g, b):
    T = x1.shape[0]
    TB = CMB_BLK
    vec = pl.BlockSpec((1, D_MODEL), lambda i: (0, 0))
    return pl.pallas_call(
        _combine_kernel,
        grid=(T // TB,),
        in_specs=[pl.BlockSpec((TB * TOP_K,), lambda i: (i,), memory_space=pltpu.SMEM),
                  pl.BlockSpec((TB, TOP_K), lambda i: (i, 0)),
                  pl.BlockSpec((TB, D_MODEL), lambda i: (i, 0)),
                  pl.BlockSpec(memory_space=pl.ANY),
                  pl.BlockSpec((D_MODEL, 2 * D_EXPERT), lambda i: (0, 0)),
                  pl.BlockSpec((D_EXPERT, D_MODEL), lambda i: (0, 0)), vec, vec],
        out_specs=pl.BlockSpec((TB, D_MODEL), lambda i: (i, 0)),
        out_shape=jax.ShapeDtypeStruct((T, D_MODEL), F32),
        scratch_shapes=[pltpu.VMEM((TOP_K * TB * ROW_T, 128), U32), pltpu.SemaphoreType.DMA],
        compiler_params=_cparams(("arbitrary",)),
        name="combine_ln",
    )(dest_flat, gate_t, x1, ys, sh_gu, sh_d, g, b)


def _token_mixer(x2, B, S, w_p, gla_p, sgu_p, ssd_p, cos_t, sin_t, tb, w_out, ln_g, ln_b):
    s_gla, s_sgu, s_ssd, s_dil = _inproj(x2, w_p)
    ya = _gla(s_gla.reshape(B, S, W_GLA), *gla_p)
    yb = _sgu(s_sgu.reshape(B, S, W_SGU), *sgu_p)
    yc = _ssd(s_ssd.reshape(B, S, W_SSD), *ssd_p)
    yd = _dil(s_dil.reshape(B, S, W_DIL), cos_t, sin_t, tb)
    T = B * S
    flat = lambda y: y.reshape(T, GROUP_W)
    return _outproj(flat(ya), flat(yb), flat(yc), flat(yd), x2, w_out, ln_g, ln_b)


def _moe(layer, x1, rw_t, rb, wg, wu, wd, sh_gu, sh_d, ln_g, ln_b):
    T = x1.shape[0]
    idx, gate, pos, cnt = _router(x1, rw_t, rb)
    counts = cnt[:, 0].astype(I32)
    padded = (counts + MOE_BLK - 1) // MOE_BLK * MOE_BLK
    pad_end = jnp.cumsum(padded)
    pad_start = pad_end - padded
    dest = _slots(pad_start.astype(I32), idx, pos)
    dest_flat = dest.T.reshape(T * TOP_K)
    nblk = T * TOP_K // MOE_BLK + N_EXPERTS
    n_used = (pad_end[-1] // MOE_BLK).astype(I32)
    blk = jnp.minimum(jnp.arange(nblk, dtype=I32), n_used - 1) * MOE_BLK
    blk_expert = jnp.sum((pad_end[None, :] <= blk[:, None]).astype(I32), axis=1)
    blk_expert = jnp.minimum(blk_expert, N_EXPERTS - 1)
    tail_blk = jnp.maximum(pad_end // MOE_BLK - 1, 0).astype(I32)
    xs = _dispatch(dest_flat, x1, _tailzero(tail_blk, nblk))
    ys = _experts(layer, blk_expert, n_used.reshape(1), xs, wg, wu, wd)
    return _combine(dest_flat, gate.T, x1, ys, sh_gu, sh_d, ln_g, ln_b)


def _prep_w_in(w_in):
    o = np.cumsum((0, 128, 128, 256, 256, 16, 256, 256, 256, 512, 4, 256, 256, 256))
    c = lambda i: w_in[..., o[i]:o[i + 1]]
    xbc = c(8)
    x_cols = xbc[..., 0:256]
    per_head = lambda m: jnp.concatenate(
        [m[..., SSM_N * (h // (SSM_H // SSM_G)):SSM_N * (h // (SSM_H // SSM_G) + 1)] for h in range(SSM_H)], axis=-1)
    b_cols = per_head(xbc[..., 256:384])
    c_cols = per_head(xbc[..., 384:512])
    dt_cols = jnp.repeat(c(9), SSM_P, axis=-1)
    pad = jnp.zeros(w_in.shape[:-1] + (W_GLA - o[5],), w_in.dtype)
    cols = [c(0), c(1), c(2), c(3), c(4), pad, c(5), c(6), c(7), x_cols, b_cols, c_cols, dt_cols,
            c(10), c(11), c(12)]
    return jnp.concatenate(cols, axis=-1).astype(BF16)


def _per_head(v):
    rep = SSM_H // SSM_G
    xp = v[..., 0:256]
    pick = lambda m: jnp.concatenate([m[..., SSM_N * (h // rep):SSM_N * (h // rep + 1)] for h in range(SSM_H)], axis=-1)
    return jnp.concatenate([xp, pick(v[..., 256:384]), pick(v[..., 384:512])], axis=-1)


def kernel(x, positions, w_in, gla_w_gate, gla_b_gate, gla_norm_w, sgu_ln_g, sgu_ln_b, sgu_w, sgu_b, ssm_conv_w, ssm_conv_b, ssm_dt_bias, ssm_a_log, ssm_d, ssm_norm_w, w_out, ln1_g, ln1_b, router_w, router_bias, exp_w_gate, exp_w_up, exp_w_down, sh_w_gate, sh_w_up, sh_w_down, ln2_g, ln2_b):
    B, S, _ = x.shape
    L = w_in.shape[0]
    T = B * S

    inv_freq = ROPE_THETA ** (-jnp.arange(0, ROT_DIM, 2, dtype=F32) / ROT_DIM)
    ang = positions.astype(F32)[..., None] * inv_freq
    cos, sin = jnp.cos(ang), jnp.sin(ang)
    ones = jnp.ones((B, S, DIL_DH - ROT_DIM), F32)
    cos_t = jnp.tile(jnp.concatenate([cos, cos, ones], axis=-1), (1, 1, 2))
    sin_t = jnp.tile(jnp.concatenate([-sin, sin, 0.0 * ones], axis=-1), (1, 1, 2))
    tb = jnp.asarray(_branch_log_multiplicity(S))

    w_p = _prep_w_in(w_in)
    wg_p = jnp.pad(gla_w_gate, ((0, 0), (0, 128 - GLA_RANK), (0, 0)))
    row = lambda v: v.reshape(L, 1, -1)
    gla_nw = jnp.tile(gla_norm_w, (1, GLA_H))
    sgu_wc = jnp.transpose(sgu_w, (0, 2, 1, 3)).reshape(L, SGU_CHUNK, SGU_G * SGU_CHUNK)
    sgu_bt = jnp.repeat(jnp.transpose(sgu_b, (0, 2, 1)), GROUP_W // SGU_G, axis=-1)
    exp64 = lambda v: jnp.repeat(v, SSM_P, axis=-1)
    w_out_b = w_out.astype(BF16)
    rw_t = jnp.transpose(router_w, (0, 2, 1))
    sh_gu = jnp.concatenate([sh_w_gate, sh_w_up], axis=-1).astype(BF16)
    sh_d = sh_w_down.astype(BF16)

    x2 = x.reshape(T, D_MODEL)
    for l in range(L):
        gla_p = (wg_p[l], row(gla_b_gate)[l], row(gla_nw)[l])
        sgu_p = (sgu_wc[l], sgu_bt[l], row(sgu_ln_g)[l], row(sgu_ln_b)[l])
        ssd_p = (_per_head(ssm_conv_w[l]), row(_per_head(ssm_conv_b))[l], row(exp64(ssm_dt_bias))[l],
                 row(exp64(ssm_a_log))[l], row(exp64(ssm_d))[l], row(ssm_norm_w)[l])
        x1 = _token_mixer(x2, B, S, w_p[l], gla_p, sgu_p, ssd_p, cos_t, sin_t, tb,
                          w_out_b[l], row(ln1_g)[l], row(ln1_b)[l])
        x2 = _moe(l, x1, rw_t[l], router_bias[l].reshape(N_EXPERTS, 1), exp_w_gate, exp_w_up, exp_w_down,
                  sh_gu[l], sh_d[l], row(ln2_g)[l], row(ln2_b)[l])
    return x2.reshape(B, S, D_MODEL)
```

```python
import functools
import math

import numpy as np
import jax
import jax.numpy as jnp
from jax import lax
from jax.experimental import pallas as pl
from jax.experimental.pallas import tpu as pltpu

F32 = jnp.float32
BF16 = jnp.bfloat16
I32 = jnp.int32

D_MODEL = 1024
N_LAYERS = 4
GROUP_W = 256

GLA_H, GLA_DK, GLA_DV, GLA_RANK, GLA_TAU = 4, 32, 64, 16, 16.0
SGU_G, SGU_CHUNK = 4, 128
SSM_H, SSM_P, SSM_G, SSM_N, SSM_CONV, SSM_CHUNK = 4, 64, 2, 64, 4, 128
DIL_H, DIL_DH, ROT_DIM, ROPE_THETA = 4, 64, 16, 500000.0
DIL_BRANCHES = ((128, 1), (512, 4), (2048, 16))

N_EXPERTS, TOP_K, N_EXPERT_GROUPS, TOPK_GROUPS, D_EXPERT = 128, 8, 8, 4, 256
ROUTED_SCALE = 1.0

ALPHA = (2 * N_LAYERS) ** 0.25
LN_EPS = 1e-5
RMS_EPS = 1e-6

W_GLA = 896
W_SGU = 512
W_SSD = 1280
W_DIL = 768
W_PROJ = W_GLA + W_SGU + W_SSD + W_DIL

BLK = 128
QBLK = 256
MOE_BLK = 512
TOK_BLK = 256
CMB_BLK = 128
NEG = -1e30

VMEM_LIMIT = 56 * 1024 * 1024


def _cparams(sem):
    return pltpu.CompilerParams(dimension_semantics=sem, vmem_limit_bytes=VMEM_LIMIT)


def _dot(a, b):
    return jnp.dot(a.astype(BF16), b.astype(BF16), preferred_element_type=F32)


def _dot_nt(a, b):
    return lax.dot_general(a.astype(BF16), b.astype(BF16), (((1,), (1,)), ((), ())),
                           preferred_element_type=F32)


def _dot_tn(a, b):
    return lax.dot_general(a.astype(BF16), b.astype(BF16), (((0,), (0,)), ((), ())),
                           preferred_element_type=F32)


def _split(a, n):
    out = []
    r = a
    for _ in range(n):
        p = r.astype(BF16)
        out.append(p)
        r = r - p.astype(F32)
    return out


def _dot_hi(a, b):
    a1, a2 = _split(a, 2)
    b1, b2 = _split(b, 2)
    return _dot(a1, b1) + (_dot(a1, b2) + _dot(a2, b1))


def _dot_nt_hi(a, b):
    a1, a2 = _split(a, 2)
    b1, b2 = _split(b, 2)
    return _dot_nt(a1, b1) + (_dot_nt(a1, b2) + _dot_nt(a2, b1))


def _dot_lhs01(m, b):
    b1, b2, b3 = _split(b, 3)
    return _dot(m, b1) + (_dot(m, b2) + _dot(m, b3))


def _dot_rhs01(a, m):
    a1, a2, a3 = _split(a, 3)
    return _dot(a1, m) + (_dot(a2, m) + _dot(a3, m))


def _iota(shape, dim):
    return lax.broadcasted_iota(I32, shape, dim)


def _sigmoid(x):
    return 1.0 / (1.0 + jnp.exp(-x))


def _silu(x):
    return x * _sigmoid(x)


def _softplus(x):
    return jnp.maximum(x, 0.0) + jnp.log1p(jnp.exp(-jnp.abs(x)))


def _gelu(x):
    return 0.5 * x * (1.0 + lax.erf(x * (2.0 ** -0.5)))


def _inproj_kernel(x_ref, w_ref, o_gla, o_sgu, o_ssd, o_dil):
    xb = x_ref[...].astype(BF16)
    off = 0
    for o in (o_gla, o_sgu, o_ssd, o_dil):
        n = o.shape[1]
        o[...] = jnp.dot(xb, w_ref[:, off:off + n], preferred_element_type=F32)
        off += n


def _inproj(x2, w_p):
    T = x2.shape[0]
    tm = 512
    widths = (W_GLA, W_SGU, W_SSD, W_DIL)
    return pl.pallas_call(
        _inproj_kernel,
        grid=(T // tm,),
        in_specs=[pl.BlockSpec((tm, D_MODEL), lambda i: (i, 0)),
                  pl.BlockSpec((D_MODEL, W_PROJ), lambda i: (0, 0))],
        out_specs=[pl.BlockSpec((tm, w), lambda i: (i, 0)) for w in widths],
        out_shape=[jax.ShapeDtypeStruct((T, w), F32) for w in widths],
        compiler_params=_cparams(("arbitrary",)),
        name="inproj",
    )(x2, w_p)


GLA_NB = 2
GLA_SUB = 16


def _gla_kernel(slab_ref, wg_ref, bg_ref, nw_ref, out_ref, st_ref, a_ref):
    S = slab_ref.shape[1]
    st_ref[...] = jnp.zeros_like(st_ref)
    row = _iota((BLK, BLK), 0)
    col = _iota((BLK, BLK), 1)
    tril = col <= row
    tril_bf = jnp.where(tril, 1.0, 0.0).astype(BF16)
    sub_row = _iota((GLA_SUB, BLK), 0)
    sub_col = _iota((GLA_SUB, BLK), 1)
    sub_head = sub_col >> 5
    v_head = _iota((BLK, GROUP_W), 1) >> 6
    st_diag = (_iota((GROUP_W, BLK), 0) >> 6) == (_iota((GROUP_W, BLK), 1) >> 5)
    seg = jnp.where((_iota((GROUP_W, GROUP_W), 0) >> 6) == (_iota((GROUP_W, GROUP_W), 1) >> 6),
                    1.0, 0.0).astype(BF16)
    wg = wg_ref[...]
    bg = bg_ref[...]
    nw = nw_ref[...]

    def block(b, r0):
        blk = slab_ref[b, pl.ds(r0, BLK), :]
        q = blk[:, 0:128] * (GLA_DK ** -0.5)
        k = blk[:, 128:256]
        v = blk[:, 256:512]
        r = blk[:, 512:768]
        lr = blk[:, 768:896]
        z = _dot_hi(lr, wg) + bg
        gk = (jnp.minimum(z, 0.0) - jnp.log1p(jnp.exp(-jnp.abs(z)))) * (1.0 / GLA_TAU)
        g = _dot_lhs01(tril_bf, gk)
        g_last = g[BLK - 1:BLK, :]
        q_in = q * jnp.exp(g)
        k_in = k * jnp.exp(g_last - g)
        vb = v.astype(BF16)
        o = _dot_nt(q_in, st_ref[b])
        for c in range(BLK // GLA_SUB):
            lo = c * GLA_SUB
            n_c = g[lo - 1:lo, :] if c else jnp.zeros((1, BLK), F32)
            qc = q[lo:lo + GLA_SUB, :] * jnp.exp(g[lo:lo + GLA_SUB, :] - n_c)
            kc = k * jnp.exp(jnp.where(row < lo + GLA_SUB, n_c - g, -jnp.inf))
            qs = jnp.concatenate([jnp.where(sub_head == h, qc, 0.0) for h in range(GLA_H)], axis=0)
            sc = _dot_nt(qs, kc)
            causal = sub_col <= sub_row + lo
            for h in range(GLA_H):
                a_ref[b, h, lo:lo + GLA_SUB, :] = jnp.where(causal, sc[h * GLA_SUB:(h + 1) * GLA_SUB, :], 0.0)
        for h in range(GLA_H):
            o = o + jnp.where(v_head == h, _dot(a_ref[b, h], vb), 0.0)
        st_ref[b] = st_ref[b] * jnp.exp(g_last) + jnp.where(st_diag, _dot_tn(vb, k_in), 0.0)
        ms = _dot_rhs01(o * o, seg) * (1.0 / GLA_DV)
        y = o * lax.rsqrt(ms + RMS_EPS) * nw * _silu(r)
        out_ref[b, pl.ds(r0, BLK), :] = y.astype(out_ref.dtype)

    def body(n, carry):
        r0 = pl.multiple_of(n * BLK, BLK)
        for b in range(slab_ref.shape[0]):
            block(b, r0)
        return carry

    lax.fori_loop(0, S // BLK, body, 0)


def _gla(slab, wg_p, bg, nw_t):
    B, S, _ = slab.shape
    nb = GLA_NB if B % GLA_NB == 0 else 1
    return pl.pallas_call(
        _gla_kernel,
        grid=(B // nb,),
        in_specs=[pl.BlockSpec((nb, S, W_GLA), lambda b: (b, 0, 0)),
                  pl.BlockSpec((128, 128), lambda b: (0, 0)),
                  pl.BlockSpec((1, 128), lambda b: (0, 0)),
                  pl.BlockSpec((1, GROUP_W), lambda b: (0, 0))],
        out_specs=pl.BlockSpec((nb, S, GROUP_W), lambda b: (b, 0, 0)),
        out_shape=jax.ShapeDtypeStruct((B, S, GROUP_W), BF16),
        scratch_shapes=[pltpu.VMEM((nb, GROUP_W, 128), F32), pltpu.VMEM((nb, GLA_H, BLK, BLK), F32)],
        compiler_params=_cparams(("arbitrary",)),
        name="gla",
    )(slab, wg_p, bg, nw_t)


def _sgu_kernel(slab_ref, w_ref, bias_ref, g_ref, b_ref, out_ref):
    S = slab_ref.shape[1]
    wmask = (_iota((BLK, SGU_G * BLK), 1) & (BLK - 1)) <= _iota((BLK, SGU_G * BLK), 0)
    w = jnp.where(wmask, w_ref[...], 0.0).astype(BF16)
    lane_grp = _iota((BLK, GROUP_W), 1) >> 6
    bias = bias_ref[...]
    ln_g = g_ref[...]
    ln_b = b_ref[...]

    def body(n, carry):
        r0 = pl.multiple_of(n * BLK, BLK)
        blk = slab_ref[0, pl.ds(r0, BLK), :]
        u = _gelu(blk[:, 0:256])
        v = _gelu(blk[:, 256:512])
        mu = jnp.mean(v, axis=-1, keepdims=True)
        var = jnp.mean(jnp.square(v - mu), axis=-1, keepdims=True)
        v = (v - mu) * lax.rsqrt(var + LN_EPS) * ln_g + ln_b
        vexp = jnp.concatenate([jnp.where(lane_grp == g, v, 0.0) for g in range(SGU_G)], axis=0)
        s = _dot(w, vexp) + bias
        out_ref[0, pl.ds(r0, BLK), :] = (u * s).astype(out_ref.dtype)
        return carry

    lax.fori_loop(0, S // BLK, body, 0)


def _sgu(slab, w_cat, bias_t, ln_g, ln_b):
    B, S, _ = slab.shape
    return pl.pallas_call(
        _sgu_kernel,
        grid=(B,),
        in_specs=[pl.BlockSpec((1, S, W_SGU), lambda b: (b, 0, 0)),
                  pl.BlockSpec((BLK, SGU_G * BLK), lambda b: (0, 0)),
                  pl.BlockSpec((BLK, GROUP_W), lambda b: (0, 0)),
                  pl.BlockSpec((1, GROUP_W), lambda b: (0, 0)),
                  pl.BlockSpec((1, GROUP_W), lambda b: (0, 0))],
        out_specs=pl.BlockSpec((1, S, GROUP_W), lambda b: (b, 0, 0)),
        out_shape=jax.ShapeDtypeStruct((B, S, GROUP_W), BF16),
        compiler_params=_cparams(("arbitrary",)),
        name="sgu",
    )(slab, w_cat, bias_t, ln_g, ln_b)


def _ssd_kernel(slab_ref, cw_ref, cb_ref, dtb_ref, a_ref, d_ref, nw_ref, out_ref, xpad_ref, st_ref):
    S = slab_ref.shape[1]
    CW = 3 * GROUP_W
    st_ref[...] = jnp.zeros_like(st_ref)
    xpad_ref[0:8, :] = jnp.zeros((8, CW), F32)
    xpad_ref[8:S + 8, :] = slab_ref[0, :, 256:256 + CW]
    row = _iota((BLK, BLK), 0)
    col = _iota((BLK, BLK), 1)
    tril = col <= row
    tril_bf = jnp.where(tril, 1.0, 0.0).astype(BF16)
    lane_head = _iota((BLK, GROUP_W), 1) >> 6
    st_diag = (_iota((GROUP_W, GROUP_W), 0) >> 6) == (_iota((GROUP_W, GROUP_W), 1) >> 6)
    cw = cw_ref[...]
    cb = cb_ref[...]
    dtb = dtb_ref[...]
    a_neg = -jnp.exp(a_ref[...])
    d_skip = d_ref[...]
    nw = nw_ref[...]

    def body(n, carry):
        r0 = pl.multiple_of(n * BLK, BLK)
        xw = xpad_ref[pl.ds(r0, BLK + 8), :]
        conv = cb + cw[0:1, :] * xw[5:5 + BLK, :]
        for w in range(1, SSM_CONV):
            conv = conv + cw[w:w + 1, :] * xw[5 + w:5 + w + BLK, :]
        xbc = _silu(conv)
        xs = xbc[:, 0:256]
        b_e = xbc[:, 256:512]
        c_e = xbc[:, 512:768]
        z = slab_ref[0, pl.ds(r0, BLK), 0:256]
        dt = _softplus(slab_ref[0, pl.ds(r0, BLK), 1024:1280] + dtb)
        acs = _dot_lhs01(tril_bf, dt * a_neg)
        acs_last = acs[BLK - 1:BLK, :]
        x_dt = (xs * dt).astype(BF16)
        acs_t = [jnp.transpose(acs[:, 0:128]), jnp.transpose(acs[:, 128:256])]
        y = _dot(c_e * jnp.exp(acs), st_ref[...])
        b_bf = b_e.astype(BF16)
        for h in range(SSM_H):
            a_col = jnp.broadcast_to(acs[:, 64 * h:64 * h + 1], (BLK, BLK))
            a_row = jnp.broadcast_to(acs_t[h // 2][64 * (h % 2):64 * (h % 2) + 1, :], (BLK, BLK))
            lm = jnp.exp(jnp.where(tril, a_col - a_row, -jnp.inf))
            ch = jnp.where(lane_head == h, c_e, 0.0)
            m = _dot_nt(ch, b_bf) * lm
            y = y + jnp.where(lane_head == h, _dot(m, x_dt), 0.0)
        new = _dot_tn(b_e * jnp.exp(acs_last - acs), x_dt)
        st_ref[...] = st_ref[...] * jnp.exp(acs_last) + jnp.where(st_diag, new, 0.0)
        y = (y + d_skip * xs) * _silu(z)
        parts = []
        for g in range(SSM_G):
            yg = y[:, 128 * g:128 * (g + 1)]
            ms = jnp.mean(yg * yg, axis=-1, keepdims=True)
            parts.append(yg * lax.rsqrt(ms + RMS_EPS))
        out = jnp.concatenate(parts, axis=1) * nw
        out_ref[0, pl.ds(r0, BLK), :] = out.astype(out_ref.dtype)
        return carry

    lax.fori_loop(0, S // BLK, body, 0)


def _ssd(slab, cw_e, cb_e, dtb_e, a_e, d_e, nw):
    B, S, _ = slab.shape
    CW = 3 * GROUP_W
    vec = lambda w: pl.BlockSpec((1, w), lambda b: (0, 0))
    return pl.pallas_call(
        _ssd_kernel,
        grid=(B,),
        in_specs=[pl.BlockSpec((1, S, W_SSD), lambda b: (b, 0, 0)),
                  pl.BlockSpec((SSM_CONV, CW), lambda b: (0, 0)),
                  vec(CW), vec(GROUP_W), vec(GROUP_W), vec(GROUP_W), vec(GROUP_W)],
        out_specs=pl.BlockSpec((1, S, GROUP_W), lambda b: (b, 0, 0)),
        out_shape=jax.ShapeDtypeStruct((B, S, GROUP_W), BF16),
        scratch_shapes=[pltpu.VMEM((S + 8, CW), F32), pltpu.VMEM((GROUP_W, GROUP_W), F32)],
        compiler_params=_cparams(("arbitrary",)),
        name="ssd",
    )(slab, cw_e, cb_e, dtb_e, a_e, d_e, nw)


def _branch_log_multiplicity(S):
    nq = S // QBLK
    r = np.arange(QBLK)[:, None]
    u = np.arange(S)[None, :]
    d = (nq - 1) * QBLK + r - u
    c = np.zeros(d.shape, np.float64)
    for window, dil in DIL_BRANCHES:
        c += (d >= 0) & (d % dil == 0) & (d <= window)
    with np.errstate(divide="ignore"):
        return np.where(c > 0, np.log(np.maximum(c, 1.0)), NEG).astype(np.float32)


def _dil_kernel(slab_ref, cos_ref, sin_ref, tb_ref, out_ref, q_ref, k_ref, v_ref):
    S = slab_ref.shape[1]
    nq = S // QBLK
    lane_d = _iota((1, 128), 1) & (DIL_DH - 1)

    def rope(t):
        fwd = pltpu.roll(t, 128 - ROT_DIM // 2, 1)
        bwd = pltpu.roll(t, ROT_DIM // 2, 1)
        return t * cos_ref[0] + jnp.where(lane_d < ROT_DIM // 2, fwd, bwd) * sin_ref[0]

    for c in range(0, GROUP_W, 128):
        q_ref[:, c:c + 128] = (rope(slab_ref[0, :, c:c + 128]) * (DIL_DH ** -0.5)).astype(BF16)
        k_ref[:, c:c + 128] = rope(slab_ref[0, :, 256 + c:384 + c]).astype(BF16)
    v_ref[...] = slab_ref[0, :, 512:768].astype(BF16)
    lane_head = _iota((QBLK, GROUP_W), 1) >> 6
    for i in range(nq):
        nk = (i + 1) * QBLK
        qi = q_ref[i * QBLK:(i + 1) * QBLK, :]
        acc = jnp.zeros((QBLK, GROUP_W), F32)
        for h in range(DIL_H):
            qh = jnp.where(lane_head == h, qi, jnp.zeros_like(qi))
            s = _dot_nt(qh, k_ref[0:nk, :]) + tb_ref[:, (nq - 1 - i) * QBLK:]
            m = jnp.max(s, axis=-1, keepdims=True)
            p = jnp.exp(s - m)
            den = jnp.sum(p, axis=-1, keepdims=True)
            o = _dot(p, v_ref[0:nk, :])
            acc = acc + jnp.where(lane_head == h, o / den, 0.0)
        out_ref[0, i * QBLK:(i + 1) * QBLK, :] = acc.astype(out_ref.dtype)


def _dil(slab, cos_t, sin_t, tb):
    B, S, _ = slab.shape
    return pl.pallas_call(
        _dil_kernel,
        grid=(B,),
        in_specs=[pl.BlockSpec((1, S, W_DIL), lambda b: (b, 0, 0)),
                  pl.BlockSpec((1, S, 128), lambda b: (b, 0, 0)),
                  pl.BlockSpec((1, S, 128), lambda b: (b, 0, 0)),
                  pl.BlockSpec((QBLK, S), lambda b: (0, 0), pipeline_mode=pl.Buffered(1))],
        out_specs=pl.BlockSpec((1, S, GROUP_W), lambda b: (b, 0, 0)),
        out_shape=jax.ShapeDtypeStruct((B, S, GROUP_W), BF16),
        scratch_shapes=[pltpu.VMEM((S, GROUP_W), BF16)] * 3,
        compiler_params=_cparams(("arbitrary",)),
        name="dil",
    )(slab, cos_t, sin_t, tb)


def _layernorm(x, g, b):
    mu = jnp.mean(x, axis=-1, keepdims=True)
    var = jnp.mean(jnp.square(x - mu), axis=-1, keepdims=True)
    return (x - mu) * lax.rsqrt(var + LN_EPS) * g + b


def _outproj_kernel(ya, yb, yc, yd, x_ref, w_ref, g_ref, b_ref, o_ref):
    y = jnp.concatenate([ya[...], yb[...], yc[...], yd[...]], axis=1)
    y = jnp.dot(y, w_ref[...], preferred_element_type=F32)
    o_ref[...] = _layernorm(ALPHA * x_ref[...] + y, g_ref[...], b_ref[...])


def _outproj(ya, yb, yc, yd, x2, w_out, g, b):
    T = x2.shape[0]
    tm = 512
    ysp = pl.BlockSpec((tm, GROUP_W), lambda i: (i, 0))
    vec = pl.BlockSpec((1, D_MODEL), lambda i: (0, 0))
    return pl.pallas_call(
        _outproj_kernel,
        grid=(T // tm,),
        in_specs=[ysp, ysp, ysp, ysp,
                  pl.BlockSpec((tm, D_MODEL), lambda i: (i, 0)),
                  pl.BlockSpec((D_MODEL, D_MODEL), lambda i: (0, 0)), vec, vec],
        out_specs=pl.BlockSpec((tm, D_MODEL), lambda i: (i, 0)),
        out_shape=jax.ShapeDtypeStruct((T, D_MODEL), F32),
        compiler_params=_cparams(("arbitrary",)),
        name="outproj_ln",
    )(ya, yb, yc, yd, x2, w_out, g, b)


def _router_kernel(x_ref, rw_ref, rb_ref, idx_ref, gate_ref, pos_ref, cnt_ref, carry_ref):
    TB = x_ref.shape[0]
    E = N_EXPERTS
    per = E // N_EXPERT_GROUPS

    @pl.when(pl.program_id(0) == 0)
    def _():
        carry_ref[...] = jnp.zeros_like(carry_ref)

    logits = _dot_nt_hi(rw_ref[...], x_ref[...])
    scores = _sigmoid(logits)
    choice = scores + rb_ref[...]
    e_iota = _iota((E, TB), 0)

    l_iota = _iota((per, TB), 0)
    grp = []
    for g in range(N_EXPERT_GROUPS):
        cg = choice[per * g:per * (g + 1), :]
        m1 = jnp.max(cg, axis=0, keepdims=True)
        i1 = jnp.min(jnp.where(cg == m1, l_iota, per), axis=0, keepdims=True)
        m2 = jnp.max(jnp.where(l_iota == i1, -jnp.inf, cg), axis=0, keepdims=True)
        grp.append(m1 + m2)
    gs = jnp.concatenate(grp, axis=0)
    g_iota = _iota((N_EXPERT_GROUPS, TB), 0)
    keep = jnp.zeros((N_EXPERT_GROUPS, TB), jnp.bool_)
    for _ in range(TOPK_GROUPS):
        m = jnp.max(gs, axis=0, keepdims=True)
        gi = jnp.min(jnp.where(gs == m, g_iota, N_EXPERT_GROUPS), axis=0, keepdims=True)
        hit = g_iota == gi
        keep = keep | hit
        gs = jnp.where(hit, -jnp.inf, gs)
    keep_f = jnp.where(keep, 1.0, 0.0)
    keep_e = jnp.concatenate([jnp.broadcast_to(keep_f[g:g + 1, :], (per, TB))
                              for g in range(N_EXPERT_GROUPS)], axis=0)
    cm = jnp.where(keep_e > 0.5, choice, -jnp.inf)

    hits, idxs, gates = [], [], []
    onehot = jnp.zeros((E, TB), F32)
    for _ in range(TOP_K):
        m = jnp.max(cm, axis=0, keepdims=True)
        ei = jnp.min(jnp.where(cm == m, e_iota, E), axis=0, keepdims=True)
        hit = e_iota == ei
        hits.append(hit)
        idxs.append(ei)
        gates.append(jnp.sum(jnp.where(hit, scores, 0.0), axis=0, keepdims=True))
        onehot = onehot + jnp.where(hit, 1.0, 0.0)
        cm = jnp.where(hit, -jnp.inf, cm)
    gate = jnp.concatenate(gates, axis=0)
    gate = gate / jnp.sum(gate, axis=0, keepdims=True) * ROUTED_SCALE

    before = jnp.where(_iota((TB, TB), 0) < _iota((TB, TB), 1), 1.0, 0.0).astype(BF16)
    cnt = carry_ref[...] + _dot(onehot, before)
    pos = jnp.concatenate([jnp.sum(jnp.where(h, cnt, 0.0), axis=0, keepdims=True) for h in hits], axis=0)
    carry_ref[...] = carry_ref[...] + jnp.sum(onehot, axis=1, keepdims=True)

    idx_ref[...] = jnp.concatenate(idxs, axis=0)
    gate_ref[...] = gate
    pos_ref[...] = pos.astype(I32)
    cnt_ref[...] = jnp.broadcast_to(carry_ref[...], cnt_ref.shape)


def _router(x1, rw_t, rb):
    T = x1.shape[0]
    TB = TOK_BLK
    kt = pl.BlockSpec((TOP_K, TB), lambda i: (0, i))
    return pl.pallas_call(
        _router_kernel,
        grid=(T // TB,),
        in_specs=[pl.BlockSpec((TB, D_MODEL), lambda i: (i, 0)),
                  pl.BlockSpec((N_EXPERTS, D_MODEL), lambda i: (0, 0)),
                  pl.BlockSpec((N_EXPERTS, 1), lambda i: (0, 0))],
        out_specs=[kt, kt, kt, pl.BlockSpec((N_EXPERTS, 128), lambda i: (0, 0))],
        out_shape=[jax.ShapeDtypeStruct((TOP_K, T), I32), jax.ShapeDtypeStruct((TOP_K, T), F32),
                   jax.ShapeDtypeStruct((TOP_K, T), I32), jax.ShapeDtypeStruct((N_EXPERTS, 128), F32)],
        scratch_shapes=[pltpu.VMEM((N_EXPERTS, 1), F32)],
        compiler_params=_cparams(("arbitrary",)),
        name="router",
    )(x1, rw_t, rb)


def _slots_kernel(ps_ref, idx_ref, pos_ref, dest_ref):
    idx = idx_ref[...]

    def body(e, acc):
        return acc + jnp.where(idx == e, ps_ref[e], 0)

    dest_ref[...] = lax.fori_loop(0, N_EXPERTS, body, pos_ref[...], unroll=8)


def _slots(pad_start, idx, pos):
    T = idx.shape[1]
    tb = min(T, 2048)
    kt = pl.BlockSpec((TOP_K, tb), lambda i, ps: (0, i))
    return pl.pallas_call(
        _slots_kernel,
        grid_spec=pltpu.PrefetchScalarGridSpec(
            num_scalar_prefetch=1, grid=(T // tb,), in_specs=[kt, kt], out_specs=kt),
        out_shape=jax.ShapeDtypeStruct((TOP_K, T), I32),
        compiler_params=_cparams(("arbitrary",)),
        name="slots",
    )(pad_start, idx, pos)


ROW_T = D_MODEL // 2 // 128
U32 = jnp.uint32


def _pack_rows(x):
    bits = lax.bitcast_convert_type(x.astype(BF16).astype(F32), U32)
    half = D_MODEL // 2
    return (bits[:, :half] >> 16) | (bits[:, half:] & U32(0xFFFF0000))


def _unpack_rows(w):
    lo = lax.bitcast_convert_type(w << 16, F32)
    hi = lax.bitcast_convert_type(w & U32(0xFFFF0000), F32)
    return jnp.concatenate([lo, hi], axis=1)


def _to_tiles(dst_ref, base, w):
    n = w.shape[0]
    for s in range(ROW_T):
        dst_ref[pl.ds(base + s, n, stride=ROW_T), :] = w[:, 128 * s:128 * (s + 1)]


def _from_tiles(src_ref, base, n):
    return jnp.concatenate([src_ref[pl.ds(base + s, n, stride=ROW_T), :] for s in range(ROW_T)], axis=1)


def _tailzero_kernel(tb_ref, xs_ref):
    del tb_ref
    xs_ref[...] = jnp.zeros_like(xs_ref)


def _tailzero(tail_blk, nblk):
    rows = MOE_BLK * ROW_T
    return pl.pallas_call(
        _tailzero_kernel,
        grid_spec=pltpu.PrefetchScalarGridSpec(
            num_scalar_prefetch=1, grid=(N_EXPERTS,), in_specs=[],
            out_specs=pl.BlockSpec((rows, 128), lambda e, tb: (tb[e], 0))),
        out_shape=jax.ShapeDtypeStruct((nblk * rows, 128), U32),
        compiler_params=_cparams(("arbitrary",)),
        name="tailzero",
    )(tail_blk)


def _dispatch_kernel(dest_ref, x_ref, xs_in, xs_out, xt, sem):
    del xs_in
    TB = x_ref.shape[0]
    _to_tiles(xt, 0, _pack_rows(x_ref[...]))

    def issue(t, c):
        src = xt.at[pl.ds(pl.multiple_of(t * ROW_T, ROW_T), ROW_T), :]
        for k in range(TOP_K):
            d = dest_ref[t * TOP_K + k]
            pltpu.make_async_copy(src, xs_out.at[pl.ds(pl.multiple_of(d * ROW_T, ROW_T), ROW_T), :], sem).start()
        return c

    lax.fori_loop(0, TB, issue, 0, unroll=2)

    for k in range(TOP_K):
        pltpu.make_async_copy(xt, xs_out.at[pl.ds(0, TB * ROW_T), :], sem).wait()


def _dispatch(dest_flat, x1, xs0):
    T = x1.shape[0]
    TB = TOK_BLK
    return pl.pallas_call(
        _dispatch_kernel,
        grid=(T // TB,),
        in_specs=[pl.BlockSpec((TB * TOP_K,), lambda i: (i,), memory_space=pltpu.SMEM),
                  pl.BlockSpec((TB, D_MODEL), lambda i: (i, 0)),
                  pl.BlockSpec(memory_space=pl.ANY)],
        out_specs=pl.BlockSpec(memory_space=pl.ANY),
        out_shape=jax.ShapeDtypeStruct(xs0.shape, xs0.dtype),
        scratch_shapes=[pltpu.VMEM((TB * ROW_T, 128), U32), pltpu.SemaphoreType.DMA],
        input_output_aliases={2: 0},
        compiler_params=_cparams(("arbitrary",)),
        name="dispatch",
    )(dest_flat, x1, xs0)


def _experts_kernel(be_ref, nu_ref, xs_ref, wg_ref, wu_ref, wd_ref, ys_ref, wgu_s, wd_s):
    i = pl.program_id(0)

    @pl.when(i < nu_ref[0])
    def _():
        prev = be_ref[jnp.maximum(i - 1, 0)]

        @pl.when((i == 0) | (be_ref[i] != prev))
        def _():
            wgu_s[:, 0:D_EXPERT] = wg_ref[0, 0].astype(BF16)
            wgu_s[:, D_EXPERT:] = wu_ref[0, 0].astype(BF16)
            wd_s[...] = wd_ref[0, 0].astype(BF16)

        x = _unpack_rows(_from_tiles(xs_ref, 0, MOE_BLK)).astype(BF16)
        h = jnp.dot(x, wgu_s[...], preferred_element_type=F32)
        a = _silu(h[:, 0:D_EXPERT]) * h[:, D_EXPERT:]
        _to_tiles(ys_ref, 0, _pack_rows(jnp.dot(a.astype(BF16), wd_s[...], preferred_element_type=F32)))


def _experts(layer, blk_expert, n_used, xs, wg, wu, wd):
    rows = MOE_BLK * ROW_T
    nblk = xs.shape[0] // rows
    row = lambda i, be, nu: (jnp.minimum(i, nu[0] - 1), 0)
    wsel = lambda i, be, nu: (layer, be[i], 0, 0)
    return pl.pallas_call(
        _experts_kernel,
        grid_spec=pltpu.PrefetchScalarGridSpec(
            num_scalar_prefetch=2,
            grid=(nblk,),
            in_specs=[pl.BlockSpec((rows, 128), row),
                      pl.BlockSpec((1, 1, D_MODEL, D_EXPERT), wsel),
                      pl.BlockSpec((1, 1, D_MODEL, D_EXPERT), wsel),
                      pl.BlockSpec((1, 1, D_EXPERT, D_MODEL), wsel)],
            out_specs=pl.BlockSpec((rows, 128), row),
            scratch_shapes=[pltpu.VMEM((D_MODEL, 2 * D_EXPERT), BF16),
                            pltpu.VMEM((D_EXPERT, D_MODEL), BF16)]),
        out_shape=jax.ShapeDtypeStruct(xs.shape, U32),
        compiler_params=_cparams(("arbitrary",)),
        name="experts",
    )(blk_expert, n_used, xs, wg, wu, wd)


def _combine_kernel(dest_ref, gate_ref, x_ref, ys_ref, sgu_ref, sd_ref, g_ref, b_ref, o_ref, buf, sem):
    TB = x_ref.shape[0]

    def issue(t, c):
        for k in range(TOP_K):
            d = dest_ref[t * TOP_K + k]
            pltpu.make_async_copy(ys_ref.at[pl.ds(pl.multiple_of(d * ROW_T, ROW_T), ROW_T), :],
                                  buf.at[pl.ds(pl.multiple_of((k * TB + t) * ROW_T, ROW_T), ROW_T), :], sem).start()
        return c

    lax.fori_loop(0, TB, issue, 0, unroll=2)

    x = x_ref[...]
    h = jnp.dot(x.astype(BF16), sgu_ref[...], preferred_element_type=F32)
    a = _silu(h[:, 0:D_EXPERT]) * h[:, D_EXPERT:]
    y = ALPHA * x + jnp.dot(a.astype(BF16), sd_ref[...], preferred_element_type=F32)

    pltpu.make_async_copy(ys_ref.at[pl.ds(0, TOP_K * TB * ROW_T), :], buf, sem).wait()

    gate = gate_ref[...]
    for k in range(TOP_K):
        y = y + gate[:, k:k + 1] * _unpack_rows(_from_tiles(buf, k * TB * ROW_T, TB))
    o_ref[...] = _layernorm(y, g_ref[...], b_ref[...])


def _combine(dest_flat, gate_t, x1, ys, sh_gu, sh_d, g, b):
    T = x1.shape[0]
    TB = CMB_BLK
    vec = pl.BlockSpec((1, D_MODEL), lambda i: (0, 0))
    return pl.pallas_call(
        _combine_kernel,
        grid=(T // TB,),
        in_specs=[pl.BlockSpec((TB * TOP_K,), lambda i: (i,), memory_space=pltpu.SMEM),
                  pl.BlockSpec((TB, TOP_K), lambda i: (i, 0)),
                  pl.BlockSpec((TB, D_MODEL), lambda i: (i, 0)),
                  pl.BlockSpec(memory_space=pl.ANY),
                  pl.BlockSpec((D_MODEL, 2 * D_EXPERT), lambda i: (0, 0)),
                  pl.BlockSpec((D_EXPERT, D_MODEL), lambda i: (0, 0)), vec, vec],
        out_specs=pl.BlockSpec((TB, D_MODEL), lambda i: (i, 0)),
        out_shape=jax.ShapeDtypeStruct((T, D_MODEL), F32),
        scratch_shapes=[pltpu.VMEM((TOP_K * TB * ROW_T, 128), U32), pltpu.SemaphoreType.DMA],
        compiler_params=_cparams(("arbitrary",)),
        name="combine_ln",
    )(dest_flat, gate_t, x1, ys, sh_gu, sh_d, g, b)


def _token_mixer(x2, B, S, w_p, gla_p, sgu_p, ssd_p, cos_t, sin_t, tb, w_out, ln_g, ln_b):
    s_gla, s_sgu, s_ssd, s_dil = _inproj(x2, w_p)
    ya = _gla(s_gla.reshape(B, S, W_GLA), *gla_p)
    yb = _sgu(s_sgu.reshape(B, S, W_SGU), *sgu_p)
    yc = _ssd(s_ssd.reshape(B, S, W_SSD), *ssd_p)
    yd = _dil(s_dil.reshape(B, S, W_DIL), cos_t, sin_t, tb)
    T = B * S
    flat = lambda y: y.reshape(T, GROUP_W)
    return _outproj(flat(ya), flat(yb), flat(yc), flat(yd), x2, w_out, ln_g, ln_b)


def _moe(layer, x1, rw_t, rb, wg, wu, wd, sh_gu, sh_d, ln_g, ln_b):
    T = x1.shape[0]
    idx, gate, pos, cnt = _router(x1, rw_t, rb)
    counts = cnt[:, 0].astype(I32)
    padded = (counts + MOE_BLK - 1) // MOE_BLK * MOE_BLK
    pad_end = jnp.cumsum(padded)
    pad_start = pad_end - padded
    dest = _slots(pad_start.astype(I32), idx, pos)
    dest_flat = dest.T.reshape(T * TOP_K)
    nblk = T * TOP_K // MOE_BLK + N_EXPERTS
    n_used = (pad_end[-1] // MOE_BLK).astype(I32)
    blk = jnp.minimum(jnp.arange(nblk, dtype=I32), n_used - 1) * MOE_BLK
    blk_expert = jnp.sum((pad_end[None, :] <= blk[:, None]).astype(I32), axis=1)
    blk_expert = jnp.minimum(blk_expert, N_EXPERTS - 1)
    tail_blk = jnp.maximum(pad_end // MOE_BLK - 1, 0).astype(I32)
    xs = _dispatch(dest_flat, x1, _tailzero(tail_blk, nblk))
    ys = _experts(layer, blk_expert, n_used.reshape(1), xs, wg, wu, wd)
    return _combine(dest_flat, gate.T, x1, ys, sh_gu, sh_d, ln_g, ln_b)


def _prep_w_in(w_in):
    o = np.cumsum((0, 128, 128, 256, 256, 16, 256, 256, 256, 512, 4, 256, 256, 256))
    c = lambda i: w_in[..., o[i]:o[i + 1]]
    xbc = c(8)
    x_cols = xbc[..., 0:256]
    per_head = lambda m: jnp.concatenate(
        [m[..., SSM_N * (h // (SSM_H // SSM_G)):SSM_N * (h // (SSM_H // SSM_G) + 1)] for h in range(SSM_H)], axis=-1)
    b_cols = per_head(xbc[..., 256:384])
    c_cols = per_head(xbc[..., 384:512])
    dt_cols = jnp.repeat(c(9), SSM_P, axis=-1)
    pad = jnp.zeros(w_in.shape[:-1] + (W_GLA - o[5],), w_in.dtype)
    cols = [c(0), c(1), c(2), c(3), c(4), pad, c(5), c(6), c(7), x_cols, b_cols, c_cols, dt_cols,
            c(10), c(11), c(12)]
    return jnp.concatenate(cols, axis=-1).astype(BF16)


def _per_head(v):
    rep = SSM_H // SSM_G
    xp = v[..., 0:256]
    pick = lambda m: jnp.concatenate([m[..., SSM_N * (h // rep):SSM_N * (h // rep + 1)] for h in range(SSM_H)], axis=-1)
    return jnp.concatenate([xp, pick(v[..., 256:384]), pick(v[..., 384:512])], axis=-1)


def kernel(x, positions, w_in, gla_w_gate, gla_b_gate, gla_norm_w, sgu_ln_g, sgu_ln_b, sgu_w, sgu_b, ssm_conv_w, ssm_conv_b, ssm_dt_bias, ssm_a_log, ssm_d, ssm_norm_w, w_out, ln1_g, ln1_b, router_w, router_bias, exp_w_gate, exp_w_up, exp_w_down, sh_w_gate, sh_w_up, sh_w_down, ln2_g, ln2_b):
    B, S, _ = x.shape
    L = w_in.shape[0]
    T = B * S

    inv_freq = ROPE_THETA ** (-jnp.arange(0, ROT_DIM, 2, dtype=F32) / ROT_DIM)
    ang = positions.astype(F32)[..., None] * inv_freq
    cos, sin = jnp.cos(ang), jnp.sin(ang)
    ones = jnp.ones((B, S, DIL_DH - ROT_DIM), F32)
    cos_t = jnp.tile(jnp.concatenate([cos, cos, ones], axis=-1), (1, 1, 2))
    sin_t = jnp.tile(jnp.concatenate([-sin, sin, 0.0 * ones], axis=-1), (1, 1, 2))
    tb = jnp.asarray(_branch_log_multiplicity(S))

    w_p = _prep_w_in(w_in)
    wg_p = jnp.pad(gla_w_gate, ((0, 0), (0, 128 - GLA_RANK), (0, 0)))
    row = lambda v: v.reshape(L, 1, -1)
    gla_nw = jnp.tile(gla_norm_w, (1, GLA_H))
    sgu_wc = jnp.transpose(sgu_w, (0, 2, 1, 3)).reshape(L, SGU_CHUNK, SGU_G * SGU_CHUNK)
    sgu_bt = jnp.repeat(jnp.transpose(sgu_b, (0, 2, 1)), GROUP_W // SGU_G, axis=-1)
    exp64 = lambda v: jnp.repeat(v, SSM_P, axis=-1)
    w_out_b = w_out.astype(BF16)
    rw_t = jnp.transpose(router_w, (0, 2, 1))
    sh_gu = jnp.concatenate([sh_w_gate, sh_w_up], axis=-1).astype(BF16)
    sh_d = sh_w_down.astype(BF16)

    x2 = x.reshape(T, D_MODEL)
    for l in range(L):
        gla_p = (wg_p[l], row(gla_b_gate)[l], row(gla_nw)[l])
        sgu_p = (sgu_wc[l], sgu_bt[l], row(sgu_ln_g)[l], row(sgu_ln_b)[l])
        ssd_p = (_per_head(ssm_conv_w[l]), row(_per_head(ssm_conv_b))[l], row(exp64(ssm_dt_bias))[l],
                 row(exp64(ssm_a_log))[l], row(exp64(ssm_d))[l], row(ssm_norm_w)[l])
        x1 = _token_mixer(x2, B, S, w_p[l], gla_p, sgu_p, ssd_p, cos_t, sin_t, tb,
                          w_out_b[l], row(ln1_g)[l], row(ln1_b)[l])
        x2 = _moe(l, x1, rw_t[l], router_bias[l].reshape(N_EXPERTS, 1), exp_w_gate, exp_w_up, exp_w_down,
                  sh_gu[l], sh_d[l], row(ln2_g)[l], row(ln2_b)[l])
    return x2.reshape(B, S, D_MODEL)
```

```python
import functools
import math

import numpy as np
import jax
import jax.numpy as jnp
from jax import lax
from jax.experimental import pallas as pl
from jax.experimental.pallas import tpu as pltpu

F32 = jnp.float32
BF16 = jnp.bfloat16
I32 = jnp.int32

D_MODEL = 1024
N_LAYERS = 4
GROUP_W = 256

GLA_H, GLA_DK, GLA_DV, GLA_RANK, GLA_TAU = 4, 32, 64, 16, 16.0
SGU_G, SGU_CHUNK = 4, 128
SSM_H, SSM_P, SSM_G, SSM_N, SSM_CONV, SSM_CHUNK = 4, 64, 2, 64, 4, 128
DIL_H, DIL_DH, ROT_DIM, ROPE_THETA = 4, 64, 16, 500000.0
DIL_BRANCHES = ((128, 1), (512, 4), (2048, 16))

N_EXPERTS, TOP_K, N_EXPERT_GROUPS, TOPK_GROUPS, D_EXPERT = 128, 8, 8, 4, 256
ROUTED_SCALE = 1.0

ALPHA = (2 * N_LAYERS) ** 0.25
LN_EPS = 1e-5
RMS_EPS = 1e-6

W_GLA = 896
W_SGU = 512
W_SSD = 1280
W_DIL = 768
W_PROJ = W_GLA + W_SGU + W_SSD + W_DIL

BLK = 128
QBLK = 256
MOE_BLK = 512
TOK_BLK = 256
CMB_BLK = 128
NEG = -1e30

VMEM_LIMIT = 56 * 1024 * 1024


def _cparams(sem):
    return pltpu.CompilerParams(dimension_semantics=sem, vmem_limit_bytes=VMEM_LIMIT)


def _dot(a, b):
    return jnp.dot(a.astype(BF16), b.astype(BF16), preferred_element_type=F32)


def _dot_nt(a, b):
    return lax.dot_general(a.astype(BF16), b.astype(BF16), (((1,), (1,)), ((), ())),
                           preferred_element_type=F32)


def _dot_tn(a, b):
    return lax.dot_general(a.astype(BF16), b.astype(BF16), (((0,), (0,)), ((), ())),
                           preferred_element_type=F32)


def _split(a, n):
    out = []
    r = a
    for _ in range(n):
        p = r.astype(BF16)
        out.append(p)
        r = r - p.astype(F32)
    return out


def _dot_hi(a, b):
    a1, a2 = _split(a, 2)
    b1, b2 = _split(b, 2)
    return _dot(a1, b1) + (_dot(a1, b2) + _dot(a2, b1))


def _dot_nt_hi(a, b):
    a1, a2 = _split(a, 2)
    b1, b2 = _split(b, 2)
    return _dot_nt(a1, b1) + (_dot_nt(a1, b2) + _dot_nt(a2, b1))


def _dot_lhs01(m, b):
    b1, b2, b3 = _split(b, 3)
    return _dot(m, b1) + (_dot(m, b2) + _dot(m, b3))


def _dot_rhs01(a, m):
    a1, a2, a3 = _split(a, 3)
    return _dot(a1, m) + (_dot(a2, m) + _dot(a3, m))


def _iota(shape, dim):
    return lax.broadcasted_iota(I32, shape, dim)


def _sigmoid(x):
    return 1.0 / (1.0 + jnp.exp(-x))


def _silu(x):
    return x * _sigmoid(x)


def _softplus(x):
    return jnp.maximum(x, 0.0) + jnp.log1p(jnp.exp(-jnp.abs(x)))


def _gelu(x):
    return 0.5 * x * (1.0 + lax.erf(x * (2.0 ** -0.5)))


def _inproj_kernel(x_ref, w_ref, o_gla, o_sgu, o_ssd, o_dil):
    xb = x_ref[...].astype(BF16)
    off = 0
    for o in (o_gla, o_sgu, o_ssd, o_dil):
        n = o.shape[1]
        o[...] = jnp.dot(xb, w_ref[:, off:off + n], preferred_element_type=F32)
        off += n


def _inproj(x2, w_p):
    T = x2.shape[0]
    tm = 512
    widths = (W_GLA, W_SGU, W_SSD, W_DIL)
    return pl.pallas_call(
        _inproj_kernel,
        grid=(T // tm,),
        in_specs=[pl.BlockSpec((tm, D_MODEL), lambda i: (i, 0)),
                  pl.BlockSpec((D_MODEL, W_PROJ), lambda i: (0, 0))],
        out_specs=[pl.BlockSpec((tm, w), lambda i: (i, 0)) for w in widths],
        out_shape=[jax.ShapeDtypeStruct((T, w), F32) for w in widths],
        compiler_params=_cparams(("arbitrary",)),
        name="inproj",
    )(x2, w_p)


GLA_NB = 2
GLA_SUB = 16


def _gla_kernel(slab_ref, wg_ref, bg_ref, nw_ref, out_ref, st_ref, a_ref):
    S = slab_ref.shape[1]
    st_ref[...] = jnp.zeros_like(st_ref)
    row = _iota((BLK, BLK), 0)
    col = _iota((BLK, BLK), 1)
    tril = col <= row
    tril_bf = jnp.where(tril, 1.0, 0.0).astype(BF16)
    sub_row = _iota((GLA_SUB, BLK), 0)
    sub_col = _iota((GLA_SUB, BLK), 1)
    sub_head = sub_col >> 5
    v_head = _iota((BLK, GROUP_W), 1) >> 6
    st_diag = (_iota((GROUP_W, BLK), 0) >> 6) == (_iota((GROUP_W, BLK), 1) >> 5)
    seg = jnp.where((_iota((GROUP_W, GROUP_W), 0) >> 6) == (_iota((GROUP_W, GROUP_W), 1) >> 6),
                    1.0, 0.0).astype(BF16)
    wg = wg_ref[...]
    bg = bg_ref[...]
    nw = nw_ref[...]

    def block(b, r0):
        blk = slab_ref[b, pl.ds(r0, BLK), :]
        q = blk[:, 0:128] * (GLA_DK ** -0.5)
        k = blk[:, 128:256]
        v = blk[:, 256:512]
        r = blk[:, 512:768]
        lr = blk[:, 768:896]
        z = _dot_hi(lr, wg) + bg
        gk = (jnp.minimum(z, 0.0) - jnp.log1p(jnp.exp(-jnp.abs(z)))) * (1.0 / GLA_TAU)
        g = _dot_lhs01(tril_bf, gk)
        g_last = g[BLK - 1:BLK, :]
        q_in = q * jnp.exp(g)
        k_in = k * jnp.exp(g_last - g)
        vb = v.astype(BF16)
        o = _dot_nt(q_in, st_ref[b])
        for c in range(BLK // GLA_SUB):
            lo = c * GLA_SUB
            n_c = g[lo - 1:lo, :] if c else jnp.zeros((1, BLK), F32)
            qc = q[lo:lo + GLA_SUB, :] * jnp.exp(g[lo:lo + GLA_SUB, :] - n_c)
            kc = k * jnp.exp(jnp.where(row < lo + GLA_SUB, n_c - g, -jnp.inf))
            qs = jnp.concatenate([jnp.where(sub_head == h, qc, 0.0) for h in range(GLA_H)], axis=0)
            sc = _dot_nt(qs, kc)
            causal = sub_col <= sub_row + lo
            for h in range(GLA_H):
                a_ref[b, h, lo:lo + GLA_SUB, :] = jnp.where(causal, sc[h * GLA_SUB:(h + 1) * GLA_SUB, :], 0.0)
        for h in range(GLA_H):
            o = o + jnp.where(v_head == h, _dot(a_ref[b, h], vb), 0.0)
        st_ref[b] = st_ref[b] * jnp.exp(g_last) + jnp.where(st_diag, _dot_tn(vb, k_in), 0.0)
        ms = _dot_rhs01(o * o, seg) * (1.0 / GLA_DV)
        y = o * lax.rsqrt(ms + RMS_EPS) * nw * _silu(r)
        out_ref[b, pl.ds(r0, BLK), :] = y.astype(out_ref.dtype)

    def body(n, carry):
        r0 = pl.multiple_of(n * BLK, BLK)
        for b in range(slab_ref.shape[0]):
            block(b, r0)
        return carry

    lax.fori_loop(0, S // BLK, body, 0)


def _gla(slab, wg_p, bg, nw_t):
    B, S, _ = slab.shape
    nb = GLA_NB if B % GLA_NB == 0 else 1
    return pl.pallas_call(
        _gla_kernel,
        grid=(B // nb,),
        in_specs=[pl.BlockSpec((nb, S, W_GLA), lambda b: (b, 0, 0)),
                  pl.BlockSpec((128, 128), lambda b: (0, 0)),
                  pl.BlockSpec((1, 128), lambda b: (0, 0)),
                  pl.BlockSpec((1, GROUP_W), lambda b: (0, 0))],
        out_specs=pl.BlockSpec((nb, S, GROUP_W), lambda b: (b, 0, 0)),
        out_shape=jax.ShapeDtypeStruct((B, S, GROUP_W), BF16),
        scratch_shapes=[pltpu.VMEM((nb, GROUP_W, 128), F32), pltpu.VMEM((nb, GLA_H, BLK, BLK), F32)],
        compiler_params=_cparams(("arbitrary",)),
        name="gla",
    )(slab, wg_p, bg, nw_t)


def _sgu_kernel(slab_ref, w_ref, bias_ref, g_ref, b_ref, out_ref):
    S = slab_ref.shape[1]
    wmask = (_iota((BLK, SGU_G * BLK), 1) & (BLK - 1)) <= _iota((BLK, SGU_G * BLK), 0)
    w = jnp.where(wmask, w_ref[...], 0.0).astype(BF16)
    lane_grp = _iota((BLK, GROUP_W), 1) >> 6
    bias = bias_ref[...]
    ln_g = g_ref[...]
    ln_b = b_ref[...]

    def body(n, carry):
        r0 = pl.multiple_of(n * BLK, BLK)
        blk = slab_ref[0, pl.ds(r0, BLK), :]
        u = _gelu(blk[:, 0:256])
        v = _gelu(blk[:, 256:512])
        mu = jnp.mean(v, axis=-1, keepdims=True)
        var = jnp.mean(jnp.square(v - mu), axis=-1, keepdims=True)
        v = (v - mu) * lax.rsqrt(var + LN_EPS) * ln_g + ln_b
        vexp = jnp.concatenate([jnp.where(lane_grp == g, v, 0.0) for g in range(SGU_G)], axis=0)
        s = _dot(w, vexp) + bias
        out_ref[0, pl.ds(r0, BLK), :] = (u * s).astype(out_ref.dtype)
        return carry

    lax.fori_loop(0, S // BLK, body, 0)


def _sgu(slab, w_cat, bias_t, ln_g, ln_b):
    B, S, _ = slab.shape
    return pl.pallas_call(
        _sgu_kernel,
        grid=(B,),
        in_specs=[pl.BlockSpec((1, S, W_SGU), lambda b: (b, 0, 0)),
                  pl.BlockSpec((BLK, SGU_G * BLK), lambda b: (0, 0)),
                  pl.BlockSpec((BLK, GROUP_W), lambda b: (0, 0)),
                  pl.BlockSpec((1, GROUP_W), lambda b: (0, 0)),
                  pl.BlockSpec((1, GROUP_W), lambda b: (0, 0))],
        out_specs=pl.BlockSpec((1, S, GROUP_W), lambda b: (b, 0, 0)),
        out_shape=jax.ShapeDtypeStruct((B, S, GROUP_W), BF16),
        compiler_params=_cparams(("arbitrary",)),
        name="sgu",
    )(slab, w_cat, bias_t, ln_g, ln_b)


def _ssd_kernel(slab_ref, cw_ref, cb_ref, dtb_ref, a_ref, d_ref, nw_ref, out_ref, xpad_ref, st_ref):
    S = slab_ref.shape[1]
    CW = 3 * GROUP_W
    st_ref[...] = jnp.zeros_like(st_ref)
    xpad_ref[0:8, :] = jnp.zeros((8, CW), F32)
    xpad_ref[8:S + 8, :] = slab_ref[0, :, 256:256 + CW]
    row = _iota((BLK, BLK), 0)
    col = _iota((BLK, BLK), 1)
    tril = col <= row
    tril_bf = jnp.where(tril, 1.0, 0.0).astype(BF16)
    lane_head = _iota((BLK, GROUP_W), 1) >> 6
    st_diag = (_iota((GROUP_W, GROUP_W), 0) >> 6) == (_iota((GROUP_W, GROUP_W), 1) >> 6)
    cw = cw_ref[...]
    cb = cb_ref[...]
    dtb = dtb_ref[...]
    a_neg = -jnp.exp(a_ref[...])
    d_skip = d_ref[...]
    nw = nw_ref[...]

    def body(n, carry):
        r0 = pl.multiple_of(n * BLK, BLK)
        xw = xpad_ref[pl.ds(r0, BLK + 8), :]
        conv = cb + cw[0:1, :] * xw[5:5 + BLK, :]
        for w in range(1, SSM_CONV):
            conv = conv + cw[w:w + 1, :] * xw[5 + w:5 + w + BLK, :]
        xbc = _silu(conv)
        xs = xbc[:, 0:256]
        b_e = xbc[:, 256:512]
        c_e = xbc[:, 512:768]
        z = slab_ref[0, pl.ds(r0, BLK), 0:256]
        dt = _softplus(slab_ref[0, pl.ds(r0, BLK), 1024:1280] + dtb)
        acs = _dot_lhs01(tril_bf, dt * a_neg)
        acs_last = acs[BLK - 1:BLK, :]
        x_dt = (xs * dt).astype(BF16)
        acs_t = [jnp.transpose(acs[:, 0:128]), jnp.transpose(acs[:, 128:256])]
        y = _dot(c_e * jnp.exp(acs), st_ref[...])
        b_bf = b_e.astype(BF16)
        for h in range(SSM_H):
            a_col = jnp.broadcast_to(acs[:, 64 * h:64 * h + 1], (BLK, BLK))
            a_row = jnp.broadcast_to(acs_t[h // 2][64 * (h % 2):64 * (h % 2) + 1, :], (BLK, BLK))
            lm = jnp.exp(jnp.where(tril, a_col - a_row, -jnp.inf))
            ch = jnp.where(lane_head == h, c_e, 0.0)
            m = _dot_nt(ch, b_bf) * lm
            y = y + jnp.where(lane_head == h, _dot(m, x_dt), 0.0)
        new = _dot_tn(b_e * jnp.exp(acs_last - acs), x_dt)
        st_ref[...] = st_ref[...] * jnp.exp(acs_last) + jnp.where(st_diag, new, 0.0)
        y = (y + d_skip * xs) * _silu(z)
        parts = []
        for g in range(SSM_G):
            yg = y[:, 128 * g:128 * (g + 1)]
            ms = jnp.mean(yg * yg, axis=-1, keepdims=True)
            parts.append(yg * lax.rsqrt(ms + RMS_EPS))
        out = jnp.concatenate(parts, axis=1) * nw
        out_ref[0, pl.ds(r0, BLK), :] = out.astype(out_ref.dtype)
        return carry

    lax.fori_loop(0, S // BLK, body, 0)


def _ssd(slab, cw_e, cb_e, dtb_e, a_e, d_e, nw):
    B, S, _ = slab.shape
    CW = 3 * GROUP_W
    vec = lambda w: pl.BlockSpec((1, w), lambda b: (0, 0))
    return pl.pallas_call(
        _ssd_kernel,
        grid=(B,),
        in_specs=[pl.BlockSpec((1, S, W_SSD), lambda b: (b, 0, 0)),
                  pl.BlockSpec((SSM_CONV, CW), lambda b: (0, 0)),
                  vec(CW), vec(GROUP_W), vec(GROUP_W), vec(GROUP_W), vec(GROUP_W)],
        out_specs=pl.BlockSpec((1, S, GROUP_W), lambda b: (b, 0, 0)),
        out_shape=jax.ShapeDtypeStruct((B, S, GROUP_W), BF16),
        scratch_shapes=[pltpu.VMEM((S + 8, CW), F32), pltpu.VMEM((GROUP_W, GROUP_W), F32)],
        compiler_params=_cparams(("arbitrary",)),
        name="ssd",
    )(slab, cw_e, cb_e, dtb_e, a_e, d_e, nw)


def _branch_log_multiplicity(S):
    nq = S // QBLK
    r = np.arange(QBLK)[:, None]
    u = np.arange(S)[None, :]
    d = (nq - 1) * QBLK + r - u
    c = np.zeros(d.shape, np.float64)
    for window, dil in DIL_BRANCHES:
        c += (d >= 0) & (d % dil == 0) & (d <= window)
    with np.errstate(divide="ignore"):
        return np.where(c > 0, np.log(np.maximum(c, 1.0)), NEG).astype(np.float32)


def _dil_kernel(slab_ref, cos_ref, sin_ref, tb_ref, out_ref, q_ref, k_ref, v_ref):
    S = slab_ref.shape[1]
    nq = S // QBLK
    lane_d = _iota((1, 128), 1) & (DIL_DH - 1)

    def rope(t):
        fwd = pltpu.roll(t, 128 - ROT_DIM // 2, 1)
        bwd = pltpu.roll(t, ROT_DIM // 2, 1)
        return t * cos_ref[0] + jnp.where(lane_d < ROT_DIM // 2, fwd, bwd) * sin_ref[0]

    for c in range(0, GROUP_W, 128):
        q_ref[:, c:c + 128] = (rope(slab_ref[0, :, c:c + 128]) * (DIL_DH ** -0.5)).astype(BF16)
        k_ref[:, c:c + 128] = rope(slab_ref[0, :, 256 + c:384 + c]).astype(BF16)
    v_ref[...] = slab_ref[0, :, 512:768].astype(BF16)
    lane_head = _iota((QBLK, GROUP_W), 1) >> 6
    for i in range(nq):
        nk = (i + 1) * QBLK
        qi = q_ref[i * QBLK:(i + 1) * QBLK, :]
        acc = jnp.zeros((QBLK, GROUP_W), F32)
        for h in range(DIL_H):
            qh = jnp.where(lane_head == h, qi, jnp.zeros_like(qi))
            s = _dot_nt(qh, k_ref[0:nk, :]) + tb_ref[:, (nq - 1 - i) * QBLK:]
            m = jnp.max(s, axis=-1, keepdims=True)
            p = jnp.exp(s - m)
            den = jnp.sum(p, axis=-1, keepdims=True)
            o = _dot(p, v_ref[0:nk, :])
            acc = acc + jnp.where(lane_head == h, o / den, 0.0)
        out_ref[0, i * QBLK:(i + 1) * QBLK, :] = acc.astype(out_ref.dtype)


def _dil(slab, cos_t, sin_t, tb):
    B, S, _ = slab.shape
    return pl.pallas_call(
        _dil_kernel,
        grid=(B,),
        in_specs=[pl.BlockSpec((1, S, W_DIL), lambda b: (b, 0, 0)),
                  pl.BlockSpec((1, S, 128), lambda b: (b, 0, 0)),
                  pl.BlockSpec((1, S, 128), lambda b: (b, 0, 0)),
                  pl.BlockSpec((QBLK, S), lambda b: (0, 0), pipeline_mode=pl.Buffered(1))],
        out_specs=pl.BlockSpec((1, S, GROUP_W), lambda b: (b, 0, 0)),
        out_shape=jax.ShapeDtypeStruct((B, S, GROUP_W), BF16),
        scratch_shapes=[pltpu.VMEM((S, GROUP_W), BF16)] * 3,
        compiler_params=_cparams(("arbitrary",)),
        name="dil",
    )(slab, cos_t, sin_t, tb)


def _layernorm(x, g, b):
    mu = jnp.mean(x, axis=-1, keepdims=True)
    var = jnp.mean(jnp.square(x - mu), axis=-1, keepdims=True)
    return (x - mu) * lax.rsqrt(var + LN_EPS) * g + b


def _outproj_kernel(ya, yb, yc, yd, x_ref, w_ref, g_ref, b_ref, o_ref):
    y = jnp.concatenate([ya[...], yb[...], yc[...], yd[...]], axis=1)
    y = jnp.dot(y, w_ref[...], preferred_element_type=F32)
    o_ref[...] = _layernorm(ALPHA * x_ref[...] + y, g_ref[...], b_ref[...])


def _outproj(ya, yb, yc, yd, x2, w_out, g, b):
    T = x2.shape[0]
    tm = 512
    ysp = pl.BlockSpec((tm, GROUP_W), lambda i: (i, 0))
    vec = pl.BlockSpec((1, D_MODEL), lambda i: (0, 0))
    return pl.pallas_call(
        _outproj_kernel,
        grid=(T // tm,),
        in_specs=[ysp, ysp, ysp, ysp,
                  pl.BlockSpec((tm, D_MODEL), lambda i: (i, 0)),
                  pl.BlockSpec((D_MODEL, D_MODEL), lambda i: (0, 0)), vec, vec],
        out_specs=pl.BlockSpec((tm, D_MODEL), lambda i: (i, 0)),
        out_shape=jax.ShapeDtypeStruct((T, D_MODEL), F32),
        compiler_params=_cparams(("arbitrary",)),
        name="outproj_ln",
    )(ya, yb, yc, yd, x2, w_out, g, b)


def _router_kernel(x_ref, rw_ref, rb_ref, idx_ref, gate_ref, pos_ref, cnt_ref, carry_ref):
    TB = x_ref.shape[0]
    E = N_EXPERTS
    per = E // N_EXPERT_GROUPS

    @pl.when(pl.program_id(0) == 0)
    def _():
        carry_ref[...] = jnp.zeros_like(carry_ref)

    logits = _dot_nt_hi(rw_ref[...], x_ref[...])
    scores = _sigmoid(logits)
    choice = scores + rb_ref[...]
    e_iota = _iota((E, TB), 0)

    l_iota = _iota((per, TB), 0)
    grp = []
    for g in range(N_EXPERT_GROUPS):
        cg = choice[per * g:per * (g + 1), :]
        m1 = jnp.max(cg, axis=0, keepdims=True)
        i1 = jnp.min(jnp.where(cg == m1, l_iota, per), axis=0, keepdims=True)
        m2 = jnp.max(jnp.where(l_iota == i1, -jnp.inf, cg), axis=0, keepdims=True)
        grp.append(m1 + m2)
    gs = jnp.concatenate(grp, axis=0)
    g_iota = _iota((N_EXPERT_GROUPS, TB), 0)
    keep = jnp.zeros((N_EXPERT_GROUPS, TB), jnp.bool_)
    for _ in range(TOPK_GROUPS):
        m = jnp.max(gs, axis=0, keepdims=True)
        gi = jnp.min(jnp.where(gs == m, g_iota, N_EXPERT_GROUPS), axis=0, keepdims=True)
        hit = g_iota == gi
        keep = keep | hit
        gs = jnp.where(hit, -jnp.inf, gs)
    keep_f = jnp.where(keep, 1.0, 0.0)
    keep_e = jnp.concatenate([jnp.broadcast_to(keep_f[g:g + 1, :], (per, TB))
                              for g in range(N_EXPERT_GROUPS)], axis=0)
    cm = jnp.where(keep_e > 0.5, choice, -jnp.inf)

    hits, idxs, gates = [], [], []
    onehot = jnp.zeros((E, TB), F32)
    for _ in range(TOP_K):
        m = jnp.max(cm, axis=0, keepdims=True)
        ei = jnp.min(jnp.where(cm == m, e_iota, E), axis=0, keepdims=True)
        hit = e_iota == ei
        hits.append(hit)
        idxs.append(ei)
        gates.append(jnp.sum(jnp.where(hit, scores, 0.0), axis=0, keepdims=True))
        onehot = onehot + jnp.where(hit, 1.0, 0.0)
        cm = jnp.where(hit, -jnp.inf, cm)
    gate = jnp.concatenate(gates, axis=0)
    gate = gate / jnp.sum(gate, axis=0, keepdims=True) * ROUTED_SCALE

    before = jnp.where(_iota((TB, TB), 0) < _iota((TB, TB), 1), 1.0, 0.0).astype(BF16)
    cnt = carry_ref[...] + _dot(onehot, before)
    pos = jnp.concatenate([jnp.sum(jnp.where(h, cnt, 0.0), axis=0, keepdims=True) for h in hits], axis=0)
    carry_ref[...] = carry_ref[...] + jnp.sum(onehot, axis=1, keepdims=True)

    idx_ref[...] = jnp.concatenate(idxs, axis=0)
    gate_ref[...] = gate
    pos_ref[...] = pos.astype(I32)
    cnt_ref[...] = jnp.broadcast_to(carry_ref[...], cnt_ref.shape)


def _router(x1, rw_t, rb):
    T = x1.shape[0]
    TB = TOK_BLK
    kt = pl.BlockSpec((TOP_K, TB), lambda i: (0, i))
    return pl.pallas_call(
        _router_kernel,
        grid=(T // TB,),
        in_specs=[pl.BlockSpec((TB, D_MODEL), lambda i: (i, 0)),
                  pl.BlockSpec((N_EXPERTS, D_MODEL), lambda i: (0, 0)),
                  pl.BlockSpec((N_EXPERTS, 1), lambda i: (0, 0))],
        out_specs=[kt, kt, kt, pl.BlockSpec((N_EXPERTS, 128), lambda i: (0, 0))],
        out_shape=[jax.ShapeDtypeStruct((TOP_K, T), I32), jax.ShapeDtypeStruct((TOP_K, T), F32),
                   jax.ShapeDtypeStruct((TOP_K, T), I32), jax.ShapeDtypeStruct((N_EXPERTS, 128), F32)],
        scratch_shapes=[pltpu.VMEM((N_EXPERTS, 1), F32)],
        compiler_params=_cparams(("arbitrary",)),
        name="router",
    )(x1, rw_t, rb)


def _slots_kernel(ps_ref, idx_ref, pos_ref, dest_ref):
    idx = idx_ref[...]

    def body(e, acc):
        return acc + jnp.where(idx == e, ps_ref[e], 0)

    dest_ref[...] = lax.fori_loop(0, N_EXPERTS, body, pos_ref[...], unroll=8)


def _slots(pad_start, idx, pos):
    T = idx.shape[1]
    tb = min(T, 2048)
    kt = pl.BlockSpec((TOP_K, tb), lambda i, ps: (0, i))
    return pl.pallas_call(
        _slots_kernel,
        grid_spec=pltpu.PrefetchScalarGridSpec(
            num_scalar_prefetch=1, grid=(T // tb,), in_specs=[kt, kt], out_specs=kt),
        out_shape=jax.ShapeDtypeStruct((TOP_K, T), I32),
        compiler_params=_cparams(("arbitrary",)),
        name="slots",
    )(pad_start, idx, pos)


ROW_T = D_MODEL // 2 // 128
U32 = jnp.uint32


def _pack_rows(x):
    bits = lax.bitcast_convert_type(x.astype(BF16).astype(F32), U32)
    half = D_MODEL // 2
    return (bits[:, :half] >> 16) | (bits[:, half:] & U32(0xFFFF0000))


def _unpack_rows(w):
    lo = lax.bitcast_convert_type(w << 16, F32)
    hi = lax.bitcast_convert_type(w & U32(0xFFFF0000), F32)
    return jnp.concatenate([lo, hi], axis=1)


def _to_tiles(dst_ref, base, w):
    n = w.shape[0]
    for s in range(ROW_T):
        dst_ref[pl.ds(base + s, n, stride=ROW_T), :] = w[:, 128 * s:128 * (s + 1)]


def _from_tiles(src_ref, base, n):
    return jnp.concatenate([src_ref[pl.ds(base + s, n, stride=ROW_T), :] for s in range(ROW_T)], axis=1)


def _tailzero_kernel(tb_ref, xs_ref):
    del tb_ref
    xs_ref[...] = jnp.zeros_like(xs_ref)


def _tailzero(tail_blk, nblk):
    rows = MOE_BLK * ROW_T
    return pl.pallas_call(
        _tailzero_kernel,
        grid_spec=pltpu.PrefetchScalarGridSpec(
            num_scalar_prefetch=1, grid=(N_EXPERTS,), in_specs=[],
            out_specs=pl.BlockSpec((rows, 128), lambda e, tb: (tb[e], 0))),
        out_shape=jax.ShapeDtypeStruct((nblk * rows, 128), U32),
        compiler_params=_cparams(("arbitrary",)),
        name="tailzero",
    )(tail_blk)


def _dispatch_kernel(dest_ref, x_ref, xs_in, xs_out, xt, sem):
    del xs_in
    TB = x_ref.shape[0]
    _to_tiles(xt, 0, _pack_rows(x_ref[...]))

    def issue(t, c):
        src = xt.at[pl.ds(pl.multiple_of(t * ROW_T, ROW_T), ROW_T), :]
        for k in range(TOP_K):
            d = dest_ref[t * TOP_K + k]
            pltpu.make_async_copy(src, xs_out.at[pl.ds(pl.multiple_of(d * ROW_T, ROW_T), ROW_T), :], sem).start()
        return c

    lax.fori_loop(0, TB, issue, 0, unroll=2)

    for k in range(TOP_K):
        pltpu.make_async_copy(xt, xs_out.at[pl.ds(0, TB * ROW_T), :], sem).wait()


def _dispatch(dest_flat, x1, xs0):
    T = x1.shape[0]
    TB = TOK_BLK
    return pl.pallas_call(
        _dispatch_kernel,
        grid=(T // TB,),
        in_specs=[pl.BlockSpec((TB * TOP_K,), lambda i: (i,), memory_space=pltpu.SMEM),
                  pl.BlockSpec((TB, D_MODEL), lambda i: (i, 0)),
                  pl.BlockSpec(memory_space=pl.ANY)],
        out_specs=pl.BlockSpec(memory_space=pl.ANY),
        out_shape=jax.ShapeDtypeStruct(xs0.shape, xs0.dtype),
        scratch_shapes=[pltpu.VMEM((TB * ROW_T, 128), U32), pltpu.SemaphoreType.DMA],
        input_output_aliases={2: 0},
        compiler_params=_cparams(("arbitrary",)),
        name="dispatch",
    )(dest_flat, x1, xs0)


def _experts_kernel(be_ref, nu_ref, xs_ref, wg_ref, wu_ref, wd_ref, ys_ref, wgu_s, wd_s):
    i = pl.program_id(0)

    @pl.when(i < nu_ref[0])
    def _():
        prev = be_ref[jnp.maximum(i - 1, 0)]

        @pl.when((i == 0) | (be_ref[i] != prev))
        def _():
            wgu_s[:, 0:D_EXPERT] = wg_ref[0, 0].astype(BF16)
            wgu_s[:, D_EXPERT:] = wu_ref[0, 0].astype(BF16)
            wd_s[...] = wd_ref[0, 0].astype(BF16)

        x = _unpack_rows(_from_tiles(xs_ref, 0, MOE_BLK)).astype(BF16)
        h = jnp.dot(x, wgu_s[...], preferred_element_type=F32)
        a = _silu(h[:, 0:D_EXPERT]) * h[:, D_EXPERT:]
        _to_tiles(ys_ref, 0, _pack_rows(jnp.dot(a.astype(BF16), wd_s[...], preferred_element_type=F32)))


def _experts(layer, blk_expert, n_used, xs, wg, wu, wd):
    rows = MOE_BLK * ROW_T
    nblk = xs.shape[0] // rows
    row = lambda i, be, nu: (jnp.minimum(i, nu[0] - 1), 0)
    wsel = lambda i, be, nu: (layer, be[i], 0, 0)
    return pl.pallas_call(
        _experts_kernel,
        grid_spec=pltpu.PrefetchScalarGridSpec(
            num_scalar_prefetch=2,
            grid=(nblk,),
            in_specs=[pl.BlockSpec((rows, 128), row),
                      pl.BlockSpec((1, 1, D_MODEL, D_EXPERT), wsel),
                      pl.BlockSpec((1, 1, D_MODEL, D_EXPERT), wsel),
                      pl.BlockSpec((1, 1, D_EXPERT, D_MODEL), wsel)],
            out_specs=pl.BlockSpec((rows, 128), row),
            scratch_shapes=[pltpu.VMEM((D_MODEL, 2 * D_EXPERT), BF16),
                            pltpu.VMEM((D_EXPERT, D_MODEL), BF16)]),
        out_shape=jax.ShapeDtypeStruct(xs.shape, U32),
        compiler_params=_cparams(("arbitrary",)),
        name="experts",
    )(blk_expert, n_used, xs, wg, wu, wd)


CMB_CHUNK = 8


def _combine_kernel(dest_ref, dnext_ref, gate_ref, x_ref, ys_ref, sgu_ref, sd_ref, g_ref, b_ref, o_ref,
                    buf, ysh, sem):
    TB = x_ref.shape[0]
    i = pl.program_id(0)
    last = i == pl.num_programs(0) - 1
    slot = i & 1
    nslot = 1 - slot
    rows = TOP_K * TB * ROW_T

    def start_rows(dref, to_slot, t):
        for k in range(TOP_K):
            d = dref[t * TOP_K + k]
            pltpu.make_async_copy(ys_ref.at[pl.ds(pl.multiple_of(d * ROW_T, ROW_T), ROW_T), :],
                                  buf.at[to_slot, pl.ds(pl.multiple_of((k * TB + t) * ROW_T, ROW_T), ROW_T), :],
                                  sem.at[to_slot]).start()

    @pl.when(i == 0)
    def _():
        def first(t, c):
            start_rows(dest_ref, 0, t)
            return c

        lax.fori_loop(0, TB, first, 0, unroll=2)

    x = x_ref[...]
    h = jnp.dot(x.astype(BF16), sgu_ref[...], preferred_element_type=F32)
    a = _silu(h[:, 0:D_EXPERT]) * h[:, D_EXPERT:]
    ysh[...] = ALPHA * x + jnp.dot(a.astype(BF16), sd_ref[...], preferred_element_type=F32)

    pltpu.make_async_copy(ys_ref.at[pl.ds(0, rows), :], buf.at[slot], sem.at[slot]).wait()

    ln_g = g_ref[...]
    ln_b = b_ref[...]

    def chunk(c, carry):
        for u in range(CMB_CHUNK):
            start_rows(dnext_ref, nslot, c * CMB_CHUNK + u)
        r0 = pl.multiple_of(c * CMB_CHUNK, CMB_CHUNK)
        y = ysh[pl.ds(r0, CMB_CHUNK), :]
        gate = gate_ref[pl.ds(r0, CMB_CHUNK), :]
        for k in range(TOP_K):
            w = _from_tiles(buf.at[slot], (k * TB + r0) * ROW_T, CMB_CHUNK)
            y = y + gate[:, k:k + 1] * _unpack_rows(w)
        o_ref[pl.ds(r0, CMB_CHUNK), :] = _layernorm(y, ln_g, ln_b)
        return carry

    lax.fori_loop(0, TB // CMB_CHUNK, chunk, 0)

    @pl.when(last)
    def _():
        pltpu.make_async_copy(ys_ref.at[pl.ds(0, rows), :], buf.at[nslot], sem.at[nslot]).wait()


def _combine(dest_flat, gate_t, x1, ys, sh_gu, sh_d, g, b):
    T = x1.shape[0]
    TB = CMB_BLK
    n = T // TB
    vec = pl.BlockSpec((1, D_MODEL), lambda i: (0, 0))
    return pl.pallas_call(
        _combine_kernel,
        grid=(n,),
        in_specs=[pl.BlockSpec((TB * TOP_K,), lambda i: (i,), memory_space=pltpu.SMEM),
                  pl.BlockSpec((TB * TOP_K,), lambda i: (jnp.minimum(i + 1, n - 1),), memory_space=pltpu.SMEM),
                  pl.BlockSpec((TB, TOP_K), lambda i: (i, 0)),
                  pl.BlockSpec((TB, D_MODEL), lambda i: (i, 0)),
                  pl.BlockSpec(memory_space=pl.ANY),
                  pl.BlockSpec((D_MODEL, 2 * D_EXPERT), lambda i: (0, 0)),
                  pl.BlockSpec((D_EXPERT, D_MODEL), lambda i: (0, 0)), vec, vec],
        out_specs=pl.BlockSpec((TB, D_MODEL), lambda i: (i, 0)),
        out_shape=jax.ShapeDtypeStruct((T, D_MODEL), F32),
        scratch_shapes=[pltpu.VMEM((2, TOP_K * TB * ROW_T, 128), U32), pltpu.VMEM((TB, D_MODEL), F32),
                        pltpu.SemaphoreType.DMA((2,))],
        compiler_params=_cparams(("arbitrary",)),
        name="combine_ln",
    )(dest_flat, dest_flat, gate_t, x1, ys, sh_gu, sh_d, g, b)


def _token_mixer(x2, B, S, w_p, gla_p, sgu_p, ssd_p, cos_t, sin_t, tb, w_out, ln_g, ln_b):
    s_gla, s_sgu, s_ssd, s_dil = _inproj(x2, w_p)
    ya = _gla(s_gla.reshape(B, S, W_GLA), *gla_p)
    yb = _sgu(s_sgu.reshape(B, S, W_SGU), *sgu_p)
    yc = _ssd(s_ssd.reshape(B, S, W_SSD), *ssd_p)
    yd = _dil(s_dil.reshape(B, S, W_DIL), cos_t, sin_t, tb)
    T = B * S
    flat = lambda y: y.reshape(T, GROUP_W)
    return _outproj(flat(ya), flat(yb), flat(yc), flat(yd), x2, w_out, ln_g, ln_b)


def _moe(layer, x1, rw_t, rb, wg, wu, wd, sh_gu, sh_d, ln_g, ln_b):
    T = x1.shape[0]
    idx, gate, pos, cnt = _router(x1, rw_t, rb)
    counts = cnt[:, 0].astype(I32)
    padded = (counts + MOE_BLK - 1) // MOE_BLK * MOE_BLK
    pad_end = jnp.cumsum(padded)
    pad_start = pad_end - padded
    dest = _slots(pad_start.astype(I32), idx, pos)
    dest_flat = dest.T.reshape(T * TOP_K)
    nblk = T * TOP_K // MOE_BLK + N_EXPERTS
    n_used = (pad_end[-1] // MOE_BLK).astype(I32)
    blk = jnp.minimum(jnp.arange(nblk, dtype=I32), n_used - 1) * MOE_BLK
    blk_expert = jnp.sum((pad_end[None, :] <= blk[:, None]).astype(I32), axis=1)
    blk_expert = jnp.minimum(blk_expert, N_EXPERTS - 1)
    tail_blk = jnp.maximum(pad_end // MOE_BLK - 1, 0).astype(I32)
    xs = _dispatch(dest_flat, x1, _tailzero(tail_blk, nblk))
    ys = _experts(layer, blk_expert, n_used.reshape(1), xs, wg, wu, wd)
    return _combine(dest_flat, gate.T, x1, ys, sh_gu, sh_d, ln_g, ln_b)


def _prep_w_in(w_in):
    o = np.cumsum((0, 128, 128, 256, 256, 16, 256, 256, 256, 512, 4, 256, 256, 256))
    c = lambda i: w_in[..., o[i]:o[i + 1]]
    xbc = c(8)
    x_cols = xbc[..., 0:256]
    per_head = lambda m: jnp.concatenate(
        [m[..., SSM_N * (h // (SSM_H // SSM_G)):SSM_N * (h // (SSM_H // SSM_G) + 1)] for h in range(SSM_H)], axis=-1)
    b_cols = per_head(xbc[..., 256:384])
    c_cols = per_head(xbc[..., 384:512])
    dt_cols = jnp.repeat(c(9), SSM_P, axis=-1)
    pad = jnp.zeros(w_in.shape[:-1] + (W_GLA - o[5],), w_in.dtype)
    cols = [c(0), c(1), c(2), c(3), c(4), pad, c(5), c(6), c(7), x_cols, b_cols, c_cols, dt_cols,
            c(10), c(11), c(12)]
    return jnp.concatenate(cols, axis=-1).astype(BF16)


def _per_head(v):
    rep = SSM_H // SSM_G
    xp = v[..., 0:256]
    pick = lambda m: jnp.concatenate([m[..., SSM_N * (h // rep):SSM_N * (h // rep + 1)] for h in range(SSM_H)], axis=-1)
    return jnp.concatenate([xp, pick(v[..., 256:384]), pick(v[..., 384:512])], axis=-1)


def kernel(x, positions, w_in, gla_w_gate, gla_b_gate, gla_norm_w, sgu_ln_g, sgu_ln_b, sgu_w, sgu_b, ssm_conv_w, ssm_conv_b, ssm_dt_bias, ssm_a_log, ssm_d, ssm_norm_w, w_out, ln1_g, ln1_b, router_w, router_bias, exp_w_gate, exp_w_up, exp_w_down, sh_w_gate, sh_w_up, sh_w_down, ln2_g, ln2_b):
    B, S, _ = x.shape
    L = w_in.shape[0]
    T = B * S

    inv_freq = ROPE_THETA ** (-jnp.arange(0, ROT_DIM, 2, dtype=F32) / ROT_DIM)
    ang = positions.astype(F32)[..., None] * inv_freq
    cos, sin = jnp.cos(ang), jnp.sin(ang)
    ones = jnp.ones((B, S, DIL_DH - ROT_DIM), F32)
    cos_t = jnp.tile(jnp.concatenate([cos, cos, ones], axis=-1), (1, 1, 2))
    sin_t = jnp.tile(jnp.concatenate([-sin, sin, 0.0 * ones], axis=-1), (1, 1, 2))
    tb = jnp.asarray(_branch_log_multiplicity(S))

    w_p = _prep_w_in(w_in)
    wg_p = jnp.pad(gla_w_gate, ((0, 0), (0, 128 - GLA_RANK), (0, 0)))
    row = lambda v: v.reshape(L, 1, -1)
    gla_nw = jnp.tile(gla_norm_w, (1, GLA_H))
    sgu_wc = jnp.transpose(sgu_w, (0, 2, 1, 3)).reshape(L, SGU_CHUNK, SGU_G * SGU_CHUNK)
    sgu_bt = jnp.repeat(jnp.transpose(sgu_b, (0, 2, 1)), GROUP_W // SGU_G, axis=-1)
    exp64 = lambda v: jnp.repeat(v, SSM_P, axis=-1)
    w_out_b = w_out.astype(BF16)
    rw_t = jnp.transpose(router_w, (0, 2, 1))
    sh_gu = jnp.concatenate([sh_w_gate, sh_w_up], axis=-1).astype(BF16)
    sh_d = sh_w_down.astype(BF16)

    x2 = x.reshape(T, D_MODEL)
    for l in range(L):
        gla_p = (wg_p[l], row(gla_b_gate)[l], row(gla_nw)[l])
        sgu_p = (sgu_wc[l], sgu_bt[l], row(sgu_ln_g)[l], row(sgu_ln_b)[l])
        ssd_p = (_per_head(ssm_conv_w[l]), row(_per_head(ssm_conv_b))[l], row(exp64(ssm_dt_bias))[l],
                 row(exp64(ssm_a_log))[l], row(exp64(ssm_d))[l], row(ssm_norm_w)[l])
        x1 = _token_mixer(x2, B, S, w_p[l], gla_p, sgu_p, ssd_p, cos_t, sin_t, tb,
                          w_out_b[l], row(ln1_g)[l], row(ln1_b)[l])
        x2 = _moe(l, x1, rw_t[l], router_bias[l].reshape(N_EXPERTS, 1), exp_w_gate, exp_w_up, exp_w_down,
                  sh_gu[l], sh_d[l], row(ln2_g)[l], row(ln2_b)[l])
    return x2.reshape(B, S, D_MODEL)
```

```python
import functools
import math

import numpy as np
import jax
import jax.numpy as jnp
from jax import lax
from jax.experimental import pallas as pl
from jax.experimental.pallas import tpu as pltpu

F32 = jnp.float32
BF16 = jnp.bfloat16
I32 = jnp.int32

D_MODEL = 1024
N_LAYERS = 4
GROUP_W = 256

GLA_H, GLA_DK, GLA_DV, GLA_RANK, GLA_TAU = 4, 32, 64, 16, 16.0
SGU_G, SGU_CHUNK = 4, 128
SSM_H, SSM_P, SSM_G, SSM_N, SSM_CONV, SSM_CHUNK = 4, 64, 2, 64, 4, 128
DIL_H, DIL_DH, ROT_DIM, ROPE_THETA = 4, 64, 16, 500000.0
DIL_BRANCHES = ((128, 1), (512, 4), (2048, 16))

N_EXPERTS, TOP_K, N_EXPERT_GROUPS, TOPK_GROUPS, D_EXPERT = 128, 8, 8, 4, 256
ROUTED_SCALE = 1.0

ALPHA = (2 * N_LAYERS) ** 0.25
LN_EPS = 1e-5
RMS_EPS = 1e-6

W_GLA = 896
W_SGU = 512
W_SSD = 1280
W_DIL = 768
W_PROJ = W_GLA + W_SGU + W_SSD + W_DIL

BLK = 128
QBLK = 256
MOE_BLK = 512
TOK_BLK = 256
DSP_BLK = 512
CMB_BLK = 256
NEG = -1e30

VMEM_LIMIT = 56 * 1024 * 1024


def _cparams(sem):
    return pltpu.CompilerParams(dimension_semantics=sem, vmem_limit_bytes=VMEM_LIMIT)


def _dot(a, b):
    return jnp.dot(a.astype(BF16), b.astype(BF16), preferred_element_type=F32)


def _dot_nt(a, b):
    return lax.dot_general(a.astype(BF16), b.astype(BF16), (((1,), (1,)), ((), ())),
                           preferred_element_type=F32)


def _dot_tn(a, b):
    return lax.dot_general(a.astype(BF16), b.astype(BF16), (((0,), (0,)), ((), ())),
                           preferred_element_type=F32)


def _split(a, n):
    out = []
    r = a
    for _ in range(n):
        p = r.astype(BF16)
        out.append(p)
        r = r - p.astype(F32)
    return out


def _dot_hi(a, b):
    a1, a2 = _split(a, 2)
    b1, b2 = _split(b, 2)
    return _dot(a1, b1) + (_dot(a1, b2) + _dot(a2, b1))


def _dot_nt_hi(a, b):
    a1, a2 = _split(a, 2)
    b1, b2 = _split(b, 2)
    return _dot_nt(a1, b1) + (_dot_nt(a1, b2) + _dot_nt(a2, b1))


def _dot_lhs01(m, b):
    b1, b2, b3 = _split(b, 3)
    return _dot(m, b1) + (_dot(m, b2) + _dot(m, b3))


def _dot_rhs01(a, m):
    a1, a2, a3 = _split(a, 3)
    return _dot(a1, m) + (_dot(a2, m) + _dot(a3, m))


def _iota(shape, dim):
    return lax.broadcasted_iota(I32, shape, dim)


def _sigmoid(x):
    return 1.0 / (1.0 + jnp.exp(-x))


def _silu(x):
    return x * _sigmoid(x)


def _softplus(x):
    return jnp.maximum(x, 0.0) + jnp.log1p(jnp.exp(-jnp.abs(x)))


def _gelu(x):
    return 0.5 * x * (1.0 + lax.erf(x * (2.0 ** -0.5)))


def _inproj_kernel(x_ref, w_ref, o_gla, o_sgu, o_ssd, o_dil):
    xb = x_ref[...].astype(BF16)
    off = 0
    for o in (o_gla, o_sgu, o_ssd, o_dil):
        n = o.shape[1]
        o[...] = jnp.dot(xb, w_ref[:, off:off + n], preferred_element_type=F32)
        off += n


def _inproj(x2, w_p):
    T = x2.shape[0]
    tm = 512
    widths = (W_GLA, W_SGU, W_SSD, W_DIL)
    return pl.pallas_call(
        _inproj_kernel,
        grid=(T // tm,),
        in_specs=[pl.BlockSpec((tm, D_MODEL), lambda i: (i, 0)),
                  pl.BlockSpec((D_MODEL, W_PROJ), lambda i: (0, 0))],
        out_specs=[pl.BlockSpec((tm, w), lambda i: (i, 0)) for w in widths],
        out_shape=[jax.ShapeDtypeStruct((T, w), F32) for w in widths],
        compiler_params=_cparams(("arbitrary",)),
        name="inproj",
    )(x2, w_p)


GLA_NB = 2
GLA_SUB = 16


def _gla_kernel(slab_ref, wg_ref, bg_ref, nw_ref, out_ref, st_ref, a_ref):
    S = slab_ref.shape[1]
    st_ref[...] = jnp.zeros_like(st_ref)
    row = _iota((BLK, BLK), 0)
    col = _iota((BLK, BLK), 1)
    tril = col <= row
    tril_bf = jnp.where(tril, 1.0, 0.0).astype(BF16)
    sub_row = _iota((GLA_SUB, BLK), 0)
    sub_col = _iota((GLA_SUB, BLK), 1)
    sub_head = sub_col >> 5
    v_head = _iota((BLK, GROUP_W), 1) >> 6
    st_diag = (_iota((GROUP_W, BLK), 0) >> 6) == (_iota((GROUP_W, BLK), 1) >> 5)
    seg = jnp.where((_iota((GROUP_W, GROUP_W), 0) >> 6) == (_iota((GROUP_W, GROUP_W), 1) >> 6),
                    1.0, 0.0).astype(BF16)
    wg = wg_ref[...]
    bg = bg_ref[...]
    nw = nw_ref[...]

    def block(b, r0):
        blk = slab_ref[b, pl.ds(r0, BLK), :]
        q = blk[:, 0:128] * (GLA_DK ** -0.5)
        k = blk[:, 128:256]
        v = blk[:, 256:512]
        r = blk[:, 512:768]
        lr = blk[:, 768:896]
        z = _dot_hi(lr, wg) + bg
        gk = (jnp.minimum(z, 0.0) - jnp.log1p(jnp.exp(-jnp.abs(z)))) * (1.0 / GLA_TAU)
        g = _dot_lhs01(tril_bf, gk)
        g_last = g[BLK - 1:BLK, :]
        q_in = q * jnp.exp(g)
        k_in = k * jnp.exp(g_last - g)
        vb = v.astype(BF16)
        o = _dot_nt(q_in, st_ref[b])
        for c in range(BLK // GLA_SUB):
            lo = c * GLA_SUB
            n_c = g[lo - 1:lo, :] if c else jnp.zeros((1, BLK), F32)
            qc = q[lo:lo + GLA_SUB, :] * jnp.exp(g[lo:lo + GLA_SUB, :] - n_c)
            kc = k * jnp.exp(jnp.where(row < lo + GLA_SUB, n_c - g, -jnp.inf))
            qs = jnp.concatenate([jnp.where(sub_head == h, qc, 0.0) for h in range(GLA_H)], axis=0)
            sc = _dot_nt(qs, kc)
            causal = sub_col <= sub_row + lo
            for h in range(GLA_H):
                a_ref[b, h, lo:lo + GLA_SUB, :] = jnp.where(causal, sc[h * GLA_SUB:(h + 1) * GLA_SUB, :], 0.0)
        for h in range(GLA_H):
            o = o + jnp.where(v_head == h, _dot(a_ref[b, h], vb), 0.0)
        st_ref[b] = st_ref[b] * jnp.exp(g_last) + jnp.where(st_diag, _dot_tn(vb, k_in), 0.0)
        ms = _dot_rhs01(o * o, seg) * (1.0 / GLA_DV)
        y = o * lax.rsqrt(ms + RMS_EPS) * nw * _silu(r)
        out_ref[b, pl.ds(r0, BLK), :] = y.astype(out_ref.dtype)

    def body(n, carry):
        r0 = pl.multiple_of(n * BLK, BLK)
        for b in range(slab_ref.shape[0]):
            block(b, r0)
        return carry

    lax.fori_loop(0, S // BLK, body, 0)


def _gla(slab, wg_p, bg, nw_t):
    B, S, _ = slab.shape
    nb = GLA_NB if B % GLA_NB == 0 else 1
    return pl.pallas_call(
        _gla_kernel,
        grid=(B // nb,),
        in_specs=[pl.BlockSpec((nb, S, W_GLA), lambda b: (b, 0, 0)),
                  pl.BlockSpec((128, 128), lambda b: (0, 0)),
                  pl.BlockSpec((1, 128), lambda b: (0, 0)),
                  pl.BlockSpec((1, GROUP_W), lambda b: (0, 0))],
        out_specs=pl.BlockSpec((nb, S, GROUP_W), lambda b: (b, 0, 0)),
        out_shape=jax.ShapeDtypeStruct((B, S, GROUP_W), BF16),
        scratch_shapes=[pltpu.VMEM((nb, GROUP_W, 128), F32), pltpu.VMEM((nb, GLA_H, BLK, BLK), F32)],
        compiler_params=_cparams(("arbitrary",)),
        name="gla",
    )(slab, wg_p, bg, nw_t)


def _sgu_kernel(slab_ref, w_ref, bias_ref, g_ref, b_ref, out_ref):
    S = slab_ref.shape[1]
    wmask = (_iota((BLK, SGU_G * BLK), 1) & (BLK - 1)) <= _iota((BLK, SGU_G * BLK), 0)
    w = jnp.where(wmask, w_ref[...], 0.0).astype(BF16)
    lane_grp = _iota((BLK, GROUP_W), 1) >> 6
    bias = bias_ref[...]
    ln_g = g_ref[...]
    ln_b = b_ref[...]

    def body(n, carry):
        r0 = pl.multiple_of(n * BLK, BLK)
        blk = slab_ref[0, pl.ds(r0, BLK), :]
        u = _gelu(blk[:, 0:256])
        v = _gelu(blk[:, 256:512])
        mu = jnp.mean(v, axis=-1, keepdims=True)
        var = jnp.mean(jnp.square(v - mu), axis=-1, keepdims=True)
        v = (v - mu) * lax.rsqrt(var + LN_EPS) * ln_g + ln_b
        vexp = jnp.concatenate([jnp.where(lane_grp == g, v, 0.0) for g in range(SGU_G)], axis=0)
        s = _dot(w, vexp) + bias
        out_ref[0, pl.ds(r0, BLK), :] = (u * s).astype(out_ref.dtype)
        return carry

    lax.fori_loop(0, S // BLK, body, 0)


def _sgu(slab, w_cat, bias_t, ln_g, ln_b):
    B, S, _ = slab.shape
    return pl.pallas_call(
        _sgu_kernel,
        grid=(B,),
        in_specs=[pl.BlockSpec((1, S, W_SGU), lambda b: (b, 0, 0)),
                  pl.BlockSpec((BLK, SGU_G * BLK), lambda b: (0, 0)),
                  pl.BlockSpec((BLK, GROUP_W), lambda b: (0, 0)),
                  pl.BlockSpec((1, GROUP_W), lambda b: (0, 0)),
                  pl.BlockSpec((1, GROUP_W), lambda b: (0, 0))],
        out_specs=pl.BlockSpec((1, S, GROUP_W), lambda b: (b, 0, 0)),
        out_shape=jax.ShapeDtypeStruct((B, S, GROUP_W), BF16),
        compiler_params=_cparams(("arbitrary",)),
        name="sgu",
    )(slab, w_cat, bias_t, ln_g, ln_b)


def _ssd_kernel(slab_ref, cw_ref, cb_ref, dtb_ref, a_ref, d_ref, nw_ref, out_ref, xpad_ref, st_ref):
    S = slab_ref.shape[1]
    CW = 3 * GROUP_W
    st_ref[...] = jnp.zeros_like(st_ref)
    xpad_ref[0:8, :] = jnp.zeros((8, CW), F32)
    xpad_ref[8:S + 8, :] = slab_ref[0, :, 256:256 + CW]
    row = _iota((BLK, BLK), 0)
    col = _iota((BLK, BLK), 1)
    tril = col <= row
    tril_bf = jnp.where(tril, 1.0, 0.0).astype(BF16)
    lane_head = _iota((BLK, GROUP_W), 1) >> 6
    st_diag = (_iota((GROUP_W, GROUP_W), 0) >> 6) == (_iota((GROUP_W, GROUP_W), 1) >> 6)
    cw = cw_ref[...]
    cb = cb_ref[...]
    dtb = dtb_ref[...]
    a_neg = -jnp.exp(a_ref[...])
    d_skip = d_ref[...]
    nw = nw_ref[...]

    def body(n, carry):
        r0 = pl.multiple_of(n * BLK, BLK)
        xw = xpad_ref[pl.ds(r0, BLK + 8), :]
        conv = cb + cw[0:1, :] * xw[5:5 + BLK, :]
        for w in range(1, SSM_CONV):
            conv = conv + cw[w:w + 1, :] * xw[5 + w:5 + w + BLK, :]
        xbc = _silu(conv)
        xs = xbc[:, 0:256]
        b_e = xbc[:, 256:512]
        c_e = xbc[:, 512:768]
        z = slab_ref[0, pl.ds(r0, BLK), 0:256]
        dt = _softplus(slab_ref[0, pl.ds(r0, BLK), 1024:1280] + dtb)
        acs = _dot_lhs01(tril_bf, dt * a_neg)
        acs_last = acs[BLK - 1:BLK, :]
        x_dt = (xs * dt).astype(BF16)
        acs_t = [jnp.transpose(acs[:, 0:128]), jnp.transpose(acs[:, 128:256])]
        y = _dot(c_e * jnp.exp(acs), st_ref[...])
        b_bf = b_e.astype(BF16)
        for h in range(SSM_H):
            a_col = jnp.broadcast_to(acs[:, 64 * h:64 * h + 1], (BLK, BLK))
            a_row = jnp.broadcast_to(acs_t[h // 2][64 * (h % 2):64 * (h % 2) + 1, :], (BLK, BLK))
            lm = jnp.exp(jnp.where(tril, a_col - a_row, -jnp.inf))
            ch = jnp.where(lane_head == h, c_e, 0.0)
            m = _dot_nt(ch, b_bf) * lm
            y = y + jnp.where(lane_head == h, _dot(m, x_dt), 0.0)
        new = _dot_tn(b_e * jnp.exp(acs_last - acs), x_dt)
        st_ref[...] = st_ref[...] * jnp.exp(acs_last) + jnp.where(st_diag, new, 0.0)
        y = (y + d_skip * xs) * _silu(z)
        parts = []
        for g in range(SSM_G):
            yg = y[:, 128 * g:128 * (g + 1)]
            ms = jnp.mean(yg * yg, axis=-1, keepdims=True)
            parts.append(yg * lax.rsqrt(ms + RMS_EPS))
        out = jnp.concatenate(parts, axis=1) * nw
        out_ref[0, pl.ds(r0, BLK), :] = out.astype(out_ref.dtype)
        return carry

    lax.fori_loop(0, S // BLK, body, 0)


def _ssd(slab, cw_e, cb_e, dtb_e, a_e, d_e, nw):
    B, S, _ = slab.shape
    CW = 3 * GROUP_W
    vec = lambda w: pl.BlockSpec((1, w), lambda b: (0, 0))
    return pl.pallas_call(
        _ssd_kernel,
        grid=(B,),
        in_specs=[pl.BlockSpec((1, S, W_SSD), lambda b: (b, 0, 0)),
                  pl.BlockSpec((SSM_CONV, CW), lambda b: (0, 0)),
                  vec(CW), vec(GROUP_W), vec(GROUP_W), vec(GROUP_W), vec(GROUP_W)],
        out_specs=pl.BlockSpec((1, S, GROUP_W), lambda b: (b, 0, 0)),
        out_shape=jax.ShapeDtypeStruct((B, S, GROUP_W), BF16),
        scratch_shapes=[pltpu.VMEM((S + 8, CW), F32), pltpu.VMEM((GROUP_W, GROUP_W), F32)],
        compiler_params=_cparams(("arbitrary",)),
        name="ssd",
    )(slab, cw_e, cb_e, dtb_e, a_e, d_e, nw)


def _branch_log_multiplicity(S):
    nq = S // QBLK
    r = np.arange(QBLK)[:, None]
    u = np.arange(S)[None, :]
    d = (nq - 1) * QBLK + r - u
    c = np.zeros(d.shape, np.float64)
    for window, dil in DIL_BRANCHES:
        c += (d >= 0) & (d % dil == 0) & (d <= window)
    with np.errstate(divide="ignore"):
        return np.where(c > 0, np.log(np.maximum(c, 1.0)), NEG).astype(np.float32)


def _dil_kernel(slab_ref, cos_ref, sin_ref, tb_ref, out_ref, q_ref, k_ref, v_ref):
    S = slab_ref.shape[1]
    nq = S // QBLK
    lane_d = _iota((1, 128), 1) & (DIL_DH - 1)

    def rope(t):
        fwd = pltpu.roll(t, 128 - ROT_DIM // 2, 1)
        bwd = pltpu.roll(t, ROT_DIM // 2, 1)
        return t * cos_ref[0] + jnp.where(lane_d < ROT_DIM // 2, fwd, bwd) * sin_ref[0]

    for c in range(0, GROUP_W, 128):
        q_ref[:, c:c + 128] = (rope(slab_ref[0, :, c:c + 128]) * (DIL_DH ** -0.5)).astype(BF16)
        k_ref[:, c:c + 128] = rope(slab_ref[0, :, 256 + c:384 + c]).astype(BF16)
    v_ref[...] = slab_ref[0, :, 512:768].astype(BF16)
    lane_head = _iota((QBLK, GROUP_W), 1) >> 6
    for i in range(nq):
        nk = (i + 1) * QBLK
        qi = q_ref[i * QBLK:(i + 1) * QBLK, :]
        acc = jnp.zeros((QBLK, GROUP_W), F32)
        for h in range(DIL_H):
            qh = jnp.where(lane_head == h, qi, jnp.zeros_like(qi))
            s = _dot_nt(qh, k_ref[0:nk, :]) + tb_ref[:, (nq - 1 - i) * QBLK:]
            m = jnp.max(s, axis=-1, keepdims=True)
            p = jnp.exp(s - m)
            den = jnp.sum(p, axis=-1, keepdims=True)
            o = _dot(p, v_ref[0:nk, :])
            acc = acc + jnp.where(lane_head == h, o / den, 0.0)
        out_ref[0, i * QBLK:(i + 1) * QBLK, :] = acc.astype(out_ref.dtype)


def _dil(slab, cos_t, sin_t, tb):
    B, S, _ = slab.shape
    return pl.pallas_call(
        _dil_kernel,
        grid=(B,),
        in_specs=[pl.BlockSpec((1, S, W_DIL), lambda b: (b, 0, 0)),
                  pl.BlockSpec((1, S, 128), lambda b: (b, 0, 0)),
                  pl.BlockSpec((1, S, 128), lambda b: (b, 0, 0)),
                  pl.BlockSpec((QBLK, S), lambda b: (0, 0), pipeline_mode=pl.Buffered(1))],
        out_specs=pl.BlockSpec((1, S, GROUP_W), lambda b: (b, 0, 0)),
        out_shape=jax.ShapeDtypeStruct((B, S, GROUP_W), BF16),
        scratch_shapes=[pltpu.VMEM((S, GROUP_W), BF16)] * 3,
        compiler_params=_cparams(("arbitrary",)),
        name="dil",
    )(slab, cos_t, sin_t, tb)


def _layernorm(x, g, b):
    mu = jnp.mean(x, axis=-1, keepdims=True)
    var = jnp.mean(jnp.square(x - mu), axis=-1, keepdims=True)
    return (x - mu) * lax.rsqrt(var + LN_EPS) * g + b


def _outproj_kernel(ya, yb, yc, yd, x_ref, w_ref, g_ref, b_ref, o_ref):
    y = jnp.concatenate([ya[...], yb[...], yc[...], yd[...]], axis=1)
    y = jnp.dot(y, w_ref[...], preferred_element_type=F32)
    o_ref[...] = _layernorm(ALPHA * x_ref[...] + y, g_ref[...], b_ref[...])


def _outproj(ya, yb, yc, yd, x2, w_out, g, b):
    T = x2.shape[0]
    tm = 512
    ysp = pl.BlockSpec((tm, GROUP_W), lambda i: (i, 0))
    vec = pl.BlockSpec((1, D_MODEL), lambda i: (0, 0))
    return pl.pallas_call(
        _outproj_kernel,
        grid=(T // tm,),
        in_specs=[ysp, ysp, ysp, ysp,
                  pl.BlockSpec((tm, D_MODEL), lambda i: (i, 0)),
                  pl.BlockSpec((D_MODEL, D_MODEL), lambda i: (0, 0)), vec, vec],
        out_specs=pl.BlockSpec((tm, D_MODEL), lambda i: (i, 0)),
        out_shape=jax.ShapeDtypeStruct((T, D_MODEL), F32),
        compiler_params=_cparams(("arbitrary",)),
        name="outproj_ln",
    )(ya, yb, yc, yd, x2, w_out, g, b)


def _router_kernel(x_ref, rw_ref, rb_ref, idx_ref, gate_ref, pos_ref, cnt_ref, carry_ref):
    TB = x_ref.shape[0]
    E = N_EXPERTS
    per = E // N_EXPERT_GROUPS

    @pl.when(pl.program_id(0) == 0)
    def _():
        carry_ref[...] = jnp.zeros_like(carry_ref)

    logits = _dot_nt_hi(rw_ref[...], x_ref[...])
    scores = _sigmoid(logits)
    choice = scores + rb_ref[...]
    e_iota = _iota((E, TB), 0)

    l_iota = _iota((per, TB), 0)
    grp = []
    for g in range(N_EXPERT_GROUPS):
        cg = choice[per * g:per * (g + 1), :]
        m1 = jnp.max(cg, axis=0, keepdims=True)
        i1 = jnp.min(jnp.where(cg == m1, l_iota, per), axis=0, keepdims=True)
        m2 = jnp.max(jnp.where(l_iota == i1, -jnp.inf, cg), axis=0, keepdims=True)
        grp.append(m1 + m2)
    gs = jnp.concatenate(grp, axis=0)
    g_iota = _iota((N_EXPERT_GROUPS, TB), 0)
    keep = jnp.zeros((N_EXPERT_GROUPS, TB), jnp.bool_)
    for _ in range(TOPK_GROUPS):
        m = jnp.max(gs, axis=0, keepdims=True)
        gi = jnp.min(jnp.where(gs == m, g_iota, N_EXPERT_GROUPS), axis=0, keepdims=True)
        hit = g_iota == gi
        keep = keep | hit
        gs = jnp.where(hit, -jnp.inf, gs)
    keep_f = jnp.where(keep, 1.0, 0.0)
    keep_e = jnp.concatenate([jnp.broadcast_to(keep_f[g:g + 1, :], (per, TB))
                              for g in range(N_EXPERT_GROUPS)], axis=0)
    cm = jnp.where(keep_e > 0.5, choice, -jnp.inf)

    hits, idxs, gates = [], [], []
    onehot = jnp.zeros((E, TB), F32)
    for _ in range(TOP_K):
        m = jnp.max(cm, axis=0, keepdims=True)
        ei = jnp.min(jnp.where(cm == m, e_iota, E), axis=0, keepdims=True)
        hit = e_iota == ei
        hits.append(hit)
        idxs.append(ei)
        gates.append(jnp.sum(jnp.where(hit, scores, 0.0), axis=0, keepdims=True))
        onehot = onehot + jnp.where(hit, 1.0, 0.0)
        cm = jnp.where(hit, -jnp.inf, cm)
    gate = jnp.concatenate(gates, axis=0)
    gate = gate / jnp.sum(gate, axis=0, keepdims=True) * ROUTED_SCALE

    before = jnp.where(_iota((TB, TB), 0) < _iota((TB, TB), 1), 1.0, 0.0).astype(BF16)
    cnt = carry_ref[...] + _dot(onehot, before)
    pos = jnp.concatenate([jnp.sum(jnp.where(h, cnt, 0.0), axis=0, keepdims=True) for h in hits], axis=0)
    carry_ref[...] = carry_ref[...] + jnp.sum(onehot, axis=1, keepdims=True)

    idx_ref[...] = jnp.concatenate(idxs, axis=0)
    gate_ref[...] = gate
    pos_ref[...] = pos.astype(I32)
    cnt_ref[...] = jnp.broadcast_to(carry_ref[...], cnt_ref.shape)


def _router(x1, rw_t, rb):
    T = x1.shape[0]
    TB = TOK_BLK
    kt = pl.BlockSpec((TOP_K, TB), lambda i: (0, i))
    return pl.pallas_call(
        _router_kernel,
        grid=(T // TB,),
        in_specs=[pl.BlockSpec((TB, D_MODEL), lambda i: (i, 0)),
                  pl.BlockSpec((N_EXPERTS, D_MODEL), lambda i: (0, 0)),
                  pl.BlockSpec((N_EXPERTS, 1), lambda i: (0, 0))],
        out_specs=[kt, kt, kt, pl.BlockSpec((N_EXPERTS, 128), lambda i: (0, 0))],
        out_shape=[jax.ShapeDtypeStruct((TOP_K, T), I32), jax.ShapeDtypeStruct((TOP_K, T), F32),
                   jax.ShapeDtypeStruct((TOP_K, T), I32), jax.ShapeDtypeStruct((N_EXPERTS, 128), F32)],
        scratch_shapes=[pltpu.VMEM((N_EXPERTS, 1), F32)],
        compiler_params=_cparams(("arbitrary",)),
        name="router",
    )(x1, rw_t, rb)


def _slots_kernel(ps_ref, idx_ref, pos_ref, dest_ref):
    idx = idx_ref[...]

    def body(e, acc):
        return acc + jnp.where(idx == e, ps_ref[e], 0)

    dest_ref[...] = lax.fori_loop(0, N_EXPERTS, body, pos_ref[...], unroll=8)


def _slots(pad_start, idx, pos):
    T = idx.shape[1]
    tb = min(T, 2048)
    kt = pl.BlockSpec((TOP_K, tb), lambda i, ps: (0, i))
    return pl.pallas_call(
        _slots_kernel,
        grid_spec=pltpu.PrefetchScalarGridSpec(
            num_scalar_prefetch=1, grid=(T // tb,), in_specs=[kt, kt], out_specs=kt),
        out_shape=jax.ShapeDtypeStruct((TOP_K, T), I32),
        compiler_params=_cparams(("arbitrary",)),
        name="slots",
    )(pad_start, idx, pos)


ROW_T = D_MODEL // 2 // 128
U32 = jnp.uint32


def _pack_rows(x):
    bits = lax.bitcast_convert_type(x.astype(BF16).astype(F32), U32)
    half = D_MODEL // 2
    return (bits[:, :half] >> 16) | (bits[:, half:] & U32(0xFFFF0000))


def _unpack_rows(w):
    lo = lax.bitcast_convert_type(w << 16, F32)
    hi = lax.bitcast_convert_type(w & U32(0xFFFF0000), F32)
    return jnp.concatenate([lo, hi], axis=1)


def _to_tiles(dst_ref, base, w):
    n = w.shape[0]
    for s in range(ROW_T):
        dst_ref[pl.ds(base + s, n, stride=ROW_T), :] = w[:, 128 * s:128 * (s + 1)]


def _from_tiles(src_ref, base, n):
    return jnp.concatenate([src_ref[pl.ds(base + s, n, stride=ROW_T), :] for s in range(ROW_T)], axis=1)


def _tailzero_kernel(tb_ref, xs_ref):
    del tb_ref
    xs_ref[...] = jnp.zeros_like(xs_ref)


def _tailzero(tail_blk, nblk):
    rows = MOE_BLK * ROW_T
    return pl.pallas_call(
        _tailzero_kernel,
        grid_spec=pltpu.PrefetchScalarGridSpec(
            num_scalar_prefetch=1, grid=(N_EXPERTS,), in_specs=[],
            out_specs=pl.BlockSpec((rows, 128), lambda e, tb: (tb[e], 0))),
        out_shape=jax.ShapeDtypeStruct((nblk * rows, 128), U32),
        compiler_params=_cparams(("arbitrary",)),
        name="tailzero",
    )(tail_blk)


def _dispatch_kernel(dest_ref, x_ref, xs_in, xs_out, xt, sem):
    del xs_in
    TB = x_ref.shape[0]
    _to_tiles(xt, 0, _pack_rows(x_ref[...]))

    def issue(t, c):
        src = xt.at[pl.ds(pl.multiple_of(t * ROW_T, ROW_T), ROW_T), :]
        for k in range(TOP_K):
            d = dest_ref[t * TOP_K + k]
            pltpu.make_async_copy(src, xs_out.at[pl.ds(pl.multiple_of(d * ROW_T, ROW_T), ROW_T), :], sem).start()
        return c

    lax.fori_loop(0, TB, issue, 0, unroll=2)

    for k in range(TOP_K):
        pltpu.make_async_copy(xt, xs_out.at[pl.ds(0, TB * ROW_T), :], sem).wait()


def _dispatch(dest_flat, x1, xs0):
    T = x1.shape[0]
    TB = DSP_BLK
    return pl.pallas_call(
        _dispatch_kernel,
        grid=(T // TB,),
        in_specs=[pl.BlockSpec((TB * TOP_K,), lambda i: (i,), memory_space=pltpu.SMEM),
                  pl.BlockSpec((TB, D_MODEL), lambda i: (i, 0)),
                  pl.BlockSpec(memory_space=pl.ANY)],
        out_specs=pl.BlockSpec(memory_space=pl.ANY),
        out_shape=jax.ShapeDtypeStruct(xs0.shape, xs0.dtype),
        scratch_shapes=[pltpu.VMEM((TB * ROW_T, 128), U32), pltpu.SemaphoreType.DMA],
        input_output_aliases={2: 0},
        compiler_params=_cparams(("arbitrary",)),
        name="dispatch",
    )(dest_flat, x1, xs0)


def _experts_kernel(be_ref, nu_ref, xs_ref, wg_ref, wu_ref, wd_ref, ys_ref, wgu_s, wd_s):
    i = pl.program_id(0)

    @pl.when(i < nu_ref[0])
    def _():
        prev = be_ref[jnp.maximum(i - 1, 0)]

        @pl.when((i == 0) | (be_ref[i] != prev))
        def _():
            wgu_s[:, 0:D_EXPERT] = wg_ref[0, 0].astype(BF16)
            wgu_s[:, D_EXPERT:] = wu_ref[0, 0].astype(BF16)
            wd_s[...] = wd_ref[0, 0].astype(BF16)

        x = _unpack_rows(_from_tiles(xs_ref, 0, MOE_BLK)).astype(BF16)
        h = jnp.dot(x, wgu_s[...], preferred_element_type=F32)
        a = _silu(h[:, 0:D_EXPERT]) * h[:, D_EXPERT:]
        _to_tiles(ys_ref, 0, _pack_rows(jnp.dot(a.astype(BF16), wd_s[...], preferred_element_type=F32)))


def _experts(layer, blk_expert, n_used, xs, wg, wu, wd):
    rows = MOE_BLK * ROW_T
    nblk = xs.shape[0] // rows
    row = lambda i, be, nu: (jnp.minimum(i, nu[0] - 1), 0)
    wsel = lambda i, be, nu: (layer, be[i], 0, 0)
    return pl.pallas_call(
        _experts_kernel,
        grid_spec=pltpu.PrefetchScalarGridSpec(
            num_scalar_prefetch=2,
            grid=(nblk,),
            in_specs=[pl.BlockSpec((rows, 128), row),
                      pl.BlockSpec((1, 1, D_MODEL, D_EXPERT), wsel),
                      pl.BlockSpec((1, 1, D_MODEL, D_EXPERT), wsel),
                      pl.BlockSpec((1, 1, D_EXPERT, D_MODEL), wsel)],
            out_specs=pl.BlockSpec((rows, 128), row),
            scratch_shapes=[pltpu.VMEM((D_MODEL, 2 * D_EXPERT), BF16),
                            pltpu.VMEM((D_EXPERT, D_MODEL), BF16)]),
        out_shape=jax.ShapeDtypeStruct(xs.shape, U32),
        compiler_params=_cparams(("arbitrary",)),
        name="experts",
    )(blk_expert, n_used, xs, wg, wu, wd)


def _combine_kernel(dest_ref, gate_ref, x_ref, ys_ref, sgu_ref, sd_ref, g_ref, b_ref, o_ref, buf, sem):
    TB = x_ref.shape[0]

    def issue(t, c):
        for k in range(TOP_K):
            d = dest_ref[t * TOP_K + k]
            pltpu.make_async_copy(ys_ref.at[pl.ds(pl.multiple_of(d * ROW_T, ROW_T), ROW_T), :],
                                  buf.at[pl.ds(pl.multiple_of((k * TB + t) * ROW_T, ROW_T), ROW_T), :], sem).start()
        return c

    lax.fori_loop(0, TB, issue, 0, unroll=2)

    x = x_ref[...]
    h = jnp.dot(x.astype(BF16), sgu_ref[...], preferred_element_type=F32)
    a = _silu(h[:, 0:D_EXPERT]) * h[:, D_EXPERT:]
    y = ALPHA * x + jnp.dot(a.astype(BF16), sd_ref[...], preferred_element_type=F32)

    pltpu.make_async_copy(ys_ref.at[pl.ds(0, TOP_K * TB * ROW_T), :], buf, sem).wait()

    gate = gate_ref[...]
    for k in range(TOP_K):
        y = y + gate[:, k:k + 1] * _unpack_rows(_from_tiles(buf, k * TB * ROW_T, TB))
    o_ref[...] = _layernorm(y, g_ref[...], b_ref[...])


def _combine(dest_flat, gate_t, x1, ys, sh_gu, sh_d, g, b):
    T = x1.shape[0]
    TB = CMB_BLK
    vec = pl.BlockSpec((1, D_MODEL), lambda i: (0, 0))
    return pl.pallas_call(
        _combine_kernel,
        grid=(T // TB,),
        in_specs=[pl.BlockSpec((TB * TOP_K,), lambda i: (i,), memory_space=pltpu.SMEM),
                  pl.BlockSpec((TB, TOP_K), lambda i: (i, 0)),
                  pl.BlockSpec((TB, D_MODEL), lambda i: (i, 0)),
                  pl.BlockSpec(memory_space=pl.ANY),
                  pl.BlockSpec((D_MODEL, 2 * D_EXPERT), lambda i: (0, 0)),
                  pl.BlockSpec((D_EXPERT, D_MODEL), lambda i: (0, 0)), vec, vec],
        out_specs=pl.BlockSpec((TB, D_MODEL), lambda i: (i, 0)),
        out_shape=jax.ShapeDtypeStruct((T, D_MODEL), F32),
        scratch_shapes=[pltpu.VMEM((TOP_K * TB * ROW_T, 128), U32), pltpu.SemaphoreType.DMA],
        compiler_params=_cparams(("arbitrary",)),
        name="combine_ln",
    )(dest_flat, gate_t, x1, ys, sh_gu, sh_d, g, b)


def _token_mixer(x2, B, S, w_p, gla_p, sgu_p, ssd_p, cos_t, sin_t, tb, w_out, ln_g, ln_b):
    s_gla, s_sgu, s_ssd, s_dil = _inproj(x2, w_p)
    ya = _gla(s_gla.reshape(B, S, W_GLA), *gla_p)
    yb = _sgu(s_sgu.reshape(B, S, W_SGU), *sgu_p)
    yc = _ssd(s_ssd.reshape(B, S, W_SSD), *ssd_p)
    yd = _dil(s_dil.reshape(B, S, W_DIL), cos_t, sin_t, tb)
    T = B * S
    flat = lambda y: y.reshape(T, GROUP_W)
    return _outproj(flat(ya), flat(yb), flat(yc), flat(yd), x2, w_out, ln_g, ln_b)


def _moe(layer, x1, rw_t, rb, wg, wu, wd, sh_gu, sh_d, ln_g, ln_b):
    T = x1.shape[0]
    idx, gate, pos, cnt = _router(x1, rw_t, rb)
    counts = cnt[:, 0].astype(I32)
    padded = (counts + MOE_BLK - 1) // MOE_BLK * MOE_BLK
    pad_end = jnp.cumsum(padded)
    pad_start = pad_end - padded
    dest = _slots(pad_start.astype(I32), idx, pos)
    dest_flat = dest.T.reshape(T * TOP_K)
    nblk = T * TOP_K // MOE_BLK + N_EXPERTS
    n_used = (pad_end[-1] // MOE_BLK).astype(I32)
    blk = jnp.minimum(jnp.arange(nblk, dtype=I32), n_used - 1) * MOE_BLK
    blk_expert = jnp.sum((pad_end[None, :] <= blk[:, None]).astype(I32), axis=1)
    blk_expert = jnp.minimum(blk_expert, N_EXPERTS - 1)
    tail_blk = jnp.maximum(pad_end // MOE_BLK - 1, 0).astype(I32)
    xs = _dispatch(dest_flat, x1, _tailzero(tail_blk, nblk))
    ys = _experts(layer, blk_expert, n_used.reshape(1), xs, wg, wu, wd)
    return _combine(dest_flat, gate.T, x1, ys, sh_gu, sh_d, ln_g, ln_b)


def _prep_w_in(w_in):
    o = np.cumsum((0, 128, 128, 256, 256, 16, 256, 256, 256, 512, 4, 256, 256, 256))
    c = lambda i: w_in[..., o[i]:o[i + 1]]
    xbc = c(8)
    x_cols = xbc[..., 0:256]
    per_head = lambda m: jnp.concatenate(
        [m[..., SSM_N * (h // (SSM_H // SSM_G)):SSM_N * (h // (SSM_H // SSM_G) + 1)] for h in range(SSM_H)], axis=-1)
    b_cols = per_head(xbc[..., 256:384])
    c_cols = per_head(xbc[..., 384:512])
    dt_cols = jnp.repeat(c(9), SSM_P, axis=-1)
    pad = jnp.zeros(w_in.shape[:-1] + (W_GLA - o[5],), w_in.dtype)
    cols = [c(0), c(1), c(2), c(3), c(4), pad, c(5), c(6), c(7), x_cols, b_cols, c_cols, dt_cols,
            c(10), c(11), c(12)]
    return jnp.concatenate(cols, axis=-1).astype(BF16)


def _per_head(v):
    rep = SSM_H // SSM_G
    xp = v[..., 0:256]
    pick = lambda m: jnp.concatenate([m[..., SSM_N * (h // rep):SSM_N * (h // rep + 1)] for h in range(SSM_H)], axis=-1)
    return jnp.concatenate([xp, pick(v[..., 256:384]), pick(v[..., 384:512])], axis=-1)


def kernel(x, positions, w_in, gla_w_gate, gla_b_gate, gla_norm_w, sgu_ln_g, sgu_ln_b, sgu_w, sgu_b, ssm_conv_w, ssm_conv_b, ssm_dt_bias, ssm_a_log, ssm_d, ssm_norm_w, w_out, ln1_g, ln1_b, router_w, router_bias, exp_w_gate, exp_w_up, exp_w_down, sh_w_gate, sh_w_up, sh_w_down, ln2_g, ln2_b):
    B, S, _ = x.shape
    L = w_in.shape[0]
    T = B * S

    inv_freq = ROPE_THETA ** (-jnp.arange(0, ROT_DIM, 2, dtype=F32) / ROT_DIM)
    ang = positions.astype(F32)[..., None] * inv_freq
    cos, sin = jnp.cos(ang), jnp.sin(ang)
    ones = jnp.ones((B, S, DIL_DH - ROT_DIM), F32)
    cos_t = jnp.tile(jnp.concatenate([cos, cos, ones], axis=-1), (1, 1, 2))
    sin_t = jnp.tile(jnp.concatenate([-sin, sin, 0.0 * ones], axis=-1), (1, 1, 2))
    tb = jnp.asarray(_branch_log_multiplicity(S))

    w_p = _prep_w_in(w_in)
    wg_p = jnp.pad(gla_w_gate, ((0, 0), (0, 128 - GLA_RANK), (0, 0)))
    row = lambda v: v.reshape(L, 1, -1)
    gla_nw = jnp.tile(gla_norm_w, (1, GLA_H))
    sgu_wc = jnp.transpose(sgu_w, (0, 2, 1, 3)).reshape(L, SGU_CHUNK, SGU_G * SGU_CHUNK)
    sgu_bt = jnp.repeat(jnp.transpose(sgu_b, (0, 2, 1)), GROUP_W // SGU_G, axis=-1)
    exp64 = lambda v: jnp.repeat(v, SSM_P, axis=-1)
    w_out_b = w_out.astype(BF16)
    rw_t = jnp.transpose(router_w, (0, 2, 1))
    sh_gu = jnp.concatenate([sh_w_gate, sh_w_up], axis=-1).astype(BF16)
    sh_d = sh_w_down.astype(BF16)

    x2 = x.reshape(T, D_MODEL)
    for l in range(L):
        gla_p = (wg_p[l], row(gla_b_gate)[l], row(gla_nw)[l])
        sgu_p = (sgu_wc[l], sgu_bt[l], row(sgu_ln_g)[l], row(sgu_ln_b)[l])
        ssd_p = (_per_head(ssm_conv_w[l]), row(_per_head(ssm_conv_b))[l], row(exp64(ssm_dt_bias))[l],
                 row(exp64(ssm_a_log))[l], row(exp64(ssm_d))[l], row(ssm_norm_w)[l])
        x1 = _token_mixer(x2, B, S, w_p[l], gla_p, sgu_p, ssd_p, cos_t, sin_t, tb,
                          w_out_b[l], row(ln1_g)[l], row(ln1_b)[l])
        x2 = _moe(l, x1, rw_t[l], router_bias[l].reshape(N_EXPERTS, 1), exp_w_gate, exp_w_up, exp_w_down,
                  sh_gu[l], sh_d[l], row(ln2_g)[l], row(ln2_b)[l])
    return x2.reshape(B, S, D_MODEL)
```

```python
import functools
import math

import numpy as np
import jax
import jax.numpy as jnp
from jax import lax
from jax.experimental import pallas as pl
from jax.experimental.pallas import tpu as pltpu

F32 = jnp.float32
BF16 = jnp.bfloat16
I32 = jnp.int32

D_MODEL = 1024
N_LAYERS = 4
GROUP_W = 256

GLA_H, GLA_DK, GLA_DV, GLA_RANK, GLA_TAU = 4, 32, 64, 16, 16.0
SGU_G, SGU_CHUNK = 4, 128
SSM_H, SSM_P, SSM_G, SSM_N, SSM_CONV, SSM_CHUNK = 4, 64, 2, 64, 4, 128
DIL_H, DIL_DH, ROT_DIM, ROPE_THETA = 4, 64, 16, 500000.0
DIL_BRANCHES = ((128, 1), (512, 4), (2048, 16))

N_EXPERTS, TOP_K, N_EXPERT_GROUPS, TOPK_GROUPS, D_EXPERT = 128, 8, 8, 4, 256
ROUTED_SCALE = 1.0

ALPHA = (2 * N_LAYERS) ** 0.25
LN_EPS = 1e-5
RMS_EPS = 1e-6

W_GLA = 896
W_SGU = 512
W_SSD = 1280
W_DIL = 768
W_PROJ = W_GLA + W_SGU + W_SSD + W_DIL

BLK = 128
QBLK = 256
MOE_BLK = 512
TOK_BLK = 256
DSP_BLK = 1024
CMB_BLK = 512
NEG = -1e30

VMEM_LIMIT = 56 * 1024 * 1024


def _cparams(sem):
    return pltpu.CompilerParams(dimension_semantics=sem, vmem_limit_bytes=VMEM_LIMIT)


def _dot(a, b):
    return jnp.dot(a.astype(BF16), b.astype(BF16), preferred_element_type=F32)


def _dot_nt(a, b):
    return lax.dot_general(a.astype(BF16), b.astype(BF16), (((1,), (1,)), ((), ())),
                           preferred_element_type=F32)


def _dot_tn(a, b):
    return lax.dot_general(a.astype(BF16), b.astype(BF16), (((0,), (0,)), ((), ())),
                           preferred_element_type=F32)


def _split(a, n):
    out = []
    r = a
    for _ in range(n):
        p = r.astype(BF16)
        out.append(p)
        r = r - p.astype(F32)
    return out


def _dot_hi(a, b):
    a1, a2 = _split(a, 2)
    b1, b2 = _split(b, 2)
    return _dot(a1, b1) + (_dot(a1, b2) + _dot(a2, b1))


def _dot_nt_hi(a, b):
    a1, a2 = _split(a, 2)
    b1, b2 = _split(b, 2)
    return _dot_nt(a1, b1) + (_dot_nt(a1, b2) + _dot_nt(a2, b1))


def _dot_lhs01(m, b):
    b1, b2, b3 = _split(b, 3)
    return _dot(m, b1) + (_dot(m, b2) + _dot(m, b3))


def _dot_rhs01(a, m):
    a1, a2, a3 = _split(a, 3)
    return _dot(a1, m) + (_dot(a2, m) + _dot(a3, m))


def _iota(shape, dim):
    return lax.broadcasted_iota(I32, shape, dim)


def _sigmoid(x):
    return 1.0 / (1.0 + jnp.exp(-x))


def _silu(x):
    return x * _sigmoid(x)


def _softplus(x):
    return jnp.maximum(x, 0.0) + jnp.log1p(jnp.exp(-jnp.abs(x)))


def _gelu(x):
    return 0.5 * x * (1.0 + lax.erf(x * (2.0 ** -0.5)))


def _inproj_kernel(x_ref, w_ref, o_gla, o_sgu, o_ssd, o_dil):
    xb = x_ref[...].astype(BF16)
    off = 0
    for o in (o_gla, o_sgu, o_ssd, o_dil):
        n = o.shape[1]
        o[...] = jnp.dot(xb, w_ref[:, off:off + n], preferred_element_type=F32)
        off += n


def _inproj(x2, w_p):
    T = x2.shape[0]
    tm = 512
    widths = (W_GLA, W_SGU, W_SSD, W_DIL)
    return pl.pallas_call(
        _inproj_kernel,
        grid=(T // tm,),
        in_specs=[pl.BlockSpec((tm, D_MODEL), lambda i: (i, 0)),
                  pl.BlockSpec((D_MODEL, W_PROJ), lambda i: (0, 0))],
        out_specs=[pl.BlockSpec((tm, w), lambda i: (i, 0)) for w in widths],
        out_shape=[jax.ShapeDtypeStruct((T, w), F32) for w in widths],
        compiler_params=_cparams(("arbitrary",)),
        name="inproj",
    )(x2, w_p)


GLA_NB = 2
GLA_SUB = 16


def _gla_kernel(slab_ref, wg_ref, bg_ref, nw_ref, out_ref, st_ref, a_ref):
    S = slab_ref.shape[1]
    st_ref[...] = jnp.zeros_like(st_ref)
    row = _iota((BLK, BLK), 0)
    col = _iota((BLK, BLK), 1)
    tril = col <= row
    tril_bf = jnp.where(tril, 1.0, 0.0).astype(BF16)
    sub_row = _iota((GLA_SUB, BLK), 0)
    sub_col = _iota((GLA_SUB, BLK), 1)
    sub_head = sub_col >> 5
    v_head = _iota((BLK, GROUP_W), 1) >> 6
    st_diag = (_iota((GROUP_W, BLK), 0) >> 6) == (_iota((GROUP_W, BLK), 1) >> 5)
    seg = jnp.where((_iota((GROUP_W, GROUP_W), 0) >> 6) == (_iota((GROUP_W, GROUP_W), 1) >> 6),
                    1.0, 0.0).astype(BF16)
    wg = wg_ref[...]
    bg = bg_ref[...]
    nw = nw_ref[...]

    def block(b, r0):
        blk = slab_ref[b, pl.ds(r0, BLK), :]
        q = blk[:, 0:128] * (GLA_DK ** -0.5)
        k = blk[:, 128:256]
        v = blk[:, 256:512]
        r = blk[:, 512:768]
        lr = blk[:, 768:896]
        z = _dot_hi(lr, wg) + bg
        gk = (jnp.minimum(z, 0.0) - jnp.log1p(jnp.exp(-jnp.abs(z)))) * (1.0 / GLA_TAU)
        g = _dot_lhs01(tril_bf, gk)
        g_last = g[BLK - 1:BLK, :]
        q_in = q * jnp.exp(g)
        k_in = k * jnp.exp(g_last - g)
        vb = v.astype(BF16)
        o = _dot_nt(q_in, st_ref[b])
        for c in range(BLK // GLA_SUB):
            lo = c * GLA_SUB
            n_c = g[lo - 1:lo, :] if c else jnp.zeros((1, BLK), F32)
            qc = q[lo:lo + GLA_SUB, :] * jnp.exp(g[lo:lo + GLA_SUB, :] - n_c)
            kc = k * jnp.exp(jnp.where(row < lo + GLA_SUB, n_c - g, -jnp.inf))
            qs = jnp.concatenate([jnp.where(sub_head == h, qc, 0.0) for h in range(GLA_H)], axis=0)
            sc = _dot_nt(qs, kc)
            causal = sub_col <= sub_row + lo
            for h in range(GLA_H):
                a_ref[b, h, lo:lo + GLA_SUB, :] = jnp.where(causal, sc[h * GLA_SUB:(h + 1) * GLA_SUB, :], 0.0)
        for h in range(GLA_H):
            o = o + jnp.where(v_head == h, _dot(a_ref[b, h], vb), 0.0)
        st_ref[b] = st_ref[b] * jnp.exp(g_last) + jnp.where(st_diag, _dot_tn(vb, k_in), 0.0)
        ms = _dot_rhs01(o * o, seg) * (1.0 / GLA_DV)
        y = o * lax.rsqrt(ms + RMS_EPS) * nw * _silu(r)
        out_ref[b, pl.ds(r0, BLK), :] = y.astype(out_ref.dtype)

    def body(n, carry):
        r0 = pl.multiple_of(n * BLK, BLK)
        for b in range(slab_ref.shape[0]):
            block(b, r0)
        return carry

    lax.fori_loop(0, S // BLK, body, 0)


def _gla(slab, wg_p, bg, nw_t):
    B, S, _ = slab.shape
    nb = GLA_NB if B % GLA_NB == 0 else 1
    return pl.pallas_call(
        _gla_kernel,
        grid=(B // nb,),
        in_specs=[pl.BlockSpec((nb, S, W_GLA), lambda b: (b, 0, 0)),
                  pl.BlockSpec((128, 128), lambda b: (0, 0)),
                  pl.BlockSpec((1, 128), lambda b: (0, 0)),
                  pl.BlockSpec((1, GROUP_W), lambda b: (0, 0))],
        out_specs=pl.BlockSpec((nb, S, GROUP_W), lambda b: (b, 0, 0)),
        out_shape=jax.ShapeDtypeStruct((B, S, GROUP_W), BF16),
        scratch_shapes=[pltpu.VMEM((nb, GROUP_W, 128), F32), pltpu.VMEM((nb, GLA_H, BLK, BLK), F32)],
        compiler_params=_cparams(("arbitrary",)),
        name="gla",
    )(slab, wg_p, bg, nw_t)


def _sgu_kernel(slab_ref, w_ref, bias_ref, g_ref, b_ref, out_ref):
    S = slab_ref.shape[1]
    wmask = (_iota((BLK, SGU_G * BLK), 1) & (BLK - 1)) <= _iota((BLK, SGU_G * BLK), 0)
    w = jnp.where(wmask, w_ref[...], 0.0).astype(BF16)
    lane_grp = _iota((BLK, GROUP_W), 1) >> 6
    bias = bias_ref[...]
    ln_g = g_ref[...]
    ln_b = b_ref[...]

    def body(n, carry):
        r0 = pl.multiple_of(n * BLK, BLK)
        blk = slab_ref[0, pl.ds(r0, BLK), :]
        u = _gelu(blk[:, 0:256])
        v = _gelu(blk[:, 256:512])
        mu = jnp.mean(v, axis=-1, keepdims=True)
        var = jnp.mean(jnp.square(v - mu), axis=-1, keepdims=True)
        v = (v - mu) * lax.rsqrt(var + LN_EPS) * ln_g + ln_b
        vexp = jnp.concatenate([jnp.where(lane_grp == g, v, 0.0) for g in range(SGU_G)], axis=0)
        s = _dot(w, vexp) + bias
        out_ref[0, pl.ds(r0, BLK), :] = (u * s).astype(out_ref.dtype)
        return carry

    lax.fori_loop(0, S // BLK, body, 0)


def _sgu(slab, w_cat, bias_t, ln_g, ln_b):
    B, S, _ = slab.shape
    return pl.pallas_call(
        _sgu_kernel,
        grid=(B,),
        in_specs=[pl.BlockSpec((1, S, W_SGU), lambda b: (b, 0, 0)),
                  pl.BlockSpec((BLK, SGU_G * BLK), lambda b: (0, 0)),
                  pl.BlockSpec((BLK, GROUP_W), lambda b: (0, 0)),
                  pl.BlockSpec((1, GROUP_W), lambda b: (0, 0)),
                  pl.BlockSpec((1, GROUP_W), lambda b: (0, 0))],
        out_specs=pl.BlockSpec((1, S, GROUP_W), lambda b: (b, 0, 0)),
        out_shape=jax.ShapeDtypeStruct((B, S, GROUP_W), BF16),
        compiler_params=_cparams(("arbitrary",)),
        name="sgu",
    )(slab, w_cat, bias_t, ln_g, ln_b)


def _ssd_kernel(slab_ref, cw_ref, cb_ref, dtb_ref, a_ref, d_ref, nw_ref, out_ref, xpad_ref, st_ref):
    S = slab_ref.shape[1]
    CW = 3 * GROUP_W
    st_ref[...] = jnp.zeros_like(st_ref)
    xpad_ref[0:8, :] = jnp.zeros((8, CW), F32)
    xpad_ref[8:S + 8, :] = slab_ref[0, :, 256:256 + CW]
    row = _iota((BLK, BLK), 0)
    col = _iota((BLK, BLK), 1)
    tril = col <= row
    tril_bf = jnp.where(tril, 1.0, 0.0).astype(BF16)
    lane_head = _iota((BLK, GROUP_W), 1) >> 6
    st_diag = (_iota((GROUP_W, GROUP_W), 0) >> 6) == (_iota((GROUP_W, GROUP_W), 1) >> 6)
    cw = cw_ref[...]
    cb = cb_ref[...]
    dtb = dtb_ref[...]
    a_neg = -jnp.exp(a_ref[...])
    d_skip = d_ref[...]
    nw = nw_ref[...]

    def body(n, carry):
        r0 = pl.multiple_of(n * BLK, BLK)
        xw = xpad_ref[pl.ds(r0, BLK + 8), :]
        conv = cb + cw[0:1, :] * xw[5:5 + BLK, :]
        for w in range(1, SSM_CONV):
            conv = conv + cw[w:w + 1, :] * xw[5 + w:5 + w + BLK, :]
        xbc = _silu(conv)
        xs = xbc[:, 0:256]
        b_e = xbc[:, 256:512]
        c_e = xbc[:, 512:768]
        z = slab_ref[0, pl.ds(r0, BLK), 0:256]
        dt = _softplus(slab_ref[0, pl.ds(r0, BLK), 1024:1280] + dtb)
        acs = _dot_lhs01(tril_bf, dt * a_neg)
        acs_last = acs[BLK - 1:BLK, :]
        x_dt = (xs * dt).astype(BF16)
        acs_t = [jnp.transpose(acs[:, 0:128]), jnp.transpose(acs[:, 128:256])]
        y = _dot(c_e * jnp.exp(acs), st_ref[...])
        b_bf = b_e.astype(BF16)
        for h in range(SSM_H):
            a_col = jnp.broadcast_to(acs[:, 64 * h:64 * h + 1], (BLK, BLK))
            a_row = jnp.broadcast_to(acs_t[h // 2][64 * (h % 2):64 * (h % 2) + 1, :], (BLK, BLK))
            lm = jnp.exp(jnp.where(tril, a_col - a_row, -jnp.inf))
            ch = jnp.where(lane_head == h, c_e, 0.0)
            m = _dot_nt(ch, b_bf) * lm
            y = y + jnp.where(lane_head == h, _dot(m, x_dt), 0.0)
        new = _dot_tn(b_e * jnp.exp(acs_last - acs), x_dt)
        st_ref[...] = st_ref[...] * jnp.exp(acs_last) + jnp.where(st_diag, new, 0.0)
        y = (y + d_skip * xs) * _silu(z)
        parts = []
        for g in range(SSM_G):
            yg = y[:, 128 * g:128 * (g + 1)]
            ms = jnp.mean(yg * yg, axis=-1, keepdims=True)
            parts.append(yg * lax.rsqrt(ms + RMS_EPS))
        out = jnp.concatenate(parts, axis=1) * nw
        out_ref[0, pl.ds(r0, BLK), :] = out.astype(out_ref.dtype)
        return carry

    lax.fori_loop(0, S // BLK, body, 0)


def _ssd(slab, cw_e, cb_e, dtb_e, a_e, d_e, nw):
    B, S, _ = slab.shape
    CW = 3 * GROUP_W
    vec = lambda w: pl.BlockSpec((1, w), lambda b: (0, 0))
    return pl.pallas_call(
        _ssd_kernel,
        grid=(B,),
        in_specs=[pl.BlockSpec((1, S, W_SSD), lambda b: (b, 0, 0)),
                  pl.BlockSpec((SSM_CONV, CW), lambda b: (0, 0)),
                  vec(CW), vec(GROUP_W), vec(GROUP_W), vec(GROUP_W), vec(GROUP_W)],
        out_specs=pl.BlockSpec((1, S, GROUP_W), lambda b: (b, 0, 0)),
        out_shape=jax.ShapeDtypeStruct((B, S, GROUP_W), BF16),
        scratch_shapes=[pltpu.VMEM((S + 8, CW), F32), pltpu.VMEM((GROUP_W, GROUP_W), F32)],
        compiler_params=_cparams(("arbitrary",)),
        name="ssd",
    )(slab, cw_e, cb_e, dtb_e, a_e, d_e, nw)


def _branch_log_multiplicity(S):
    nq = S // QBLK
    r = np.arange(QBLK)[:, None]
    u = np.arange(S)[None, :]
    d = (nq - 1) * QBLK + r - u
    c = np.zeros(d.shape, np.float64)
    for window, dil in DIL_BRANCHES:
        c += (d >= 0) & (d % dil == 0) & (d <= window)
    with np.errstate(divide="ignore"):
        return np.where(c > 0, np.log(np.maximum(c, 1.0)), NEG).astype(np.float32)


def _dil_kernel(slab_ref, cos_ref, sin_ref, tb_ref, out_ref, q_ref, k_ref, v_ref):
    S = slab_ref.shape[1]
    nq = S // QBLK
    lane_d = _iota((1, 128), 1) & (DIL_DH - 1)

    def rope(t):
        fwd = pltpu.roll(t, 128 - ROT_DIM // 2, 1)
        bwd = pltpu.roll(t, ROT_DIM // 2, 1)
        return t * cos_ref[0] + jnp.where(lane_d < ROT_DIM // 2, fwd, bwd) * sin_ref[0]

    for c in range(0, GROUP_W, 128):
        q_ref[:, c:c + 128] = (rope(slab_ref[0, :, c:c + 128]) * (DIL_DH ** -0.5)).astype(BF16)
        k_ref[:, c:c + 128] = rope(slab_ref[0, :, 256 + c:384 + c]).astype(BF16)
    v_ref[...] = slab_ref[0, :, 512:768].astype(BF16)
    lane_head = _iota((QBLK, GROUP_W), 1) >> 6
    for i in range(nq):
        nk = (i + 1) * QBLK
        qi = q_ref[i * QBLK:(i + 1) * QBLK, :]
        acc = jnp.zeros((QBLK, GROUP_W), F32)
        for h in range(DIL_H):
            qh = jnp.where(lane_head == h, qi, jnp.zeros_like(qi))
            s = _dot_nt(qh, k_ref[0:nk, :]) + tb_ref[:, (nq - 1 - i) * QBLK:]
            m = jnp.max(s, axis=-1, keepdims=True)
            p = jnp.exp(s - m)
            den = jnp.sum(p, axis=-1, keepdims=True)
            o = _dot(p, v_ref[0:nk, :])
            acc = acc + jnp.where(lane_head == h, o / den, 0.0)
        out_ref[0, i * QBLK:(i + 1) * QBLK, :] = acc.astype(out_ref.dtype)


def _dil(slab, cos_t, sin_t, tb):
    B, S, _ = slab.shape
    return pl.pallas_call(
        _dil_kernel,
        grid=(B,),
        in_specs=[pl.BlockSpec((1, S, W_DIL), lambda b: (b, 0, 0)),
                  pl.BlockSpec((1, S, 128), lambda b: (b, 0, 0)),
                  pl.BlockSpec((1, S, 128), lambda b: (b, 0, 0)),
                  pl.BlockSpec((QBLK, S), lambda b: (0, 0), pipeline_mode=pl.Buffered(1))],
        out_specs=pl.BlockSpec((1, S, GROUP_W), lambda b: (b, 0, 0)),
        out_shape=jax.ShapeDtypeStruct((B, S, GROUP_W), BF16),
        scratch_shapes=[pltpu.VMEM((S, GROUP_W), BF16)] * 3,
        compiler_params=_cparams(("arbitrary",)),
        name="dil",
    )(slab, cos_t, sin_t, tb)


def _layernorm(x, g, b):
    mu = jnp.mean(x, axis=-1, keepdims=True)
    var = jnp.mean(jnp.square(x - mu), axis=-1, keepdims=True)
    return (x - mu) * lax.rsqrt(var + LN_EPS) * g + b


def _outproj_kernel(ya, yb, yc, yd, x_ref, w_ref, g_ref, b_ref, o_ref):
    y = jnp.concatenate([ya[...], yb[...], yc[...], yd[...]], axis=1)
    y = jnp.dot(y, w_ref[...], preferred_element_type=F32)
    o_ref[...] = _layernorm(ALPHA * x_ref[...] + y, g_ref[...], b_ref[...])


def _outproj(ya, yb, yc, yd, x2, w_out, g, b):
    T = x2.shape[0]
    tm = 512
    ysp = pl.BlockSpec((tm, GROUP_W), lambda i: (i, 0))
    vec = pl.BlockSpec((1, D_MODEL), lambda i: (0, 0))
    return pl.pallas_call(
        _outproj_kernel,
        grid=(T // tm,),
        in_specs=[ysp, ysp, ysp, ysp,
                  pl.BlockSpec((tm, D_MODEL), lambda i: (i, 0)),
                  pl.BlockSpec((D_MODEL, D_MODEL), lambda i: (0, 0)), vec, vec],
        out_specs=pl.BlockSpec((tm, D_MODEL), lambda i: (i, 0)),
        out_shape=jax.ShapeDtypeStruct((T, D_MODEL), F32),
        compiler_params=_cparams(("arbitrary",)),
        name="outproj_ln",
    )(ya, yb, yc, yd, x2, w_out, g, b)


def _router_kernel(x_ref, rw_ref, rb_ref, idx_ref, gate_ref, pos_ref, cnt_ref, carry_ref):
    TB = x_ref.shape[0]
    E = N_EXPERTS
    per = E // N_EXPERT_GROUPS

    @pl.when(pl.program_id(0) == 0)
    def _():
        carry_ref[...] = jnp.zeros_like(carry_ref)

    logits = _dot_nt_hi(rw_ref[...], x_ref[...])
    scores = _sigmoid(logits)
    choice = scores + rb_ref[...]
    e_iota = _iota((E, TB), 0)

    l_iota = _iota((per, TB), 0)
    grp = []
    for g in range(N_EXPERT_GROUPS):
        cg = choice[per * g:per * (g + 1), :]
        m1 = jnp.max(cg, axis=0, keepdims=True)
        i1 = jnp.min(jnp.where(cg == m1, l_iota, per), axis=0, keepdims=True)
        m2 = jnp.max(jnp.where(l_iota == i1, -jnp.inf, cg), axis=0, keepdims=True)
        grp.append(m1 + m2)
    gs = jnp.concatenate(grp, axis=0)
    g_iota = _iota((N_EXPERT_GROUPS, TB), 0)
    keep = jnp.zeros((N_EXPERT_GROUPS, TB), jnp.bool_)
    for _ in range(TOPK_GROUPS):
        m = jnp.max(gs, axis=0, keepdims=True)
        gi = jnp.min(jnp.where(gs == m, g_iota, N_EXPERT_GROUPS), axis=0, keepdims=True)
        hit = g_iota == gi
        keep = keep | hit
        gs = jnp.where(hit, -jnp.inf, gs)
    keep_f = jnp.where(keep, 1.0, 0.0)
    keep_e = jnp.concatenate([jnp.broadcast_to(keep_f[g:g + 1, :], (per, TB))
                              for g in range(N_EXPERT_GROUPS)], axis=0)
    cm = jnp.where(keep_e > 0.5, choice, -jnp.inf)

    hits, idxs, gates = [], [], []
    onehot = jnp.zeros((E, TB), F32)
    for _ in range(TOP_K):
        m = jnp.max(cm, axis=0, keepdims=True)
        ei = jnp.min(jnp.where(cm == m, e_iota, E), axis=0, keepdims=True)
        hit = e_iota == ei
        hits.append(hit)
        idxs.append(ei)
        gates.append(jnp.sum(jnp.where(hit, scores, 0.0), axis=0, keepdims=True))
        onehot = onehot + jnp.where(hit, 1.0, 0.0)
        cm = jnp.where(hit, -jnp.inf, cm)
    gate = jnp.concatenate(gates, axis=0)
    gate = gate / jnp.sum(gate, axis=0, keepdims=True) * ROUTED_SCALE

    before = jnp.where(_iota((TB, TB), 0) < _iota((TB, TB), 1), 1.0, 0.0).astype(BF16)
    cnt = carry_ref[...] + _dot(onehot, before)
    pos = jnp.concatenate([jnp.sum(jnp.where(h, cnt, 0.0), axis=0, keepdims=True) for h in hits], axis=0)
    carry_ref[...] = carry_ref[...] + jnp.sum(onehot, axis=1, keepdims=True)

    idx_ref[...] = jnp.concatenate(idxs, axis=0)
    gate_ref[...] = gate
    pos_ref[...] = pos.astype(I32)
    cnt_ref[...] = jnp.broadcast_to(carry_ref[...], cnt_ref.shape)


def _router(x1, rw_t, rb):
    T = x1.shape[0]
    TB = TOK_BLK
    kt = pl.BlockSpec((TOP_K, TB), lambda i: (0, i))
    return pl.pallas_call(
        _router_kernel,
        grid=(T // TB,),
        in_specs=[pl.BlockSpec((TB, D_MODEL), lambda i: (i, 0)),
                  pl.BlockSpec((N_EXPERTS, D_MODEL), lambda i: (0, 0)),
                  pl.BlockSpec((N_EXPERTS, 1), lambda i: (0, 0))],
        out_specs=[kt, kt, kt, pl.BlockSpec((N_EXPERTS, 128), lambda i: (0, 0))],
        out_shape=[jax.ShapeDtypeStruct((TOP_K, T), I32), jax.ShapeDtypeStruct((TOP_K, T), F32),
                   jax.ShapeDtypeStruct((TOP_K, T), I32), jax.ShapeDtypeStruct((N_EXPERTS, 128), F32)],
        scratch_shapes=[pltpu.VMEM((N_EXPERTS, 1), F32)],
        compiler_params=_cparams(("arbitrary",)),
        name="router",
    )(x1, rw_t, rb)


def _slots_kernel(ps_ref, idx_ref, pos_ref, dest_ref):
    idx = idx_ref[...]

    def body(e, acc):
        return acc + jnp.where(idx == e, ps_ref[e], 0)

    dest_ref[...] = lax.fori_loop(0, N_EXPERTS, body, pos_ref[...], unroll=8)


def _slots(pad_start, idx, pos):
    T = idx.shape[1]
    tb = min(T, 2048)
    kt = pl.BlockSpec((TOP_K, tb), lambda i, ps: (0, i))
    return pl.pallas_call(
        _slots_kernel,
        grid_spec=pltpu.PrefetchScalarGridSpec(
            num_scalar_prefetch=1, grid=(T // tb,), in_specs=[kt, kt], out_specs=kt),
        out_shape=jax.ShapeDtypeStruct((TOP_K, T), I32),
        compiler_params=_cparams(("arbitrary",)),
        name="slots",
    )(pad_start, idx, pos)


ROW_T = D_MODEL // 2 // 128
U32 = jnp.uint32


def _pack_rows(x):
    bits = lax.bitcast_convert_type(x.astype(BF16).astype(F32), U32)
    half = D_MODEL // 2
    return (bits[:, :half] >> 16) | (bits[:, half:] & U32(0xFFFF0000))


def _unpack_rows(w):
    lo = lax.bitcast_convert_type(w << 16, F32)
    hi = lax.bitcast_convert_type(w & U32(0xFFFF0000), F32)
    return jnp.concatenate([lo, hi], axis=1)


def _to_tiles(dst_ref, base, w):
    n = w.shape[0]
    for s in range(ROW_T):
        dst_ref[pl.ds(base + s, n, stride=ROW_T), :] = w[:, 128 * s:128 * (s + 1)]


def _from_tiles(src_ref, base, n):
    return jnp.concatenate([src_ref[pl.ds(base + s, n, stride=ROW_T), :] for s in range(ROW_T)], axis=1)


def _tailzero_kernel(tb_ref, xs_ref):
    del tb_ref
    xs_ref[...] = jnp.zeros_like(xs_ref)


def _tailzero(tail_blk, nblk):
    rows = MOE_BLK * ROW_T
    return pl.pallas_call(
        _tailzero_kernel,
        grid_spec=pltpu.PrefetchScalarGridSpec(
            num_scalar_prefetch=1, grid=(N_EXPERTS,), in_specs=[],
            out_specs=pl.BlockSpec((rows, 128), lambda e, tb: (tb[e], 0))),
        out_shape=jax.ShapeDtypeStruct((nblk * rows, 128), U32),
        compiler_params=_cparams(("arbitrary",)),
        name="tailzero",
    )(tail_blk)


def _dispatch_kernel(dest_ref, x_ref, xs_in, xs_out, xt, sem):
    del xs_in
    TB = x_ref.shape[0]
    _to_tiles(xt, 0, _pack_rows(x_ref[...]))

    def issue(t, c):
        src = xt.at[pl.ds(pl.multiple_of(t * ROW_T, ROW_T), ROW_T), :]
        for k in range(TOP_K):
            d = dest_ref[t * TOP_K + k]
            pltpu.make_async_copy(src, xs_out.at[pl.ds(pl.multiple_of(d * ROW_T, ROW_T), ROW_T), :], sem).start()
        return c

    lax.fori_loop(0, TB, issue, 0, unroll=2)

    for k in range(TOP_K):
        pltpu.make_async_copy(xt, xs_out.at[pl.ds(0, TB * ROW_T), :], sem).wait()


def _dispatch(dest_flat, x1, xs0):
    T = x1.shape[0]
    TB = DSP_BLK
    return pl.pallas_call(
        _dispatch_kernel,
        grid=(T // TB,),
        in_specs=[pl.BlockSpec((TB * TOP_K,), lambda i: (i,), memory_space=pltpu.SMEM),
                  pl.BlockSpec((TB, D_MODEL), lambda i: (i, 0)),
                  pl.BlockSpec(memory_space=pl.ANY)],
        out_specs=pl.BlockSpec(memory_space=pl.ANY),
        out_shape=jax.ShapeDtypeStruct(xs0.shape, xs0.dtype),
        scratch_shapes=[pltpu.VMEM((TB * ROW_T, 128), U32), pltpu.SemaphoreType.DMA],
        input_output_aliases={2: 0},
        compiler_params=_cparams(("arbitrary",)),
        name="dispatch",
    )(dest_flat, x1, xs0)


def _experts_kernel(be_ref, nu_ref, xs_ref, wg_ref, wu_ref, wd_ref, ys_ref, wgu_s, wd_s):
    i = pl.program_id(0)

    @pl.when(i < nu_ref[0])
    def _():
        prev = be_ref[jnp.maximum(i - 1, 0)]

        @pl.when((i == 0) | (be_ref[i] != prev))
        def _():
            wgu_s[:, 0:D_EXPERT] = wg_ref[0, 0].astype(BF16)
            wgu_s[:, D_EXPERT:] = wu_ref[0, 0].astype(BF16)
            wd_s[...] = wd_ref[0, 0].astype(BF16)

        x = _unpack_rows(_from_tiles(xs_ref, 0, MOE_BLK)).astype(BF16)
        h = jnp.dot(x, wgu_s[...], preferred_element_type=F32)
        a = _silu(h[:, 0:D_EXPERT]) * h[:, D_EXPERT:]
        _to_tiles(ys_ref, 0, _pack_rows(jnp.dot(a.astype(BF16), wd_s[...], preferred_element_type=F32)))


def _experts(layer, blk_expert, n_used, xs, wg, wu, wd):
    rows = MOE_BLK * ROW_T
    nblk = xs.shape[0] // rows
    row = lambda i, be, nu: (jnp.minimum(i, nu[0] - 1), 0)
    wsel = lambda i, be, nu: (layer, be[i], 0, 0)
    return pl.pallas_call(
        _experts_kernel,
        grid_spec=pltpu.PrefetchScalarGridSpec(
            num_scalar_prefetch=2,
            grid=(nblk,),
            in_specs=[pl.BlockSpec((rows, 128), row),
                      pl.BlockSpec((1, 1, D_MODEL, D_EXPERT), wsel),
                      pl.BlockSpec((1, 1, D_MODEL, D_EXPERT), wsel),
                      pl.BlockSpec((1, 1, D_EXPERT, D_MODEL), wsel)],
            out_specs=pl.BlockSpec((rows, 128), row),
            scratch_shapes=[pltpu.VMEM((D_MODEL, 2 * D_EXPERT), BF16),
                            pltpu.VMEM((D_EXPERT, D_MODEL), BF16)]),
        out_shape=jax.ShapeDtypeStruct(xs.shape, U32),
        compiler_params=_cparams(("arbitrary",)),
        name="experts",
    )(blk_expert, n_used, xs, wg, wu, wd)


def _combine_kernel(dest_ref, gate_ref, x_ref, ys_ref, sgu_ref, sd_ref, g_ref, b_ref, o_ref, buf, sem):
    TB = x_ref.shape[0]

    def issue(t, c):
        for k in range(TOP_K):
            d = dest_ref[t * TOP_K + k]
            pltpu.make_async_copy(ys_ref.at[pl.ds(pl.multiple_of(d * ROW_T, ROW_T), ROW_T), :],
                                  buf.at[pl.ds(pl.multiple_of((k * TB + t) * ROW_T, ROW_T), ROW_T), :], sem).start()
        return c

    lax.fori_loop(0, TB, issue, 0, unroll=2)

    x = x_ref[...]
    h = jnp.dot(x.astype(BF16), sgu_ref[...], preferred_element_type=F32)
    a = _silu(h[:, 0:D_EXPERT]) * h[:, D_EXPERT:]
    y = ALPHA * x + jnp.dot(a.astype(BF16), sd_ref[...], preferred_element_type=F32)

    pltpu.make_async_copy(ys_ref.at[pl.ds(0, TOP_K * TB * ROW_T), :], buf, sem).wait()

    gate = gate_ref[...]
    for k in range(TOP_K):
        y = y + gate[:, k:k + 1] * _unpack_rows(_from_tiles(buf, k * TB * ROW_T, TB))
    o_ref[...] = _layernorm(y, g_ref[...], b_ref[...])


def _combine(dest_flat, gate_t, x1, ys, sh_gu, sh_d, g, b):
    T = x1.shape[0]
    TB = CMB_BLK
    vec = pl.BlockSpec((1, D_MODEL), lambda i: (0, 0))
    return pl.pallas_call(
        _combine_kernel,
        grid=(T // TB,),
        in_specs=[pl.BlockSpec((TB * TOP_K,), lambda i: (i,), memory_space=pltpu.SMEM),
                  pl.BlockSpec((TB, TOP_K), lambda i: (i, 0)),
                  pl.BlockSpec((TB, D_MODEL), lambda i: (i, 0)),
                  pl.BlockSpec(memory_space=pl.ANY),
                  pl.BlockSpec((D_MODEL, 2 * D_EXPERT), lambda i: (0, 0)),
                  pl.BlockSpec((D_EXPERT, D_MODEL), lambda i: (0, 0)), vec, vec],
        out_specs=pl.BlockSpec((TB, D_MODEL), lambda i: (i, 0)),
        out_shape=jax.ShapeDtypeStruct((T, D_MODEL), F32),
        scratch_shapes=[pltpu.VMEM((TOP_K * TB * ROW_T, 128), U32), pltpu.SemaphoreType.DMA],
        compiler_params=_cparams(("arbitrary",)),
        name="combine_ln",
    )(dest_flat, gate_t, x1, ys, sh_gu, sh_d, g, b)


def _token_mixer(x2, B, S, w_p, gla_p, sgu_p, ssd_p, cos_t, sin_t, tb, w_out, ln_g, ln_b):
    s_gla, s_sgu, s_ssd, s_dil = _inproj(x2, w_p)
    ya = _gla(s_gla.reshape(B, S, W_GLA), *gla_p)
    yb = _sgu(s_sgu.reshape(B, S, W_SGU), *sgu_p)
    yc = _ssd(s_ssd.reshape(B, S, W_SSD), *ssd_p)
    yd = _dil(s_dil.reshape(B, S, W_DIL), cos_t, sin_t, tb)
    T = B * S
    flat = lambda y: y.reshape(T, GROUP_W)
    return _outproj(flat(ya), flat(yb), flat(yc), flat(yd), x2, w_out, ln_g, ln_b)


def _moe(layer, x1, rw_t, rb, wg, wu, wd, sh_gu, sh_d, ln_g, ln_b):
    T = x1.shape[0]
    idx, gate, pos, cnt = _router(x1, rw_t, rb)
    counts = cnt[:, 0].astype(I32)
    padded = (counts + MOE_BLK - 1) // MOE_BLK * MOE_BLK
    pad_end = jnp.cumsum(padded)
    pad_start = pad_end - padded
    dest = _slots(pad_start.astype(I32), idx, pos)
    dest_flat = dest.T.reshape(T * TOP_K)
    nblk = T * TOP_K // MOE_BLK + N_EXPERTS
    n_used = (pad_end[-1] // MOE_BLK).astype(I32)
    blk = jnp.minimum(jnp.arange(nblk, dtype=I32), n_used - 1) * MOE_BLK
    blk_expert = jnp.sum((pad_end[None, :] <= blk[:, None]).astype(I32), axis=1)
    blk_expert = jnp.minimum(blk_expert, N_EXPERTS - 1)
    tail_blk = jnp.maximum(pad_end // MOE_BLK - 1, 0).astype(I32)
    xs = _dispatch(dest_flat, x1, _tailzero(tail_blk, nblk))
    ys = _experts(layer, blk_expert, n_used.reshape(1), xs, wg, wu, wd)
    return _combine(dest_flat, gate.T, x1, ys, sh_gu, sh_d, ln_g, ln_b)


def _prep_w_in(w_in):
    o = np.cumsum((0, 128, 128, 256, 256, 16, 256, 256, 256, 512, 4, 256, 256, 256))
    c = lambda i: w_in[..., o[i]:o[i + 1]]
    xbc = c(8)
    x_cols = xbc[..., 0:256]
    per_head = lambda m: jnp.concatenate(
        [m[..., SSM_N * (h // (SSM_H // SSM_G)):SSM_N * (h // (SSM_H // SSM_G) + 1)] for h in range(SSM_H)], axis=-1)
    b_cols = per_head(xbc[..., 256:384])
    c_cols = per_head(xbc[..., 384:512])
    dt_cols = jnp.repeat(c(9), SSM_P, axis=-1)
    pad = jnp.zeros(w_in.shape[:-1] + (W_GLA - o[5],), w_in.dtype)
    cols = [c(0), c(1), c(2), c(3), c(4), pad, c(5), c(6), c(7), x_cols, b_cols, c_cols, dt_cols,
            c(10), c(11), c(12)]
    return jnp.concatenate(cols, axis=-1).astype(BF16)


def _per_head(v):
    rep = SSM_H // SSM_G
    xp = v[..., 0:256]
    pick = lambda m: jnp.concatenate([m[..., SSM_N * (h // rep):SSM_N * (h // rep + 1)] for h in range(SSM_H)], axis=-1)
    return jnp.concatenate([xp, pick(v[..., 256:384]), pick(v[..., 384:512])], axis=-1)


def kernel(x, positions, w_in, gla_w_gate, gla_b_gate, gla_norm_w, sgu_ln_g, sgu_ln_b, sgu_w, sgu_b, ssm_conv_w, ssm_conv_b, ssm_dt_bias, ssm_a_log, ssm_d, ssm_norm_w, w_out, ln1_g, ln1_b, router_w, router_bias, exp_w_gate, exp_w_up, exp_w_down, sh_w_gate, sh_w_up, sh_w_down, ln2_g, ln2_b):
    B, S, _ = x.shape
    L = w_in.shape[0]
    T = B * S

    inv_freq = ROPE_THETA ** (-jnp.arange(0, ROT_DIM, 2, dtype=F32) / ROT_DIM)
    ang = positions.astype(F32)[..., None] * inv_freq
    cos, sin = jnp.cos(ang), jnp.sin(ang)
    ones = jnp.ones((B, S, DIL_DH - ROT_DIM), F32)
    cos_t = jnp.tile(jnp.concatenate([cos, cos, ones], axis=-1), (1, 1, 2))
    sin_t = jnp.tile(jnp.concatenate([-sin, sin, 0.0 * ones], axis=-1), (1, 1, 2))
    tb = jnp.asarray(_branch_log_multiplicity(S))

    w_p = _prep_w_in(w_in)
    wg_p = jnp.pad(gla_w_gate, ((0, 0), (0, 128 - GLA_RANK), (0, 0)))
    row = lambda v: v.reshape(L, 1, -1)
    gla_nw = jnp.tile(gla_norm_w, (1, GLA_H))
    sgu_wc = jnp.transpose(sgu_w, (0, 2, 1, 3)).reshape(L, SGU_CHUNK, SGU_G * SGU_CHUNK)
    sgu_bt = jnp.repeat(jnp.transpose(sgu_b, (0, 2, 1)), GROUP_W // SGU_G, axis=-1)
    exp64 = lambda v: jnp.repeat(v, SSM_P, axis=-1)
    w_out_b = w_out.astype(BF16)
    rw_t = jnp.transpose(router_w, (0, 2, 1))
    sh_gu = jnp.concatenate([sh_w_gate, sh_w_up], axis=-1).astype(BF16)
    sh_d = sh_w_down.astype(BF16)

    x2 = x.reshape(T, D_MODEL)
    for l in range(L):
        gla_p = (wg_p[l], row(gla_b_gate)[l], row(gla_nw)[l])
        sgu_p = (sgu_wc[l], sgu_bt[l], row(sgu_ln_g)[l], row(sgu_ln_b)[l])
        ssd_p = (_per_head(ssm_conv_w[l]), row(_per_head(ssm_conv_b))[l], row(exp64(ssm_dt_bias))[l],
                 row(exp64(ssm_a_log))[l], row(exp64(ssm_d))[l], row(ssm_norm_w)[l])
        x1 = _token_mixer(x2, B, S, w_p[l], gla_p, sgu_p, ssd_p, cos_t, sin_t, tb,
                          w_out_b[l], row(ln1_g)[l], row(ln1_b)[l])
        x2 = _moe(l, x1, rw_t[l], router_bias[l].reshape(N_EXPERTS, 1), exp_w_gate, exp_w_up, exp_w_down,
                  sh_gu[l], sh_d[l], row(ln2_g)[l], row(ln2_b)[l])
    return x2.reshape(B, S, D_MODEL)
```

```python
import numpy as np
import jax
import jax.numpy as jnp
from jax import lax
from jax.experimental import pallas as pl
from jax.experimental.pallas import tpu as pltpu

F32 = jnp.float32
BF16 = jnp.bfloat16
I32 = jnp.int32

D_MODEL = 1024
N_LAYERS = 4
GROUP_W = 256

GLA_H, GLA_DK, GLA_DV, GLA_RANK, GLA_TAU = 4, 32, 64, 16, 16.0
SGU_G, SGU_CHUNK = 4, 128
SSM_H, SSM_P, SSM_G, SSM_N, SSM_CONV = 4, 64, 2, 64, 4
DIL_H, DIL_DH, ROT_DIM, ROPE_THETA = 4, 64, 16, 500000.0
DIL_BRANCHES = ((128, 1), (512, 4), (2048, 16))

N_EXPERTS, TOP_K, N_EXPERT_GROUPS, TOPK_GROUPS, D_EXPERT = 128, 8, 8, 4, 256
ROUTED_SCALE = 1.0

ALPHA = (2 * N_LAYERS) ** 0.25
LN_EPS = 1e-5
RMS_EPS = 1e-6

W_GLA = 896
W_SGU = 512
W_SSD = 1280
W_DIL = 768
W_PROJ = W_GLA + W_SGU + W_SSD + W_DIL

BLK = 128
QBLK = 256
MOE_BLK = 512
TOK_BLK = 256
DSP_BLK = 1024
CMB_BLK = 512
NEG = -1e30

VMEM_LIMIT = 56 * 1024 * 1024


def _cparams(sem):
    return pltpu.CompilerParams(dimension_semantics=sem, vmem_limit_bytes=VMEM_LIMIT)


def _dot(a, b):
    return jnp.dot(a.astype(BF16), b.astype(BF16), preferred_element_type=F32)


def _dot_nt(a, b):
    return lax.dot_general(a.astype(BF16), b.astype(BF16), (((1,), (1,)), ((), ())),
                           preferred_element_type=F32)


def _dot_tn(a, b):
    return lax.dot_general(a.astype(BF16), b.astype(BF16), (((0,), (0,)), ((), ())),
                           preferred_element_type=F32)


def _split(a, n):
    out = []
    r = a
    for _ in range(n):
        p = r.astype(BF16)
        out.append(p)
        r = r - p.astype(F32)
    return out


def _dot_hi(a, b):
    a1, a2 = _split(a, 2)
    b1, b2 = _split(b, 2)
    return _dot(a1, b1) + (_dot(a1, b2) + _dot(a2, b1))


def _dot_nt_hi(a, b):
    a1, a2 = _split(a, 2)
    b1, b2 = _split(b, 2)
    return _dot_nt(a1, b1) + (_dot_nt(a1, b2) + _dot_nt(a2, b1))


def _dot_lhs01(m, b):
    b1, b2, b3 = _split(b, 3)
    return _dot(m, b1) + (_dot(m, b2) + _dot(m, b3))


def _dot_rhs01(a, m):
    a1, a2, a3 = _split(a, 3)
    return _dot(a1, m) + (_dot(a2, m) + _dot(a3, m))


def _iota(shape, dim):
    return lax.broadcasted_iota(I32, shape, dim)


def _sigmoid(x):
    return 1.0 / (1.0 + jnp.exp(-x))


def _silu(x):
    return x * _sigmoid(x)


def _softplus(x):
    return jnp.maximum(x, 0.0) + jnp.log1p(jnp.exp(-jnp.abs(x)))


def _gelu(x):
    return 0.5 * x * (1.0 + lax.erf(x * (2.0 ** -0.5)))


def _inproj_kernel(x_ref, w_ref, o_gla, o_sgu, o_ssd, o_dil):
    xb = x_ref[...].astype(BF16)
    off = 0
    for o in (o_gla, o_sgu, o_ssd, o_dil):
        n = o.shape[1]
        o[...] = jnp.dot(xb, w_ref[:, off:off + n], preferred_element_type=F32)
        off += n


def _inproj(x2, w_p):
    T = x2.shape[0]
    tm = 512
    widths = (W_GLA, W_SGU, W_SSD, W_DIL)
    return pl.pallas_call(
        _inproj_kernel,
        grid=(T // tm,),
        in_specs=[pl.BlockSpec((tm, D_MODEL), lambda i: (i, 0)),
                  pl.BlockSpec((D_MODEL, W_PROJ), lambda i: (0, 0))],
        out_specs=[pl.BlockSpec((tm, w), lambda i: (i, 0)) for w in widths],
        out_shape=[jax.ShapeDtypeStruct((T, w), F32) for w in widths],
        compiler_params=_cparams(("arbitrary",)),
        name="inproj",
    )(x2, w_p)


GLA_NB = 2
GLA_SUB = 16


def _gla_kernel(slab_ref, wg_ref, bg_ref, nw_ref, out_ref, st_ref, a_ref):
    S = slab_ref.shape[1]
    st_ref[...] = jnp.zeros_like(st_ref)
    row = _iota((BLK, BLK), 0)
    col = _iota((BLK, BLK), 1)
    tril = col <= row
    tril_bf = jnp.where(tril, 1.0, 0.0).astype(BF16)
    sub_row = _iota((GLA_SUB, BLK), 0)
    sub_col = _iota((GLA_SUB, BLK), 1)
    sub_head = sub_col >> 5
    v_head = _iota((BLK, GROUP_W), 1) >> 6
    st_diag = (_iota((GROUP_W, BLK), 0) >> 6) == (_iota((GROUP_W, BLK), 1) >> 5)
    seg = jnp.where((_iota((GROUP_W, GROUP_W), 0) >> 6) == (_iota((GROUP_W, GROUP_W), 1) >> 6),
                    1.0, 0.0).astype(BF16)
    wg = wg_ref[...]
    bg = bg_ref[...]
    nw = nw_ref[...]

    def block(b, r0):
        blk = slab_ref[b, pl.ds(r0, BLK), :]
        q = blk[:, 0:128] * (GLA_DK ** -0.5)
        k = blk[:, 128:256]
        v = blk[:, 256:512]
        r = blk[:, 512:768]
        lr = blk[:, 768:896]
        z = _dot_hi(lr, wg) + bg
        gk = (jnp.minimum(z, 0.0) - jnp.log1p(jnp.exp(-jnp.abs(z)))) * (1.0 / GLA_TAU)
        g = _dot_lhs01(tril_bf, gk)
        g_last = g[BLK - 1:BLK, :]
        q_in = q * jnp.exp(g)
        k_in = k * jnp.exp(g_last - g)
        vb = v.astype(BF16)
        o = _dot_nt(q_in, st_ref[b])
        for c in range(BLK // GLA_SUB):
            lo = c * GLA_SUB
            n_c = g[lo - 1:lo, :] if c else jnp.zeros((1, BLK), F32)
            qc = q[lo:lo + GLA_SUB, :] * jnp.exp(g[lo:lo + GLA_SUB, :] - n_c)
            kc = k * jnp.exp(jnp.where(row < lo + GLA_SUB, n_c - g, -jnp.inf))
            qs = jnp.concatenate([jnp.where(sub_head == h, qc, 0.0) for h in range(GLA_H)], axis=0)
            sc = _dot_nt(qs, kc)
            causal = sub_col <= sub_row + lo
            for h in range(GLA_H):
                a_ref[b, h, lo:lo + GLA_SUB, :] = jnp.where(causal, sc[h * GLA_SUB:(h + 1) * GLA_SUB, :], 0.0)
        for h in range(GLA_H):
            o = o + jnp.where(v_head == h, _dot(a_ref[b, h], vb), 0.0)
        st_ref[b] = st_ref[b] * jnp.exp(g_last) + jnp.where(st_diag, _dot_tn(vb, k_in), 0.0)
        ms = _dot_rhs01(o * o, seg) * (1.0 / GLA_DV)
        y = o * lax.rsqrt(ms + RMS_EPS) * nw * _silu(r)
        out_ref[b, pl.ds(r0, BLK), :] = y.astype(out_ref.dtype)

    def body(n, carry):
        r0 = pl.multiple_of(n * BLK, BLK)
        for b in range(slab_ref.shape[0]):
            block(b, r0)
        return carry

    lax.fori_loop(0, S // BLK, body, 0)


def _gla(slab, wg_p, bg, nw_t):
    B, S, _ = slab.shape
    nb = GLA_NB if B % GLA_NB == 0 else 1
    return pl.pallas_call(
        _gla_kernel,
        grid=(B // nb,),
        in_specs=[pl.BlockSpec((nb, S, W_GLA), lambda b: (b, 0, 0)),
                  pl.BlockSpec((128, 128), lambda b: (0, 0)),
                  pl.BlockSpec((1, 128), lambda b: (0, 0)),
                  pl.BlockSpec((1, GROUP_W), lambda b: (0, 0))],
        out_specs=pl.BlockSpec((nb, S, GROUP_W), lambda b: (b, 0, 0)),
        out_shape=jax.ShapeDtypeStruct((B, S, GROUP_W), BF16),
        scratch_shapes=[pltpu.VMEM((nb, GROUP_W, 128), F32), pltpu.VMEM((nb, GLA_H, BLK, BLK), F32)],
        compiler_params=_cparams(("arbitrary",)),
        name="gla",
    )(slab, wg_p, bg, nw_t)


def _sgu_kernel(slab_ref, w_ref, bias_ref, g_ref, b_ref, out_ref):
    S = slab_ref.shape[1]
    wmask = (_iota((BLK, SGU_G * BLK), 1) & (BLK - 1)) <= _iota((BLK, SGU_G * BLK), 0)
    w = jnp.where(wmask, w_ref[...], 0.0).astype(BF16)
    lane_grp = _iota((BLK, GROUP_W), 1) >> 6
    bias = bias_ref[...]
    ln_g = g_ref[...]
    ln_b = b_ref[...]

    def body(n, carry):
        r0 = pl.multiple_of(n * BLK, BLK)
        blk = slab_ref[0, pl.ds(r0, BLK), :]
        u = _gelu(blk[:, 0:256])
        v = _gelu(blk[:, 256:512])
        mu = jnp.mean(v, axis=-1, keepdims=True)
        var = jnp.mean(jnp.square(v - mu), axis=-1, keepdims=True)
        v = (v - mu) * lax.rsqrt(var + LN_EPS) * ln_g + ln_b
        vexp = jnp.concatenate([jnp.where(lane_grp == g, v, 0.0) for g in range(SGU_G)], axis=0)
        s = _dot(w, vexp) + bias
        out_ref[0, pl.ds(r0, BLK), :] = (u * s).astype(out_ref.dtype)
        return carry

    lax.fori_loop(0, S // BLK, body, 0)


def _sgu(slab, w_cat, bias_t, ln_g, ln_b):
    B, S, _ = slab.shape
    return pl.pallas_call(
        _sgu_kernel,
        grid=(B,),
        in_specs=[pl.BlockSpec((1, S, W_SGU), lambda b: (b, 0, 0)),
                  pl.BlockSpec((BLK, SGU_G * BLK), lambda b: (0, 0)),
                  pl.BlockSpec((BLK, GROUP_W), lambda b: (0, 0)),
                  pl.BlockSpec((1, GROUP_W), lambda b: (0, 0)),
                  pl.BlockSpec((1, GROUP_W), lambda b: (0, 0))],
        out_specs=pl.BlockSpec((1, S, GROUP_W), lambda b: (b, 0, 0)),
        out_shape=jax.ShapeDtypeStruct((B, S, GROUP_W), BF16),
        compiler_params=_cparams(("arbitrary",)),
        name="sgu",
    )(slab, w_cat, bias_t, ln_g, ln_b)


def _ssd_kernel(slab_ref, cw_ref, cb_ref, dtb_ref, a_ref, d_ref, nw_ref, out_ref, xpad_ref, st_ref):
    S = slab_ref.shape[1]
    CW = 3 * GROUP_W
    st_ref[...] = jnp.zeros_like(st_ref)
    xpad_ref[0:8, :] = jnp.zeros((8, CW), F32)
    xpad_ref[8:S + 8, :] = slab_ref[0, :, 256:256 + CW]
    row = _iota((BLK, BLK), 0)
    col = _iota((BLK, BLK), 1)
    tril = col <= row
    tril_bf = jnp.where(tril, 1.0, 0.0).astype(BF16)
    lane_head = _iota((BLK, GROUP_W), 1) >> 6
    st_diag = (_iota((GROUP_W, GROUP_W), 0) >> 6) == (_iota((GROUP_W, GROUP_W), 1) >> 6)
    cw = cw_ref[...]
    cb = cb_ref[...]
    dtb = dtb_ref[...]
    a_neg = -jnp.exp(a_ref[...])
    d_skip = d_ref[...]
    nw = nw_ref[...]

    def body(n, carry):
        r0 = pl.multiple_of(n * BLK, BLK)
        xw = xpad_ref[pl.ds(r0, BLK + 8), :]
        conv = cb + cw[0:1, :] * xw[5:5 + BLK, :]
        for w in range(1, SSM_CONV):
            conv = conv + cw[w:w + 1, :] * xw[5 + w:5 + w + BLK, :]
        xbc = _silu(conv)
        xs = xbc[:, 0:256]
        b_e = xbc[:, 256:512]
        c_e = xbc[:, 512:768]
        z = slab_ref[0, pl.ds(r0, BLK), 0:256]
        dt = _softplus(slab_ref[0, pl.ds(r0, BLK), 1024:1280] + dtb)
        acs = _dot_lhs01(tril_bf, dt * a_neg)
        acs_last = acs[BLK - 1:BLK, :]
        x_dt = (xs * dt).astype(BF16)
        acs_t = [jnp.transpose(acs[:, 0:128]), jnp.transpose(acs[:, 128:256])]
        y = _dot(c_e * jnp.exp(acs), st_ref[...])
        b_bf = b_e.astype(BF16)
        for h in range(SSM_H):
            a_col = jnp.broadcast_to(acs[:, 64 * h:64 * h + 1], (BLK, BLK))
            a_row = jnp.broadcast_to(acs_t[h // 2][64 * (h % 2):64 * (h % 2) + 1, :], (BLK, BLK))
            lm = jnp.exp(jnp.where(tril, a_col - a_row, -jnp.inf))
            ch = jnp.where(lane_head == h, c_e, 0.0)
            m = _dot_nt(ch, b_bf) * lm
            y = y + jnp.where(lane_head == h, _dot(m, x_dt), 0.0)
        new = _dot_tn(b_e * jnp.exp(acs_last - acs), x_dt)
        st_ref[...] = st_ref[...] * jnp.exp(acs_last) + jnp.where(st_diag, new, 0.0)
        y = (y + d_skip * xs) * _silu(z)
        parts = []
        for g in range(SSM_G):
            yg = y[:, 128 * g:128 * (g + 1)]
            ms = jnp.mean(yg * yg, axis=-1, keepdims=True)
            parts.append(yg * lax.rsqrt(ms + RMS_EPS))
        out = jnp.concatenate(parts, axis=1) * nw
        out_ref[0, pl.ds(r0, BLK), :] = out.astype(out_ref.dtype)
        return carry

    lax.fori_loop(0, S // BLK, body, 0)


def _ssd(slab, cw_e, cb_e, dtb_e, a_e, d_e, nw):
    B, S, _ = slab.shape
    CW = 3 * GROUP_W
    vec = lambda w: pl.BlockSpec((1, w), lambda b: (0, 0))
    return pl.pallas_call(
        _ssd_kernel,
        grid=(B,),
        in_specs=[pl.BlockSpec((1, S, W_SSD), lambda b: (b, 0, 0)),
                  pl.BlockSpec((SSM_CONV, CW), lambda b: (0, 0)),
                  vec(CW), vec(GROUP_W), vec(GROUP_W), vec(GROUP_W), vec(GROUP_W)],
        out_specs=pl.BlockSpec((1, S, GROUP_W), lambda b: (b, 0, 0)),
        out_shape=jax.ShapeDtypeStruct((B, S, GROUP_W), BF16),
        scratch_shapes=[pltpu.VMEM((S + 8, CW), F32), pltpu.VMEM((GROUP_W, GROUP_W), F32)],
        compiler_params=_cparams(("arbitrary",)),
        name="ssd",
    )(slab, cw_e, cb_e, dtb_e, a_e, d_e, nw)


def _branch_log_multiplicity(S):
    nq = S // QBLK
    r = np.arange(QBLK)[:, None]
    u = np.arange(S)[None, :]
    d = (nq - 1) * QBLK + r - u
    c = np.zeros(d.shape, np.float64)
    for window, dil in DIL_BRANCHES:
        c += (d >= 0) & (d % dil == 0) & (d <= window)
    with np.errstate(divide="ignore"):
        return np.where(c > 0, np.log(np.maximum(c, 1.0)), NEG).astype(np.float32)


def _dil_kernel(slab_ref, cos_ref, sin_ref, tb_ref, out_ref, q_ref, k_ref, v_ref):
    S = slab_ref.shape[1]
    nq = S // QBLK
    lane_d = _iota((1, 128), 1) & (DIL_DH - 1)

    def rope(t):
        fwd = pltpu.roll(t, 128 - ROT_DIM // 2, 1)
        bwd = pltpu.roll(t, ROT_DIM // 2, 1)
        return t * cos_ref[0] + jnp.where(lane_d < ROT_DIM // 2, fwd, bwd) * sin_ref[0]

    for c in range(0, GROUP_W, 128):
        q_ref[:, c:c + 128] = (rope(slab_ref[0, :, c:c + 128]) * (DIL_DH ** -0.5)).astype(BF16)
        k_ref[:, c:c + 128] = rope(slab_ref[0, :, 256 + c:384 + c]).astype(BF16)
    v_ref[...] = slab_ref[0, :, 512:768].astype(BF16)
    lane_head = _iota((QBLK, GROUP_W), 1) >> 6
    for i in range(nq):
        nk = (i + 1) * QBLK
        qi = q_ref[i * QBLK:(i + 1) * QBLK, :]
        acc = jnp.zeros((QBLK, GROUP_W), F32)
        for h in range(DIL_H):
            qh = jnp.where(lane_head == h, qi, jnp.zeros_like(qi))
            s = _dot_nt(qh, k_ref[0:nk, :]) + tb_ref[:, (nq - 1 - i) * QBLK:]
            m = jnp.max(s, axis=-1, keepdims=True)
            p = jnp.exp(s - m)
            den = jnp.sum(p, axis=-1, keepdims=True)
            o = _dot(p, v_ref[0:nk, :])
            acc = acc + jnp.where(lane_head == h, o / den, 0.0)
        out_ref[0, i * QBLK:(i + 1) * QBLK, :] = acc.astype(out_ref.dtype)


def _dil(slab, cos_t, sin_t, tb):
    B, S, _ = slab.shape
    return pl.pallas_call(
        _dil_kernel,
        grid=(B,),
        in_specs=[pl.BlockSpec((1, S, W_DIL), lambda b: (b, 0, 0)),
                  pl.BlockSpec((1, S, 128), lambda b: (b, 0, 0)),
                  pl.BlockSpec((1, S, 128), lambda b: (b, 0, 0)),
                  pl.BlockSpec((QBLK, S), lambda b: (0, 0), pipeline_mode=pl.Buffered(1))],
        out_specs=pl.BlockSpec((1, S, GROUP_W), lambda b: (b, 0, 0)),
        out_shape=jax.ShapeDtypeStruct((B, S, GROUP_W), BF16),
        scratch_shapes=[pltpu.VMEM((S, GROUP_W), BF16)] * 3,
        compiler_params=_cparams(("arbitrary",)),
        name="dil",
    )(slab, cos_t, sin_t, tb)


def _layernorm(x, g, b):
    mu = jnp.mean(x, axis=-1, keepdims=True)
    var = jnp.mean(jnp.square(x - mu), axis=-1, keepdims=True)
    return (x - mu) * lax.rsqrt(var + LN_EPS) * g + b


def _outproj_kernel(ya, yb, yc, yd, x_ref, w_ref, g_ref, b_ref, o_ref):
    y = jnp.concatenate([ya[...], yb[...], yc[...], yd[...]], axis=1)
    y = jnp.dot(y, w_ref[...], preferred_element_type=F32)
    o_ref[...] = _layernorm(ALPHA * x_ref[...] + y, g_ref[...], b_ref[...])


def _outproj(ya, yb, yc, yd, x2, w_out, g, b):
    T = x2.shape[0]
    tm = 512
    ysp = pl.BlockSpec((tm, GROUP_W), lambda i: (i, 0))
    vec = pl.BlockSpec((1, D_MODEL), lambda i: (0, 0))
    return pl.pallas_call(
        _outproj_kernel,
        grid=(T // tm,),
        in_specs=[ysp, ysp, ysp, ysp,
                  pl.BlockSpec((tm, D_MODEL), lambda i: (i, 0)),
                  pl.BlockSpec((D_MODEL, D_MODEL), lambda i: (0, 0)), vec, vec],
        out_specs=pl.BlockSpec((tm, D_MODEL), lambda i: (i, 0)),
        out_shape=jax.ShapeDtypeStruct((T, D_MODEL), F32),
        compiler_params=_cparams(("arbitrary",)),
        name="outproj_ln",
    )(ya, yb, yc, yd, x2, w_out, g, b)


def _router_kernel(x_ref, rw_ref, rb_ref, idx_ref, gate_ref, pos_ref, cnt_ref, carry_ref):
    TB = x_ref.shape[0]
    E = N_EXPERTS
    per = E // N_EXPERT_GROUPS

    @pl.when(pl.program_id(0) == 0)
    def _():
        carry_ref[...] = jnp.zeros_like(carry_ref)

    logits = _dot_nt_hi(rw_ref[...], x_ref[...])
    scores = _sigmoid(logits)
    choice = scores + rb_ref[...]
    e_iota = _iota((E, TB), 0)

    l_iota = _iota((per, TB), 0)
    grp = []
    for g in range(N_EXPERT_GROUPS):
        cg = choice[per * g:per * (g + 1), :]
        m1 = jnp.max(cg, axis=0, keepdims=True)
        i1 = jnp.min(jnp.where(cg == m1, l_iota, per), axis=0, keepdims=True)
        m2 = jnp.max(jnp.where(l_iota == i1, -jnp.inf, cg), axis=0, keepdims=True)
        grp.append(m1 + m2)
    gs = jnp.concatenate(grp, axis=0)
    g_iota = _iota((N_EXPERT_GROUPS, TB), 0)
    keep = jnp.zeros((N_EXPERT_GROUPS, TB), jnp.bool_)
    for _ in range(TOPK_GROUPS):
        m = jnp.max(gs, axis=0, keepdims=True)
        gi = jnp.min(jnp.where(gs == m, g_iota, N_EXPERT_GROUPS), axis=0, keepdims=True)
        hit = g_iota == gi
        keep = keep | hit
        gs = jnp.where(hit, -jnp.inf, gs)
    keep_f = jnp.where(keep, 1.0, 0.0)
    keep_e = jnp.concatenate([jnp.broadcast_to(keep_f[g:g + 1, :], (per, TB))
                              for g in range(N_EXPERT_GROUPS)], axis=0)
    cm = jnp.where(keep_e > 0.5, choice, -jnp.inf)

    hits, idxs, gates = [], [], []
    onehot = jnp.zeros((E, TB), F32)
    for _ in range(TOP_K):
        m = jnp.max(cm, axis=0, keepdims=True)
        ei = jnp.min(jnp.where(cm == m, e_iota, E), axis=0, keepdims=True)
        hit = e_iota == ei
        hits.append(hit)
        idxs.append(ei)
        gates.append(jnp.sum(jnp.where(hit, scores, 0.0), axis=0, keepdims=True))
        onehot = onehot + jnp.where(hit, 1.0, 0.0)
        cm = jnp.where(hit, -jnp.inf, cm)
    gate = jnp.concatenate(gates, axis=0)
    gate = gate / jnp.sum(gate, axis=0, keepdims=True) * ROUTED_SCALE

    before = jnp.where(_iota((TB, TB), 0) < _iota((TB, TB), 1), 1.0, 0.0).astype(BF16)
    cnt = carry_ref[...] + _dot(onehot, before)
    pos = jnp.concatenate([jnp.sum(jnp.where(h, cnt, 0.0), axis=0, keepdims=True) for h in hits], axis=0)
    carry_ref[...] = carry_ref[...] + jnp.sum(onehot, axis=1, keepdims=True)

    idx_ref[...] = jnp.concatenate(idxs, axis=0)
    gate_ref[...] = gate
    pos_ref[...] = pos.astype(I32)
    cnt_ref[...] = jnp.broadcast_to(carry_ref[...], cnt_ref.shape)


def _router(x1, rw_t, rb):
    T = x1.shape[0]
    TB = TOK_BLK
    kt = pl.BlockSpec((TOP_K, TB), lambda i: (0, i))
    return pl.pallas_call(
        _router_kernel,
        grid=(T // TB,),
        in_specs=[pl.BlockSpec((TB, D_MODEL), lambda i: (i, 0)),
                  pl.BlockSpec((N_EXPERTS, D_MODEL), lambda i: (0, 0)),
                  pl.BlockSpec((N_EXPERTS, 1), lambda i: (0, 0))],
        out_specs=[kt, kt, kt, pl.BlockSpec((N_EXPERTS, 128), lambda i: (0, 0))],
        out_shape=[jax.ShapeDtypeStruct((TOP_K, T), I32), jax.ShapeDtypeStruct((TOP_K, T), F32),
                   jax.ShapeDtypeStruct((TOP_K, T), I32), jax.ShapeDtypeStruct((N_EXPERTS, 128), F32)],
        scratch_shapes=[pltpu.VMEM((N_EXPERTS, 1), F32)],
        compiler_params=_cparams(("arbitrary",)),
        name="router",
    )(x1, rw_t, rb)


def _slots_kernel(ps_ref, idx_ref, pos_ref, dest_ref):
    idx = idx_ref[...]

    def body(e, acc):
        return acc + jnp.where(idx == e, ps_ref[e], 0)

    dest_ref[...] = lax.fori_loop(0, N_EXPERTS, body, pos_ref[...], unroll=8)


def _slots(pad_start, idx, pos):
    T = idx.shape[1]
    tb = min(T, 2048)
    kt = pl.BlockSpec((TOP_K, tb), lambda i, ps: (0, i))
    return pl.pallas_call(
        _slots_kernel,
        grid_spec=pltpu.PrefetchScalarGridSpec(
            num_scalar_prefetch=1, grid=(T // tb,), in_specs=[kt, kt], out_specs=kt),
        out_shape=jax.ShapeDtypeStruct((TOP_K, T), I32),
        compiler_params=_cparams(("arbitrary",)),
        name="slots",
    )(pad_start, idx, pos)


ROW_T = D_MODEL // 2 // 128
U32 = jnp.uint32


def _pack_rows(x):
    bits = lax.bitcast_convert_type(x.astype(BF16).astype(F32), U32)
    half = D_MODEL // 2
    return (bits[:, :half] >> 16) | (bits[:, half:] & U32(0xFFFF0000))


def _unpack_rows(w):
    lo = lax.bitcast_convert_type(w << 16, F32)
    hi = lax.bitcast_convert_type(w & U32(0xFFFF0000), F32)
    return jnp.concatenate([lo, hi], axis=1)


def _to_tiles(dst_ref, base, w):
    n = w.shape[0]
    for s in range(ROW_T):
        dst_ref[pl.ds(base + s, n, stride=ROW_T), :] = w[:, 128 * s:128 * (s + 1)]


def _from_tiles(src_ref, base, n):
    return jnp.concatenate([src_ref[pl.ds(base + s, n, stride=ROW_T), :] for s in range(ROW_T)], axis=1)


def _tailzero_kernel(tb_ref, xs_ref):
    del tb_ref
    xs_ref[...] = jnp.zeros_like(xs_ref)


def _tailzero(tail_blk, nblk):
    rows = MOE_BLK * ROW_T
    return pl.pallas_call(
        _tailzero_kernel,
        grid_spec=pltpu.PrefetchScalarGridSpec(
            num_scalar_prefetch=1, grid=(N_EXPERTS,), in_specs=[],
            out_specs=pl.BlockSpec((rows, 128), lambda e, tb: (tb[e], 0))),
        out_shape=jax.ShapeDtypeStruct((nblk * rows, 128), U32),
        compiler_params=_cparams(("arbitrary",)),
        name="tailzero",
    )(tail_blk)


def _dispatch_kernel(dest_ref, x_ref, xs_in, xs_out, xt, sem):
    del xs_in
    TB = x_ref.shape[0]
    _to_tiles(xt, 0, _pack_rows(x_ref[...]))

    def issue(t, c):
        src = xt.at[pl.ds(pl.multiple_of(t * ROW_T, ROW_T), ROW_T), :]
        for k in range(TOP_K):
            d = dest_ref[t * TOP_K + k]
            pltpu.make_async_copy(src, xs_out.at[pl.ds(pl.multiple_of(d * ROW_T, ROW_T), ROW_T), :], sem).start()
        return c

    lax.fori_loop(0, TB, issue, 0, unroll=2)

    for k in range(TOP_K):
        pltpu.make_async_copy(xt, xs_out.at[pl.ds(0, TB * ROW_T), :], sem).wait()


def _dispatch(dest_flat, x1, xs0):
    T = x1.shape[0]
    TB = DSP_BLK
    return pl.pallas_call(
        _dispatch_kernel,
        grid=(T // TB,),
        in_specs=[pl.BlockSpec((TB * TOP_K,), lambda i: (i,), memory_space=pltpu.SMEM),
                  pl.BlockSpec((TB, D_MODEL), lambda i: (i, 0)),
                  pl.BlockSpec(memory_space=pl.ANY)],
        out_specs=pl.BlockSpec(memory_space=pl.ANY),
        out_shape=jax.ShapeDtypeStruct(xs0.shape, xs0.dtype),
        scratch_shapes=[pltpu.VMEM((TB * ROW_T, 128), U32), pltpu.SemaphoreType.DMA],
        input_output_aliases={2: 0},
        compiler_params=_cparams(("arbitrary",)),
        name="dispatch",
    )(dest_flat, x1, xs0)


def _experts_kernel(be_ref, nu_ref, xs_ref, wg_ref, wu_ref, wd_ref, ys_ref, wgu_s, wd_s):
    i = pl.program_id(0)

    @pl.when(i < nu_ref[0])
    def _():
        prev = be_ref[jnp.maximum(i - 1, 0)]

        @pl.when((i == 0) | (be_ref[i] != prev))
        def _():
            wgu_s[:, 0:D_EXPERT] = wg_ref[0, 0].astype(BF16)
            wgu_s[:, D_EXPERT:] = wu_ref[0, 0].astype(BF16)
            wd_s[...] = wd_ref[0, 0].astype(BF16)

        x = _unpack_rows(_from_tiles(xs_ref, 0, MOE_BLK)).astype(BF16)
        h = jnp.dot(x, wgu_s[...], preferred_element_type=F32)
        a = _silu(h[:, 0:D_EXPERT]) * h[:, D_EXPERT:]
        _to_tiles(ys_ref, 0, _pack_rows(jnp.dot(a.astype(BF16), wd_s[...], preferred_element_type=F32)))


def _experts(layer, blk_expert, n_used, xs, wg, wu, wd):
    rows = MOE_BLK * ROW_T
    nblk = xs.shape[0] // rows
    row = lambda i, be, nu: (jnp.minimum(i, nu[0] - 1), 0)
    wsel = lambda i, be, nu: (layer, be[i], 0, 0)
    return pl.pallas_call(
        _experts_kernel,
        grid_spec=pltpu.PrefetchScalarGridSpec(
            num_scalar_prefetch=2,
            grid=(nblk,),
            in_specs=[pl.BlockSpec((rows, 128), row),
                      pl.BlockSpec((1, 1, D_MODEL, D_EXPERT), wsel),
                      pl.BlockSpec((1, 1, D_MODEL, D_EXPERT), wsel),
                      pl.BlockSpec((1, 1, D_EXPERT, D_MODEL), wsel)],
            out_specs=pl.BlockSpec((rows, 128), row),
            scratch_shapes=[pltpu.VMEM((D_MODEL, 2 * D_EXPERT), BF16),
                            pltpu.VMEM((D_EXPERT, D_MODEL), BF16)]),
        out_shape=jax.ShapeDtypeStruct(xs.shape, U32),
        compiler_params=_cparams(("arbitrary",)),
        name="experts",
    )(blk_expert, n_used, xs, wg, wu, wd)


def _combine_kernel(dest_ref, gate_ref, x_ref, ys_ref, sgu_ref, sd_ref, g_ref, b_ref, o_ref, buf, sem):
    TB = x_ref.shape[0]

    def issue(t, c):
        for k in range(TOP_K):
            d = dest_ref[t * TOP_K + k]
            pltpu.make_async_copy(ys_ref.at[pl.ds(pl.multiple_of(d * ROW_T, ROW_T), ROW_T), :],
                                  buf.at[pl.ds(pl.multiple_of((k * TB + t) * ROW_T, ROW_T), ROW_T), :], sem).start()
        return c

    lax.fori_loop(0, TB, issue, 0, unroll=2)

    x = x_ref[...]
    h = jnp.dot(x.astype(BF16), sgu_ref[...], preferred_element_type=F32)
    a = _silu(h[:, 0:D_EXPERT]) * h[:, D_EXPERT:]
    y = ALPHA * x + jnp.dot(a.astype(BF16), sd_ref[...], preferred_element_type=F32)

    pltpu.make_async_copy(ys_ref.at[pl.ds(0, TOP_K * TB * ROW_T), :], buf, sem).wait()

    gate = gate_ref[...]
    for k in range(TOP_K):
        y = y + gate[:, k:k + 1] * _unpack_rows(_from_tiles(buf, k * TB * ROW_T, TB))
    o_ref[...] = _layernorm(y, g_ref[...], b_ref[...])


def _combine(dest_flat, gate_t, x1, ys, sh_gu, sh_d, g, b):
    T = x1.shape[0]
    TB = CMB_BLK
    vec = pl.BlockSpec((1, D_MODEL), lambda i: (0, 0))
    return pl.pallas_call(
        _combine_kernel,
        grid=(T // TB,),
        in_specs=[pl.BlockSpec((TB * TOP_K,), lambda i: (i,), memory_space=pltpu.SMEM),
                  pl.BlockSpec((TB, TOP_K), lambda i: (i, 0)),
                  pl.BlockSpec((TB, D_MODEL), lambda i: (i, 0)),
                  pl.BlockSpec(memory_space=pl.ANY),
                  pl.BlockSpec((D_MODEL, 2 * D_EXPERT), lambda i: (0, 0)),
                  pl.BlockSpec((D_EXPERT, D_MODEL), lambda i: (0, 0)), vec, vec],
        out_specs=pl.BlockSpec((TB, D_MODEL), lambda i: (i, 0)),
        out_shape=jax.ShapeDtypeStruct((T, D_MODEL), F32),
        scratch_shapes=[pltpu.VMEM((TOP_K * TB * ROW_T, 128), U32), pltpu.SemaphoreType.DMA],
        compiler_params=_cparams(("arbitrary",)),
        name="combine_ln",
    )(dest_flat, gate_t, x1, ys, sh_gu, sh_d, g, b)


def _token_mixer(x2, B, S, w_p, gla_p, sgu_p, ssd_p, cos_t, sin_t, tb, w_out, ln_g, ln_b):
    s_gla, s_sgu, s_ssd, s_dil = _inproj(x2, w_p)
    ya = _gla(s_gla.reshape(B, S, W_GLA), *gla_p)
    yb = _sgu(s_sgu.reshape(B, S, W_SGU), *sgu_p)
    yc = _ssd(s_ssd.reshape(B, S, W_SSD), *ssd_p)
    yd = _dil(s_dil.reshape(B, S, W_DIL), cos_t, sin_t, tb)
    T = B * S
    flat = lambda y: y.reshape(T, GROUP_W)
    return _outproj(flat(ya), flat(yb), flat(yc), flat(yd), x2, w_out, ln_g, ln_b)


def _moe(layer, x1, rw_t, rb, wg, wu, wd, sh_gu, sh_d, ln_g, ln_b):
    T = x1.shape[0]
    idx, gate, pos, cnt = _router(x1, rw_t, rb)
    counts = cnt[:, 0].astype(I32)
    padded = (counts + MOE_BLK - 1) // MOE_BLK * MOE_BLK
    pad_end = jnp.cumsum(padded)
    pad_start = pad_end - padded
    dest = _slots(pad_start.astype(I32), idx, pos)
    dest_flat = dest.T.reshape(T * TOP_K)
    nblk = T * TOP_K // MOE_BLK + N_EXPERTS
    n_used = (pad_end[-1] // MOE_BLK).astype(I32)
    blk = jnp.minimum(jnp.arange(nblk, dtype=I32), n_used - 1) * MOE_BLK
    blk_expert = jnp.sum((pad_end[None, :] <= blk[:, None]).astype(I32), axis=1)
    blk_expert = jnp.minimum(blk_expert, N_EXPERTS - 1)
    tail_blk = jnp.maximum(pad_end // MOE_BLK - 1, 0).astype(I32)
    xs = _dispatch(dest_flat, x1, _tailzero(tail_blk, nblk))
    ys = _experts(layer, blk_expert, n_used.reshape(1), xs, wg, wu, wd)
    return _combine(dest_flat, gate.T, x1, ys, sh_gu, sh_d, ln_g, ln_b)


def _prep_w_in(w_in):
    o = np.cumsum((0, 128, 128, 256, 256, 16, 256, 256, 256, 512, 4, 256, 256, 256))
    c = lambda i: w_in[..., o[i]:o[i + 1]]
    xbc = c(8)
    x_cols = xbc[..., 0:256]
    per_head = lambda m: jnp.concatenate(
        [m[..., SSM_N * (h // (SSM_H // SSM_G)):SSM_N * (h // (SSM_H // SSM_G) + 1)] for h in range(SSM_H)], axis=-1)
    b_cols = per_head(xbc[..., 256:384])
    c_cols = per_head(xbc[..., 384:512])
    dt_cols = jnp.repeat(c(9), SSM_P, axis=-1)
    pad = jnp.zeros(w_in.shape[:-1] + (W_GLA - o[5],), w_in.dtype)
    cols = [c(0), c(1), c(2), c(3), c(4), pad, c(5), c(6), c(7), x_cols, b_cols, c_cols, dt_cols,
            c(10), c(11), c(12)]
    return jnp.concatenate(cols, axis=-1).astype(BF16)


def _per_head(v):
    rep = SSM_H // SSM_G
    xp = v[..., 0:256]
    pick = lambda m: jnp.concatenate([m[..., SSM_N * (h // rep):SSM_N * (h // rep + 1)] for h in range(SSM_H)], axis=-1)
    return jnp.concatenate([xp, pick(v[..., 256:384]), pick(v[..., 384:512])], axis=-1)


def kernel(x, positions, w_in, gla_w_gate, gla_b_gate, gla_norm_w, sgu_ln_g, sgu_ln_b, sgu_w, sgu_b, ssm_conv_w, ssm_conv_b, ssm_dt_bias, ssm_a_log, ssm_d, ssm_norm_w, w_out, ln1_g, ln1_b, router_w, router_bias, exp_w_gate, exp_w_up, exp_w_down, sh_w_gate, sh_w_up, sh_w_down, ln2_g, ln2_b):
    B, S, _ = x.shape
    L = w_in.shape[0]
    T = B * S

    inv_freq = ROPE_THETA ** (-jnp.arange(0, ROT_DIM, 2, dtype=F32) / ROT_DIM)
    ang = positions.astype(F32)[..., None] * inv_freq
    cos, sin = jnp.cos(ang), jnp.sin(ang)
    ones = jnp.ones((B, S, DIL_DH - ROT_DIM), F32)
    cos_t = jnp.tile(jnp.concatenate([cos, cos, ones], axis=-1), (1, 1, 2))
    sin_t = jnp.tile(jnp.concatenate([-sin, sin, 0.0 * ones], axis=-1), (1, 1, 2))
    tb = jnp.asarray(_branch_log_multiplicity(S))

    w_p = _prep_w_in(w_in)
    wg_p = jnp.pad(gla_w_gate, ((0, 0), (0, 128 - GLA_RANK), (0, 0)))
    row = lambda v: v.reshape(L, 1, -1)
    gla_nw = jnp.tile(gla_norm_w, (1, GLA_H))
    sgu_wc = jnp.transpose(sgu_w, (0, 2, 1, 3)).reshape(L, SGU_CHUNK, SGU_G * SGU_CHUNK)
    sgu_bt = jnp.repeat(jnp.transpose(sgu_b, (0, 2, 1)), GROUP_W // SGU_G, axis=-1)
    exp64 = lambda v: jnp.repeat(v, SSM_P, axis=-1)
    w_out_b = w_out.astype(BF16)
    rw_t = jnp.transpose(router_w, (0, 2, 1))
    sh_gu = jnp.concatenate([sh_w_gate, sh_w_up], axis=-1).astype(BF16)
    sh_d = sh_w_down.astype(BF16)

    x2 = x.reshape(T, D_MODEL)
    for l in range(L):
        gla_p = (wg_p[l], row(gla_b_gate)[l], row(gla_nw)[l])
        sgu_p = (sgu_wc[l], sgu_bt[l], row(sgu_ln_g)[l], row(sgu_ln_b)[l])
        ssd_p = (_per_head(ssm_conv_w[l]), row(_per_head(ssm_conv_b))[l], row(exp64(ssm_dt_bias))[l],
                 row(exp64(ssm_a_log))[l], row(exp64(ssm_d))[l], row(ssm_norm_w)[l])
        x1 = _token_mixer(x2, B, S, w_p[l], gla_p, sgu_p, ssd_p, cos_t, sin_t, tb,
                          w_out_b[l], row(ln1_g)[l], row(ln1_b)[l])
        x2 = _moe(l, x1, rw_t[l], router_bias[l].reshape(N_EXPERTS, 1), exp_w_gate, exp_w_up, exp_w_down,
                  sh_gu[l], sh_d[l], row(ln2_g)[l], row(ln2_b)[l])
    return x2.reshape(B, S, D_MODEL)
```

```python
import numpy as np
import jax
import jax.numpy as jnp
from jax import lax
from jax.experimental import pallas as pl
from jax.experimental.pallas import tpu as pltpu

F32 = jnp.float32
BF16 = jnp.bfloat16
I32 = jnp.int32

D_MODEL = 1024
N_LAYERS = 4
GROUP_W = 256

GLA_H, GLA_DK, GLA_DV, GLA_RANK, GLA_TAU = 4, 32, 64, 16, 16.0
SGU_G, SGU_CHUNK = 4, 128
SSM_H, SSM_P, SSM_G, SSM_N, SSM_CONV = 4, 64, 2, 64, 4
DIL_H, DIL_DH, ROT_DIM, ROPE_THETA = 4, 64, 16, 500000.0
DIL_BRANCHES = ((128, 1), (512, 4), (2048, 16))

N_EXPERTS, TOP_K, N_EXPERT_GROUPS, TOPK_GROUPS, D_EXPERT = 128, 8, 8, 4, 256
ROUTED_SCALE = 1.0

ALPHA = (2 * N_LAYERS) ** 0.25
LN_EPS = 1e-5
RMS_EPS = 1e-6

W_GLA = 896
W_SGU = 512
W_SSD = 1280
W_DIL = 768
W_PROJ = W_GLA + W_SGU + W_SSD + W_DIL

BLK = 128
QBLK = 256
MOE_BLK = 1024
TOK_BLK = 256
DSP_BLK = 1024
CMB_BLK = 512
NEG = -1e30

VMEM_LIMIT = 56 * 1024 * 1024


def _cparams(sem):
    return pltpu.CompilerParams(dimension_semantics=sem, vmem_limit_bytes=VMEM_LIMIT)


def _dot(a, b):
    return jnp.dot(a.astype(BF16), b.astype(BF16), preferred_element_type=F32)


def _dot_nt(a, b):
    return lax.dot_general(a.astype(BF16), b.astype(BF16), (((1,), (1,)), ((), ())),
                           preferred_element_type=F32)


def _dot_tn(a, b):
    return lax.dot_general(a.astype(BF16), b.astype(BF16), (((0,), (0,)), ((), ())),
                           preferred_element_type=F32)


def _split(a, n):
    out = []
    r = a
    for _ in range(n):
        p = r.astype(BF16)
        out.append(p)
        r = r - p.astype(F32)
    return out


def _dot_hi(a, b):
    a1, a2 = _split(a, 2)
    b1, b2 = _split(b, 2)
    return _dot(a1, b1) + (_dot(a1, b2) + _dot(a2, b1))


def _dot_nt_hi(a, b):
    a1, a2 = _split(a, 2)
    b1, b2 = _split(b, 2)
    return _dot_nt(a1, b1) + (_dot_nt(a1, b2) + _dot_nt(a2, b1))


def _dot_lhs01(m, b):
    b1, b2, b3 = _split(b, 3)
    return _dot(m, b1) + (_dot(m, b2) + _dot(m, b3))


def _dot_rhs01(a, m):
    a1, a2, a3 = _split(a, 3)
    return _dot(a1, m) + (_dot(a2, m) + _dot(a3, m))


def _iota(shape, dim):
    return lax.broadcasted_iota(I32, shape, dim)


def _sigmoid(x):
    return 1.0 / (1.0 + jnp.exp(-x))


def _silu(x):
    return x * _sigmoid(x)


def _softplus(x):
    return jnp.maximum(x, 0.0) + jnp.log1p(jnp.exp(-jnp.abs(x)))


def _gelu(x):
    return 0.5 * x * (1.0 + lax.erf(x * (2.0 ** -0.5)))


def _inproj_kernel(x_ref, w_ref, o_gla, o_sgu, o_ssd, o_dil):
    xb = x_ref[...].astype(BF16)
    off = 0
    for o in (o_gla, o_sgu, o_ssd, o_dil):
        n = o.shape[1]
        o[...] = jnp.dot(xb, w_ref[:, off:off + n], preferred_element_type=F32)
        off += n


def _inproj(x2, w_p):
    T = x2.shape[0]
    tm = 512
    widths = (W_GLA, W_SGU, W_SSD, W_DIL)
    return pl.pallas_call(
        _inproj_kernel,
        grid=(T // tm,),
        in_specs=[pl.BlockSpec((tm, D_MODEL), lambda i: (i, 0)),
                  pl.BlockSpec((D_MODEL, W_PROJ), lambda i: (0, 0))],
        out_specs=[pl.BlockSpec((tm, w), lambda i: (i, 0)) for w in widths],
        out_shape=[jax.ShapeDtypeStruct((T, w), F32) for w in widths],
        compiler_params=_cparams(("arbitrary",)),
        name="inproj",
    )(x2, w_p)


GLA_NB = 2
GLA_SUB = 16


def _gla_kernel(slab_ref, wg_ref, bg_ref, nw_ref, out_ref, st_ref, a_ref):
    S = slab_ref.shape[1]
    st_ref[...] = jnp.zeros_like(st_ref)
    row = _iota((BLK, BLK), 0)
    col = _iota((BLK, BLK), 1)
    tril = col <= row
    tril_bf = jnp.where(tril, 1.0, 0.0).astype(BF16)
    sub_row = _iota((GLA_SUB, BLK), 0)
    sub_col = _iota((GLA_SUB, BLK), 1)
    sub_head = sub_col >> 5
    v_head = _iota((BLK, GROUP_W), 1) >> 6
    st_diag = (_iota((GROUP_W, BLK), 0) >> 6) == (_iota((GROUP_W, BLK), 1) >> 5)
    seg = jnp.where((_iota((GROUP_W, GROUP_W), 0) >> 6) == (_iota((GROUP_W, GROUP_W), 1) >> 6),
                    1.0, 0.0).astype(BF16)
    wg = wg_ref[...]
    bg = bg_ref[...]
    nw = nw_ref[...]

    def block(b, r0):
        blk = slab_ref[b, pl.ds(r0, BLK), :]
        q = blk[:, 0:128] * (GLA_DK ** -0.5)
        k = blk[:, 128:256]
        v = blk[:, 256:512]
        r = blk[:, 512:768]
        lr = blk[:, 768:896]
        z = _dot_hi(lr, wg) + bg
        gk = (jnp.minimum(z, 0.0) - jnp.log1p(jnp.exp(-jnp.abs(z)))) * (1.0 / GLA_TAU)
        g = _dot_lhs01(tril_bf, gk)
        g_last = g[BLK - 1:BLK, :]
        q_in = q * jnp.exp(g)
        k_in = k * jnp.exp(g_last - g)
        vb = v.astype(BF16)
        o = _dot_nt(q_in, st_ref[b])
        for c in range(BLK // GLA_SUB):
            lo = c * GLA_SUB
            n_c = g[lo - 1:lo, :] if c else jnp.zeros((1, BLK), F32)
            qc = q[lo:lo + GLA_SUB, :] * jnp.exp(g[lo:lo + GLA_SUB, :] - n_c)
            kc = k * jnp.exp(jnp.where(row < lo + GLA_SUB, n_c - g, -jnp.inf))
            qs = jnp.concatenate([jnp.where(sub_head == h, qc, 0.0) for h in range(GLA_H)], axis=0)
            sc = _dot_nt(qs, kc)
            causal = sub_col <= sub_row + lo
            for h in range(GLA_H):
                a_ref[b, h, lo:lo + GLA_SUB, :] = jnp.where(causal, sc[h * GLA_SUB:(h + 1) * GLA_SUB, :], 0.0)
        for h in range(GLA_H):
            o = o + jnp.where(v_head == h, _dot(a_ref[b, h], vb), 0.0)
        st_ref[b] = st_ref[b] * jnp.exp(g_last) + jnp.where(st_diag, _dot_tn(vb, k_in), 0.0)
        ms = _dot_rhs01(o * o, seg) * (1.0 / GLA_DV)
        y = o * lax.rsqrt(ms + RMS_EPS) * nw * _silu(r)
        out_ref[b, pl.ds(r0, BLK), :] = y.astype(out_ref.dtype)

    def body(n, carry):
        r0 = pl.multiple_of(n * BLK, BLK)
        for b in range(slab_ref.shape[0]):
            block(b, r0)
        return carry

    lax.fori_loop(0, S // BLK, body, 0)


def _gla(slab, wg_p, bg, nw_t):
    B, S, _ = slab.shape
    nb = GLA_NB if B % GLA_NB == 0 else 1
    return pl.pallas_call(
        _gla_kernel,
        grid=(B // nb,),
        in_specs=[pl.BlockSpec((nb, S, W_GLA), lambda b: (b, 0, 0)),
                  pl.BlockSpec((128, 128), lambda b: (0, 0)),
                  pl.BlockSpec((1, 128), lambda b: (0, 0)),
                  pl.BlockSpec((1, GROUP_W), lambda b: (0, 0))],
        out_specs=pl.BlockSpec((nb, S, GROUP_W), lambda b: (b, 0, 0)),
        out_shape=jax.ShapeDtypeStruct((B, S, GROUP_W), BF16),
        scratch_shapes=[pltpu.VMEM((nb, GROUP_W, 128), F32), pltpu.VMEM((nb, GLA_H, BLK, BLK), F32)],
        compiler_params=_cparams(("arbitrary",)),
        name="gla",
    )(slab, wg_p, bg, nw_t)


def _sgu_kernel(slab_ref, w_ref, bias_ref, g_ref, b_ref, out_ref):
    S = slab_ref.shape[1]
    wmask = (_iota((BLK, SGU_G * BLK), 1) & (BLK - 1)) <= _iota((BLK, SGU_G * BLK), 0)
    w = jnp.where(wmask, w_ref[...], 0.0).astype(BF16)
    lane_grp = _iota((BLK, GROUP_W), 1) >> 6
    bias = bias_ref[...]
    ln_g = g_ref[...]
    ln_b = b_ref[...]

    def body(n, carry):
        r0 = pl.multiple_of(n * BLK, BLK)
        blk = slab_ref[0, pl.ds(r0, BLK), :]
        u = _gelu(blk[:, 0:256])
        v = _gelu(blk[:, 256:512])
        mu = jnp.mean(v, axis=-1, keepdims=True)
        var = jnp.mean(jnp.square(v - mu), axis=-1, keepdims=True)
        v = (v - mu) * lax.rsqrt(var + LN_EPS) * ln_g + ln_b
        vexp = jnp.concatenate([jnp.where(lane_grp == g, v, 0.0) for g in range(SGU_G)], axis=0)
        s = _dot(w, vexp) + bias
        out_ref[0, pl.ds(r0, BLK), :] = (u * s).astype(out_ref.dtype)
        return carry

    lax.fori_loop(0, S // BLK, body, 0)


def _sgu(slab, w_cat, bias_t, ln_g, ln_b):
    B, S, _ = slab.shape
    return pl.pallas_call(
        _sgu_kernel,
        grid=(B,),
        in_specs=[pl.BlockSpec((1, S, W_SGU), lambda b: (b, 0, 0)),
                  pl.BlockSpec((BLK, SGU_G * BLK), lambda b: (0, 0)),
                  pl.BlockSpec((BLK, GROUP_W), lambda b: (0, 0)),
                  pl.BlockSpec((1, GROUP_W), lambda b: (0, 0)),
                  pl.BlockSpec((1, GROUP_W), lambda b: (0, 0))],
        out_specs=pl.BlockSpec((1, S, GROUP_W), lambda b: (b, 0, 0)),
        out_shape=jax.ShapeDtypeStruct((B, S, GROUP_W), BF16),
        compiler_params=_cparams(("arbitrary",)),
        name="sgu",
    )(slab, w_cat, bias_t, ln_g, ln_b)


def _ssd_kernel(slab_ref, cw_ref, cb_ref, dtb_ref, a_ref, d_ref, nw_ref, out_ref, xpad_ref, st_ref):
    S = slab_ref.shape[1]
    CW = 3 * GROUP_W
    st_ref[...] = jnp.zeros_like(st_ref)
    xpad_ref[0:8, :] = jnp.zeros((8, CW), F32)
    xpad_ref[8:S + 8, :] = slab_ref[0, :, 256:256 + CW]
    row = _iota((BLK, BLK), 0)
    col = _iota((BLK, BLK), 1)
    tril = col <= row
    tril_bf = jnp.where(tril, 1.0, 0.0).astype(BF16)
    lane_head = _iota((BLK, GROUP_W), 1) >> 6
    st_diag = (_iota((GROUP_W, GROUP_W), 0) >> 6) == (_iota((GROUP_W, GROUP_W), 1) >> 6)
    cw = cw_ref[...]
    cb = cb_ref[...]
    dtb = dtb_ref[...]
    a_neg = -jnp.exp(a_ref[...])
    d_skip = d_ref[...]
    nw = nw_ref[...]

    def body(n, carry):
        r0 = pl.multiple_of(n * BLK, BLK)
        xw = xpad_ref[pl.ds(r0, BLK + 8), :]
        conv = cb + cw[0:1, :] * xw[5:5 + BLK, :]
        for w in range(1, SSM_CONV):
            conv = conv + cw[w:w + 1, :] * xw[5 + w:5 + w + BLK, :]
        xbc = _silu(conv)
        xs = xbc[:, 0:256]
        b_e = xbc[:, 256:512]
        c_e = xbc[:, 512:768]
        z = slab_ref[0, pl.ds(r0, BLK), 0:256]
        dt = _softplus(slab_ref[0, pl.ds(r0, BLK), 1024:1280] + dtb)
        acs = _dot_lhs01(tril_bf, dt * a_neg)
        acs_last = acs[BLK - 1:BLK, :]
        x_dt = (xs * dt).astype(BF16)
        acs_t = [jnp.transpose(acs[:, 0:128]), jnp.transpose(acs[:, 128:256])]
        y = _dot(c_e * jnp.exp(acs), st_ref[...])
        b_bf = b_e.astype(BF16)
        for h in range(SSM_H):
            a_col = jnp.broadcast_to(acs[:, 64 * h:64 * h + 1], (BLK, BLK))
            a_row = jnp.broadcast_to(acs_t[h // 2][64 * (h % 2):64 * (h % 2) + 1, :], (BLK, BLK))
            lm = jnp.exp(jnp.where(tril, a_col - a_row, -jnp.inf))
            ch = jnp.where(lane_head == h, c_e, 0.0)
            m = _dot_nt(ch, b_bf) * lm
            y = y + jnp.where(lane_head == h, _dot(m, x_dt), 0.0)
        new = _dot_tn(b_e * jnp.exp(acs_last - acs), x_dt)
        st_ref[...] = st_ref[...] * jnp.exp(acs_last) + jnp.where(st_diag, new, 0.0)
        y = (y + d_skip * xs) * _silu(z)
        parts = []
        for g in range(SSM_G):
            yg = y[:, 128 * g:128 * (g + 1)]
            ms = jnp.mean(yg * yg, axis=-1, keepdims=True)
            parts.append(yg * lax.rsqrt(ms + RMS_EPS))
        out = jnp.concatenate(parts, axis=1) * nw
        out_ref[0, pl.ds(r0, BLK), :] = out.astype(out_ref.dtype)
        return carry

    lax.fori_loop(0, S // BLK, body, 0)


def _ssd(slab, cw_e, cb_e, dtb_e, a_e, d_e, nw):
    B, S, _ = slab.shape
    CW = 3 * GROUP_W
    vec = lambda w: pl.BlockSpec((1, w), lambda b: (0, 0))
    return pl.pallas_call(
        _ssd_kernel,
        grid=(B,),
        in_specs=[pl.BlockSpec((1, S, W_SSD), lambda b: (b, 0, 0)),
                  pl.BlockSpec((SSM_CONV, CW), lambda b: (0, 0)),
                  vec(CW), vec(GROUP_W), vec(GROUP_W), vec(GROUP_W), vec(GROUP_W)],
        out_specs=pl.BlockSpec((1, S, GROUP_W), lambda b: (b, 0, 0)),
        out_shape=jax.ShapeDtypeStruct((B, S, GROUP_W), BF16),
        scratch_shapes=[pltpu.VMEM((S + 8, CW), F32), pltpu.VMEM((GROUP_W, GROUP_W), F32)],
        compiler_params=_cparams(("arbitrary",)),
        name="ssd",
    )(slab, cw_e, cb_e, dtb_e, a_e, d_e, nw)


def _branch_log_multiplicity(S):
    nq = S // QBLK
    r = np.arange(QBLK)[:, None]
    u = np.arange(S)[None, :]
    d = (nq - 1) * QBLK + r - u
    c = np.zeros(d.shape, np.float64)
    for window, dil in DIL_BRANCHES:
        c += (d >= 0) & (d % dil == 0) & (d <= window)
    with np.errstate(divide="ignore"):
        return np.where(c > 0, np.log(np.maximum(c, 1.0)), NEG).astype(np.float32)


def _dil_kernel(slab_ref, cos_ref, sin_ref, tb_ref, out_ref, q_ref, k_ref, v_ref):
    S = slab_ref.shape[1]
    nq = S // QBLK
    lane_d = _iota((1, 128), 1) & (DIL_DH - 1)

    def rope(t):
        fwd = pltpu.roll(t, 128 - ROT_DIM // 2, 1)
        bwd = pltpu.roll(t, ROT_DIM // 2, 1)
        return t * cos_ref[0] + jnp.where(lane_d < ROT_DIM // 2, fwd, bwd) * sin_ref[0]

    for c in range(0, GROUP_W, 128):
        q_ref[:, c:c + 128] = (rope(slab_ref[0, :, c:c + 128]) * (DIL_DH ** -0.5)).astype(BF16)
        k_ref[:, c:c + 128] = rope(slab_ref[0, :, 256 + c:384 + c]).astype(BF16)
    v_ref[...] = slab_ref[0, :, 512:768].astype(BF16)
    lane_head = _iota((QBLK, GROUP_W), 1) >> 6
    for i in range(nq):
        nk = (i + 1) * QBLK
        qi = q_ref[i * QBLK:(i + 1) * QBLK, :]
        acc = jnp.zeros((QBLK, GROUP_W), F32)
        for h in range(DIL_H):
            qh = jnp.where(lane_head == h, qi, jnp.zeros_like(qi))
            s = _dot_nt(qh, k_ref[0:nk, :]) + tb_ref[:, (nq - 1 - i) * QBLK:]
            m = jnp.max(s, axis=-1, keepdims=True)
            p = jnp.exp(s - m)
            den = jnp.sum(p, axis=-1, keepdims=True)
            o = _dot(p, v_ref[0:nk, :])
            acc = acc + jnp.where(lane_head == h, o / den, 0.0)
        out_ref[0, i * QBLK:(i + 1) * QBLK, :] = acc.astype(out_ref.dtype)


def _dil(slab, cos_t, sin_t, tb):
    B, S, _ = slab.shape
    return pl.pallas_call(
        _dil_kernel,
        grid=(B,),
        in_specs=[pl.BlockSpec((1, S, W_DIL), lambda b: (b, 0, 0)),
                  pl.BlockSpec((1, S, 128), lambda b: (b, 0, 0)),
                  pl.BlockSpec((1, S, 128), lambda b: (b, 0, 0)),
                  pl.BlockSpec((QBLK, S), lambda b: (0, 0), pipeline_mode=pl.Buffered(1))],
        out_specs=pl.BlockSpec((1, S, GROUP_W), lambda b: (b, 0, 0)),
        out_shape=jax.ShapeDtypeStruct((B, S, GROUP_W), BF16),
        scratch_shapes=[pltpu.VMEM((S, GROUP_W), BF16)] * 3,
        compiler_params=_cparams(("arbitrary",)),
        name="dil",
    )(slab, cos_t, sin_t, tb)


def _layernorm(x, g, b):
    mu = jnp.mean(x, axis=-1, keepdims=True)
    var = jnp.mean(jnp.square(x - mu), axis=-1, keepdims=True)
    return (x - mu) * lax.rsqrt(var + LN_EPS) * g + b


def _outproj_kernel(ya, yb, yc, yd, x_ref, w_ref, g_ref, b_ref, o_ref):
    y = jnp.concatenate([ya[...], yb[...], yc[...], yd[...]], axis=1)
    y = jnp.dot(y, w_ref[...], preferred_element_type=F32)
    o_ref[...] = _layernorm(ALPHA * x_ref[...] + y, g_ref[...], b_ref[...])


def _outproj(ya, yb, yc, yd, x2, w_out, g, b):
    T = x2.shape[0]
    tm = 512
    ysp = pl.BlockSpec((tm, GROUP_W), lambda i: (i, 0))
    vec = pl.BlockSpec((1, D_MODEL), lambda i: (0, 0))
    return pl.pallas_call(
        _outproj_kernel,
        grid=(T // tm,),
        in_specs=[ysp, ysp, ysp, ysp,
                  pl.BlockSpec((tm, D_MODEL), lambda i: (i, 0)),
                  pl.BlockSpec((D_MODEL, D_MODEL), lambda i: (0, 0)), vec, vec],
        out_specs=pl.BlockSpec((tm, D_MODEL), lambda i: (i, 0)),
        out_shape=jax.ShapeDtypeStruct((T, D_MODEL), F32),
        compiler_params=_cparams(("arbitrary",)),
        name="outproj_ln",
    )(ya, yb, yc, yd, x2, w_out, g, b)


def _router_kernel(x_ref, rw_ref, rb_ref, idx_ref, gate_ref, pos_ref, cnt_ref, carry_ref):
    TB = x_ref.shape[0]
    E = N_EXPERTS
    per = E // N_EXPERT_GROUPS

    @pl.when(pl.program_id(0) == 0)
    def _():
        carry_ref[...] = jnp.zeros_like(carry_ref)

    logits = _dot_nt_hi(rw_ref[...], x_ref[...])
    scores = _sigmoid(logits)
    choice = scores + rb_ref[...]
    e_iota = _iota((E, TB), 0)

    l_iota = _iota((per, TB), 0)
    grp = []
    for g in range(N_EXPERT_GROUPS):
        cg = choice[per * g:per * (g + 1), :]
        m1 = jnp.max(cg, axis=0, keepdims=True)
        i1 = jnp.min(jnp.where(cg == m1, l_iota, per), axis=0, keepdims=True)
        m2 = jnp.max(jnp.where(l_iota == i1, -jnp.inf, cg), axis=0, keepdims=True)
        grp.append(m1 + m2)
    gs = jnp.concatenate(grp, axis=0)
    g_iota = _iota((N_EXPERT_GROUPS, TB), 0)
    keep = jnp.zeros((N_EXPERT_GROUPS, TB), jnp.bool_)
    for _ in range(TOPK_GROUPS):
        m = jnp.max(gs, axis=0, keepdims=True)
        gi = jnp.min(jnp.where(gs == m, g_iota, N_EXPERT_GROUPS), axis=0, keepdims=True)
        hit = g_iota == gi
        keep = keep | hit
        gs = jnp.where(hit, -jnp.inf, gs)
    keep_f = jnp.where(keep, 1.0, 0.0)
    keep_e = jnp.concatenate([jnp.broadcast_to(keep_f[g:g + 1, :], (per, TB))
                              for g in range(N_EXPERT_GROUPS)], axis=0)
    cm = jnp.where(keep_e > 0.5, choice, -jnp.inf)

    hits, idxs, gates = [], [], []
    onehot = jnp.zeros((E, TB), F32)
    for _ in range(TOP_K):
        m = jnp.max(cm, axis=0, keepdims=True)
        ei = jnp.min(jnp.where(cm == m, e_iota, E), axis=0, keepdims=True)
        hit = e_iota == ei
        hits.append(hit)
        idxs.append(ei)
        gates.append(jnp.sum(jnp.where(hit, scores, 0.0), axis=0, keepdims=True))
        onehot = onehot + jnp.where(hit, 1.0, 0.0)
        cm = jnp.where(hit, -jnp.inf, cm)
    gate = jnp.concatenate(gates, axis=0)
    gate = gate / jnp.sum(gate, axis=0, keepdims=True) * ROUTED_SCALE

    before = jnp.where(_iota((TB, TB), 0) < _iota((TB, TB), 1), 1.0, 0.0).astype(BF16)
    cnt = carry_ref[...] + _dot(onehot, before)
    pos = jnp.concatenate([jnp.sum(jnp.where(h, cnt, 0.0), axis=0, keepdims=True) for h in hits], axis=0)
    carry_ref[...] = carry_ref[...] + jnp.sum(onehot, axis=1, keepdims=True)

    idx_ref[...] = jnp.concatenate(idxs, axis=0)
    gate_ref[...] = gate
    pos_ref[...] = pos.astype(I32)
    cnt_ref[...] = jnp.broadcast_to(carry_ref[...], cnt_ref.shape)


def _router(x1, rw_t, rb):
    T = x1.shape[0]
    TB = TOK_BLK
    kt = pl.BlockSpec((TOP_K, TB), lambda i: (0, i))
    return pl.pallas_call(
        _router_kernel,
        grid=(T // TB,),
        in_specs=[pl.BlockSpec((TB, D_MODEL), lambda i: (i, 0)),
                  pl.BlockSpec((N_EXPERTS, D_MODEL), lambda i: (0, 0)),
                  pl.BlockSpec((N_EXPERTS, 1), lambda i: (0, 0))],
        out_specs=[kt, kt, kt, pl.BlockSpec((N_EXPERTS, 128), lambda i: (0, 0))],
        out_shape=[jax.ShapeDtypeStruct((TOP_K, T), I32), jax.ShapeDtypeStruct((TOP_K, T), F32),
                   jax.ShapeDtypeStruct((TOP_K, T), I32), jax.ShapeDtypeStruct((N_EXPERTS, 128), F32)],
        scratch_shapes=[pltpu.VMEM((N_EXPERTS, 1), F32)],
        compiler_params=_cparams(("arbitrary",)),
        name="router",
    )(x1, rw_t, rb)


def _slots_kernel(ps_ref, idx_ref, pos_ref, dest_ref):
    idx = idx_ref[...]

    def body(e, acc):
        return acc + jnp.where(idx == e, ps_ref[e], 0)

    dest_ref[...] = lax.fori_loop(0, N_EXPERTS, body, pos_ref[...], unroll=8)


def _slots(pad_start, idx, pos):
    T = idx.shape[1]
    tb = min(T, 2048)
    kt = pl.BlockSpec((TOP_K, tb), lambda i, ps: (0, i))
    return pl.pallas_call(
        _slots_kernel,
        grid_spec=pltpu.PrefetchScalarGridSpec(
            num_scalar_prefetch=1, grid=(T // tb,), in_specs=[kt, kt], out_specs=kt),
        out_shape=jax.ShapeDtypeStruct((TOP_K, T), I32),
        compiler_params=_cparams(("arbitrary",)),
        name="slots",
    )(pad_start, idx, pos)


ROW_T = D_MODEL // 2 // 128
U32 = jnp.uint32


def _pack_rows(x):
    bits = lax.bitcast_convert_type(x.astype(BF16).astype(F32), U32)
    half = D_MODEL // 2
    return (bits[:, :half] >> 16) | (bits[:, half:] & U32(0xFFFF0000))


def _unpack_rows(w):
    lo = lax.bitcast_convert_type(w << 16, F32)
    hi = lax.bitcast_convert_type(w & U32(0xFFFF0000), F32)
    return jnp.concatenate([lo, hi], axis=1)


def _to_tiles(dst_ref, base, w):
    n = w.shape[0]
    for s in range(ROW_T):
        dst_ref[pl.ds(base + s, n, stride=ROW_T), :] = w[:, 128 * s:128 * (s + 1)]


def _from_tiles(src_ref, base, n):
    return jnp.concatenate([src_ref[pl.ds(base + s, n, stride=ROW_T), :] for s in range(ROW_T)], axis=1)


def _tailzero_kernel(tb_ref, xs_ref):
    del tb_ref
    xs_ref[...] = jnp.zeros_like(xs_ref)


def _tailzero(tail_blk, nblk):
    rows = MOE_BLK * ROW_T
    return pl.pallas_call(
        _tailzero_kernel,
        grid_spec=pltpu.PrefetchScalarGridSpec(
            num_scalar_prefetch=1, grid=(N_EXPERTS,), in_specs=[],
            out_specs=pl.BlockSpec((rows, 128), lambda e, tb: (tb[e], 0))),
        out_shape=jax.ShapeDtypeStruct((nblk * rows, 128), U32),
        compiler_params=_cparams(("arbitrary",)),
        name="tailzero",
    )(tail_blk)


def _dispatch_kernel(dest_ref, x_ref, xs_in, xs_out, xt, sem):
    del xs_in
    TB = x_ref.shape[0]
    _to_tiles(xt, 0, _pack_rows(x_ref[...]))

    def issue(t, c):
        src = xt.at[pl.ds(pl.multiple_of(t * ROW_T, ROW_T), ROW_T), :]
        for k in range(TOP_K):
            d = dest_ref[t * TOP_K + k]
            pltpu.make_async_copy(src, xs_out.at[pl.ds(pl.multiple_of(d * ROW_T, ROW_T), ROW_T), :], sem).start()
        return c

    lax.fori_loop(0, TB, issue, 0, unroll=2)

    for k in range(TOP_K):
        pltpu.make_async_copy(xt, xs_out.at[pl.ds(0, TB * ROW_T), :], sem).wait()


def _dispatch(dest_flat, x1, xs0):
    T = x1.shape[0]
    TB = DSP_BLK
    return pl.pallas_call(
        _dispatch_kernel,
        grid=(T // TB,),
        in_specs=[pl.BlockSpec((TB * TOP_K,), lambda i: (i,), memory_space=pltpu.SMEM),
                  pl.BlockSpec((TB, D_MODEL), lambda i: (i, 0)),
                  pl.BlockSpec(memory_space=pl.ANY)],
        out_specs=pl.BlockSpec(memory_space=pl.ANY),
        out_shape=jax.ShapeDtypeStruct(xs0.shape, xs0.dtype),
        scratch_shapes=[pltpu.VMEM((TB * ROW_T, 128), U32), pltpu.SemaphoreType.DMA],
        input_output_aliases={2: 0},
        compiler_params=_cparams(("arbitrary",)),
        name="dispatch",
    )(dest_flat, x1, xs0)


def _experts_kernel(be_ref, nu_ref, xs_ref, wg_ref, wu_ref, wd_ref, ys_ref, wgu_s, wd_s):
    i = pl.program_id(0)

    @pl.when(i < nu_ref[0])
    def _():
        prev = be_ref[jnp.maximum(i - 1, 0)]

        @pl.when((i == 0) | (be_ref[i] != prev))
        def _():
            wgu_s[:, 0:D_EXPERT] = wg_ref[0, 0].astype(BF16)
            wgu_s[:, D_EXPERT:] = wu_ref[0, 0].astype(BF16)
            wd_s[...] = wd_ref[0, 0].astype(BF16)

        x = _unpack_rows(_from_tiles(xs_ref, 0, MOE_BLK)).astype(BF16)
        h = jnp.dot(x, wgu_s[...], preferred_element_type=F32)
        a = _silu(h[:, 0:D_EXPERT]) * h[:, D_EXPERT:]
        _to_tiles(ys_ref, 0, _pack_rows(jnp.dot(a.astype(BF16), wd_s[...], preferred_element_type=F32)))


def _experts(layer, blk_expert, n_used, xs, wg, wu, wd):
    rows = MOE_BLK * ROW_T
    nblk = xs.shape[0] // rows
    row = lambda i, be, nu: (jnp.minimum(i, nu[0] - 1), 0)
    wsel = lambda i, be, nu: (layer, be[i], 0, 0)
    return pl.pallas_call(
        _experts_kernel,
        grid_spec=pltpu.PrefetchScalarGridSpec(
            num_scalar_prefetch=2,
            grid=(nblk,),
            in_specs=[pl.BlockSpec((rows, 128), row),
                      pl.BlockSpec((1, 1, D_MODEL, D_EXPERT), wsel),
                      pl.BlockSpec((1, 1, D_MODEL, D_EXPERT), wsel),
                      pl.BlockSpec((1, 1, D_EXPERT, D_MODEL), wsel)],
            out_specs=pl.BlockSpec((rows, 128), row),
            scratch_shapes=[pltpu.VMEM((D_MODEL, 2 * D_EXPERT), BF16),
                            pltpu.VMEM((D_EXPERT, D_MODEL), BF16)]),
        out_shape=jax.ShapeDtypeStruct(xs.shape, U32),
        compiler_params=_cparams(("arbitrary",)),
        name="experts",
    )(blk_expert, n_used, xs, wg, wu, wd)


def _combine_kernel(dest_ref, gate_ref, x_ref, ys_ref, sgu_ref, sd_ref, g_ref, b_ref, o_ref, buf, sem):
    TB = x_ref.shape[0]

    def issue(t, c):
        for k in range(TOP_K):
            d = dest_ref[t * TOP_K + k]
            pltpu.make_async_copy(ys_ref.at[pl.ds(pl.multiple_of(d * ROW_T, ROW_T), ROW_T), :],
                                  buf.at[pl.ds(pl.multiple_of((k * TB + t) * ROW_T, ROW_T), ROW_T), :], sem).start()
        return c

    lax.fori_loop(0, TB, issue, 0, unroll=2)

    x = x_ref[...]
    h = jnp.dot(x.astype(BF16), sgu_ref[...], preferred_element_type=F32)
    a = _silu(h[:, 0:D_EXPERT]) * h[:, D_EXPERT:]
    y = ALPHA * x + jnp.dot(a.astype(BF16), sd_ref[...], preferred_element_type=F32)

    pltpu.make_async_copy(ys_ref.at[pl.ds(0, TOP_K * TB * ROW_T), :], buf, sem).wait()

    gate = gate_ref[...]
    for k in range(TOP_K):
        y = y + gate[:, k:k + 1] * _unpack_rows(_from_tiles(buf, k * TB * ROW_T, TB))
    o_ref[...] = _layernorm(y, g_ref[...], b_ref[...])


def _combine(dest_flat, gate_t, x1, ys, sh_gu, sh_d, g, b):
    T = x1.shape[0]
    TB = CMB_BLK
    vec = pl.BlockSpec((1, D_MODEL), lambda i: (0, 0))
    return pl.pallas_call(
        _combine_kernel,
        grid=(T // TB,),
        in_specs=[pl.BlockSpec((TB * TOP_K,), lambda i: (i,), memory_space=pltpu.SMEM),
                  pl.BlockSpec((TB, TOP_K), lambda i: (i, 0)),
                  pl.BlockSpec((TB, D_MODEL), lambda i: (i, 0)),
                  pl.BlockSpec(memory_space=pl.ANY),
                  pl.BlockSpec((D_MODEL, 2 * D_EXPERT), lambda i: (0, 0)),
                  pl.BlockSpec((D_EXPERT, D_MODEL), lambda i: (0, 0)), vec, vec],
        out_specs=pl.BlockSpec((TB, D_MODEL), lambda i: (i, 0)),
        out_shape=jax.ShapeDtypeStruct((T, D_MODEL), F32),
        scratch_shapes=[pltpu.VMEM((TOP_K * TB * ROW_T, 128), U32), pltpu.SemaphoreType.DMA],
        compiler_params=_cparams(("arbitrary",)),
        name="combine_ln",
    )(dest_flat, gate_t, x1, ys, sh_gu, sh_d, g, b)


def _token_mixer(x2, B, S, w_p, gla_p, sgu_p, ssd_p, cos_t, sin_t, tb, w_out, ln_g, ln_b):
    s_gla, s_sgu, s_ssd, s_dil = _inproj(x2, w_p)
    ya = _gla(s_gla.reshape(B, S, W_GLA), *gla_p)
    yb = _sgu(s_sgu.reshape(B, S, W_SGU), *sgu_p)
    yc = _ssd(s_ssd.reshape(B, S, W_SSD), *ssd_p)
    yd = _dil(s_dil.reshape(B, S, W_DIL), cos_t, sin_t, tb)
    T = B * S
    flat = lambda y: y.reshape(T, GROUP_W)
    return _outproj(flat(ya), flat(yb), flat(yc), flat(yd), x2, w_out, ln_g, ln_b)


def _moe(layer, x1, rw_t, rb, wg, wu, wd, sh_gu, sh_d, ln_g, ln_b):
    T = x1.shape[0]
    idx, gate, pos, cnt = _router(x1, rw_t, rb)
    counts = cnt[:, 0].astype(I32)
    padded = (counts + MOE_BLK - 1) // MOE_BLK * MOE_BLK
    pad_end = jnp.cumsum(padded)
    pad_start = pad_end - padded
    dest = _slots(pad_start.astype(I32), idx, pos)
    dest_flat = dest.T.reshape(T * TOP_K)
    nblk = T * TOP_K // MOE_BLK + N_EXPERTS
    n_used = (pad_end[-1] // MOE_BLK).astype(I32)
    blk = jnp.minimum(jnp.arange(nblk, dtype=I32), n_used - 1) * MOE_BLK
    blk_expert = jnp.sum((pad_end[None, :] <= blk[:, None]).astype(I32), axis=1)
    blk_expert = jnp.minimum(blk_expert, N_EXPERTS - 1)
    tail_blk = jnp.maximum(pad_end // MOE_BLK - 1, 0).astype(I32)
    xs = _dispatch(dest_flat, x1, _tailzero(tail_blk, nblk))
    ys = _experts(layer, blk_expert, n_used.reshape(1), xs, wg, wu, wd)
    return _combine(dest_flat, gate.T, x1, ys, sh_gu, sh_d, ln_g, ln_b)


def _prep_w_in(w_in):
    o = np.cumsum((0, 128, 128, 256, 256, 16, 256, 256, 256, 512, 4, 256, 256, 256))
    c = lambda i: w_in[..., o[i]:o[i + 1]]
    xbc = c(8)
    x_cols = xbc[..., 0:256]
    per_head = lambda m: jnp.concatenate(
        [m[..., SSM_N * (h // (SSM_H // SSM_G)):SSM_N * (h // (SSM_H // SSM_G) + 1)] for h in range(SSM_H)], axis=-1)
    b_cols = per_head(xbc[..., 256:384])
    c_cols = per_head(xbc[..., 384:512])
    dt_cols = jnp.repeat(c(9), SSM_P, axis=-1)
    pad = jnp.zeros(w_in.shape[:-1] + (W_GLA - o[5],), w_in.dtype)
    cols = [c(0), c(1), c(2), c(3), c(4), pad, c(5), c(6), c(7), x_cols, b_cols, c_cols, dt_cols,
            c(10), c(11), c(12)]
    return jnp.concatenate(cols, axis=-1).astype(BF16)


def _per_head(v):
    rep = SSM_H // SSM_G
    xp = v[..., 0:256]
    pick = lambda m: jnp.concatenate([m[..., SSM_N * (h // rep):SSM_N * (h // rep + 1)] for h in range(SSM_H)], axis=-1)
    return jnp.concatenate([xp, pick(v[..., 256:384]), pick(v[..., 384:512])], axis=-1)


def kernel(x, positions, w_in, gla_w_gate, gla_b_gate, gla_norm_w, sgu_ln_g, sgu_ln_b, sgu_w, sgu_b, ssm_conv_w, ssm_conv_b, ssm_dt_bias, ssm_a_log, ssm_d, ssm_norm_w, w_out, ln1_g, ln1_b, router_w, router_bias, exp_w_gate, exp_w_up, exp_w_down, sh_w_gate, sh_w_up, sh_w_down, ln2_g, ln2_b):
    B, S, _ = x.shape
    L = w_in.shape[0]
    T = B * S

    inv_freq = ROPE_THETA ** (-jnp.arange(0, ROT_DIM, 2, dtype=F32) / ROT_DIM)
    ang = positions.astype(F32)[..., None] * inv_freq
    cos, sin = jnp.cos(ang), jnp.sin(ang)
    ones = jnp.ones((B, S, DIL_DH - ROT_DIM), F32)
    cos_t = jnp.tile(jnp.concatenate([cos, cos, ones], axis=-1), (1, 1, 2))
    sin_t = jnp.tile(jnp.concatenate([-sin, sin, 0.0 * ones], axis=-1), (1, 1, 2))
    tb = jnp.asarray(_branch_log_multiplicity(S))

    w_p = _prep_w_in(w_in)
    wg_p = jnp.pad(gla_w_gate, ((0, 0), (0, 128 - GLA_RANK), (0, 0)))
    row = lambda v: v.reshape(L, 1, -1)
    gla_nw = jnp.tile(gla_norm_w, (1, GLA_H))
    sgu_wc = jnp.transpose(sgu_w, (0, 2, 1, 3)).reshape(L, SGU_CHUNK, SGU_G * SGU_CHUNK)
    sgu_bt = jnp.repeat(jnp.transpose(sgu_b, (0, 2, 1)), GROUP_W // SGU_G, axis=-1)
    exp64 = lambda v: jnp.repeat(v, SSM_P, axis=-1)
    w_out_b = w_out.astype(BF16)
    rw_t = jnp.transpose(router_w, (0, 2, 1))
    sh_gu = jnp.concatenate([sh_w_gate, sh_w_up], axis=-1).astype(BF16)
    sh_d = sh_w_down.astype(BF16)

    x2 = x.reshape(T, D_MODEL)
    for l in range(L):
        gla_p = (wg_p[l], row(gla_b_gate)[l], row(gla_nw)[l])
        sgu_p = (sgu_wc[l], sgu_bt[l], row(sgu_ln_g)[l], row(sgu_ln_b)[l])
        ssd_p = (_per_head(ssm_conv_w[l]), row(_per_head(ssm_conv_b))[l], row(exp64(ssm_dt_bias))[l],
                 row(exp64(ssm_a_log))[l], row(exp64(ssm_d))[l], row(ssm_norm_w)[l])
        x1 = _token_mixer(x2, B, S, w_p[l], gla_p, sgu_p, ssd_p, cos_t, sin_t, tb,
                          w_out_b[l], row(ln1_g)[l], row(ln1_b)[l])
        x2 = _moe(l, x1, rw_t[l], router_bias[l].reshape(N_EXPERTS, 1), exp_w_gate, exp_w_up, exp_w_down,
                  sh_gu[l], sh_d[l], row(ln2_g)[l], row(ln2_b)[l])
    return x2.reshape(B, S, D_MODEL)
```

```python
import numpy as np
import jax
import jax.numpy as jnp
from jax import lax
from jax.experimental import pallas as pl
from jax.experimental.pallas import tpu as pltpu

F32 = jnp.float32
BF16 = jnp.bfloat16
I32 = jnp.int32

D_MODEL = 1024
N_LAYERS = 4
GROUP_W = 256

GLA_H, GLA_DK, GLA_DV, GLA_RANK, GLA_TAU = 4, 32, 64, 16, 16.0
SGU_G, SGU_CHUNK = 4, 128
SSM_H, SSM_P, SSM_G, SSM_N, SSM_CONV = 4, 64, 2, 64, 4
DIL_H, DIL_DH, ROT_DIM, ROPE_THETA = 4, 64, 16, 500000.0
DIL_BRANCHES = ((128, 1), (512, 4), (2048, 16))

N_EXPERTS, TOP_K, N_EXPERT_GROUPS, TOPK_GROUPS, D_EXPERT = 128, 8, 8, 4, 256
ROUTED_SCALE = 1.0

ALPHA = (2 * N_LAYERS) ** 0.25
LN_EPS = 1e-5
RMS_EPS = 1e-6

W_GLA = 896
W_SGU = 512
W_SSD = 1280
W_DIL = 768
W_PROJ = W_GLA + W_SGU + W_SSD + W_DIL

BLK = 128
QBLK = 256
MOE_BLK = 1024
TOK_BLK = 256
DSP_BLK = 1024
CMB_BLK = 512
NEG = -1e30

VMEM_LIMIT = 56 * 1024 * 1024


def _cparams(sem):
    return pltpu.CompilerParams(dimension_semantics=sem, vmem_limit_bytes=VMEM_LIMIT)


def _dot(a, b):
    return jnp.dot(a.astype(BF16), b.astype(BF16), preferred_element_type=F32)


def _dot_nt(a, b):
    return lax.dot_general(a.astype(BF16), b.astype(BF16), (((1,), (1,)), ((), ())),
                           preferred_element_type=F32)


def _dot_tn(a, b):
    return lax.dot_general(a.astype(BF16), b.astype(BF16), (((0,), (0,)), ((), ())),
                           preferred_element_type=F32)


def _split(a, n):
    out = []
    r = a
    for _ in range(n):
        p = r.astype(BF16)
        out.append(p)
        r = r - p.astype(F32)
    return out


def _dot_hi(a, b):
    a1, a2 = _split(a, 2)
    b1, b2 = _split(b, 2)
    return _dot(a1, b1) + (_dot(a1, b2) + _dot(a2, b1))


def _dot_nt_hi(a, b):
    a1, a2 = _split(a, 2)
    b1, b2 = _split(b, 2)
    return _dot_nt(a1, b1) + (_dot_nt(a1, b2) + _dot_nt(a2, b1))


def _dot_lhs01(m, b):
    b1, b2, b3 = _split(b, 3)
    return _dot(m, b1) + (_dot(m, b2) + _dot(m, b3))


def _dot_rhs01(a, m):
    a1, a2, a3 = _split(a, 3)
    return _dot(a1, m) + (_dot(a2, m) + _dot(a3, m))


def _iota(shape, dim):
    return lax.broadcasted_iota(I32, shape, dim)


def _sigmoid(x):
    return 1.0 / (1.0 + jnp.exp(-x))


def _silu(x):
    return x * _sigmoid(x)


def _softplus(x):
    return jnp.maximum(x, 0.0) + jnp.log1p(jnp.exp(-jnp.abs(x)))


def _gelu(x):
    return 0.5 * x * (1.0 + lax.erf(x * (2.0 ** -0.5)))


def _inproj_kernel(x_ref, w_ref, o_gla, o_sgu, o_ssd, o_dil):
    xb = x_ref[...].astype(BF16)
    off = 0
    for o in (o_gla, o_sgu, o_ssd, o_dil):
        n = o.shape[1]
        o[...] = jnp.dot(xb, w_ref[:, off:off + n], preferred_element_type=F32)
        off += n


def _inproj(x2, w_p):
    T = x2.shape[0]
    tm = 512
    widths = (W_GLA, W_SGU, W_SSD, W_DIL)
    return pl.pallas_call(
        _inproj_kernel,
        grid=(T // tm,),
        in_specs=[pl.BlockSpec((tm, D_MODEL), lambda i: (i, 0)),
                  pl.BlockSpec((D_MODEL, W_PROJ), lambda i: (0, 0))],
        out_specs=[pl.BlockSpec((tm, w), lambda i: (i, 0)) for w in widths],
        out_shape=[jax.ShapeDtypeStruct((T, w), F32) for w in widths],
        compiler_params=_cparams(("arbitrary",)),
        name="inproj",
    )(x2, w_p)


GLA_NB = 2
GLA_SUB = 16


def _gla_kernel(slab_ref, wg_ref, bg_ref, nw_ref, out_ref, st_ref, a_ref):
    S = slab_ref.shape[1]
    st_ref[...] = jnp.zeros_like(st_ref)
    row = _iota((BLK, BLK), 0)
    col = _iota((BLK, BLK), 1)
    tril = col <= row
    tril_bf = jnp.where(tril, 1.0, 0.0).astype(BF16)
    sub_row = _iota((GLA_SUB, BLK), 0)
    sub_col = _iota((GLA_SUB, BLK), 1)
    sub_head = sub_col >> 5
    v_head = _iota((BLK, GROUP_W), 1) >> 6
    st_diag = (_iota((GROUP_W, BLK), 0) >> 6) == (_iota((GROUP_W, BLK), 1) >> 5)
    seg = jnp.where((_iota((GROUP_W, GROUP_W), 0) >> 6) == (_iota((GROUP_W, GROUP_W), 1) >> 6),
                    1.0, 0.0).astype(BF16)
    wg = wg_ref[...]
    bg = bg_ref[...]
    nw = nw_ref[...]

    def block(b, r0):
        blk = slab_ref[b, pl.ds(r0, BLK), :]
        q = blk[:, 0:128] * (GLA_DK ** -0.5)
        k = blk[:, 128:256]
        v = blk[:, 256:512]
        r = blk[:, 512:768]
        lr = blk[:, 768:896]
        z = _dot_hi(lr, wg) + bg
        gk = (jnp.minimum(z, 0.0) - jnp.log1p(jnp.exp(-jnp.abs(z)))) * (1.0 / GLA_TAU)
        g = _dot_lhs01(tril_bf, gk)
        g_last = g[BLK - 1:BLK, :]
        q_in = q * jnp.exp(g)
        k_in = k * jnp.exp(g_last - g)
        vb = v.astype(BF16)
        o = _dot_nt(q_in, st_ref[b])
        for c in range(BLK // GLA_SUB):
            lo = c * GLA_SUB
            n_c = g[lo - 1:lo, :] if c else jnp.zeros((1, BLK), F32)
            qc = q[lo:lo + GLA_SUB, :] * jnp.exp(g[lo:lo + GLA_SUB, :] - n_c)
            kc = k * jnp.exp(jnp.where(row < lo + GLA_SUB, n_c - g, -jnp.inf))
            qs = jnp.concatenate([jnp.where(sub_head == h, qc, 0.0) for h in range(GLA_H)], axis=0)
            sc = _dot_nt(qs, kc)
            causal = sub_col <= sub_row + lo
            for h in range(GLA_H):
                a_ref[b, h, lo:lo + GLA_SUB, :] = jnp.where(causal, sc[h * GLA_SUB:(h + 1) * GLA_SUB, :], 0.0)
        for h in range(GLA_H):
            o = o + jnp.where(v_head == h, _dot(a_ref[b, h], vb), 0.0)
        st_ref[b] = st_ref[b] * jnp.exp(g_last) + jnp.where(st_diag, _dot_tn(vb, k_in), 0.0)
        ms = _dot_rhs01(o * o, seg) * (1.0 / GLA_DV)
        y = o * lax.rsqrt(ms + RMS_EPS) * nw * _silu(r)
        out_ref[b, pl.ds(r0, BLK), :] = y.astype(out_ref.dtype)

    def body(n, carry):
        r0 = pl.multiple_of(n * BLK, BLK)
        for b in range(slab_ref.shape[0]):
            block(b, r0)
        return carry

    lax.fori_loop(0, S // BLK, body, 0)


def _gla(slab, wg_p, bg, nw_t):
    B, S, _ = slab.shape
    nb = GLA_NB if B % GLA_NB == 0 else 1
    return pl.pallas_call(
        _gla_kernel,
        grid=(B // nb,),
        in_specs=[pl.BlockSpec((nb, S, W_GLA), lambda b: (b, 0, 0)),
                  pl.BlockSpec((128, 128), lambda b: (0, 0)),
                  pl.BlockSpec((1, 128), lambda b: (0, 0)),
                  pl.BlockSpec((1, GROUP_W), lambda b: (0, 0))],
        out_specs=pl.BlockSpec((nb, S, GROUP_W), lambda b: (b, 0, 0)),
        out_shape=jax.ShapeDtypeStruct((B, S, GROUP_W), BF16),
        scratch_shapes=[pltpu.VMEM((nb, GROUP_W, 128), F32), pltpu.VMEM((nb, GLA_H, BLK, BLK), F32)],
        compiler_params=_cparams(("arbitrary",)),
        name="gla",
    )(slab, wg_p, bg, nw_t)


def _sgu_kernel(slab_ref, w_ref, bias_ref, g_ref, b_ref, out_ref):
    S = slab_ref.shape[1]
    wmask = (_iota((BLK, SGU_G * BLK), 1) & (BLK - 1)) <= _iota((BLK, SGU_G * BLK), 0)
    w = jnp.where(wmask, w_ref[...], 0.0).astype(BF16)
    lane_grp = _iota((BLK, GROUP_W), 1) >> 6
    bias = bias_ref[...]
    ln_g = g_ref[...]
    ln_b = b_ref[...]

    def body(n, carry):
        r0 = pl.multiple_of(n * BLK, BLK)
        blk = slab_ref[0, pl.ds(r0, BLK), :]
        u = _gelu(blk[:, 0:256])
        v = _gelu(blk[:, 256:512])
        mu = jnp.mean(v, axis=-1, keepdims=True)
        var = jnp.mean(jnp.square(v - mu), axis=-1, keepdims=True)
        v = (v - mu) * lax.rsqrt(var + LN_EPS) * ln_g + ln_b
        vexp = jnp.concatenate([jnp.where(lane_grp == g, v, 0.0) for g in range(SGU_G)], axis=0)
        s = _dot(w, vexp) + bias
        out_ref[0, pl.ds(r0, BLK), :] = (u * s).astype(out_ref.dtype)
        return carry

    lax.fori_loop(0, S // BLK, body, 0, unroll=2)


def _sgu(slab, w_cat, bias_t, ln_g, ln_b):
    B, S, _ = slab.shape
    return pl.pallas_call(
        _sgu_kernel,
        grid=(B,),
        in_specs=[pl.BlockSpec((1, S, W_SGU), lambda b: (b, 0, 0)),
                  pl.BlockSpec((BLK, SGU_G * BLK), lambda b: (0, 0)),
                  pl.BlockSpec((BLK, GROUP_W), lambda b: (0, 0)),
                  pl.BlockSpec((1, GROUP_W), lambda b: (0, 0)),
                  pl.BlockSpec((1, GROUP_W), lambda b: (0, 0))],
        out_specs=pl.BlockSpec((1, S, GROUP_W), lambda b: (b, 0, 0)),
        out_shape=jax.ShapeDtypeStruct((B, S, GROUP_W), BF16),
        compiler_params=_cparams(("arbitrary",)),
        name="sgu",
    )(slab, w_cat, bias_t, ln_g, ln_b)


def _ssd_kernel(slab_ref, cw_ref, cb_ref, dtb_ref, a_ref, d_ref, nw_ref, out_ref, xpad_ref, st_ref):
    S = slab_ref.shape[1]
    CW = 3 * GROUP_W
    st_ref[...] = jnp.zeros_like(st_ref)
    xpad_ref[0:8, :] = jnp.zeros((8, CW), F32)
    xpad_ref[8:S + 8, :] = slab_ref[0, :, 256:256 + CW]
    row = _iota((BLK, BLK), 0)
    col = _iota((BLK, BLK), 1)
    tril = col <= row
    tril_bf = jnp.where(tril, 1.0, 0.0).astype(BF16)
    lane_head = _iota((BLK, GROUP_W), 1) >> 6
    st_diag = (_iota((GROUP_W, GROUP_W), 0) >> 6) == (_iota((GROUP_W, GROUP_W), 1) >> 6)
    cw = cw_ref[...]
    cb = cb_ref[...]
    dtb = dtb_ref[...]
    a_neg = -jnp.exp(a_ref[...])
    d_skip = d_ref[...]
    nw = nw_ref[...]

    def body(n, carry):
        r0 = pl.multiple_of(n * BLK, BLK)
        xw = xpad_ref[pl.ds(r0, BLK + 8), :]
        conv = cb + cw[0:1, :] * xw[5:5 + BLK, :]
        for w in range(1, SSM_CONV):
            conv = conv + cw[w:w + 1, :] * xw[5 + w:5 + w + BLK, :]
        xbc = _silu(conv)
        xs = xbc[:, 0:256]
        b_e = xbc[:, 256:512]
        c_e = xbc[:, 512:768]
        z = slab_ref[0, pl.ds(r0, BLK), 0:256]
        dt = _softplus(slab_ref[0, pl.ds(r0, BLK), 1024:1280] + dtb)
        acs = _dot_lhs01(tril_bf, dt * a_neg)
        acs_last = acs[BLK - 1:BLK, :]
        x_dt = (xs * dt).astype(BF16)
        acs_t = [jnp.transpose(acs[:, 0:128]), jnp.transpose(acs[:, 128:256])]
        y = _dot(c_e * jnp.exp(acs), st_ref[...])
        b_bf = b_e.astype(BF16)
        for h in range(SSM_H):
            a_col = jnp.broadcast_to(acs[:, 64 * h:64 * h + 1], (BLK, BLK))
            a_row = jnp.broadcast_to(acs_t[h // 2][64 * (h % 2):64 * (h % 2) + 1, :], (BLK, BLK))
            lm = jnp.exp(jnp.where(tril, a_col - a_row, -jnp.inf))
            ch = jnp.where(lane_head == h, c_e, 0.0)
            m = _dot_nt(ch, b_bf) * lm
            y = y + jnp.where(lane_head == h, _dot(m, x_dt), 0.0)
        new = _dot_tn(b_e * jnp.exp(acs_last - acs), x_dt)
        st_ref[...] = st_ref[...] * jnp.exp(acs_last) + jnp.where(st_diag, new, 0.0)
        y = (y + d_skip * xs) * _silu(z)
        parts = []
        for g in range(SSM_G):
            yg = y[:, 128 * g:128 * (g + 1)]
            ms = jnp.mean(yg * yg, axis=-1, keepdims=True)
            parts.append(yg * lax.rsqrt(ms + RMS_EPS))
        out = jnp.concatenate(parts, axis=1) * nw
        out_ref[0, pl.ds(r0, BLK), :] = out.astype(out_ref.dtype)
        return carry

    lax.fori_loop(0, S // BLK, body, 0, unroll=2)


def _ssd(slab, cw_e, cb_e, dtb_e, a_e, d_e, nw):
    B, S, _ = slab.shape
    CW = 3 * GROUP_W
    vec = lambda w: pl.BlockSpec((1, w), lambda b: (0, 0))
    return pl.pallas_call(
        _ssd_kernel,
        grid=(B,),
        in_specs=[pl.BlockSpec((1, S, W_SSD), lambda b: (b, 0, 0)),
                  pl.BlockSpec((SSM_CONV, CW), lambda b: (0, 0)),
                  vec(CW), vec(GROUP_W), vec(GROUP_W), vec(GROUP_W), vec(GROUP_W)],
        out_specs=pl.BlockSpec((1, S, GROUP_W), lambda b: (b, 0, 0)),
        out_shape=jax.ShapeDtypeStruct((B, S, GROUP_W), BF16),
        scratch_shapes=[pltpu.VMEM((S + 8, CW), F32), pltpu.VMEM((GROUP_W, GROUP_W), F32)],
        compiler_params=_cparams(("arbitrary",)),
        name="ssd",
    )(slab, cw_e, cb_e, dtb_e, a_e, d_e, nw)


def _branch_log_multiplicity(S):
    nq = S // QBLK
    r = np.arange(QBLK)[:, None]
    u = np.arange(S)[None, :]
    d = (nq - 1) * QBLK + r - u
    c = np.zeros(d.shape, np.float64)
    for window, dil in DIL_BRANCHES:
        c += (d >= 0) & (d % dil == 0) & (d <= window)
    with np.errstate(divide="ignore"):
        return np.where(c > 0, np.log(np.maximum(c, 1.0)), NEG).astype(np.float32)


def _dil_kernel(slab_ref, cos_ref, sin_ref, tb_ref, out_ref, q_ref, k_ref, v_ref):
    S = slab_ref.shape[1]
    nq = S // QBLK
    lane_d = _iota((1, 128), 1) & (DIL_DH - 1)

    def rope(t):
        fwd = pltpu.roll(t, 128 - ROT_DIM // 2, 1)
        bwd = pltpu.roll(t, ROT_DIM // 2, 1)
        return t * cos_ref[0] + jnp.where(lane_d < ROT_DIM // 2, fwd, bwd) * sin_ref[0]

    for c in range(0, GROUP_W, 128):
        q_ref[:, c:c + 128] = (rope(slab_ref[0, :, c:c + 128]) * (DIL_DH ** -0.5)).astype(BF16)
        k_ref[:, c:c + 128] = rope(slab_ref[0, :, 256 + c:384 + c]).astype(BF16)
    v_ref[...] = slab_ref[0, :, 512:768].astype(BF16)
    lane_head = _iota((QBLK, GROUP_W), 1) >> 6
    for i in range(nq):
        nk = (i + 1) * QBLK
        qi = q_ref[i * QBLK:(i + 1) * QBLK, :]
        acc = jnp.zeros((QBLK, GROUP_W), F32)
        for h in range(DIL_H):
            qh = jnp.where(lane_head == h, qi, jnp.zeros_like(qi))
            s = _dot_nt(qh, k_ref[0:nk, :]) + tb_ref[:, (nq - 1 - i) * QBLK:]
            m = jnp.max(s, axis=-1, keepdims=True)
            p = jnp.exp(s - m)
            den = jnp.sum(p, axis=-1, keepdims=True)
            o = _dot(p, v_ref[0:nk, :])
            acc = acc + jnp.where(lane_head == h, o / den, 0.0)
        out_ref[0, i * QBLK:(i + 1) * QBLK, :] = acc.astype(out_ref.dtype)


def _dil(slab, cos_t, sin_t, tb):
    B, S, _ = slab.shape
    return pl.pallas_call(
        _dil_kernel,
        grid=(B,),
        in_specs=[pl.BlockSpec((1, S, W_DIL), lambda b: (b, 0, 0)),
                  pl.BlockSpec((1, S, 128), lambda b: (b, 0, 0)),
                  pl.BlockSpec((1, S, 128), lambda b: (b, 0, 0)),
                  pl.BlockSpec((QBLK, S), lambda b: (0, 0), pipeline_mode=pl.Buffered(1))],
        out_specs=pl.BlockSpec((1, S, GROUP_W), lambda b: (b, 0, 0)),
        out_shape=jax.ShapeDtypeStruct((B, S, GROUP_W), BF16),
        scratch_shapes=[pltpu.VMEM((S, GROUP_W), BF16)] * 3,
        compiler_params=_cparams(("arbitrary",)),
        name="dil",
    )(slab, cos_t, sin_t, tb)


def _layernorm(x, g, b):
    mu = jnp.mean(x, axis=-1, keepdims=True)
    var = jnp.mean(jnp.square(x - mu), axis=-1, keepdims=True)
    return (x - mu) * lax.rsqrt(var + LN_EPS) * g + b


def _outproj_kernel(ya, yb, yc, yd, x_ref, w_ref, g_ref, b_ref, o_ref):
    y = jnp.concatenate([ya[...], yb[...], yc[...], yd[...]], axis=1)
    y = jnp.dot(y, w_ref[...], preferred_element_type=F32)
    o_ref[...] = _layernorm(ALPHA * x_ref[...] + y, g_ref[...], b_ref[...])


def _outproj(ya, yb, yc, yd, x2, w_out, g, b):
    T = x2.shape[0]
    tm = 512
    ysp = pl.BlockSpec((tm, GROUP_W), lambda i: (i, 0))
    vec = pl.BlockSpec((1, D_MODEL), lambda i: (0, 0))
    return pl.pallas_call(
        _outproj_kernel,
        grid=(T // tm,),
        in_specs=[ysp, ysp, ysp, ysp,
                  pl.BlockSpec((tm, D_MODEL), lambda i: (i, 0)),
                  pl.BlockSpec((D_MODEL, D_MODEL), lambda i: (0, 0)), vec, vec],
        out_specs=pl.BlockSpec((tm, D_MODEL), lambda i: (i, 0)),
        out_shape=jax.ShapeDtypeStruct((T, D_MODEL), F32),
        compiler_params=_cparams(("arbitrary",)),
        name="outproj_ln",
    )(ya, yb, yc, yd, x2, w_out, g, b)


def _router_kernel(x_ref, rw_ref, rb_ref, idx_ref, gate_ref, pos_ref, cnt_ref, carry_ref):
    TB = x_ref.shape[0]
    E = N_EXPERTS
    per = E // N_EXPERT_GROUPS

    @pl.when(pl.program_id(0) == 0)
    def _():
        carry_ref[...] = jnp.zeros_like(carry_ref)

    logits = _dot_nt_hi(rw_ref[...], x_ref[...])
    scores = _sigmoid(logits)
    choice = scores + rb_ref[...]
    e_iota = _iota((E, TB), 0)

    l_iota = _iota((per, TB), 0)
    grp = []
    for g in range(N_EXPERT_GROUPS):
        cg = choice[per * g:per * (g + 1), :]
        m1 = jnp.max(cg, axis=0, keepdims=True)
        i1 = jnp.min(jnp.where(cg == m1, l_iota, per), axis=0, keepdims=True)
        m2 = jnp.max(jnp.where(l_iota == i1, -jnp.inf, cg), axis=0, keepdims=True)
        grp.append(m1 + m2)
    gs = jnp.concatenate(grp, axis=0)
    g_iota = _iota((N_EXPERT_GROUPS, TB), 0)
    keep = jnp.zeros((N_EXPERT_GROUPS, TB), jnp.bool_)
    for _ in range(TOPK_GROUPS):
        m = jnp.max(gs, axis=0, keepdims=True)
        gi = jnp.min(jnp.where(gs == m, g_iota, N_EXPERT_GROUPS), axis=0, keepdims=True)
        hit = g_iota == gi
        keep = keep | hit
        gs = jnp.where(hit, -jnp.inf, gs)
    keep_f = jnp.where(keep, 1.0, 0.0)
    keep_e = jnp.concatenate([jnp.broadcast_to(keep_f[g:g + 1, :], (per, TB))
                              for g in range(N_EXPERT_GROUPS)], axis=0)
    cm = jnp.where(keep_e > 0.5, choice, -jnp.inf)

    hits, idxs, gates = [], [], []
    onehot = jnp.zeros((E, TB), F32)
    for _ in range(TOP_K):
        m = jnp.max(cm, axis=0, keepdims=True)
        ei = jnp.min(jnp.where(cm == m, e_iota, E), axis=0, keepdims=True)
        hit = e_iota == ei
        hits.append(hit)
        idxs.append(ei)
        gates.append(jnp.sum(jnp.where(hit, scores, 0.0), axis=0, keepdims=True))
        onehot = onehot + jnp.where(hit, 1.0, 0.0)
        cm = jnp.where(hit, -jnp.inf, cm)
    gate = jnp.concatenate(gates, axis=0)
    gate = gate / jnp.sum(gate, axis=0, keepdims=True) * ROUTED_SCALE

    before = jnp.where(_iota((TB, TB), 0) < _iota((TB, TB), 1), 1.0, 0.0).astype(BF16)
    cnt = carry_ref[...] + _dot(onehot, before)
    pos = jnp.concatenate([jnp.sum(jnp.where(h, cnt, 0.0), axis=0, keepdims=True) for h in hits], axis=0)
    carry_ref[...] = carry_ref[...] + jnp.sum(onehot, axis=1, keepdims=True)

    idx_ref[...] = jnp.concatenate(idxs, axis=0)
    gate_ref[...] = gate
    pos_ref[...] = pos.astype(I32)
    cnt_ref[...] = jnp.broadcast_to(carry_ref[...], cnt_ref.shape)


def _router(x1, rw_t, rb):
    T = x1.shape[0]
    TB = TOK_BLK
    kt = pl.BlockSpec((TOP_K, TB), lambda i: (0, i))
    return pl.pallas_call(
        _router_kernel,
        grid=(T // TB,),
        in_specs=[pl.BlockSpec((TB, D_MODEL), lambda i: (i, 0)),
                  pl.BlockSpec((N_EXPERTS, D_MODEL), lambda i: (0, 0)),
                  pl.BlockSpec((N_EXPERTS, 1), lambda i: (0, 0))],
        out_specs=[kt, kt, kt, pl.BlockSpec((N_EXPERTS, 128), lambda i: (0, 0))],
        out_shape=[jax.ShapeDtypeStruct((TOP_K, T), I32), jax.ShapeDtypeStruct((TOP_K, T), F32),
                   jax.ShapeDtypeStruct((TOP_K, T), I32), jax.ShapeDtypeStruct((N_EXPERTS, 128), F32)],
        scratch_shapes=[pltpu.VMEM((N_EXPERTS, 1), F32)],
        compiler_params=_cparams(("arbitrary",)),
        name="router",
    )(x1, rw_t, rb)


def _slots_kernel(ps_ref, idx_ref, pos_ref, dest_ref):
    idx = idx_ref[...]

    def body(e, acc):
        return acc + jnp.where(idx == e, ps_ref[e], 0)

    dest_ref[...] = lax.fori_loop(0, N_EXPERTS, body, pos_ref[...], unroll=8)


def _slots(pad_start, idx, pos):
    T = idx.shape[1]
    tb = min(T, 2048)
    kt = pl.BlockSpec((TOP_K, tb), lambda i, ps: (0, i))
    return pl.pallas_call(
        _slots_kernel,
        grid_spec=pltpu.PrefetchScalarGridSpec(
            num_scalar_prefetch=1, grid=(T // tb,), in_specs=[kt, kt], out_specs=kt),
        out_shape=jax.ShapeDtypeStruct((TOP_K, T), I32),
        compiler_params=_cparams(("arbitrary",)),
        name="slots",
    )(pad_start, idx, pos)


ROW_T = D_MODEL // 2 // 128
U32 = jnp.uint32


def _pack_rows(x):
    bits = lax.bitcast_convert_type(x.astype(BF16).astype(F32), U32)
    half = D_MODEL // 2
    return (bits[:, :half] >> 16) | (bits[:, half:] & U32(0xFFFF0000))


def _unpack_rows(w):
    lo = lax.bitcast_convert_type(w << 16, F32)
    hi = lax.bitcast_convert_type(w & U32(0xFFFF0000), F32)
    return jnp.concatenate([lo, hi], axis=1)


def _to_tiles(dst_ref, base, w):
    n = w.shape[0]
    for s in range(ROW_T):
        dst_ref[pl.ds(base + s, n, stride=ROW_T), :] = w[:, 128 * s:128 * (s + 1)]


def _from_tiles(src_ref, base, n):
    return jnp.concatenate([src_ref[pl.ds(base + s, n, stride=ROW_T), :] for s in range(ROW_T)], axis=1)


def _tailzero_kernel(tb_ref, xs_ref):
    del tb_ref
    xs_ref[...] = jnp.zeros_like(xs_ref)


def _tailzero(tail_blk, nblk):
    rows = MOE_BLK * ROW_T
    return pl.pallas_call(
        _tailzero_kernel,
        grid_spec=pltpu.PrefetchScalarGridSpec(
            num_scalar_prefetch=1, grid=(N_EXPERTS,), in_specs=[],
            out_specs=pl.BlockSpec((rows, 128), lambda e, tb: (tb[e], 0))),
        out_shape=jax.ShapeDtypeStruct((nblk * rows, 128), U32),
        compiler_params=_cparams(("arbitrary",)),
        name="tailzero",
    )(tail_blk)


def _dispatch_kernel(dest_ref, x_ref, xs_in, xs_out, xt, sem):
    del xs_in
    TB = x_ref.shape[0]
    _to_tiles(xt, 0, _pack_rows(x_ref[...]))

    def issue(t, c):
        src = xt.at[pl.ds(pl.multiple_of(t * ROW_T, ROW_T), ROW_T), :]
        for k in range(TOP_K):
            d = dest_ref[t * TOP_K + k]
            pltpu.make_async_copy(src, xs_out.at[pl.ds(pl.multiple_of(d * ROW_T, ROW_T), ROW_T), :], sem).start()
        return c

    lax.fori_loop(0, TB, issue, 0, unroll=2)

    for k in range(TOP_K):
        pltpu.make_async_copy(xt, xs_out.at[pl.ds(0, TB * ROW_T), :], sem).wait()


def _dispatch(dest_flat, x1, xs0):
    T = x1.shape[0]
    TB = DSP_BLK
    return pl.pallas_call(
        _dispatch_kernel,
        grid=(T // TB,),
        in_specs=[pl.BlockSpec((TB * TOP_K,), lambda i: (i,), memory_space=pltpu.SMEM),
                  pl.BlockSpec((TB, D_MODEL), lambda i: (i, 0)),
                  pl.BlockSpec(memory_space=pl.ANY)],
        out_specs=pl.BlockSpec(memory_space=pl.ANY),
        out_shape=jax.ShapeDtypeStruct(xs0.shape, xs0.dtype),
        scratch_shapes=[pltpu.VMEM((TB * ROW_T, 128), U32), pltpu.SemaphoreType.DMA],
        input_output_aliases={2: 0},
        compiler_params=_cparams(("arbitrary",)),
        name="dispatch",
    )(dest_flat, x1, xs0)


def _experts_kernel(be_ref, nu_ref, xs_ref, wg_ref, wu_ref, wd_ref, ys_ref, wgu_s, wd_s):
    i = pl.program_id(0)

    @pl.when(i < nu_ref[0])
    def _():
        prev = be_ref[jnp.maximum(i - 1, 0)]

        @pl.when((i == 0) | (be_ref[i] != prev))
        def _():
            wgu_s[:, 0:D_EXPERT] = wg_ref[0, 0].astype(BF16)
            wgu_s[:, D_EXPERT:] = wu_ref[0, 0].astype(BF16)
            wd_s[...] = wd_ref[0, 0].astype(BF16)

        x = _unpack_rows(_from_tiles(xs_ref, 0, MOE_BLK)).astype(BF16)
        h = jnp.dot(x, wgu_s[...], preferred_element_type=F32)
        a = _silu(h[:, 0:D_EXPERT]) * h[:, D_EXPERT:]
        _to_tiles(ys_ref, 0, _pack_rows(jnp.dot(a.astype(BF16), wd_s[...], preferred_element_type=F32)))


def _experts(layer, blk_expert, n_used, xs, wg, wu, wd):
    rows = MOE_BLK * ROW_T
    nblk = xs.shape[0] // rows
    row = lambda i, be, nu: (jnp.minimum(i, nu[0] - 1), 0)
    wsel = lambda i, be, nu: (layer, be[i], 0, 0)
    return pl.pallas_call(
        _experts_kernel,
        grid_spec=pltpu.PrefetchScalarGridSpec(
            num_scalar_prefetch=2,
            grid=(nblk,),
            in_specs=[pl.BlockSpec((rows, 128), row),
                      pl.BlockSpec((1, 1, D_MODEL, D_EXPERT), wsel),
                      pl.BlockSpec((1, 1, D_MODEL, D_EXPERT), wsel),
                      pl.BlockSpec((1, 1, D_EXPERT, D_MODEL), wsel)],
            out_specs=pl.BlockSpec((rows, 128), row),
            scratch_shapes=[pltpu.VMEM((D_MODEL, 2 * D_EXPERT), BF16),
                            pltpu.VMEM((D_EXPERT, D_MODEL), BF16)]),
        out_shape=jax.ShapeDtypeStruct(xs.shape, U32),
        compiler_params=_cparams(("arbitrary",)),
        name="experts",
    )(blk_expert, n_used, xs, wg, wu, wd)


def _combine_kernel(dest_ref, gate_ref, x_ref, ys_ref, sgu_ref, sd_ref, g_ref, b_ref, o_ref, buf, sem):
    TB = x_ref.shape[0]

    def issue(t, c):
        for k in range(TOP_K):
            d = dest_ref[t * TOP_K + k]
            pltpu.make_async_copy(ys_ref.at[pl.ds(pl.multiple_of(d * ROW_T, ROW_T), ROW_T), :],
                                  buf.at[pl.ds(pl.multiple_of((k * TB + t) * ROW_T, ROW_T), ROW_T), :], sem).start()
        return c

    lax.fori_loop(0, TB, issue, 0, unroll=2)

    x = x_ref[...]
    h = jnp.dot(x.astype(BF16), sgu_ref[...], preferred_element_type=F32)
    a = _silu(h[:, 0:D_EXPERT]) * h[:, D_EXPERT:]
    y = ALPHA * x + jnp.dot(a.astype(BF16), sd_ref[...], preferred_element_type=F32)

    pltpu.make_async_copy(ys_ref.at[pl.ds(0, TOP_K * TB * ROW_T), :], buf, sem).wait()

    gate = gate_ref[...]
    for k in range(TOP_K):
        y = y + gate[:, k:k + 1] * _unpack_rows(_from_tiles(buf, k * TB * ROW_T, TB))
    o_ref[...] = _layernorm(y, g_ref[...], b_ref[...])


def _combine(dest_flat, gate_t, x1, ys, sh_gu, sh_d, g, b):
    T = x1.shape[0]
    TB = CMB_BLK
    vec = pl.BlockSpec((1, D_MODEL), lambda i: (0, 0))
    return pl.pallas_call(
        _combine_kernel,
        grid=(T // TB,),
        in_specs=[pl.BlockSpec((TB * TOP_K,), lambda i: (i,), memory_space=pltpu.SMEM),
                  pl.BlockSpec((TB, TOP_K), lambda i: (i, 0)),
                  pl.BlockSpec((TB, D_MODEL), lambda i: (i, 0)),
                  pl.BlockSpec(memory_space=pl.ANY),
                  pl.BlockSpec((D_MODEL, 2 * D_EXPERT), lambda i: (0, 0)),
                  pl.BlockSpec((D_EXPERT, D_MODEL), lambda i: (0, 0)), vec, vec],
        out_specs=pl.BlockSpec((TB, D_MODEL), lambda i: (i, 0)),
        out_shape=jax.ShapeDtypeStruct((T, D_MODEL), F32),
        scratch_shapes=[pltpu.VMEM((TOP_K * TB * ROW_T, 128), U32), pltpu.SemaphoreType.DMA],
        compiler_params=_cparams(("arbitrary",)),
        name="combine_ln",
    )(dest_flat, gate_t, x1, ys, sh_gu, sh_d, g, b)


def _token_mixer(x2, B, S, w_p, gla_p, sgu_p, ssd_p, cos_t, sin_t, tb, w_out, ln_g, ln_b):
    s_gla, s_sgu, s_ssd, s_dil = _inproj(x2, w_p)
    ya = _gla(s_gla.reshape(B, S, W_GLA), *gla_p)
    yb = _sgu(s_sgu.reshape(B, S, W_SGU), *sgu_p)
    yc = _ssd(s_ssd.reshape(B, S, W_SSD), *ssd_p)
    yd = _dil(s_dil.reshape(B, S, W_DIL), cos_t, sin_t, tb)
    T = B * S
    flat = lambda y: y.reshape(T, GROUP_W)
    return _outproj(flat(ya), flat(yb), flat(yc), flat(yd), x2, w_out, ln_g, ln_b)


def _moe(layer, x1, rw_t, rb, wg, wu, wd, sh_gu, sh_d, ln_g, ln_b):
    T = x1.shape[0]
    idx, gate, pos, cnt = _router(x1, rw_t, rb)
    counts = cnt[:, 0].astype(I32)
    padded = (counts + MOE_BLK - 1) // MOE_BLK * MOE_BLK
    pad_end = jnp.cumsum(padded)
    pad_start = pad_end - padded
    dest = _slots(pad_start.astype(I32), idx, pos)
    dest_flat = dest.T.reshape(T * TOP_K)
    nblk = T * TOP_K // MOE_BLK + N_EXPERTS
    n_used = (pad_end[-1] // MOE_BLK).astype(I32)
    blk = jnp.minimum(jnp.arange(nblk, dtype=I32), n_used - 1) * MOE_BLK
    blk_expert = jnp.sum((pad_end[None, :] <= blk[:, None]).astype(I32), axis=1)
    blk_expert = jnp.minimum(blk_expert, N_EXPERTS - 1)
    tail_blk = jnp.maximum(pad_end // MOE_BLK - 1, 0).astype(I32)
    xs = _dispatch(dest_flat, x1, _tailzero(tail_blk, nblk))
    ys = _experts(layer, blk_expert, n_used.reshape(1), xs, wg, wu, wd)
    return _combine(dest_flat, gate.T, x1, ys, sh_gu, sh_d, ln_g, ln_b)


def _prep_w_in(w_in):
    o = np.cumsum((0, 128, 128, 256, 256, 16, 256, 256, 256, 512, 4, 256, 256, 256))
    c = lambda i: w_in[..., o[i]:o[i + 1]]
    xbc = c(8)
    x_cols = xbc[..., 0:256]
    per_head = lambda m: jnp.concatenate(
        [m[..., SSM_N * (h // (SSM_H // SSM_G)):SSM_N * (h // (SSM_H // SSM_G) + 1)] for h in range(SSM_H)], axis=-1)
    b_cols = per_head(xbc[..., 256:384])
    c_cols = per_head(xbc[..., 384:512])
    dt_cols = jnp.repeat(c(9), SSM_P, axis=-1)
    pad = jnp.zeros(w_in.shape[:-1] + (W_GLA - o[5],), w_in.dtype)
    cols = [c(0), c(1), c(2), c(3), c(4), pad, c(5), c(6), c(7), x_cols, b_cols, c_cols, dt_cols,
            c(10), c(11), c(12)]
    return jnp.concatenate(cols, axis=-1).astype(BF16)


def _per_head(v):
    rep = SSM_H // SSM_G
    xp = v[..., 0:256]
    pick = lambda m: jnp.concatenate([m[..., SSM_N * (h // rep):SSM_N * (h // rep + 1)] for h in range(SSM_H)], axis=-1)
    return jnp.concatenate([xp, pick(v[..., 256:384]), pick(v[..., 384:512])], axis=-1)


def kernel(x, positions, w_in, gla_w_gate, gla_b_gate, gla_norm_w, sgu_ln_g, sgu_ln_b, sgu_w, sgu_b, ssm_conv_w, ssm_conv_b, ssm_dt_bias, ssm_a_log, ssm_d, ssm_norm_w, w_out, ln1_g, ln1_b, router_w, router_bias, exp_w_gate, exp_w_up, exp_w_down, sh_w_gate, sh_w_up, sh_w_down, ln2_g, ln2_b):
    B, S, _ = x.shape
    L = w_in.shape[0]
    T = B * S

    inv_freq = ROPE_THETA ** (-jnp.arange(0, ROT_DIM, 2, dtype=F32) / ROT_DIM)
    ang = positions.astype(F32)[..., None] * inv_freq
    cos, sin = jnp.cos(ang), jnp.sin(ang)
    ones = jnp.ones((B, S, DIL_DH - ROT_DIM), F32)
    cos_t = jnp.tile(jnp.concatenate([cos, cos, ones], axis=-1), (1, 1, 2))
    sin_t = jnp.tile(jnp.concatenate([-sin, sin, 0.0 * ones], axis=-1), (1, 1, 2))
    tb = jnp.asarray(_branch_log_multiplicity(S))

    w_p = _prep_w_in(w_in)
    wg_p = jnp.pad(gla_w_gate, ((0, 0), (0, 128 - GLA_RANK), (0, 0)))
    row = lambda v: v.reshape(L, 1, -1)
    gla_nw = jnp.tile(gla_norm_w, (1, GLA_H))
    sgu_wc = jnp.transpose(sgu_w, (0, 2, 1, 3)).reshape(L, SGU_CHUNK, SGU_G * SGU_CHUNK)
    sgu_bt = jnp.repeat(jnp.transpose(sgu_b, (0, 2, 1)), GROUP_W // SGU_G, axis=-1)
    exp64 = lambda v: jnp.repeat(v, SSM_P, axis=-1)
    w_out_b = w_out.astype(BF16)
    rw_t = jnp.transpose(router_w, (0, 2, 1))
    sh_gu = jnp.concatenate([sh_w_gate, sh_w_up], axis=-1).astype(BF16)
    sh_d = sh_w_down.astype(BF16)

    x2 = x.reshape(T, D_MODEL)
    for l in range(L):
        gla_p = (wg_p[l], row(gla_b_gate)[l], row(gla_nw)[l])
        sgu_p = (sgu_wc[l], sgu_bt[l], row(sgu_ln_g)[l], row(sgu_ln_b)[l])
        ssd_p = (_per_head(ssm_conv_w[l]), row(_per_head(ssm_conv_b))[l], row(exp64(ssm_dt_bias))[l],
                 row(exp64(ssm_a_log))[l], row(exp64(ssm_d))[l], row(ssm_norm_w)[l])
        x1 = _token_mixer(x2, B, S, w_p[l], gla_p, sgu_p, ssd_p, cos_t, sin_t, tb,
                          w_out_b[l], row(ln1_g)[l], row(ln1_b)[l])
        x2 = _moe(l, x1, rw_t[l], router_bias[l].reshape(N_EXPERTS, 1), exp_w_gate, exp_w_up, exp_w_down,
                  sh_gu[l], sh_d[l], row(ln2_g)[l], row(ln2_b)[l])
    return x2.reshape(B, S, D_MODEL)
```

```python
import numpy as np
import jax
import jax.numpy as jnp
from jax import lax
from jax.experimental import pallas as pl
from jax.experimental.pallas import tpu as pltpu

F32 = jnp.float32
BF16 = jnp.bfloat16
I32 = jnp.int32

D_MODEL = 1024
N_LAYERS = 4
GROUP_W = 256

GLA_H, GLA_DK, GLA_DV, GLA_RANK, GLA_TAU = 4, 32, 64, 16, 16.0
SGU_G, SGU_CHUNK = 4, 128
SSM_H, SSM_P, SSM_G, SSM_N, SSM_CONV = 4, 64, 2, 64, 4
DIL_H, DIL_DH, ROT_DIM, ROPE_THETA = 4, 64, 16, 500000.0
DIL_BRANCHES = ((128, 1), (512, 4), (2048, 16))

N_EXPERTS, TOP_K, N_EXPERT_GROUPS, TOPK_GROUPS, D_EXPERT = 128, 8, 8, 4, 256
ROUTED_SCALE = 1.0

ALPHA = (2 * N_LAYERS) ** 0.25
LN_EPS = 1e-5
RMS_EPS = 1e-6

W_GLA = 896
W_SGU = 512
W_SSD = 1280
W_DIL = 768
W_PROJ = W_GLA + W_SGU + W_SSD + W_DIL

BLK = 128
QBLK = 256
MOE_BLK = 1024
TOK_BLK = 256
DSP_BLK = 1024
CMB_BLK = 512
NEG = -1e30

VMEM_LIMIT = 56 * 1024 * 1024


def _cparams(sem):
    return pltpu.CompilerParams(dimension_semantics=sem, vmem_limit_bytes=VMEM_LIMIT)


def _dot(a, b):
    return jnp.dot(a.astype(BF16), b.astype(BF16), preferred_element_type=F32)


def _dot_nt(a, b):
    return lax.dot_general(a.astype(BF16), b.astype(BF16), (((1,), (1,)), ((), ())),
                           preferred_element_type=F32)


def _dot_tn(a, b):
    return lax.dot_general(a.astype(BF16), b.astype(BF16), (((0,), (0,)), ((), ())),
                           preferred_element_type=F32)


def _split(a, n):
    out = []
    r = a
    for _ in range(n):
        p = r.astype(BF16)
        out.append(p)
        r = r - p.astype(F32)
    return out


def _dot_hi(a, b):
    a1, a2 = _split(a, 2)
    b1, b2 = _split(b, 2)
    return _dot(a1, b1) + (_dot(a1, b2) + _dot(a2, b1))


def _dot_nt_hi(a, b):
    a1, a2 = _split(a, 2)
    b1, b2 = _split(b, 2)
    return _dot_nt(a1, b1) + (_dot_nt(a1, b2) + _dot_nt(a2, b1))


def _dot_lhs01(m, b):
    b1, b2, b3 = _split(b, 3)
    return _dot(m, b1) + (_dot(m, b2) + _dot(m, b3))


def _dot_rhs01(a, m):
    a1, a2, a3 = _split(a, 3)
    return _dot(a1, m) + (_dot(a2, m) + _dot(a3, m))


def _iota(shape, dim):
    return lax.broadcasted_iota(I32, shape, dim)


def _sigmoid(x):
    return 1.0 / (1.0 + jnp.exp(-x))


def _silu(x):
    return x * _sigmoid(x)


def _softplus(x):
    return jnp.maximum(x, 0.0) + jnp.log1p(jnp.exp(-jnp.abs(x)))


def _gelu(x):
    return 0.5 * x * (1.0 + lax.erf(x * (2.0 ** -0.5)))


def _inproj_kernel(x_ref, w_ref, o_gla, o_sgu, o_ssd, o_dil):
    xb = x_ref[...].astype(BF16)
    off = 0
    for o in (o_gla, o_sgu, o_ssd, o_dil):
        n = o.shape[1]
        o[...] = jnp.dot(xb, w_ref[:, off:off + n], preferred_element_type=F32)
        off += n


def _inproj(x2, w_p):
    T = x2.shape[0]
    tm = 512
    widths = (W_GLA, W_SGU, W_SSD, W_DIL)
    return pl.pallas_call(
        _inproj_kernel,
        grid=(T // tm,),
        in_specs=[pl.BlockSpec((tm, D_MODEL), lambda i: (i, 0)),
                  pl.BlockSpec((D_MODEL, W_PROJ), lambda i: (0, 0))],
        out_specs=[pl.BlockSpec((tm, w), lambda i: (i, 0)) for w in widths],
        out_shape=[jax.ShapeDtypeStruct((T, w), F32) for w in widths],
        compiler_params=_cparams(("arbitrary",)),
        name="inproj",
    )(x2, w_p)


GLA_NB = 2
GLA_SUB = 16


def _gla_kernel(slab_ref, wg_ref, bg_ref, nw_ref, out_ref, st_ref, a_ref):
    S = slab_ref.shape[1]
    st_ref[...] = jnp.zeros_like(st_ref)
    row = _iota((BLK, BLK), 0)
    col = _iota((BLK, BLK), 1)
    tril = col <= row
    tril_bf = jnp.where(tril, 1.0, 0.0).astype(BF16)
    sub_row = _iota((GLA_SUB, BLK), 0)
    sub_col = _iota((GLA_SUB, BLK), 1)
    sub_head = sub_col >> 5
    v_head = _iota((BLK, GROUP_W), 1) >> 6
    st_diag = (_iota((GROUP_W, BLK), 0) >> 6) == (_iota((GROUP_W, BLK), 1) >> 5)
    seg = jnp.where((_iota((GROUP_W, GROUP_W), 0) >> 6) == (_iota((GROUP_W, GROUP_W), 1) >> 6),
                    1.0, 0.0).astype(BF16)
    wg = wg_ref[...]
    bg = bg_ref[...]
    nw = nw_ref[...]

    def block(b, r0):
        blk = slab_ref[b, pl.ds(r0, BLK), :]
        q = blk[:, 0:128] * (GLA_DK ** -0.5)
        k = blk[:, 128:256]
        v = blk[:, 256:512]
        r = blk[:, 512:768]
        lr = blk[:, 768:896]
        z = _dot_hi(lr, wg) + bg
        gk = (jnp.minimum(z, 0.0) - jnp.log1p(jnp.exp(-jnp.abs(z)))) * (1.0 / GLA_TAU)
        g = _dot_lhs01(tril_bf, gk)
        g_last = g[BLK - 1:BLK, :]
        q_in = q * jnp.exp(g)
        k_in = k * jnp.exp(g_last - g)
        vb = v.astype(BF16)
        o = _dot_nt(q_in, st_ref[b])
        for c in range(BLK // GLA_SUB):
            lo = c * GLA_SUB
            n_c = g[lo - 1:lo, :] if c else jnp.zeros((1, BLK), F32)
            qc = q[lo:lo + GLA_SUB, :] * jnp.exp(g[lo:lo + GLA_SUB, :] - n_c)
            kc = k * jnp.exp(jnp.where(row < lo + GLA_SUB, n_c - g, -jnp.inf))
            qs = jnp.concatenate([jnp.where(sub_head == h, qc, 0.0) for h in range(GLA_H)], axis=0)
            sc = _dot_nt(qs, kc)
            causal = sub_col <= sub_row + lo
            for h in range(GLA_H):
                a_ref[b, h, lo:lo + GLA_SUB, :] = jnp.where(causal, sc[h * GLA_SUB:(h + 1) * GLA_SUB, :], 0.0)
        for h in range(GLA_H):
            o = o + jnp.where(v_head == h, _dot(a_ref[b, h], vb), 0.0)
        st_ref[b] = st_ref[b] * jnp.exp(g_last) + jnp.where(st_diag, _dot_tn(vb, k_in), 0.0)
        ms = _dot_rhs01(o * o, seg) * (1.0 / GLA_DV)
        y = o * lax.rsqrt(ms + RMS_EPS) * nw * _silu(r)
        out_ref[b, pl.ds(r0, BLK), :] = y.astype(out_ref.dtype)

    def body(n, carry):
        r0 = pl.multiple_of(n * BLK, BLK)
        for b in range(slab_ref.shape[0]):
            block(b, r0)
        return carry

    lax.fori_loop(0, S // BLK, body, 0, unroll=2)


def _gla(slab, wg_p, bg, nw_t):
    B, S, _ = slab.shape
    nb = GLA_NB if B % GLA_NB == 0 else 1
    return pl.pallas_call(
        _gla_kernel,
        grid=(B // nb,),
        in_specs=[pl.BlockSpec((nb, S, W_GLA), lambda b: (b, 0, 0)),
                  pl.BlockSpec((128, 128), lambda b: (0, 0)),
                  pl.BlockSpec((1, 128), lambda b: (0, 0)),
                  pl.BlockSpec((1, GROUP_W), lambda b: (0, 0))],
        out_specs=pl.BlockSpec((nb, S, GROUP_W), lambda b: (b, 0, 0)),
        out_shape=jax.ShapeDtypeStruct((B, S, GROUP_W), BF16),
        scratch_shapes=[pltpu.VMEM((nb, GROUP_W, 128), F32), pltpu.VMEM((nb, GLA_H, BLK, BLK), F32)],
        compiler_params=_cparams(("arbitrary",)),
        name="gla",
    )(slab, wg_p, bg, nw_t)


def _sgu_kernel(slab_ref, w_ref, bias_ref, g_ref, b_ref, out_ref):
    S = slab_ref.shape[1]
    wmask = (_iota((BLK, SGU_G * BLK), 1) & (BLK - 1)) <= _iota((BLK, SGU_G * BLK), 0)
    w = jnp.where(wmask, w_ref[...], 0.0).astype(BF16)
    lane_grp = _iota((BLK, GROUP_W), 1) >> 6
    bias = bias_ref[...]
    ln_g = g_ref[...]
    ln_b = b_ref[...]

    def body(n, carry):
        r0 = pl.multiple_of(n * BLK, BLK)
        blk = slab_ref[0, pl.ds(r0, BLK), :]
        u = _gelu(blk[:, 0:256])
        v = _gelu(blk[:, 256:512])
        mu = jnp.mean(v, axis=-1, keepdims=True)
        var = jnp.mean(jnp.square(v - mu), axis=-1, keepdims=True)
        v = (v - mu) * lax.rsqrt(var + LN_EPS) * ln_g + ln_b
        vexp = jnp.concatenate([jnp.where(lane_grp == g, v, 0.0) for g in range(SGU_G)], axis=0)
        s = _dot(w, vexp) + bias
        out_ref[0, pl.ds(r0, BLK), :] = (u * s).astype(out_ref.dtype)
        return carry

    lax.fori_loop(0, S // BLK, body, 0, unroll=2)


def _sgu(slab, w_cat, bias_t, ln_g, ln_b):
    B, S, _ = slab.shape
    return pl.pallas_call(
        _sgu_kernel,
        grid=(B,),
        in_specs=[pl.BlockSpec((1, S, W_SGU), lambda b: (b, 0, 0)),
                  pl.BlockSpec((BLK, SGU_G * BLK), lambda b: (0, 0)),
                  pl.BlockSpec((BLK, GROUP_W), lambda b: (0, 0)),
                  pl.BlockSpec((1, GROUP_W), lambda b: (0, 0)),
                  pl.BlockSpec((1, GROUP_W), lambda b: (0, 0))],
        out_specs=pl.BlockSpec((1, S, GROUP_W), lambda b: (b, 0, 0)),
        out_shape=jax.ShapeDtypeStruct((B, S, GROUP_W), BF16),
        compiler_params=_cparams(("arbitrary",)),
        name="sgu",
    )(slab, w_cat, bias_t, ln_g, ln_b)


def _ssd_kernel(slab_ref, cw_ref, cb_ref, dtb_ref, a_ref, d_ref, nw_ref, out_ref, xpad_ref, st_ref):
    S = slab_ref.shape[1]
    CW = 3 * GROUP_W
    st_ref[...] = jnp.zeros_like(st_ref)
    xpad_ref[0:8, :] = jnp.zeros((8, CW), F32)
    xpad_ref[8:S + 8, :] = slab_ref[0, :, 256:256 + CW]
    row = _iota((BLK, BLK), 0)
    col = _iota((BLK, BLK), 1)
    tril = col <= row
    tril_bf = jnp.where(tril, 1.0, 0.0).astype(BF16)
    lane_head = _iota((BLK, GROUP_W), 1) >> 6
    st_diag = (_iota((GROUP_W, GROUP_W), 0) >> 6) == (_iota((GROUP_W, GROUP_W), 1) >> 6)
    cw = cw_ref[...]
    cb = cb_ref[...]
    dtb = dtb_ref[...]
    a_neg = -jnp.exp(a_ref[...])
    d_skip = d_ref[...]
    nw = nw_ref[...]

    def body(n, carry):
        r0 = pl.multiple_of(n * BLK, BLK)
        xw = xpad_ref[pl.ds(r0, BLK + 8), :]
        conv = cb + cw[0:1, :] * xw[5:5 + BLK, :]
        for w in range(1, SSM_CONV):
            conv = conv + cw[w:w + 1, :] * xw[5 + w:5 + w + BLK, :]
        xbc = _silu(conv)
        xs = xbc[:, 0:256]
        b_e = xbc[:, 256:512]
        c_e = xbc[:, 512:768]
        z = slab_ref[0, pl.ds(r0, BLK), 0:256]
        dt = _softplus(slab_ref[0, pl.ds(r0, BLK), 1024:1280] + dtb)
        acs = _dot_lhs01(tril_bf, dt * a_neg)
        acs_last = acs[BLK - 1:BLK, :]
        x_dt = (xs * dt).astype(BF16)
        acs_t = [jnp.transpose(acs[:, 0:128]), jnp.transpose(acs[:, 128:256])]
        y = _dot(c_e * jnp.exp(acs), st_ref[...])
        b_bf = b_e.astype(BF16)
        for h in range(SSM_H):
            a_col = jnp.broadcast_to(acs[:, 64 * h:64 * h + 1], (BLK, BLK))
            a_row = jnp.broadcast_to(acs_t[h // 2][64 * (h % 2):64 * (h % 2) + 1, :], (BLK, BLK))
            lm = jnp.exp(jnp.where(tril, a_col - a_row, -jnp.inf))
            ch = jnp.where(lane_head == h, c_e, 0.0)
            m = _dot_nt(ch, b_bf) * lm
            y = y + jnp.where(lane_head == h, _dot(m, x_dt), 0.0)
        new = _dot_tn(b_e * jnp.exp(acs_last - acs), x_dt)
        st_ref[...] = st_ref[...] * jnp.exp(acs_last) + jnp.where(st_diag, new, 0.0)
        y = (y + d_skip * xs) * _silu(z)
        parts = []
        for g in range(SSM_G):
            yg = y[:, 128 * g:128 * (g + 1)]
            ms = jnp.mean(yg * yg, axis=-1, keepdims=True)
            parts.append(yg * lax.rsqrt(ms + RMS_EPS))
        out = jnp.concatenate(parts, axis=1) * nw
        out_ref[0, pl.ds(r0, BLK), :] = out.astype(out_ref.dtype)
        return carry

    lax.fori_loop(0, S // BLK, body, 0, unroll=2)


def _ssd(slab, cw_e, cb_e, dtb_e, a_e, d_e, nw):
    B, S, _ = slab.shape
    CW = 3 * GROUP_W
    vec = lambda w: pl.BlockSpec((1, w), lambda b: (0, 0))
    return pl.pallas_call(
        _ssd_kernel,
        grid=(B,),
        in_specs=[pl.BlockSpec((1, S, W_SSD), lambda b: (b, 0, 0)),
                  pl.BlockSpec((SSM_CONV, CW), lambda b: (0, 0)),
                  vec(CW), vec(GROUP_W), vec(GROUP_W), vec(GROUP_W), vec(GROUP_W)],
        out_specs=pl.BlockSpec((1, S, GROUP_W), lambda b: (b, 0, 0)),
        out_shape=jax.ShapeDtypeStruct((B, S, GROUP_W), BF16),
        scratch_shapes=[pltpu.VMEM((S + 8, CW), F32), pltpu.VMEM((GROUP_W, GROUP_W), F32)],
        compiler_params=_cparams(("arbitrary",)),
        name="ssd",
    )(slab, cw_e, cb_e, dtb_e, a_e, d_e, nw)


def _branch_log_multiplicity(S):
    nq = S // QBLK
    r = np.arange(QBLK)[:, None]
    u = np.arange(S)[None, :]
    d = (nq - 1) * QBLK + r - u
    c = np.zeros(d.shape, np.float64)
    for window, dil in DIL_BRANCHES:
        c += (d >= 0) & (d % dil == 0) & (d <= window)
    with np.errstate(divide="ignore"):
        return np.where(c > 0, np.log(np.maximum(c, 1.0)), NEG).astype(np.float32)


def _dil_kernel(slab_ref, cos_ref, sin_ref, tb_ref, out_ref, q_ref, k_ref, v_ref):
    S = slab_ref.shape[1]
    nq = S // QBLK
    lane_d = _iota((1, 128), 1) & (DIL_DH - 1)

    def rope(t):
        fwd = pltpu.roll(t, 128 - ROT_DIM // 2, 1)
        bwd = pltpu.roll(t, ROT_DIM // 2, 1)
        return t * cos_ref[0] + jnp.where(lane_d < ROT_DIM // 2, fwd, bwd) * sin_ref[0]

    for c in range(0, GROUP_W, 128):
        q_ref[:, c:c + 128] = (rope(slab_ref[0, :, c:c + 128]) * (DIL_DH ** -0.5)).astype(BF16)
        k_ref[:, c:c + 128] = rope(slab_ref[0, :, 256 + c:384 + c]).astype(BF16)
    v_ref[...] = slab_ref[0, :, 512:768].astype(BF16)
    lane_head = _iota((QBLK, GROUP_W), 1) >> 6
    for i in range(nq):
        nk = (i + 1) * QBLK
        qi = q_ref[i * QBLK:(i + 1) * QBLK, :]
        acc = jnp.zeros((QBLK, GROUP_W), F32)
        for h in range(DIL_H):
            qh = jnp.where(lane_head == h, qi, jnp.zeros_like(qi))
            s = _dot_nt(qh, k_ref[0:nk, :]) + tb_ref[:, (nq - 1 - i) * QBLK:]
            m = jnp.max(s, axis=-1, keepdims=True)
            p = jnp.exp(s - m)
            den = jnp.sum(p, axis=-1, keepdims=True)
            o = _dot(p, v_ref[0:nk, :])
            acc = acc + jnp.where(lane_head == h, o / den, 0.0)
        out_ref[0, i * QBLK:(i + 1) * QBLK, :] = acc.astype(out_ref.dtype)


def _dil(slab, cos_t, sin_t, tb):
    B, S, _ = slab.shape
    return pl.pallas_call(
        _dil_kernel,
        grid=(B,),
        in_specs=[pl.BlockSpec((1, S, W_DIL), lambda b: (b, 0, 0)),
                  pl.BlockSpec((1, S, 128), lambda b: (b, 0, 0)),
                  pl.BlockSpec((1, S, 128), lambda b: (b, 0, 0)),
                  pl.BlockSpec((QBLK, S), lambda b: (0, 0), pipeline_mode=pl.Buffered(1))],
        out_specs=pl.BlockSpec((1, S, GROUP_W), lambda b: (b, 0, 0)),
        out_shape=jax.ShapeDtypeStruct((B, S, GROUP_W), BF16),
        scratch_shapes=[pltpu.VMEM((S, GROUP_W), BF16)] * 3,
        compiler_params=_cparams(("arbitrary",)),
        name="dil",
    )(slab, cos_t, sin_t, tb)


def _layernorm(x, g, b):
    mu = jnp.mean(x, axis=-1, keepdims=True)
    var = jnp.mean(jnp.square(x - mu), axis=-1, keepdims=True)
    return (x - mu) * lax.rsqrt(var + LN_EPS) * g + b


def _outproj_kernel(ya, yb, yc, yd, x_ref, w_ref, g_ref, b_ref, o_ref):
    y = jnp.concatenate([ya[...], yb[...], yc[...], yd[...]], axis=1)
    y = jnp.dot(y, w_ref[...], preferred_element_type=F32)
    o_ref[...] = _layernorm(ALPHA * x_ref[...] + y, g_ref[...], b_ref[...])


def _outproj(ya, yb, yc, yd, x2, w_out, g, b):
    T = x2.shape[0]
    tm = 512
    ysp = pl.BlockSpec((tm, GROUP_W), lambda i: (i, 0))
    vec = pl.BlockSpec((1, D_MODEL), lambda i: (0, 0))
    return pl.pallas_call(
        _outproj_kernel,
        grid=(T // tm,),
        in_specs=[ysp, ysp, ysp, ysp,
                  pl.BlockSpec((tm, D_MODEL), lambda i: (i, 0)),
                  pl.BlockSpec((D_MODEL, D_MODEL), lambda i: (0, 0)), vec, vec],
        out_specs=pl.BlockSpec((tm, D_MODEL), lambda i: (i, 0)),
        out_shape=jax.ShapeDtypeStruct((T, D_MODEL), F32),
        compiler_params=_cparams(("arbitrary",)),
        name="outproj_ln",
    )(ya, yb, yc, yd, x2, w_out, g, b)


def _router_kernel(x_ref, rw_ref, rb_ref, idx_ref, gate_ref, pos_ref, cnt_ref, carry_ref):
    TB = x_ref.shape[0]
    E = N_EXPERTS
    per = E // N_EXPERT_GROUPS

    @pl.when(pl.program_id(0) == 0)
    def _():
        carry_ref[...] = jnp.zeros_like(carry_ref)

    logits = _dot_nt_hi(rw_ref[...], x_ref[...])
    scores = _sigmoid(logits)
    choice = scores + rb_ref[...]
    e_iota = _iota((E, TB), 0)

    l_iota = _iota((per, TB), 0)
    grp = []
    for g in range(N_EXPERT_GROUPS):
        cg = choice[per * g:per * (g + 1), :]
        m1 = jnp.max(cg, axis=0, keepdims=True)
        i1 = jnp.min(jnp.where(cg == m1, l_iota, per), axis=0, keepdims=True)
        m2 = jnp.max(jnp.where(l_iota == i1, -jnp.inf, cg), axis=0, keepdims=True)
        grp.append(m1 + m2)
    gs = jnp.concatenate(grp, axis=0)
    g_iota = _iota((N_EXPERT_GROUPS, TB), 0)
    keep = jnp.zeros((N_EXPERT_GROUPS, TB), jnp.bool_)
    for _ in range(TOPK_GROUPS):
        m = jnp.max(gs, axis=0, keepdims=True)
        gi = jnp.min(jnp.where(gs == m, g_iota, N_EXPERT_GROUPS), axis=0, keepdims=True)
        hit = g_iota == gi
        keep = keep | hit
        gs = jnp.where(hit, -jnp.inf, gs)
    keep_f = jnp.where(keep, 1.0, 0.0)
    keep_e = jnp.concatenate([jnp.broadcast_to(keep_f[g:g + 1, :], (per, TB))
                              for g in range(N_EXPERT_GROUPS)], axis=0)
    cm = jnp.where(keep_e > 0.5, choice, -jnp.inf)

    hits, idxs, gates = [], [], []
    onehot = jnp.zeros((E, TB), F32)
    for _ in range(TOP_K):
        m = jnp.max(cm, axis=0, keepdims=True)
        ei = jnp.min(jnp.where(cm == m, e_iota, E), axis=0, keepdims=True)
        hit = e_iota == ei
        hits.append(hit)
        idxs.append(ei)
        gates.append(jnp.sum(jnp.where(hit, scores, 0.0), axis=0, keepdims=True))
        onehot = onehot + jnp.where(hit, 1.0, 0.0)
        cm = jnp.where(hit, -jnp.inf, cm)
    gate = jnp.concatenate(gates, axis=0)
    gate = gate / jnp.sum(gate, axis=0, keepdims=True) * ROUTED_SCALE

    before = jnp.where(_iota((TB, TB), 0) < _iota((TB, TB), 1), 1.0, 0.0).astype(BF16)
    cnt = carry_ref[...] + _dot(onehot, before)
    pos = jnp.concatenate([jnp.sum(jnp.where(h, cnt, 0.0), axis=0, keepdims=True) for h in hits], axis=0)
    carry_ref[...] = carry_ref[...] + jnp.sum(onehot, axis=1, keepdims=True)

    idx_ref[...] = jnp.concatenate(idxs, axis=0)
    gate_ref[...] = gate
    pos_ref[...] = pos.astype(I32)
    cnt_ref[...] = jnp.broadcast_to(carry_ref[...], cnt_ref.shape)


def _router(x1, rw_t, rb):
    T = x1.shape[0]
    TB = TOK_BLK
    kt = pl.BlockSpec((TOP_K, TB), lambda i: (0, i))
    return pl.pallas_call(
        _router_kernel,
        grid=(T // TB,),
        in_specs=[pl.BlockSpec((TB, D_MODEL), lambda i: (i, 0)),
                  pl.BlockSpec((N_EXPERTS, D_MODEL), lambda i: (0, 0)),
                  pl.BlockSpec((N_EXPERTS, 1), lambda i: (0, 0))],
        out_specs=[kt, kt, kt, pl.BlockSpec((N_EXPERTS, 128), lambda i: (0, 0))],
        out_shape=[jax.ShapeDtypeStruct((TOP_K, T), I32), jax.ShapeDtypeStruct((TOP_K, T), F32),
                   jax.ShapeDtypeStruct((TOP_K, T), I32), jax.ShapeDtypeStruct((N_EXPERTS, 128), F32)],
        scratch_shapes=[pltpu.VMEM((N_EXPERTS, 1), F32)],
        compiler_params=_cparams(("arbitrary",)),
        name="router",
    )(x1, rw_t, rb)


def _slots_kernel(ps_ref, idx_ref, pos_ref, dest_ref):
    idx = idx_ref[...]

    def body(e, acc):
        return acc + jnp.where(idx == e, ps_ref[e], 0)

    dest_ref[...] = lax.fori_loop(0, N_EXPERTS, body, pos_ref[...], unroll=8)


def _slots(pad_start, idx, pos):
    T = idx.shape[1]
    tb = min(T, 2048)
    kt = pl.BlockSpec((TOP_K, tb), lambda i, ps: (0, i))
    return pl.pallas_call(
        _slots_kernel,
        grid_spec=pltpu.PrefetchScalarGridSpec(
            num_scalar_prefetch=1, grid=(T // tb,), in_specs=[kt, kt], out_specs=kt),
        out_shape=jax.ShapeDtypeStruct((TOP_K, T), I32),
        compiler_params=_cparams(("arbitrary",)),
        name="slots",
    )(pad_start, idx, pos)


ROW_T = D_MODEL // 2 // 128
U32 = jnp.uint32


def _pack_rows(x):
    bits = lax.bitcast_convert_type(x.astype(BF16).astype(F32), U32)
    half = D_MODEL // 2
    return (bits[:, :half] >> 16) | (bits[:, half:] & U32(0xFFFF0000))


def _unpack_rows(w):
    lo = lax.bitcast_convert_type(w << 16, F32)
    hi = lax.bitcast_convert_type(w & U32(0xFFFF0000), F32)
    return jnp.concatenate([lo, hi], axis=1)


def _to_tiles(dst_ref, base, w):
    n = w.shape[0]
    for s in range(ROW_T):
        dst_ref[pl.ds(base + s, n, stride=ROW_T), :] = w[:, 128 * s:128 * (s + 1)]


def _from_tiles(src_ref, base, n):
    return jnp.concatenate([src_ref[pl.ds(base + s, n, stride=ROW_T), :] for s in range(ROW_T)], axis=1)


def _tailzero_kernel(tb_ref, xs_ref):
    del tb_ref
    xs_ref[...] = jnp.zeros_like(xs_ref)


def _tailzero(tail_blk, nblk):
    rows = MOE_BLK * ROW_T
    return pl.pallas_call(
        _tailzero_kernel,
        grid_spec=pltpu.PrefetchScalarGridSpec(
            num_scalar_prefetch=1, grid=(N_EXPERTS,), in_specs=[],
            out_specs=pl.BlockSpec((rows, 128), lambda e, tb: (tb[e], 0))),
        out_shape=jax.ShapeDtypeStruct((nblk * rows, 128), U32),
        compiler_params=_cparams(("arbitrary",)),
        name="tailzero",
    )(tail_blk)


def _dispatch_kernel(dest_ref, x_ref, xs_in, xs_out, xt, sem):
    del xs_in
    TB = x_ref.shape[0]
    _to_tiles(xt, 0, _pack_rows(x_ref[...]))

    def issue(t, c):
        src = xt.at[pl.ds(pl.multiple_of(t * ROW_T, ROW_T), ROW_T), :]
        for k in range(TOP_K):
            d = dest_ref[t * TOP_K + k]
            pltpu.make_async_copy(src, xs_out.at[pl.ds(pl.multiple_of(d * ROW_T, ROW_T), ROW_T), :], sem).start()
        return c

    lax.fori_loop(0, TB, issue, 0, unroll=2)

    for k in range(TOP_K):
        pltpu.make_async_copy(xt, xs_out.at[pl.ds(0, TB * ROW_T), :], sem).wait()


def _dispatch(dest_flat, x1, xs0):
    T = x1.shape[0]
    TB = DSP_BLK
    return pl.pallas_call(
        _dispatch_kernel,
        grid=(T // TB,),
        in_specs=[pl.BlockSpec((TB * TOP_K,), lambda i: (i,), memory_space=pltpu.SMEM),
                  pl.BlockSpec((TB, D_MODEL), lambda i: (i, 0)),
                  pl.BlockSpec(memory_space=pl.ANY)],
        out_specs=pl.BlockSpec(memory_space=pl.ANY),
        out_shape=jax.ShapeDtypeStruct(xs0.shape, xs0.dtype),
        scratch_shapes=[pltpu.VMEM((TB * ROW_T, 128), U32), pltpu.SemaphoreType.DMA],
        input_output_aliases={2: 0},
        compiler_params=_cparams(("arbitrary",)),
        name="dispatch",
    )(dest_flat, x1, xs0)


def _experts_kernel(be_ref, nu_ref, xs_ref, wg_ref, wu_ref, wd_ref, ys_ref, wgu_s, wd_s):
    i = pl.program_id(0)

    @pl.when(i < nu_ref[0])
    def _():
        prev = be_ref[jnp.maximum(i - 1, 0)]

        @pl.when((i == 0) | (be_ref[i] != prev))
        def _():
            wgu_s[:, 0:D_EXPERT] = wg_ref[0, 0].astype(BF16)
            wgu_s[:, D_EXPERT:] = wu_ref[0, 0].astype(BF16)
            wd_s[...] = wd_ref[0, 0].astype(BF16)

        x = _unpack_rows(_from_tiles(xs_ref, 0, MOE_BLK)).astype(BF16)
        h = jnp.dot(x, wgu_s[...], preferred_element_type=F32)
        a = _silu(h[:, 0:D_EXPERT]) * h[:, D_EXPERT:]
        _to_tiles(ys_ref, 0, _pack_rows(jnp.dot(a.astype(BF16), wd_s[...], preferred_element_type=F32)))


def _experts(layer, blk_expert, n_used, xs, wg, wu, wd):
    rows = MOE_BLK * ROW_T
    nblk = xs.shape[0] // rows
    row = lambda i, be, nu: (jnp.minimum(i, nu[0] - 1), 0)
    wsel = lambda i, be, nu: (layer, be[i], 0, 0)
    return pl.pallas_call(
        _experts_kernel,
        grid_spec=pltpu.PrefetchScalarGridSpec(
            num_scalar_prefetch=2,
            grid=(nblk,),
            in_specs=[pl.BlockSpec((rows, 128), row),
                      pl.BlockSpec((1, 1, D_MODEL, D_EXPERT), wsel),
                      pl.BlockSpec((1, 1, D_MODEL, D_EXPERT), wsel),
                      pl.BlockSpec((1, 1, D_EXPERT, D_MODEL), wsel)],
            out_specs=pl.BlockSpec((rows, 128), row),
            scratch_shapes=[pltpu.VMEM((D_MODEL, 2 * D_EXPERT), BF16),
                            pltpu.VMEM((D_EXPERT, D_MODEL), BF16)]),
        out_shape=jax.ShapeDtypeStruct(xs.shape, U32),
        compiler_params=_cparams(("arbitrary",)),
        name="experts",
    )(blk_expert, n_used, xs, wg, wu, wd)


def _combine_kernel(dest_ref, gate_ref, x_ref, ys_ref, sgu_ref, sd_ref, g_ref, b_ref, o_ref, buf, sem):
    TB = x_ref.shape[0]

    def issue(t, c):
        for k in range(TOP_K):
            d = dest_ref[t * TOP_K + k]
            pltpu.make_async_copy(ys_ref.at[pl.ds(pl.multiple_of(d * ROW_T, ROW_T), ROW_T), :],
                                  buf.at[pl.ds(pl.multiple_of((k * TB + t) * ROW_T, ROW_T), ROW_T), :], sem).start()
        return c

    lax.fori_loop(0, TB, issue, 0, unroll=2)

    x = x_ref[...]
    h = jnp.dot(x.astype(BF16), sgu_ref[...], preferred_element_type=F32)
    a = _silu(h[:, 0:D_EXPERT]) * h[:, D_EXPERT:]
    y = ALPHA * x + jnp.dot(a.astype(BF16), sd_ref[...], preferred_element_type=F32)

    pltpu.make_async_copy(ys_ref.at[pl.ds(0, TOP_K * TB * ROW_T), :], buf, sem).wait()

    gate = gate_ref[...]
    for k in range(TOP_K):
        y = y + gate[:, k:k + 1] * _unpack_rows(_from_tiles(buf, k * TB * ROW_T, TB))
    o_ref[...] = _layernorm(y, g_ref[...], b_ref[...])


def _combine(dest_flat, gate_t, x1, ys, sh_gu, sh_d, g, b):
    T = x1.shape[0]
    TB = CMB_BLK
    vec = pl.BlockSpec((1, D_MODEL), lambda i: (0, 0))
    return pl.pallas_call(
        _combine_kernel,
        grid=(T // TB,),
        in_specs=[pl.BlockSpec((TB * TOP_K,), lambda i: (i,), memory_space=pltpu.SMEM),
                  pl.BlockSpec((TB, TOP_K), lambda i: (i, 0)),
                  pl.BlockSpec((TB, D_MODEL), lambda i: (i, 0)),
                  pl.BlockSpec(memory_space=pl.ANY),
                  pl.BlockSpec((D_MODEL, 2 * D_EXPERT), lambda i: (0, 0)),
                  pl.BlockSpec((D_EXPERT, D_MODEL), lambda i: (0, 0)), vec, vec],
        out_specs=pl.BlockSpec((TB, D_MODEL), lambda i: (i, 0)),
        out_shape=jax.ShapeDtypeStruct((T, D_MODEL), F32),
        scratch_shapes=[pltpu.VMEM((TOP_K * TB * ROW_T, 128), U32), pltpu.SemaphoreType.DMA],
        compiler_params=_cparams(("arbitrary",)),
        name="combine_ln",
    )(dest_flat, gate_t, x1, ys, sh_gu, sh_d, g, b)


def _token_mixer(x2, B, S, w_p, gla_p, sgu_p, ssd_p, cos_t, sin_t, tb, w_out, ln_g, ln_b):
    s_gla, s_sgu, s_ssd, s_dil = _inproj(x2, w_p)
    ya = _gla(s_gla.reshape(B, S, W_GLA), *gla_p)
    yb = _sgu(s_sgu.reshape(B, S, W_SGU), *sgu_p)
    yc = _ssd(s_ssd.reshape(B, S, W_SSD), *ssd_p)
    yd = _dil(s_dil.reshape(B, S, W_DIL), cos_t, sin_t, tb)
    T = B * S
    flat = lambda y: y.reshape(T, GROUP_W)
    return _outproj(flat(ya), flat(yb), flat(yc), flat(yd), x2, w_out, ln_g, ln_b)


def _moe(layer, x1, rw_t, rb, wg, wu, wd, sh_gu, sh_d, ln_g, ln_b):
    T = x1.shape[0]
    idx, gate, pos, cnt = _router(x1, rw_t, rb)
    counts = cnt[:, 0].astype(I32)
    padded = (counts + MOE_BLK - 1) // MOE_BLK * MOE_BLK
    pad_end = jnp.cumsum(padded)
    pad_start = pad_end - padded
    dest = _slots(pad_start.astype(I32), idx, pos)
    dest_flat = dest.T.reshape(T * TOP_K)
    nblk = T * TOP_K // MOE_BLK + N_EXPERTS
    n_used = (pad_end[-1] // MOE_BLK).astype(I32)
    blk = jnp.minimum(jnp.arange(nblk, dtype=I32), n_used - 1) * MOE_BLK
    blk_expert = jnp.sum((pad_end[None, :] <= blk[:, None]).astype(I32), axis=1)
    blk_expert = jnp.minimum(blk_expert, N_EXPERTS - 1)
    tail_blk = jnp.maximum(pad_end // MOE_BLK - 1, 0).astype(I32)
    xs = _dispatch(dest_flat, x1, _tailzero(tail_blk, nblk))
    ys = _experts(layer, blk_expert, n_used.reshape(1), xs, wg, wu, wd)
    return _combine(dest_flat, gate.T, x1, ys, sh_gu, sh_d, ln_g, ln_b)


def _prep_w_in(w_in):
    o = np.cumsum((0, 128, 128, 256, 256, 16, 256, 256, 256, 512, 4, 256, 256, 256))
    c = lambda i: w_in[..., o[i]:o[i + 1]]
    xbc = c(8)
    x_cols = xbc[..., 0:256]
    per_head = lambda m: jnp.concatenate(
        [m[..., SSM_N * (h // (SSM_H // SSM_G)):SSM_N * (h // (SSM_H // SSM_G) + 1)] for h in range(SSM_H)], axis=-1)
    b_cols = per_head(xbc[..., 256:384])
    c_cols = per_head(xbc[..., 384:512])
    dt_cols = jnp.repeat(c(9), SSM_P, axis=-1)
    pad = jnp.zeros(w_in.shape[:-1] + (W_GLA - o[5],), w_in.dtype)
    cols = [c(0), c(1), c(2), c(3), c(4), pad, c(5), c(6), c(7), x_cols, b_cols, c_cols, dt_cols,
            c(10), c(11), c(12)]
    return jnp.concatenate(cols, axis=-1).astype(BF16)


def _per_head(v):
    rep = SSM_H // SSM_G
    xp = v[..., 0:256]
    pick = lambda m: jnp.concatenate([m[..., SSM_N * (h // rep):SSM_N * (h // rep + 1)] for h in range(SSM_H)], axis=-1)
    return jnp.concatenate([xp, pick(v[..., 256:384]), pick(v[..., 384:512])], axis=-1)


def kernel(x, positions, w_in, gla_w_gate, gla_b_gate, gla_norm_w, sgu_ln_g, sgu_ln_b, sgu_w, sgu_b, ssm_conv_w, ssm_conv_b, ssm_dt_bias, ssm_a_log, ssm_d, ssm_norm_w, w_out, ln1_g, ln1_b, router_w, router_bias, exp_w_gate, exp_w_up, exp_w_down, sh_w_gate, sh_w_up, sh_w_down, ln2_g, ln2_b):
    B, S, _ = x.shape
    L = w_in.shape[0]
    T = B * S

    inv_freq = ROPE_THETA ** (-jnp.arange(0, ROT_DIM, 2, dtype=F32) / ROT_DIM)
    ang = positions.astype(F32)[..., None] * inv_freq
    cos, sin = jnp.cos(ang), jnp.sin(ang)
    ones = jnp.ones((B, S, DIL_DH - ROT_DIM), F32)
    cos_t = jnp.tile(jnp.concatenate([cos, cos, ones], axis=-1), (1, 1, 2))
    sin_t = jnp.tile(jnp.concatenate([-sin, sin, 0.0 * ones], axis=-1), (1, 1, 2))
    tb = jnp.asarray(_branch_log_multiplicity(S))

    w_p = _prep_w_in(w_in)
    wg_p = jnp.pad(gla_w_gate, ((0, 0), (0, 128 - GLA_RANK), (0, 0)))
    row = lambda v: v.reshape(L, 1, -1)
    gla_nw = jnp.tile(gla_norm_w, (1, GLA_H))
    sgu_wc = jnp.transpose(sgu_w, (0, 2, 1, 3)).reshape(L, SGU_CHUNK, SGU_G * SGU_CHUNK)
    sgu_bt = jnp.repeat(jnp.transpose(sgu_b, (0, 2, 1)), GROUP_W // SGU_G, axis=-1)
    exp64 = lambda v: jnp.repeat(v, SSM_P, axis=-1)
    w_out_b = w_out.astype(BF16)
    rw_t = jnp.transpose(router_w, (0, 2, 1))
    sh_gu = jnp.concatenate([sh_w_gate, sh_w_up], axis=-1).astype(BF16)
    sh_d = sh_w_down.astype(BF16)

    x2 = x.reshape(T, D_MODEL)
    for l in range(L):
        gla_p = (wg_p[l], row(gla_b_gate)[l], row(gla_nw)[l])
        sgu_p = (sgu_wc[l], sgu_bt[l], row(sgu_ln_g)[l], row(sgu_ln_b)[l])
        ssd_p = (_per_head(ssm_conv_w[l]), row(_per_head(ssm_conv_b))[l], row(exp64(ssm_dt_bias))[l],
                 row(exp64(ssm_a_log))[l], row(exp64(ssm_d))[l], row(ssm_norm_w)[l])
        x1 = _token_mixer(x2, B, S, w_p[l], gla_p, sgu_p, ssd_p, cos_t, sin_t, tb,
                          w_out_b[l], row(ln1_g)[l], row(ln1_b)[l])
        x2 = _moe(l, x1, rw_t[l], router_bias[l].reshape(N_EXPERTS, 1), exp_w_gate, exp_w_up, exp_w_down,
                  sh_gu[l], sh_d[l], row(ln2_g)[l], row(ln2_b)[l])
    return x2.reshape(B, S, D_MODEL)
```

```python
import numpy as np
import jax
import jax.numpy as jnp
from jax import lax
from jax.experimental import pallas as pl
from jax.experimental.pallas import tpu as pltpu

F32 = jnp.float32
BF16 = jnp.bfloat16
I32 = jnp.int32

D_MODEL = 1024
N_LAYERS = 4
GROUP_W = 256

GLA_H, GLA_DK, GLA_DV, GLA_RANK, GLA_TAU = 4, 32, 64, 16, 16.0
SGU_G, SGU_CHUNK = 4, 128
SSM_H, SSM_P, SSM_G, SSM_N, SSM_CONV = 4, 64, 2, 64, 4
DIL_H, DIL_DH, ROT_DIM, ROPE_THETA = 4, 64, 16, 500000.0
DIL_BRANCHES = ((128, 1), (512, 4), (2048, 16))

N_EXPERTS, TOP_K, N_EXPERT_GROUPS, TOPK_GROUPS, D_EXPERT = 128, 8, 8, 4, 256
ROUTED_SCALE = 1.0

ALPHA = (2 * N_LAYERS) ** 0.25
LN_EPS = 1e-5
RMS_EPS = 1e-6

W_GLA = 896
W_SGU = 512
W_SSD = 1280
W_DIL = 768
W_PROJ = W_GLA + W_SGU + W_SSD + W_DIL

BLK = 128
QBLK = 256
MOE_BLK = 1024
TOK_BLK = 256
DSP_BLK = 1024
CMB_BLK = 1024
NEG = -1e30

VMEM_LIMIT = 56 * 1024 * 1024


def _cparams(sem):
    return pltpu.CompilerParams(dimension_semantics=sem, vmem_limit_bytes=VMEM_LIMIT)


def _dot(a, b):
    return jnp.dot(a.astype(BF16), b.astype(BF16), preferred_element_type=F32)


def _dot_nt(a, b):
    return lax.dot_general(a.astype(BF16), b.astype(BF16), (((1,), (1,)), ((), ())),
                           preferred_element_type=F32)


def _dot_tn(a, b):
    return lax.dot_general(a.astype(BF16), b.astype(BF16), (((0,), (0,)), ((), ())),
                           preferred_element_type=F32)


def _split(a, n):
    out = []
    r = a
    for _ in range(n):
        p = r.astype(BF16)
        out.append(p)
        r = r - p.astype(F32)
    return out


def _dot_hi(a, b):
    a1, a2 = _split(a, 2)
    b1, b2 = _split(b, 2)
    return _dot(a1, b1) + (_dot(a1, b2) + _dot(a2, b1))


def _dot_nt_hi(a, b):
    a1, a2 = _split(a, 2)
    b1, b2 = _split(b, 2)
    return _dot_nt(a1, b1) + (_dot_nt(a1, b2) + _dot_nt(a2, b1))


def _dot_lhs01(m, b):
    b1, b2, b3 = _split(b, 3)
    return _dot(m, b1) + (_dot(m, b2) + _dot(m, b3))


def _dot_rhs01(a, m):
    a1, a2, a3 = _split(a, 3)
    return _dot(a1, m) + (_dot(a2, m) + _dot(a3, m))


def _iota(shape, dim):
    return lax.broadcasted_iota(I32, shape, dim)


def _sigmoid(x):
    return 1.0 / (1.0 + jnp.exp(-x))


def _silu(x):
    return x * _sigmoid(x)


def _softplus(x):
    return jnp.maximum(x, 0.0) + jnp.log1p(jnp.exp(-jnp.abs(x)))


def _gelu(x):
    return 0.5 * x * (1.0 + lax.erf(x * (2.0 ** -0.5)))


def _inproj_kernel(x_ref, w_ref, o_gla, o_sgu, o_ssd, o_dil):
    xb = x_ref[...].astype(BF16)
    off = 0
    for o in (o_gla, o_sgu, o_ssd, o_dil):
        n = o.shape[1]
        o[...] = jnp.dot(xb, w_ref[:, off:off + n], preferred_element_type=F32)
        off += n


def _inproj(x2, w_p):
    T = x2.shape[0]
    tm = 512
    widths = (W_GLA, W_SGU, W_SSD, W_DIL)
    return pl.pallas_call(
        _inproj_kernel,
        grid=(T // tm,),
        in_specs=[pl.BlockSpec((tm, D_MODEL), lambda i: (i, 0)),
                  pl.BlockSpec((D_MODEL, W_PROJ), lambda i: (0, 0))],
        out_specs=[pl.BlockSpec((tm, w), lambda i: (i, 0)) for w in widths],
        out_shape=[jax.ShapeDtypeStruct((T, w), F32) for w in widths],
        compiler_params=_cparams(("arbitrary",)),
        name="inproj",
    )(x2, w_p)


GLA_NB = 2
GLA_SUB = 16


def _gla_kernel(slab_ref, wg_ref, bg_ref, nw_ref, out_ref, st_ref, a_ref):
    S = slab_ref.shape[1]
    st_ref[...] = jnp.zeros_like(st_ref)
    row = _iota((BLK, BLK), 0)
    col = _iota((BLK, BLK), 1)
    tril = col <= row
    tril_bf = jnp.where(tril, 1.0, 0.0).astype(BF16)
    sub_row = _iota((GLA_SUB, BLK), 0)
    sub_col = _iota((GLA_SUB, BLK), 1)
    sub_head = sub_col >> 5
    v_head = _iota((BLK, GROUP_W), 1) >> 6
    st_diag = (_iota((GROUP_W, BLK), 0) >> 6) == (_iota((GROUP_W, BLK), 1) >> 5)
    seg = jnp.where((_iota((GROUP_W, GROUP_W), 0) >> 6) == (_iota((GROUP_W, GROUP_W), 1) >> 6),
                    1.0, 0.0).astype(BF16)
    wg = wg_ref[...]
    bg = bg_ref[...]
    nw = nw_ref[...]

    def block(b, r0):
        blk = slab_ref[b, pl.ds(r0, BLK), :]
        q = blk[:, 0:128] * (GLA_DK ** -0.5)
        k = blk[:, 128:256]
        v = blk[:, 256:512]
        r = blk[:, 512:768]
        lr = blk[:, 768:896]
        z = _dot_hi(lr, wg) + bg
        gk = (jnp.minimum(z, 0.0) - jnp.log1p(jnp.exp(-jnp.abs(z)))) * (1.0 / GLA_TAU)
        g = _dot_lhs01(tril_bf, gk)
        g_last = g[BLK - 1:BLK, :]
        q_in = q * jnp.exp(g)
        k_in = k * jnp.exp(g_last - g)
        vb = v.astype(BF16)
        o = _dot_nt(q_in, st_ref[b])
        for c in range(BLK // GLA_SUB):
            lo = c * GLA_SUB
            n_c = g[lo - 1:lo, :] if c else jnp.zeros((1, BLK), F32)
            qc = q[lo:lo + GLA_SUB, :] * jnp.exp(g[lo:lo + GLA_SUB, :] - n_c)
            kc = k * jnp.exp(jnp.where(row < lo + GLA_SUB, n_c - g, -jnp.inf))
            qs = jnp.concatenate([jnp.where(sub_head == h, qc, 0.0) for h in range(GLA_H)], axis=0)
            sc = _dot_nt(qs, kc)
            causal = sub_col <= sub_row + lo
            for h in range(GLA_H):
                a_ref[b, h, lo:lo + GLA_SUB, :] = jnp.where(causal, sc[h * GLA_SUB:(h + 1) * GLA_SUB, :], 0.0)
        for h in range(GLA_H):
            o = o + jnp.where(v_head == h, _dot(a_ref[b, h], vb), 0.0)
        st_ref[b] = st_ref[b] * jnp.exp(g_last) + jnp.where(st_diag, _dot_tn(vb, k_in), 0.0)
        ms = _dot_rhs01(o * o, seg) * (1.0 / GLA_DV)
        y = o * lax.rsqrt(ms + RMS_EPS) * nw * _silu(r)
        out_ref[b, pl.ds(r0, BLK), :] = y.astype(out_ref.dtype)

    def body(n, carry):
        r0 = pl.multiple_of(n * BLK, BLK)
        for b in range(slab_ref.shape[0]):
            block(b, r0)
        return carry

    lax.fori_loop(0, S // BLK, body, 0, unroll=2)


def _gla(slab, wg_p, bg, nw_t):
    B, S, _ = slab.shape
    nb = GLA_NB if B % GLA_NB == 0 else 1
    return pl.pallas_call(
        _gla_kernel,
        grid=(B // nb,),
        in_specs=[pl.BlockSpec((nb, S, W_GLA), lambda b: (b, 0, 0)),
                  pl.BlockSpec((128, 128), lambda b: (0, 0)),
                  pl.BlockSpec((1, 128), lambda b: (0, 0)),
                  pl.BlockSpec((1, GROUP_W), lambda b: (0, 0))],
        out_specs=pl.BlockSpec((nb, S, GROUP_W), lambda b: (b, 0, 0)),
        out_shape=jax.ShapeDtypeStruct((B, S, GROUP_W), BF16),
        scratch_shapes=[pltpu.VMEM((nb, GROUP_W, 128), F32), pltpu.VMEM((nb, GLA_H, BLK, BLK), F32)],
        compiler_params=_cparams(("arbitrary",)),
        name="gla",
    )(slab, wg_p, bg, nw_t)


def _sgu_kernel(slab_ref, w_ref, bias_ref, g_ref, b_ref, out_ref):
    S = slab_ref.shape[1]
    wmask = (_iota((BLK, SGU_G * BLK), 1) & (BLK - 1)) <= _iota((BLK, SGU_G * BLK), 0)
    w = jnp.where(wmask, w_ref[...], 0.0).astype(BF16)
    lane_grp = _iota((BLK, GROUP_W), 1) >> 6
    bias = bias_ref[...]
    ln_g = g_ref[...]
    ln_b = b_ref[...]

    def body(n, carry):
        r0 = pl.multiple_of(n * BLK, BLK)
        blk = slab_ref[0, pl.ds(r0, BLK), :]
        u = _gelu(blk[:, 0:256])
        v = _gelu(blk[:, 256:512])
        mu = jnp.mean(v, axis=-1, keepdims=True)
        var = jnp.mean(jnp.square(v - mu), axis=-1, keepdims=True)
        v = (v - mu) * lax.rsqrt(var + LN_EPS) * ln_g + ln_b
        vexp = jnp.concatenate([jnp.where(lane_grp == g, v, 0.0) for g in range(SGU_G)], axis=0)
        s = _dot(w, vexp) + bias
        out_ref[0, pl.ds(r0, BLK), :] = (u * s).astype(out_ref.dtype)
        return carry

    lax.fori_loop(0, S // BLK, body, 0, unroll=2)


def _sgu(slab, w_cat, bias_t, ln_g, ln_b):
    B, S, _ = slab.shape
    return pl.pallas_call(
        _sgu_kernel,
        grid=(B,),
        in_specs=[pl.BlockSpec((1, S, W_SGU), lambda b: (b, 0, 0)),
                  pl.BlockSpec((BLK, SGU_G * BLK), lambda b: (0, 0)),
                  pl.BlockSpec((BLK, GROUP_W), lambda b: (0, 0)),
                  pl.BlockSpec((1, GROUP_W), lambda b: (0, 0)),
                  pl.BlockSpec((1, GROUP_W), lambda b: (0, 0))],
        out_specs=pl.BlockSpec((1, S, GROUP_W), lambda b: (b, 0, 0)),
        out_shape=jax.ShapeDtypeStruct((B, S, GROUP_W), BF16),
        compiler_params=_cparams(("arbitrary",)),
        name="sgu",
    )(slab, w_cat, bias_t, ln_g, ln_b)


def _ssd_kernel(slab_ref, cw_ref, cb_ref, dtb_ref, a_ref, d_ref, nw_ref, out_ref, xpad_ref, st_ref):
    S = slab_ref.shape[1]
    CW = 3 * GROUP_W
    st_ref[...] = jnp.zeros_like(st_ref)
    xpad_ref[0:8, :] = jnp.zeros((8, CW), F32)
    xpad_ref[8:S + 8, :] = slab_ref[0, :, 256:256 + CW]
    row = _iota((BLK, BLK), 0)
    col = _iota((BLK, BLK), 1)
    tril = col <= row
    tril_bf = jnp.where(tril, 1.0, 0.0).astype(BF16)
    lane_head = _iota((BLK, GROUP_W), 1) >> 6
    st_diag = (_iota((GROUP_W, GROUP_W), 0) >> 6) == (_iota((GROUP_W, GROUP_W), 1) >> 6)
    cw = cw_ref[...]
    cb = cb_ref[...]
    dtb = dtb_ref[...]
    a_neg = -jnp.exp(a_ref[...])
    d_skip = d_ref[...]
    nw = nw_ref[...]

    def body(n, carry):
        r0 = pl.multiple_of(n * BLK, BLK)
        xw = xpad_ref[pl.ds(r0, BLK + 8), :]
        conv = cb + cw[0:1, :] * xw[5:5 + BLK, :]
        for w in range(1, SSM_CONV):
            conv = conv + cw[w:w + 1, :] * xw[5 + w:5 + w + BLK, :]
        xbc = _silu(conv)
        xs = xbc[:, 0:256]
        b_e = xbc[:, 256:512]
        c_e = xbc[:, 512:768]
        z = slab_ref[0, pl.ds(r0, BLK), 0:256]
        dt = _softplus(slab_ref[0, pl.ds(r0, BLK), 1024:1280] + dtb)
        acs = _dot_lhs01(tril_bf, dt * a_neg)
        acs_last = acs[BLK - 1:BLK, :]
        x_dt = (xs * dt).astype(BF16)
        acs_t = [jnp.transpose(acs[:, 0:128]), jnp.transpose(acs[:, 128:256])]
        y = _dot(c_e * jnp.exp(acs), st_ref[...])
        b_bf = b_e.astype(BF16)
        for h in range(SSM_H):
            a_col = jnp.broadcast_to(acs[:, 64 * h:64 * h + 1], (BLK, BLK))
            a_row = jnp.broadcast_to(acs_t[h // 2][64 * (h % 2):64 * (h % 2) + 1, :], (BLK, BLK))
            lm = jnp.exp(jnp.where(tril, a_col - a_row, -jnp.inf))
            ch = jnp.where(lane_head == h, c_e, 0.0)
            m = _dot_nt(ch, b_bf) * lm
            y = y + jnp.where(lane_head == h, _dot(m, x_dt), 0.0)
        new = _dot_tn(b_e * jnp.exp(acs_last - acs), x_dt)
        st_ref[...] = st_ref[...] * jnp.exp(acs_last) + jnp.where(st_diag, new, 0.0)
        y = (y + d_skip * xs) * _silu(z)
        parts = []
        for g in range(SSM_G):
            yg = y[:, 128 * g:128 * (g + 1)]
            ms = jnp.mean(yg * yg, axis=-1, keepdims=True)
            parts.append(yg * lax.rsqrt(ms + RMS_EPS))
        out = jnp.concatenate(parts, axis=1) * nw
        out_ref[0, pl.ds(r0, BLK), :] = out.astype(out_ref.dtype)
        return carry

    lax.fori_loop(0, S // BLK, body, 0, unroll=2)


def _ssd(slab, cw_e, cb_e, dtb_e, a_e, d_e, nw):
    B, S, _ = slab.shape
    CW = 3 * GROUP_W
    vec = lambda w: pl.BlockSpec((1, w), lambda b: (0, 0))
    return pl.pallas_call(
        _ssd_kernel,
        grid=(B,),
        in_specs=[pl.BlockSpec((1, S, W_SSD), lambda b: (b, 0, 0)),
                  pl.BlockSpec((SSM_CONV, CW), lambda b: (0, 0)),
                  vec(CW), vec(GROUP_W), vec(GROUP_W), vec(GROUP_W), vec(GROUP_W)],
        out_specs=pl.BlockSpec((1, S, GROUP_W), lambda b: (b, 0, 0)),
        out_shape=jax.ShapeDtypeStruct((B, S, GROUP_W), BF16),
        scratch_shapes=[pltpu.VMEM((S + 8, CW), F32), pltpu.VMEM((GROUP_W, GROUP_W), F32)],
        compiler_params=_cparams(("arbitrary",)),
        name="ssd",
    )(slab, cw_e, cb_e, dtb_e, a_e, d_e, nw)


def _branch_log_multiplicity(S):
    nq = S // QBLK
    r = np.arange(QBLK)[:, None]
    u = np.arange(S)[None, :]
    d = (nq - 1) * QBLK + r - u
    c = np.zeros(d.shape, np.float64)
    for window, dil in DIL_BRANCHES:
        c += (d >= 0) & (d % dil == 0) & (d <= window)
    with np.errstate(divide="ignore"):
        return np.where(c > 0, np.log(np.maximum(c, 1.0)), NEG).astype(np.float32)


def _dil_kernel(slab_ref, cos_ref, sin_ref, tb_ref, out_ref, q_ref, k_ref, v_ref):
    S = slab_ref.shape[1]
    nq = S // QBLK
    lane_d = _iota((1, 128), 1) & (DIL_DH - 1)

    def rope(t):
        fwd = pltpu.roll(t, 128 - ROT_DIM // 2, 1)
        bwd = pltpu.roll(t, ROT_DIM // 2, 1)
        return t * cos_ref[0] + jnp.where(lane_d < ROT_DIM // 2, fwd, bwd) * sin_ref[0]

    for c in range(0, GROUP_W, 128):
        q_ref[:, c:c + 128] = (rope(slab_ref[0, :, c:c + 128]) * (DIL_DH ** -0.5)).astype(BF16)
        k_ref[:, c:c + 128] = rope(slab_ref[0, :, 256 + c:384 + c]).astype(BF16)
    v_ref[...] = slab_ref[0, :, 512:768].astype(BF16)
    lane_head = _iota((QBLK, GROUP_W), 1) >> 6
    for i in range(nq):
        nk = (i + 1) * QBLK
        qi = q_ref[i * QBLK:(i + 1) * QBLK, :]
        acc = jnp.zeros((QBLK, GROUP_W), F32)
        for h in range(DIL_H):
            qh = jnp.where(lane_head == h, qi, jnp.zeros_like(qi))
            s = _dot_nt(qh, k_ref[0:nk, :]) + tb_ref[:, (nq - 1 - i) * QBLK:]
            m = jnp.max(s, axis=-1, keepdims=True)
            p = jnp.exp(s - m)
            den = jnp.sum(p, axis=-1, keepdims=True)
            o = _dot(p, v_ref[0:nk, :])
            acc = acc + jnp.where(lane_head == h, o / den, 0.0)
        out_ref[0, i * QBLK:(i + 1) * QBLK, :] = acc.astype(out_ref.dtype)


def _dil(slab, cos_t, sin_t, tb):
    B, S, _ = slab.shape
    return pl.pallas_call(
        _dil_kernel,
        grid=(B,),
        in_specs=[pl.BlockSpec((1, S, W_DIL), lambda b: (b, 0, 0)),
                  pl.BlockSpec((1, S, 128), lambda b: (b, 0, 0)),
                  pl.BlockSpec((1, S, 128), lambda b: (b, 0, 0)),
                  pl.BlockSpec((QBLK, S), lambda b: (0, 0), pipeline_mode=pl.Buffered(1))],
        out_specs=pl.BlockSpec((1, S, GROUP_W), lambda b: (b, 0, 0)),
        out_shape=jax.ShapeDtypeStruct((B, S, GROUP_W), BF16),
        scratch_shapes=[pltpu.VMEM((S, GROUP_W), BF16)] * 3,
        compiler_params=_cparams(("arbitrary",)),
        name="dil",
    )(slab, cos_t, sin_t, tb)


def _layernorm(x, g, b):
    mu = jnp.mean(x, axis=-1, keepdims=True)
    var = jnp.mean(jnp.square(x - mu), axis=-1, keepdims=True)
    return (x - mu) * lax.rsqrt(var + LN_EPS) * g + b


def _outproj_kernel(ya, yb, yc, yd, x_ref, w_ref, g_ref, b_ref, o_ref):
    y = jnp.concatenate([ya[...], yb[...], yc[...], yd[...]], axis=1)
    y = jnp.dot(y, w_ref[...], preferred_element_type=F32)
    o_ref[...] = _layernorm(ALPHA * x_ref[...] + y, g_ref[...], b_ref[...])


def _outproj(ya, yb, yc, yd, x2, w_out, g, b):
    T = x2.shape[0]
    tm = 512
    ysp = pl.BlockSpec((tm, GROUP_W), lambda i: (i, 0))
    vec = pl.BlockSpec((1, D_MODEL), lambda i: (0, 0))
    return pl.pallas_call(
        _outproj_kernel,
        grid=(T // tm,),
        in_specs=[ysp, ysp, ysp, ysp,
                  pl.BlockSpec((tm, D_MODEL), lambda i: (i, 0)),
                  pl.BlockSpec((D_MODEL, D_MODEL), lambda i: (0, 0)), vec, vec],
        out_specs=pl.BlockSpec((tm, D_MODEL), lambda i: (i, 0)),
        out_shape=jax.ShapeDtypeStruct((T, D_MODEL), F32),
        compiler_params=_cparams(("arbitrary",)),
        name="outproj_ln",
    )(ya, yb, yc, yd, x2, w_out, g, b)


def _router_kernel(x_ref, rw_ref, rb_ref, idx_ref, gate_ref, pos_ref, cnt_ref, carry_ref):
    TB = x_ref.shape[0]
    E = N_EXPERTS
    per = E // N_EXPERT_GROUPS

    @pl.when(pl.program_id(0) == 0)
    def _():
        carry_ref[...] = jnp.zeros_like(carry_ref)

    logits = _dot_nt_hi(rw_ref[...], x_ref[...])
    scores = _sigmoid(logits)
    choice = scores + rb_ref[...]
    e_iota = _iota((E, TB), 0)

    l_iota = _iota((per, TB), 0)
    grp = []
    for g in range(N_EXPERT_GROUPS):
        cg = choice[per * g:per * (g + 1), :]
        m1 = jnp.max(cg, axis=0, keepdims=True)
        i1 = jnp.min(jnp.where(cg == m1, l_iota, per), axis=0, keepdims=True)
        m2 = jnp.max(jnp.where(l_iota == i1, -jnp.inf, cg), axis=0, keepdims=True)
        grp.append(m1 + m2)
    gs = jnp.concatenate(grp, axis=0)
    g_iota = _iota((N_EXPERT_GROUPS, TB), 0)
    keep = jnp.zeros((N_EXPERT_GROUPS, TB), jnp.bool_)
    for _ in range(TOPK_GROUPS):
        m = jnp.max(gs, axis=0, keepdims=True)
        gi = jnp.min(jnp.where(gs == m, g_iota, N_EXPERT_GROUPS), axis=0, keepdims=True)
        hit = g_iota == gi
        keep = keep | hit
        gs = jnp.where(hit, -jnp.inf, gs)
    keep_f = jnp.where(keep, 1.0, 0.0)
    keep_e = jnp.concatenate([jnp.broadcast_to(keep_f[g:g + 1, :], (per, TB))
                              for g in range(N_EXPERT_GROUPS)], axis=0)
    cm = jnp.where(keep_e > 0.5, choice, -jnp.inf)

    hits, idxs, gates = [], [], []
    onehot = jnp.zeros((E, TB), F32)
    for _ in range(TOP_K):
        m = jnp.max(cm, axis=0, keepdims=True)
        ei = jnp.min(jnp.where(cm == m, e_iota, E), axis=0, keepdims=True)
        hit = e_iota == ei
        hits.append(hit)
        idxs.append(ei)
        gates.append(jnp.sum(jnp.where(hit, scores, 0.0), axis=0, keepdims=True))
        onehot = onehot + jnp.where(hit, 1.0, 0.0)
        cm = jnp.where(hit, -jnp.inf, cm)
    gate = jnp.concatenate(gates, axis=0)
    gate = gate / jnp.sum(gate, axis=0, keepdims=True) * ROUTED_SCALE

    before = jnp.where(_iota((TB, TB), 0) < _iota((TB, TB), 1), 1.0, 0.0).astype(BF16)
    cnt = carry_ref[...] + _dot(onehot, before)
    pos = jnp.concatenate([jnp.sum(jnp.where(h, cnt, 0.0), axis=0, keepdims=True) for h in hits], axis=0)
    carry_ref[...] = carry_ref[...] + jnp.sum(onehot, axis=1, keepdims=True)

    idx_ref[...] = jnp.concatenate(idxs, axis=0)
    gate_ref[...] = gate
    pos_ref[...] = pos.astype(I32)
    cnt_ref[...] = jnp.broadcast_to(carry_ref[...], cnt_ref.shape)


def _router(x1, rw_t, rb):
    T = x1.shape[0]
    TB = TOK_BLK
    kt = pl.BlockSpec((TOP_K, TB), lambda i: (0, i))
    return pl.pallas_call(
        _router_kernel,
        grid=(T // TB,),
        in_specs=[pl.BlockSpec((TB, D_MODEL), lambda i: (i, 0)),
                  pl.BlockSpec((N_EXPERTS, D_MODEL), lambda i: (0, 0)),
                  pl.BlockSpec((N_EXPERTS, 1), lambda i: (0, 0))],
        out_specs=[kt, kt, kt, pl.BlockSpec((N_EXPERTS, 128), lambda i: (0, 0))],
        out_shape=[jax.ShapeDtypeStruct((TOP_K, T), I32), jax.ShapeDtypeStruct((TOP_K, T), F32),
                   jax.ShapeDtypeStruct((TOP_K, T), I32), jax.ShapeDtypeStruct((N_EXPERTS, 128), F32)],
        scratch_shapes=[pltpu.VMEM((N_EXPERTS, 1), F32)],
        compiler_params=_cparams(("arbitrary",)),
        name="router",
    )(x1, rw_t, rb)


def _slots_kernel(ps_ref, idx_ref, pos_ref, dest_ref):
    idx = idx_ref[...]

    def body(e, acc):
        return acc + jnp.where(idx == e, ps_ref[e], 0)

    dest_ref[...] = lax.fori_loop(0, N_EXPERTS, body, pos_ref[...], unroll=8)


def _slots(pad_start, idx, pos):
    T = idx.shape[1]
    tb = min(T, 2048)
    kt = pl.BlockSpec((TOP_K, tb), lambda i, ps: (0, i))
    return pl.pallas_call(
        _slots_kernel,
        grid_spec=pltpu.PrefetchScalarGridSpec(
            num_scalar_prefetch=1, grid=(T // tb,), in_specs=[kt, kt], out_specs=kt),
        out_shape=jax.ShapeDtypeStruct((TOP_K, T), I32),
        compiler_params=_cparams(("arbitrary",)),
        name="slots",
    )(pad_start, idx, pos)


ROW_T = D_MODEL // 2 // 128
U32 = jnp.uint32


def _pack_rows(x):
    bits = lax.bitcast_convert_type(x.astype(BF16).astype(F32), U32)
    half = D_MODEL // 2
    return (bits[:, :half] >> 16) | (bits[:, half:] & U32(0xFFFF0000))


def _unpack_rows(w):
    lo = lax.bitcast_convert_type(w << 16, F32)
    hi = lax.bitcast_convert_type(w & U32(0xFFFF0000), F32)
    return jnp.concatenate([lo, hi], axis=1)


def _to_tiles(dst_ref, base, w):
    n = w.shape[0]
    for s in range(ROW_T):
        dst_ref[pl.ds(base + s, n, stride=ROW_T), :] = w[:, 128 * s:128 * (s + 1)]


def _from_tiles(src_ref, base, n):
    return jnp.concatenate([src_ref[pl.ds(base + s, n, stride=ROW_T), :] for s in range(ROW_T)], axis=1)


def _tailzero_kernel(tb_ref, xs_ref):
    del tb_ref
    xs_ref[...] = jnp.zeros_like(xs_ref)


def _tailzero(tail_blk, nblk):
    rows = MOE_BLK * ROW_T
    return pl.pallas_call(
        _tailzero_kernel,
        grid_spec=pltpu.PrefetchScalarGridSpec(
            num_scalar_prefetch=1, grid=(N_EXPERTS,), in_specs=[],
            out_specs=pl.BlockSpec((rows, 128), lambda e, tb: (tb[e], 0))),
        out_shape=jax.ShapeDtypeStruct((nblk * rows, 128), U32),
        compiler_params=_cparams(("arbitrary",)),
        name="tailzero",
    )(tail_blk)


def _dispatch_kernel(dest_ref, x_ref, xs_in, xs_out, xt, sem):
    del xs_in
    TB = x_ref.shape[0]
    _to_tiles(xt, 0, _pack_rows(x_ref[...]))

    def issue(t, c):
        src = xt.at[pl.ds(pl.multiple_of(t * ROW_T, ROW_T), ROW_T), :]
        for k in range(TOP_K):
            d = dest_ref[t * TOP_K + k]
            pltpu.make_async_copy(src, xs_out.at[pl.ds(pl.multiple_of(d * ROW_T, ROW_T), ROW_T), :], sem).start()
        return c

    lax.fori_loop(0, TB, issue, 0, unroll=2)

    for k in range(TOP_K):
        pltpu.make_async_copy(xt, xs_out.at[pl.ds(0, TB * ROW_T), :], sem).wait()


def _dispatch(dest_flat, x1, xs0):
    T = x1.shape[0]
    TB = DSP_BLK
    return pl.pallas_call(
        _dispatch_kernel,
        grid=(T // TB,),
        in_specs=[pl.BlockSpec((TB * TOP_K,), lambda i: (i,), memory_space=pltpu.SMEM),
                  pl.BlockSpec((TB, D_MODEL), lambda i: (i, 0)),
                  pl.BlockSpec(memory_space=pl.ANY)],
        out_specs=pl.BlockSpec(memory_space=pl.ANY),
        out_shape=jax.ShapeDtypeStruct(xs0.shape, xs0.dtype),
        scratch_shapes=[pltpu.VMEM((TB * ROW_T, 128), U32), pltpu.SemaphoreType.DMA],
        input_output_aliases={2: 0},
        compiler_params=_cparams(("arbitrary",)),
        name="dispatch",
    )(dest_flat, x1, xs0)


def _experts_kernel(be_ref, nu_ref, xs_ref, wg_ref, wu_ref, wd_ref, ys_ref, wgu_s, wd_s):
    i = pl.program_id(0)

    @pl.when(i < nu_ref[0])
    def _():
        prev = be_ref[jnp.maximum(i - 1, 0)]

        @pl.when((i == 0) | (be_ref[i] != prev))
        def _():
            wgu_s[:, 0:D_EXPERT] = wg_ref[0, 0].astype(BF16)
            wgu_s[:, D_EXPERT:] = wu_ref[0, 0].astype(BF16)
            wd_s[...] = wd_ref[0, 0].astype(BF16)

        x = _unpack_rows(_from_tiles(xs_ref, 0, MOE_BLK)).astype(BF16)
        h = jnp.dot(x, wgu_s[...], preferred_element_type=F32)
        a = _silu(h[:, 0:D_EXPERT]) * h[:, D_EXPERT:]
        _to_tiles(ys_ref, 0, _pack_rows(jnp.dot(a.astype(BF16), wd_s[...], preferred_element_type=F32)))


def _experts(layer, blk_expert, n_used, xs, wg, wu, wd):
    rows = MOE_BLK * ROW_T
    nblk = xs.shape[0] // rows
    row = lambda i, be, nu: (jnp.minimum(i, nu[0] - 1), 0)
    wsel = lambda i, be, nu: (layer, be[i], 0, 0)
    return pl.pallas_call(
        _experts_kernel,
        grid_spec=pltpu.PrefetchScalarGridSpec(
            num_scalar_prefetch=2,
            grid=(nblk,),
            in_specs=[pl.BlockSpec((rows, 128), row),
                      pl.BlockSpec((1, 1, D_MODEL, D_EXPERT), wsel),
                      pl.BlockSpec((1, 1, D_MODEL, D_EXPERT), wsel),
                      pl.BlockSpec((1, 1, D_EXPERT, D_MODEL), wsel)],
            out_specs=pl.BlockSpec((rows, 128), row),
            scratch_shapes=[pltpu.VMEM((D_MODEL, 2 * D_EXPERT), BF16),
                            pltpu.VMEM((D_EXPERT, D_MODEL), BF16)]),
        out_shape=jax.ShapeDtypeStruct(xs.shape, U32),
        compiler_params=_cparams(("arbitrary",)),
        name="experts",
    )(blk_expert, n_used, xs, wg, wu, wd)


def _combine_kernel(dest_ref, gate_ref, x_ref, ys_ref, sgu_ref, sd_ref, g_ref, b_ref, o_ref, buf, sem):
    TB = x_ref.shape[0]

    def issue(t, c):
        for k in range(TOP_K):
            d = dest_ref[t * TOP_K + k]
            pltpu.make_async_copy(ys_ref.at[pl.ds(pl.multiple_of(d * ROW_T, ROW_T), ROW_T), :],
                                  buf.at[pl.ds(pl.multiple_of((k * TB + t) * ROW_T, ROW_T), ROW_T), :], sem).start()
        return c

    lax.fori_loop(0, TB, issue, 0, unroll=2)

    x = x_ref[...]
    h = jnp.dot(x.astype(BF16), sgu_ref[...], preferred_element_type=F32)
    a = _silu(h[:, 0:D_EXPERT]) * h[:, D_EXPERT:]
    y = ALPHA * x + jnp.dot(a.astype(BF16), sd_ref[...], preferred_element_type=F32)

    pltpu.make_async_copy(ys_ref.at[pl.ds(0, TOP_K * TB * ROW_T), :], buf, sem).wait()

    gate = gate_ref[...]
    for k in range(TOP_K):
        y = y + gate[:, k:k + 1] * _unpack_rows(_from_tiles(buf, k * TB * ROW_T, TB))
    o_ref[...] = _layernorm(y, g_ref[...], b_ref[...])


def _combine(dest_flat, gate_t, x1, ys, sh_gu, sh_d, g, b):
    T = x1.shape[0]
    TB = CMB_BLK
    vec = pl.BlockSpec((1, D_MODEL), lambda i: (0, 0))
    return pl.pallas_call(
        _combine_kernel,
        grid=(T // TB,),
        in_specs=[pl.BlockSpec((TB * TOP_K,), lambda i: (i,), memory_space=pltpu.SMEM),
                  pl.BlockSpec((TB, TOP_K), lambda i: (i, 0)),
                  pl.BlockSpec((TB, D_MODEL), lambda i: (i, 0)),
                  pl.BlockSpec(memory_space=pl.ANY),
                  pl.BlockSpec((D_MODEL, 2 * D_EXPERT), lambda i: (0, 0)),
                  pl.BlockSpec((D_EXPERT, D_MODEL), lambda i: (0, 0)), vec, vec],
        out_specs=pl.BlockSpec((TB, D_MODEL), lambda i: (i, 0)),
        out_shape=jax.ShapeDtypeStruct((T, D_MODEL), F32),
        scratch_shapes=[pltpu.VMEM((TOP_K * TB * ROW_T, 128), U32), pltpu.SemaphoreType.DMA],
        compiler_params=_cparams(("arbitrary",)),
        name="combine_ln",
    )(dest_flat, gate_t, x1, ys, sh_gu, sh_d, g, b)


def _token_mixer(x2, B, S, w_p, gla_p, sgu_p, ssd_p, cos_t, sin_t, tb, w_out, ln_g, ln_b):
    s_gla, s_sgu, s_ssd, s_dil = _inproj(x2, w_p)
    ya = _gla(s_gla.reshape(B, S, W_GLA), *gla_p)
    yb = _sgu(s_sgu.reshape(B, S, W_SGU), *sgu_p)
    yc = _ssd(s_ssd.reshape(B, S, W_SSD), *ssd_p)
    yd = _dil(s_dil.reshape(B, S, W_DIL), cos_t, sin_t, tb)
    T = B * S
    flat = lambda y: y.reshape(T, GROUP_W)
    return _outproj(flat(ya), flat(yb), flat(yc), flat(yd), x2, w_out, ln_g, ln_b)


def _moe(layer, x1, rw_t, rb, wg, wu, wd, sh_gu, sh_d, ln_g, ln_b):
    T = x1.shape[0]
    idx, gate, pos, cnt = _router(x1, rw_t, rb)
    counts = cnt[:, 0].astype(I32)
    padded = (counts + MOE_BLK - 1) // MOE_BLK * MOE_BLK
    pad_end = jnp.cumsum(padded)
    pad_start = pad_end - padded
    dest = _slots(pad_start.astype(I32), idx, pos)
    dest_flat = dest.T.reshape(T * TOP_K)
    nblk = T * TOP_K // MOE_BLK + N_EXPERTS
    n_used = (pad_end[-1] // MOE_BLK).astype(I32)
    blk = jnp.minimum(jnp.arange(nblk, dtype=I32), n_used - 1) * MOE_BLK
    blk_expert = jnp.sum((pad_end[None, :] <= blk[:, None]).astype(I32), axis=1)
    blk_expert = jnp.minimum(blk_expert, N_EXPERTS - 1)
    tail_blk = jnp.maximum(pad_end // MOE_BLK - 1, 0).astype(I32)
    xs = _dispatch(dest_flat, x1, _tailzero(tail_blk, nblk))
    ys = _experts(layer, blk_expert, n_used.reshape(1), xs, wg, wu, wd)
    return _combine(dest_flat, gate.T, x1, ys, sh_gu, sh_d, ln_g, ln_b)


def _prep_w_in(w_in):
    o = np.cumsum((0, 128, 128, 256, 256, 16, 256, 256, 256, 512, 4, 256, 256, 256))
    c = lambda i: w_in[..., o[i]:o[i + 1]]
    xbc = c(8)
    x_cols = xbc[..., 0:256]
    per_head = lambda m: jnp.concatenate(
        [m[..., SSM_N * (h // (SSM_H // SSM_G)):SSM_N * (h // (SSM_H // SSM_G) + 1)] for h in range(SSM_H)], axis=-1)
    b_cols = per_head(xbc[..., 256:384])
    c_cols = per_head(xbc[..., 384:512])
    dt_cols = jnp.repeat(c(9), SSM_P, axis=-1)
    pad = jnp.zeros(w_in.shape[:-1] + (W_GLA - o[5],), w_in.dtype)
    cols = [c(0), c(1), c(2), c(3), c(4), pad, c(5), c(6), c(7), x_cols, b_cols, c_cols, dt_cols,
            c(10), c(11), c(12)]
    return jnp.concatenate(cols, axis=-1).astype(BF16)


def _per_head(v):
    rep = SSM_H // SSM_G
    xp = v[..., 0:256]
    pick = lambda m: jnp.concatenate([m[..., SSM_N * (h // rep):SSM_N * (h // rep + 1)] for h in range(SSM_H)], axis=-1)
    return jnp.concatenate([xp, pick(v[..., 256:384]), pick(v[..., 384:512])], axis=-1)


def kernel(x, positions, w_in, gla_w_gate, gla_b_gate, gla_norm_w, sgu_ln_g, sgu_ln_b, sgu_w, sgu_b, ssm_conv_w, ssm_conv_b, ssm_dt_bias, ssm_a_log, ssm_d, ssm_norm_w, w_out, ln1_g, ln1_b, router_w, router_bias, exp_w_gate, exp_w_up, exp_w_down, sh_w_gate, sh_w_up, sh_w_down, ln2_g, ln2_b):
    B, S, _ = x.shape
    L = w_in.shape[0]
    T = B * S

    inv_freq = ROPE_THETA ** (-jnp.arange(0, ROT_DIM, 2, dtype=F32) / ROT_DIM)
    ang = positions.astype(F32)[..., None] * inv_freq
    cos, sin = jnp.cos(ang), jnp.sin(ang)
    ones = jnp.ones((B, S, DIL_DH - ROT_DIM), F32)
    cos_t = jnp.tile(jnp.concatenate([cos, cos, ones], axis=-1), (1, 1, 2))
    sin_t = jnp.tile(jnp.concatenate([-sin, sin, 0.0 * ones], axis=-1), (1, 1, 2))
    tb = jnp.asarray(_branch_log_multiplicity(S))

    w_p = _prep_w_in(w_in)
    wg_p = jnp.pad(gla_w_gate, ((0, 0), (0, 128 - GLA_RANK), (0, 0)))
    row = lambda v: v.reshape(L, 1, -1)
    gla_nw = jnp.tile(gla_norm_w, (1, GLA_H))
    sgu_wc = jnp.transpose(sgu_w, (0, 2, 1, 3)).reshape(L, SGU_CHUNK, SGU_G * SGU_CHUNK)
    sgu_bt = jnp.repeat(jnp.transpose(sgu_b, (0, 2, 1)), GROUP_W // SGU_G, axis=-1)
    exp64 = lambda v: jnp.repeat(v, SSM_P, axis=-1)
    w_out_b = w_out.astype(BF16)
    rw_t = jnp.transpose(router_w, (0, 2, 1))
    sh_gu = jnp.concatenate([sh_w_gate, sh_w_up], axis=-1).astype(BF16)
    sh_d = sh_w_down.astype(BF16)

    x2 = x.reshape(T, D_MODEL)
    for l in range(L):
        gla_p = (wg_p[l], row(gla_b_gate)[l], row(gla_nw)[l])
        sgu_p = (sgu_wc[l], sgu_bt[l], row(sgu_ln_g)[l], row(sgu_ln_b)[l])
        ssd_p = (_per_head(ssm_conv_w[l]), row(_per_head(ssm_conv_b))[l], row(exp64(ssm_dt_bias))[l],
                 row(exp64(ssm_a_log))[l], row(exp64(ssm_d))[l], row(ssm_norm_w)[l])
        x1 = _token_mixer(x2, B, S, w_p[l], gla_p, sgu_p, ssd_p, cos_t, sin_t, tb,
                          w_out_b[l], row(ln1_g)[l], row(ln1_b)[l])
        x2 = _moe(l, x1, rw_t[l], router_bias[l].reshape(N_EXPERTS, 1), exp_w_gate, exp_w_up, exp_w_down,
                  sh_gu[l], sh_d[l], row(ln2_g)[l], row(ln2_b)[l])
    return x2.reshape(B, S, D_MODEL)
```
